```python
import math
import jax, jax.numpy as jnp
from jax import lax
import numpy as np

D_MODEL = 1024
BATCH = 8
SEQ = 2048
DEPTH = 1

D_MIX = D_MODEL
D_HYENA = D_MIX // 2
D_ATTN = D_MIX - D_HYENA
HEAD_DIM = 64
N_HEADS = D_ATTN // HEAD_DIM
HYENA_ORDER = 2
SHORT_CONV = 3
FILTER_EMB = 33
FILTER_BANDS = (FILTER_EMB - 1) // 2
FILTER_WIDTH = 64
N_DIRS = 2
DECAY_TARGET = 1e-2
FAST_DECAY_PCT = 0.3
SLOW_DECAY_PCT = 1.5
DILATED_PATTERNS = ((128, 1), (512, 4), (2048, 16))
D_FF = 2816
RMS_EPS = 1e-6
NEG_INF = -1e30

kernel_name = "hybrid_hyena_dilated_attn_macaron_block"


def rmsnorm(x, g):
    xf = x.astype(jnp.float32)
    y = xf * lax.rsqrt(jnp.mean(xf * xf, axis=-1, keepdims=True) + RMS_EPS)
    return (y * g.astype(jnp.float32)).astype(x.dtype)


def swiglu(h, w_gate, w_up, w_down):
    return (jax.nn.silu(h @ w_gate) * (h @ w_up)) @ w_down


def alibi_slopes():
    return jnp.asarray(np.array([2.0 ** (-8.0 * (i + 1) / N_HEADS) for i in range(N_HEADS)], np.float32))


def short_conv(u, w, b):
    L = u.shape[1]
    r = SHORT_CONV // 2
    up = jnp.pad(u, ((0, 0), (r, r), (0, 0)))
    y = b
    for i in range(SHORT_CONV):
        y = y + up[:, i:i + L] * w[i]
    return y


def hyena_filters(L, fw1, fb1, fw2, fb2, fw3, fb3, fw_out, f_freq):
    t = jnp.linspace(0.0, 1.0, L, dtype=jnp.float32)[:, None]
    w = 2.0 * math.pi * jnp.arange(L, dtype=jnp.float32)[:, None] / L
    f = jnp.linspace(1e-4, FILTER_BANDS - 1, FILTER_BANDS, dtype=jnp.float32)[None, :]
    z = jnp.concatenate([t, jnp.cos(f * w), -jnp.sin(f * w)], axis=-1)
    freq = f_freq.astype(jnp.float32)
    h = jnp.sin(freq * (z @ fw1.astype(jnp.float32) + fb1.astype(jnp.float32)))
    h = jnp.sin(freq * (h @ fw2.astype(jnp.float32) + fb2.astype(jnp.float32)))
    h = jnp.sin(freq * (h @ fw3.astype(jnp.float32) + fb3.astype(jnp.float32)))
    k = (h @ fw_out.astype(jnp.float32)).reshape(L, HYENA_ORDER, N_DIRS, D_HYENA)
    max_decay = math.log(DECAY_TARGET) / FAST_DECAY_PCT
    min_decay = math.log(DECAY_TARGET) / SLOW_DECAY_PCT
    deltas = jnp.linspace(min_decay, max_decay, D_HYENA, dtype=jnp.float32)
    decay = jnp.exp(-t * jnp.abs(deltas)[None, :])
    return k * decay[:, None, None, :]


def bidir_long_conv(u, h_fwd, h_bwd, skip):
    L = u.shape[1]
    h_full = jnp.concatenate([h_fwd, jnp.zeros_like(h_fwd[:1]), h_bwd[:0:-1]], axis=0)
    Hf = jnp.fft.rfft(h_full, axis=0)
    U = jnp.fft.rfft(u, n=2 * L, axis=1)
    y = jnp.fft.irfft(U * Hf[None], n=2 * L, axis=1)[:, :L]
    return y + u * skip


def hyena_mixer(u3, conv_w, conv_b, fw1, fb1, fw2, fb2, fw3, fb3, fw_out, f_freq, f_skip):
    L = u3.shape[1]
    u = short_conv(u3.astype(jnp.float32), conv_w.astype(jnp.float32), conv_b.astype(jnp.float32))
    v, x1, x2 = jnp.split(u, 3, axis=-1)
    filt = hyena_filters(L, fw1, fb1, fw2, fb2, fw3, fb3, fw_out, f_freq)
    skip = f_skip.astype(jnp.float32)
    z = v
    for o, gate in enumerate((x1, x2)):
        z = gate * bidir_long_conv(z, filt[:, o, 0], filt[:, o, 1], skip[o])
    return z


def dilated_window_attention(q, k, v, window, dilation, slopes):
    B, S, H, Dh = q.shape
    R = window // (2 * dilation)
    Lsub = S // dilation
    nblk = -(-Lsub // R)
    Lp = nblk * R

    def to_sub(a):
        return a.reshape(B, Lsub, dilation, H, Dh).transpose(0, 2, 3, 1, 4)

    qs, ks, vs = to_sub(q), to_sub(k), to_sub(v)
    qs = jnp.pad(qs, ((0, 0), (0, 0), (0, 0), (0, Lp - Lsub), (0, 0)))
    pad_kv = ((0, 0), (0, 0), (0, 0), (R, Lp - Lsub + R), (0, 0))
    ks, vs = jnp.pad(ks, pad_kv), jnp.pad(vs, pad_kv)
    qb = qs.reshape(B, dilation, H, nblk, R, Dh)

    def band(a):
        ab = a.reshape(B, dilation, H, nblk + 2, R, Dh)
        return jnp.concatenate([ab[:, :, :, :nblk], ab[:, :, :, 1:nblk + 1], ab[:, :, :, 2:]], axis=4)

    kb, vb = band(ks), band(vs)
    scores = jnp.einsum('bdhnqc,bdhnkc->bdhnqk', qb, kb) / math.sqrt(Dh)
    qi = jnp.arange(nblk)[:, None, None] * R + jnp.arange(R)[None, :, None]
    kj = jnp.arange(nblk)[:, None, None] * R + jnp.arange(3 * R)[None, None, :] - R
    rel = kj - qi
    valid = (jnp.abs(rel) <= R) & (kj >= 0) & (kj < Lsub)
    alibi = -slopes[:, None, None, None] * (dilation * jnp.abs(rel)).astype(jnp.float32)[None]
    scores = jnp.where(valid, scores + alibi[None, None], NEG_INF)
    m = jnp.max(scores, axis=-1, keepdims=True)
    p = jnp.exp(scores - m)
    den = jnp.sum(p, axis=-1, keepdims=True)
    out = jnp.einsum('bdhnqk,bdhnkc->bdhnqc', p, vb) / den
    lse = (m + jnp.log(den))[..., 0]
    out = out.reshape(B, dilation, H, Lp, Dh)[:, :, :, :Lsub].transpose(0, 3, 1, 2, 4).reshape(B, S, H, Dh)
    lse = lse.reshape(B, dilation, H, Lp)[..., :Lsub].transpose(0, 3, 1, 2).reshape(B, S, H)
    return out, lse


def dilated_attention_mixer(a3):
    B, S, _ = a3.shape
    a3 = a3.astype(jnp.float32)
    q, k, v = [t.reshape(B, S, N_HEADS, HEAD_DIM) for t in jnp.split(a3, 3, axis=-1)]
    slopes = alibi_slopes()
    outs, lses = [], []
    for window, dilation in DILATED_PATTERNS:
        o, l = dilated_window_attention(q, k, v, window, dilation, slopes)
        outs.append(o)
        lses.append(l)
    wts = jax.nn.softmax(jnp.stack(lses, axis=0), axis=0)
    out = jnp.sum(wts[..., None] * jnp.stack(outs, axis=0), axis=0)
    return out.reshape(B, S, D_ATTN)


def setup_inputs(seed: int = 0) -> dict:
    key = jax.random.key(seed)
    ks = jax.random.split(key, 32)
    f32 = jnp.float32
    nrm = lambda k, shape, s: jax.random.normal(k, shape, f32) * s
    gain = lambda k, n: 1.0 + 0.02 * jax.random.normal(k, (n,), f32)
    D, F = D_MODEL, D_FF
    return {
        "x": jax.random.normal(ks[0], (BATCH, SEQ, D), f32),
        "ffn1_norm_g": gain(ks[1], D),
        "ffn1_w_gate": nrm(ks[2], (D, F), D ** -0.5),
        "ffn1_w_up": nrm(ks[3], (D, F), D ** -0.5),
        "ffn1_w_down": nrm(ks[4], (F, D), F ** -0.5),
        "mix_norm_g": gain(ks[5], D),
        "w_in": nrm(ks[6], (D, 3 * D_HYENA + 3 * D_ATTN), D ** -0.5),
        "hy_conv_w": nrm(ks[7], (SHORT_CONV, 3 * D_HYENA), SHORT_CONV ** -0.5),
        "hy_conv_b": nrm(ks[8], (3 * D_HYENA,), 0.02),
        "hy_filt_w1": nrm(ks[9], (FILTER_EMB, FILTER_WIDTH), FILTER_EMB ** -0.5),
        "hy_filt_b1": nrm(ks[10], (FILTER_WIDTH,), 0.1),
        "hy_filt_w2": nrm(ks[11], (FILTER_WIDTH, FILTER_WIDTH), FILTER_WIDTH ** -0.5),
        "hy_filt_b2": nrm(ks[12], (FILTER_WIDTH,), 0.1),
        "hy_filt_w3": nrm(ks[13], (FILTER_WIDTH, FILTER_WIDTH), FILTER_WIDTH ** -0.5),
        "hy_filt_b3": nrm(ks[14], (FILTER_WIDTH,), 0.1),
        "hy_filt_w_out": nrm(ks[15], (FILTER_WIDTH, HYENA_ORDER * N_DIRS * D_HYENA), 0.1 * FILTER_WIDTH ** -0.5),
        "hy_filt_freq": 1.0 + 0.1 * jax.random.normal(ks[16], (FILTER_WIDTH,), f32),
        "hy_filt_skip": nrm(ks[17], (HYENA_ORDER, D_HYENA), 0.5),
        "hy_out_norm_g": gain(ks[18], D_HYENA),
        "attn_out_norm_g": gain(ks[19], D_ATTN),
        "w_out": nrm(ks[20], (D_MIX, D), D_MIX ** -0.5),
        "ffn2_norm_g": gain(ks[21], D),
        "ffn2_w_gate": nrm(ks[22], (D, F), D ** -0.5),
        "ffn2_w_up": nrm(ks[23], (D, F), D ** -0.5),
        "ffn2_w_down": nrm(ks[24], (F, D), F ** -0.5),
        "final_norm_g": gain(ks[25], D),
    }


def reference(x, ffn1_norm_g, ffn1_w_gate, ffn1_w_up, ffn1_w_down, mix_norm_g, w_in,
              hy_conv_w, hy_conv_b, hy_filt_w1, hy_filt_b1, hy_filt_w2, hy_filt_b2,
              hy_filt_w3, hy_filt_b3, hy_filt_w_out, hy_filt_freq, hy_filt_skip,
              hy_out_norm_g, attn_out_norm_g, w_out, ffn2_norm_g, ffn2_w_gate, ffn2_w_up,
              ffn2_w_down, final_norm_g):
    for _ in range(DEPTH):
        x = x + 0.5 * swiglu(rmsnorm(x, ffn1_norm_g), ffn1_w_gate, ffn1_w_up, ffn1_w_down)
        h = rmsnorm(x, mix_norm_g)
        proj = h @ w_in
        y_hy = hyena_mixer(proj[..., :3 * D_HYENA], hy_conv_w, hy_conv_b,
                           hy_filt_w1, hy_filt_b1, hy_filt_w2, hy_filt_b2,
                           hy_filt_w3, hy_filt_b3, hy_filt_w_out, hy_filt_freq, hy_filt_skip)
        y_at = dilated_attention_mixer(proj[..., 3 * D_HYENA:])
        y = jnp.concatenate([rmsnorm(y_hy, hy_out_norm_g), rmsnorm(y_at, attn_out_norm_g)], axis=-1)
        x = x + (y.astype(x.dtype) @ w_out)
        x = x + 0.5 * swiglu(rmsnorm(x, ffn2_norm_g), ffn2_w_gate, ffn2_w_up, ffn2_w_down)
    return rmsnorm(x, final_norm_g)
```

```python
import functools
import math

import numpy as np
import jax
import jax.numpy as jnp
from jax import lax
from jax.experimental import pallas as pl
from jax.experimental.pallas import tpu as pltpu

F32 = jnp.float32
BF16 = jnp.bfloat16

D_MODEL = 1024
D_HYENA = 512
D_ATTN = 512
HEAD_DIM = 64
N_HEADS = D_ATTN // HEAD_DIM
D_FF = 2816
FILTER_EMB = 33
FILTER_BANDS = 16
FILTER_WIDTH = 64
DECAY_TARGET = 1e-2
FAST_DECAY_PCT = 0.3
SLOW_DECAY_PCT = 1.5
DILATIONS = (1, 4, 16)
HALF_WINDOW = 64
RMS_EPS = 1e-6
NEG_INF = -1e30

V7X_LANES = 128
V7X_VMEM_LIMIT_BYTES = 56 * 1024 * 1024

ROW_TILE = 512
FF_CHUNK = 256
FREQ_BLOCK = 512
Q_BLOCK = 128
K_WINDOW = 256
CONV_ROWS = 256
FEAT_PAD = 128


def _dot(a, b):
    return jnp.dot(a, b, preferred_element_type=F32)


def _rms(x, g):
    return x * lax.rsqrt(jnp.mean(x * x, axis=-1, keepdims=True) + RMS_EPS) * g


def _swiglu(h, wg_ref, wu_ref, wd_ref, act_ref):
    for c in range(D_FF // FF_CHUNK):
        cols = slice(c * FF_CHUNK, (c + 1) * FF_CHUNK)
        g = _dot(h, wg_ref[:, cols])
        u = _dot(h, wu_ref[:, cols])
        act_ref[:, cols] = (g * jax.nn.sigmoid(g) * u).astype(BF16)
    return _dot(act_ref[...], wd_ref[...])


def _resident(shape):
    return pl.BlockSpec(shape, lambda *_: (0,) * len(shape), pipeline_mode=pl.Buffered(1))


def _ffn1_proj_kernel(x_ref, g1_ref, wg_ref, wu_ref, wd_ref, g2_ref, win_ref,
                      x1_ref, hy_ref, q_ref, k_ref, v_ref, act_ref):
    x = x_ref[...]
    h = _rms(x, g1_ref[...]).astype(BF16)
    x1 = x + 0.5 * _swiglu(h, wg_ref, wu_ref, wd_ref, act_ref)
    x1_ref[...] = x1
    h2 = _rms(x1, g2_ref[...]).astype(BF16)
    nh = 3 * D_HYENA
    hy_ref[...] = _dot(h2, win_ref[:, :nh])
    q_ref[...] = _dot(h2, win_ref[:, nh:nh + D_ATTN])
    k_ref[...] = _dot(h2, win_ref[:, nh + D_ATTN:nh + 2 * D_ATTN])
    v_ref[...] = _dot(h2, win_ref[:, nh + 2 * D_ATTN:])


def _ffn1_proj(x2d, g1, wg, wu, wd, g2, win):
    n = x2d.shape[0]
    row = lambda w: pl.BlockSpec((ROW_TILE, w), lambda i: (i, 0))
    return pl.pallas_call(
        _ffn1_proj_kernel,
        grid=(n // ROW_TILE,),
        in_specs=[row(D_MODEL), _resident((1, D_MODEL)), _resident(wg.shape), _resident(wu.shape),
                  _resident(wd.shape), _resident((1, D_MODEL)), _resident(win.shape)],
        out_specs=[row(D_MODEL), row(3 * D_HYENA), row(D_ATTN), row(D_ATTN), row(D_ATTN)],
        out_shape=[jax.ShapeDtypeStruct((n, D_MODEL), F32),
                   jax.ShapeDtypeStruct((n, 3 * D_HYENA), F32),
                   jax.ShapeDtypeStruct((n, D_ATTN), F32),
                   jax.ShapeDtypeStruct((n, D_ATTN), F32),
                   jax.ShapeDtypeStruct((n, D_ATTN), F32)],
        scratch_shapes=[pltpu.VMEM((ROW_TILE, D_FF), BF16)],
        compiler_params=pltpu.CompilerParams(dimension_semantics=("parallel",),
                                             vmem_limit_bytes=V7X_VMEM_LIMIT_BYTES),
        name="ffn1_proj",
    )(x2d, g1, wg, wu, wd, g2, win)


def _filter_kernel(seq, feat_ref, w1_ref, b1_ref, w2_ref, b2_ref, w3_ref, b3_ref, wout_ref,
                   freq_ref, delta_ref, f_ref, h_ref, hcat_ref):
    c_ = D_HYENA

    @pl.when(pl.program_id(0) == 0)
    def _():
        hi = lax.Precision.HIGHEST
        dot_hi = lambda a, b: jnp.dot(a, b, precision=hi, preferred_element_type=F32)
        freq = freq_ref[...]
        h = jnp.sin(freq * (dot_hi(feat_ref[...], w1_ref[...]) + b1_ref[...]))
        h = jnp.sin(freq * (dot_hi(h, w2_ref[...]) + b2_ref[...]))
        h = jnp.sin(freq * (dot_hi(h, w3_ref[...]) + b3_ref[...]))
        t = lax.broadcasted_iota(jnp.int32, (seq, 1), 0).astype(F32) * (1.0 / (seq - 1))
        decay = jnp.exp(-t * jnp.abs(delta_ref[...]))
        row = lax.broadcasted_iota(jnp.int32, (seq, c_), 0)
        for c in range(4):
            kc = dot_hi(h, wout_ref[:, c * c_:(c + 1) * c_]) * decay
            if c % 2 == 1:
                kc = jnp.where(row == 0, 0.0, kc)
            hcat_ref[:, c * c_:(c + 1) * c_] = kc.astype(BF16)

    p = _dot(f_ref[...], hcat_ref[...])
    hb = FREQ_BLOCK // 2
    scale = 1.0 / seq
    for o in range(2):
        pf = p[:, (2 * o) * c_:(2 * o + 1) * c_]
        pb = p[:, (2 * o + 1) * c_:(2 * o + 2) * c_]
        h_ref[o, :hb, :] = (pf[:hb] + pb[:hb]) * scale
        h_ref[o, hb:, :] = (pf[hb:] - pb[hb:]) * scale


def _hy_filter(seq, feat, w1, b1, w2, b2, w3, b3, wout, freq, delta, fmat):
    full = lambda a: _resident(a.shape)
    return pl.pallas_call(
        functools.partial(_filter_kernel, seq),
        grid=(2 * seq // FREQ_BLOCK,),
        in_specs=[full(feat), full(w1), full(b1), full(w2), full(b2), full(w3), full(b3), full(wout),
                  full(freq), full(delta), pl.BlockSpec((FREQ_BLOCK, seq), lambda j: (j, 0))],
        out_specs=pl.BlockSpec((2, FREQ_BLOCK, D_HYENA), lambda j: (0, j, 0)),
        out_shape=jax.ShapeDtypeStruct((2, 2 * seq, D_HYENA), F32),
        scratch_shapes=[pltpu.VMEM((seq, 4 * D_HYENA), BF16)],
        compiler_params=pltpu.CompilerParams(dimension_semantics=("arbitrary",),
                                             vmem_limit_bytes=V7X_VMEM_LIMIT_BYTES),
        name="hy_filter",
    )(feat, w1, b1, w2, b2, w3, b3, wout, freq, delta, fmat)


def _short_conv_rows(u_ref, w_ref, b_ref, t0, seq):
    cur = u_ref[0, t0:t0 + CONV_ROWS, :]
    row = lax.broadcasted_iota(jnp.int32, cur.shape, 0)
    zero = jnp.zeros((1, cur.shape[1]), F32)
    before = u_ref[0, t0 - 1:t0, :] if t0 > 0 else zero
    after = u_ref[0, t0 + CONV_ROWS:t0 + CONV_ROWS + 1, :] if t0 + CONV_ROWS < seq else zero
    prev = jnp.where(row == 0, before, pltpu.roll(cur, 1, axis=0))
    nxt = jnp.where(row == CONV_ROWS - 1, after, pltpu.roll(cur, CONV_ROWS - 1, axis=0))
    return b_ref[0] + prev * w_ref[0, 0:1, :] + cur * w_ref[0, 1:2, :] + nxt * w_ref[0, 2:3, :]


def _hyena_kernel(seq, v_ref, gate_ref, wv_ref, bv_ref, wg_ref, bg_ref, skip_ref, gain_ref,
                  f_ref, ft_ref, h_ref, out_ref, zf_ref, zb_ref, acc_ref):
    o = pl.program_id(1)
    j = pl.program_id(2)
    last = pl.num_programs(2) - 1
    chunks = range(0, seq, CONV_ROWS)

    @pl.when((o == 0) & (j == 0))
    def _():
        for t0 in chunks:
            z = _short_conv_rows(v_ref, wv_ref, bv_ref, t0, seq)
            zf_ref[t0:t0 + CONV_ROWS, :] = z
            zb_ref[t0:t0 + CONV_ROWS, :] = z.astype(BF16)

    hb = FREQ_BLOCK // 2
    zfreq = _dot(f_ref[...], zb_ref[...])
    zr, zi = zfreq[:hb], zfreq[hb:]
    hr, hi = h_ref[0, :hb, :], h_ref[0, hb:, :]
    yr = (zr * hr - zi * hi).astype(BF16)
    yi = (zr * hi + zi * hr).astype(BF16)
    contrib = _dot(ft_ref[:, :hb], yr) + _dot(ft_ref[:, hb:], yi)

    @pl.when(j == 0)
    def _():
        acc_ref[...] = contrib

    @pl.when(j > 0)
    def _():
        acc_ref[...] += contrib

    def gated(t0):
        rows = slice(t0, t0 + CONV_ROWS)
        gate = _short_conv_rows(gate_ref, wg_ref, bg_ref, t0, seq)
        return gate * (acc_ref[rows, :] + zf_ref[rows, :] * skip_ref[0])

    @pl.when((j == last) & (o == 0))
    def _():
        for t0 in chunks:
            z = gated(t0)
            zf_ref[t0:t0 + CONV_ROWS, :] = z
            zb_ref[t0:t0 + CONV_ROWS, :] = z.astype(BF16)

    @pl.when((j == last) & (o == 1))
    def _():
        for t0 in chunks:
            out_ref[0, t0:t0 + CONV_ROWS, :] = _rms(gated(t0), gain_ref[...]).astype(BF16)


def _hyena(hy, conv_w, conv_b, skip, gain, fmat, fmat_t, hspec):
    b, seq, _ = hy.shape
    c = D_HYENA
    nfb = 2 * seq // FREQ_BLOCK
    part = lambda sel: pl.BlockSpec((1, seq, c), lambda bi, o, j: (bi, 0, sel(o)))
    wpart = lambda sel: pl.BlockSpec((1, 3, c), lambda bi, o, j: (sel(o), 0, 0))
    bpart = lambda sel: pl.BlockSpec((1, 1, c), lambda bi, o, j: (sel(o), 0, 0))
    value = lambda o: 0
    gate = lambda o: 1 + o
    return pl.pallas_call(
        functools.partial(_hyena_kernel, seq),
        grid=(b, 2, nfb),
        in_specs=[part(value), part(gate), wpart(value), bpart(value), wpart(gate), bpart(gate),
                  pl.BlockSpec((1, 1, c), lambda bi, o, j: (o, 0, 0)),
                  pl.BlockSpec((1, c), lambda bi, o, j: (0, 0)),
                  pl.BlockSpec((FREQ_BLOCK, seq), lambda bi, o, j: (j, 0)),
                  pl.BlockSpec((seq, FREQ_BLOCK), lambda bi, o, j: (0, j)),
                  pl.BlockSpec((1, FREQ_BLOCK, c), lambda bi, o, j: (o, j, 0))],
        out_specs=pl.BlockSpec((1, seq, c), lambda bi, o, j: (bi, 0, 0)),
        out_shape=jax.ShapeDtypeStruct((b, seq, c), BF16),
        scratch_shapes=[pltpu.VMEM((seq, c), F32), pltpu.VMEM((seq, c), BF16), pltpu.VMEM((seq, c), F32)],
        compiler_params=pltpu.CompilerParams(dimension_semantics=("parallel", "arbitrary", "arbitrary"),
                                             vmem_limit_bytes=V7X_VMEM_LIMIT_BYTES),
        name="hyena",
    )(hy, hy, conv_w, conv_b, conv_w, conv_b, skip, gain, fmat, fmat_t, hspec)


def _attn_kernel(seq, q_ref, k_ref, v_ref, bias_ref, bias16_ref, o_ref,
                 qs_ref, ks_ref, vs_ref, res_ref, nat_ref):
    lanes = V7X_LANES
    qscale = 1.0 / math.sqrt(HEAD_DIM)

    for p, d in enumerate(DILATIONS):
        ls = seq // d
        first = lax.broadcasted_iota(jnp.int32, (ls, lanes), 1) < HEAD_DIM
        for r in range(d):
            src = pl.ds(r, ls, stride=d) if d > 1 else pl.ds(0, ls)
            rows = slice(r * ls, (r + 1) * ls)
            qq = q_ref[0, src, :] * qscale
            qs_ref[p, 0, rows, :] = jnp.where(first, qq, 0.0).astype(BF16)
            qs_ref[p, 1, rows, :] = jnp.where(first, 0.0, qq).astype(BF16)
            ks_ref[p, rows, :] = k_ref[0, src, :].astype(BF16)
            vs_ref[p, rows, :] = v_ref[0, src, :].astype(BF16)

    first = lax.broadcasted_iota(jnp.int32, (Q_BLOCK, lanes), 1) < HEAD_DIM

    def block(p, row0, krow0, nkeys, bias_of_head):
        kw = ks_ref[p, pl.ds(krow0, nkeys), :]
        vw = vs_ref[p, pl.ds(krow0, nkeys), :]
        stats = []
        for h in range(2):
            qh = qs_ref[p, h, pl.ds(row0, Q_BLOCK), :]
            s = lax.dot_general(qh, kw, (((1,), (1,)), ((), ())), preferred_element_type=F32)
            s = s + bias_of_head(h)
            m = jnp.max(s, axis=-1, keepdims=True)
            e = jnp.exp(s - m)
            l = jnp.sum(e, axis=-1, keepdims=True)
            stats.append((_dot(e.astype(BF16), vw), m, l))
        for kind in range(3):
            res_ref[p, kind, pl.ds(row0, Q_BLOCK), :] = jnp.where(first, stats[0][kind], stats[1][kind])

    for p, d in enumerate(DILATIONS[:2]):
        ls = seq // d
        nblk = ls // Q_BLOCK

        def banded(n, carry, p=p, ls=ls, nblk=nblk):
            r = n // nblk
            ib = n % nblk
            i0 = ib * Q_BLOCK
            k0 = jnp.clip(i0 - HALF_WINDOW, 0, ls - K_WINDOW)
            case = jnp.where(ib == 0, 0, jnp.where(ib == nblk - 1, 2, 1))
            row0 = pl.multiple_of(r * ls + i0, Q_BLOCK)
            krow0 = pl.multiple_of(r * ls + k0, HALF_WINDOW)
            block(p, row0, krow0, K_WINDOW, lambda h: bias_ref[0, (p * 3 + case) * 2 + h])
            return carry

        lax.fori_loop(0, seq // Q_BLOCK, banded, 0)

    def full(n, carry):
        row0 = pl.multiple_of(n * Q_BLOCK, Q_BLOCK)
        block(2, row0, row0, Q_BLOCK, lambda h: bias16_ref[0, h])
        return carry

    lax.fori_loop(0, seq // Q_BLOCK, full, 0)

    for p, d in enumerate(DILATIONS[1:], start=1):
        ls = seq // d
        for r in range(d):
            for kind in range(3):
                nat_ref[p - 1, kind, pl.ds(r, ls, stride=d), :] = res_ref[p, kind, r * ls:(r + 1) * ls, :]

    for t0 in range(0, seq, CONV_ROWS):
        rows = slice(t0, t0 + CONV_ROWS)
        parts = [tuple(res_ref[0, kind, rows, :] for kind in range(3))]
        parts += [tuple(nat_ref[p, kind, rows, :] for kind in range(3)) for p in range(2)]
        m = functools.reduce(jnp.maximum, [pt[1] for pt in parts])
        num = 0.0
        den = 0.0
        for out, mp, lp in parts:
            w = jnp.exp(mp - m)
            num = num + w * out
            den = den + w * lp
        o_ref[0, rows, :] = num / den


def _attn_bias_tables():
    slopes = np.array([2.0 ** (-8.0 * (i + 1) / N_HEADS) for i in range(N_HEADS)], np.float32)
    slopes = jnp.asarray(slopes.reshape(N_HEADS // 2, 1, 2, 1, 1))
    qi = lax.broadcasted_iota(jnp.int32, (Q_BLOCK, K_WINDOW), 0)
    kj = lax.broadcasted_iota(jnp.int32, (Q_BLOCK, K_WINDOW), 1)
    offsets = (0, -HALF_WINDOW, -2 * HALF_WINDOW)
    dist = jnp.stack([jnp.abs(kj - qi + off) for off in offsets])
    valid = dist <= HALF_WINDOW
    dil = jnp.asarray(np.array(DILATIONS[:2], np.float32).reshape(2, 1, 1, 1))
    scaled = (dil * dist.astype(F32)[None])[None, :, :, None]
    banded = jnp.where(valid[None, None, :, None], -slopes[:, None] * scaled, NEG_INF)
    banded = banded.reshape(N_HEADS // 2, 12, Q_BLOCK, K_WINDOW)
    d16 = dist[0, :, :Q_BLOCK]
    full = jnp.where(d16 <= HALF_WINDOW, -slopes[:, 0] * (DILATIONS[2] * d16.astype(F32)), NEG_INF)
    return banded.astype(F32), full.astype(F32)


def _dil_attn(q, k, v):
    b, seq, _ = q.shape
    nhp = N_HEADS // 2
    bias, bias16 = _attn_bias_tables()
    head_pair = pl.BlockSpec((1, seq, V7X_LANES), lambda bi, hp: (bi, 0, hp))
    return pl.pallas_call(
        functools.partial(_attn_kernel, seq),
        grid=(b, nhp),
        in_specs=[head_pair, head_pair, head_pair,
                  pl.BlockSpec((1, 12, Q_BLOCK, K_WINDOW), lambda bi, hp: (hp, 0, 0, 0)),
                  pl.BlockSpec((1, 2, Q_BLOCK, Q_BLOCK), lambda bi, hp: (hp, 0, 0, 0))],
        out_specs=head_pair,
        out_shape=jax.ShapeDtypeStruct((b, seq, D_ATTN), F32),
        scratch_shapes=[pltpu.VMEM((3, 2, seq, V7X_LANES), BF16),
                        pltpu.VMEM((3, seq, V7X_LANES), BF16),
                        pltpu.VMEM((3, seq, V7X_LANES), BF16),
                        pltpu.VMEM((3, 3, seq, V7X_LANES), F32),
                        pltpu.VMEM((2, 3, seq, V7X_LANES), F32)],
        compiler_params=pltpu.CompilerParams(dimension_semantics=("parallel", "parallel"),
                                             vmem_limit_bytes=V7X_VMEM_LIMIT_BYTES),
        name="dil_attn",
    )(q, k, v, bias, bias16)


def _out_ffn2_kernel(x1_ref, yh_ref, ya_ref, ga_ref, wo_ref, g3_ref, wg_ref, wu_ref, wd_ref, gf_ref,
                     out_ref, act_ref):
    ya = _rms(ya_ref[...], ga_ref[...]).astype(BF16)
    mix = _dot(yh_ref[...], wo_ref[:D_HYENA, :]) + _dot(ya, wo_ref[D_HYENA:, :])
    x2 = x1_ref[...] + mix
    h = _rms(x2, g3_ref[...]).astype(BF16)
    x3 = x2 + 0.5 * _swiglu(h, wg_ref, wu_ref, wd_ref, act_ref)
    out_ref[...] = _rms(x3, gf_ref[...])


def _out_ffn2(x1, yh, ya, ga, wo, g3, wg, wu, wd, gf):
    n = x1.shape[0]
    row = lambda w: pl.BlockSpec((ROW_TILE, w), lambda i: (i, 0))
    return pl.pallas_call(
        _out_ffn2_kernel,
        grid=(n // ROW_TILE,),
        in_specs=[row(D_MODEL), row(D_HYENA), row(D_ATTN), _resident((1, D_ATTN)), _resident(wo.shape),
                  _resident((1, D_MODEL)), _resident(wg.shape), _resident(wu.shape), _resident(wd.shape),
                  _resident((1, D_MODEL))],
        out_specs=row(D_MODEL),
        out_shape=jax.ShapeDtypeStruct((n, D_MODEL), F32),
        scratch_shapes=[pltpu.VMEM((ROW_TILE, D_FF), BF16)],
        compiler_params=pltpu.CompilerParams(dimension_semantics=("parallel",),
                                             vmem_limit_bytes=V7X_VMEM_LIMIT_BYTES),
        name="out_ffn2",
    )(x1, yh, ya, ga, wo, g3, wg, wu, wd, gf)


def _dft_matrices(seq):
    n = 2 * seq
    split = 64
    k = np.arange(seq, dtype=np.int64)[:, None]
    ang = lambda s: np.pi * (((2 * k + 1) * s) % (2 * n)) / n
    a1 = ang(split * np.arange(seq // split, dtype=np.int64)[None, :])
    a0 = ang(np.arange(split, dtype=np.int64)[None, :])
    tab = lambda x: jnp.asarray(x.astype(np.float32))
    c1, s1 = tab(np.cos(a1))[:, :, None], tab(np.sin(a1))[:, :, None]
    c0, s0 = tab(np.cos(a0))[:, None, :], tab(np.sin(a0))[:, None, :]
    cos = (c1 * c0 - s1 * s0).reshape(seq, seq)
    nsin = -(s1 * c0 + c1 * s0).reshape(seq, seq)
    hb = FREQ_BLOCK // 2
    fmat = jnp.concatenate([cos.reshape(seq // hb, hb, seq), nsin.reshape(seq // hb, hb, seq)], axis=1)
    fmat = fmat.reshape(n, seq)
    return fmat.astype(BF16), fmat.T.astype(BF16)


def _filter_features(seq):
    t = jnp.linspace(0.0, 1.0, seq, dtype=F32)[:, None]
    w = 2.0 * math.pi * jnp.arange(seq, dtype=F32)[:, None] / seq
    f = jnp.linspace(1e-4, FILTER_BANDS - 1, FILTER_BANDS, dtype=F32)[None, :]
    z = jnp.concatenate([t, jnp.cos(f * w), -jnp.sin(f * w)], axis=-1)
    return jnp.pad(z, ((0, 0), (0, FEAT_PAD - FILTER_EMB)))


def _decay_rates():
    max_decay = math.log(DECAY_TARGET) / FAST_DECAY_PCT
    min_decay = math.log(DECAY_TARGET) / SLOW_DECAY_PCT
    return jnp.linspace(min_decay, max_decay, D_HYENA, dtype=F32)[None, :]


def kernel(x, ffn1_norm_g, ffn1_w_gate, ffn1_w_up, ffn1_w_down, mix_norm_g, w_in, hy_conv_w, hy_conv_b, hy_filt_w1, hy_filt_b1, hy_filt_w2, hy_filt_b2, hy_filt_w3, hy_filt_b3, hy_filt_w_out, hy_filt_freq, hy_filt_skip, hy_out_norm_g, attn_out_norm_g, w_out, ffn2_norm_g, ffn2_w_gate, ffn2_w_up, ffn2_w_down, final_norm_g):
    b, seq, d = x.shape
    assert d == D_MODEL and (b * seq) % ROW_TILE == 0 and seq % (DILATIONS[-1] * Q_BLOCK) == 0
    row = lambda a: a.reshape(1, -1).astype(F32)
    bf = lambda a: a.astype(BF16)

    x1, hy, q, k, v = _ffn1_proj(x.reshape(b * seq, d), row(ffn1_norm_g), bf(ffn1_w_gate), bf(ffn1_w_up),
                                 bf(ffn1_w_down), row(mix_norm_g), bf(w_in))

    fmat, fmat_t = _dft_matrices(seq)
    w1 = jnp.pad(hy_filt_w1.astype(F32), ((0, FEAT_PAD - FILTER_EMB), (0, 0)))
    hspec = _hy_filter(seq, _filter_features(seq), w1, row(hy_filt_b1), hy_filt_w2.astype(F32),
                       row(hy_filt_b2), hy_filt_w3.astype(F32), row(hy_filt_b3),
                       hy_filt_w_out.astype(F32), row(hy_filt_freq), _decay_rates(), fmat)

    conv_w = hy_conv_w.astype(F32).reshape(3, 3, D_HYENA).transpose(1, 0, 2)
    conv_b = hy_conv_b.astype(F32).reshape(3, 1, D_HYENA)
    skip = hy_filt_skip.astype(F32).reshape(2, 1, D_HYENA)
    y_hy = _hyena(hy.reshape(b, seq, 3 * D_HYENA), conv_w, conv_b, skip, row(hy_out_norm_g),
                  fmat, fmat_t, hspec)

    shape3 = lambda a: a.reshape(b, seq, D_ATTN)
    y_at = _dil_attn(shape3(q), shape3(k), shape3(v))

    out = _out_ffn2(x1, y_hy.reshape(b * seq, D_HYENA), y_at.reshape(b * seq, D_ATTN),
                    row(attn_out_norm_g), bf(w_out), row(ffn2_norm_g), bf(ffn2_w_gate), bf(ffn2_w_up),
                    bf(ffn2_w_down), row(final_norm_g))
    return out.reshape(b, seq, d)
```

```python
import functools
import math

import numpy as np
import jax
import jax.numpy as jnp
from jax import lax
from jax.experimental import pallas as pl
from jax.experimental.pallas import tpu as pltpu

F32 = jnp.float32
BF16 = jnp.bfloat16

D_MODEL = 1024
D_HYENA = 512
D_ATTN = 512
HEAD_DIM = 64
N_HEADS = D_ATTN // HEAD_DIM
D_FF = 2816
FILTER_EMB = 33
FILTER_BANDS = 16
FILTER_WIDTH = 64
DECAY_TARGET = 1e-2
FAST_DECAY_PCT = 0.3
SLOW_DECAY_PCT = 1.5
DILATIONS = (1, 4, 16)
HALF_WINDOW = 64
RMS_EPS = 1e-6
NEG_INF = -1e30

V7X_LANES = 128
V7X_VMEM_LIMIT_BYTES = 56 * 1024 * 1024

ROW_TILE = 512
FF_CHUNK = 256
FREQ_BLOCK = 512
Q_BLOCK = 128
K_WINDOW = 256
ATTN_GROUP = 4
CONV_ROWS = 256
FEAT_PAD = 128


def _dot(a, b):
    return jnp.dot(a, b, preferred_element_type=F32)


def _rms(x, g):
    return x * lax.rsqrt(jnp.mean(x * x, axis=-1, keepdims=True) + RMS_EPS) * g


def _swiglu(h, wg_ref, wu_ref, wd_ref, act_ref):
    for c in range(D_FF // FF_CHUNK):
        cols = slice(c * FF_CHUNK, (c + 1) * FF_CHUNK)
        g = _dot(h, wg_ref[:, cols])
        u = _dot(h, wu_ref[:, cols])
        act_ref[:, cols] = (g * jax.nn.sigmoid(g) * u).astype(BF16)
    return _dot(act_ref[...], wd_ref[...])


def _resident(shape):
    return pl.BlockSpec(shape, lambda *_: (0,) * len(shape), pipeline_mode=pl.Buffered(1))


def _ffn1_proj_kernel(x_ref, g1_ref, wg_ref, wu_ref, wd_ref, g2_ref, win_ref,
                      x1_ref, hy_ref, q_ref, k_ref, v_ref, act_ref):
    x = x_ref[...]
    h = _rms(x, g1_ref[...]).astype(BF16)
    x1 = x + 0.5 * _swiglu(h, wg_ref, wu_ref, wd_ref, act_ref)
    x1_ref[...] = x1
    h2 = _rms(x1, g2_ref[...]).astype(BF16)
    nh = 3 * D_HYENA
    hy_ref[...] = _dot(h2, win_ref[:, :nh])
    q_ref[...] = _dot(h2, win_ref[:, nh:nh + D_ATTN])
    k_ref[...] = _dot(h2, win_ref[:, nh + D_ATTN:nh + 2 * D_ATTN])
    v_ref[...] = _dot(h2, win_ref[:, nh + 2 * D_ATTN:])


def _ffn1_proj(x2d, g1, wg, wu, wd, g2, win):
    n = x2d.shape[0]
    row = lambda w: pl.BlockSpec((ROW_TILE, w), lambda i: (i, 0))
    return pl.pallas_call(
        _ffn1_proj_kernel,
        grid=(n // ROW_TILE,),
        in_specs=[row(D_MODEL), _resident((1, D_MODEL)), _resident(wg.shape), _resident(wu.shape),
                  _resident(wd.shape), _resident((1, D_MODEL)), _resident(win.shape)],
        out_specs=[row(D_MODEL), row(3 * D_HYENA), row(D_ATTN), row(D_ATTN), row(D_ATTN)],
        out_shape=[jax.ShapeDtypeStruct((n, D_MODEL), F32),
                   jax.ShapeDtypeStruct((n, 3 * D_HYENA), F32),
                   jax.ShapeDtypeStruct((n, D_ATTN), F32),
                   jax.ShapeDtypeStruct((n, D_ATTN), F32),
                   jax.ShapeDtypeStruct((n, D_ATTN), F32)],
        scratch_shapes=[pltpu.VMEM((ROW_TILE, D_FF), BF16)],
        compiler_params=pltpu.CompilerParams(dimension_semantics=("parallel",),
                                             vmem_limit_bytes=V7X_VMEM_LIMIT_BYTES),
        name="ffn1_proj",
    )(x2d, g1, wg, wu, wd, g2, win)


def _filter_kernel(seq, feat_ref, w1_ref, b1_ref, w2_ref, b2_ref, w3_ref, b3_ref, wout_ref,
                   freq_ref, delta_ref, f_ref, h_ref, hcat_ref):
    c_ = D_HYENA

    @pl.when(pl.program_id(0) == 0)
    def _():
        hi = lax.Precision.HIGHEST
        dot_hi = lambda a, b: jnp.dot(a, b, precision=hi, preferred_element_type=F32)
        freq = freq_ref[...]
        h = jnp.sin(freq * (dot_hi(feat_ref[...], w1_ref[...]) + b1_ref[...]))
        h = jnp.sin(freq * (dot_hi(h, w2_ref[...]) + b2_ref[...]))
        h = jnp.sin(freq * (dot_hi(h, w3_ref[...]) + b3_ref[...]))
        t = lax.broadcasted_iota(jnp.int32, (seq, 1), 0).astype(F32) * (1.0 / (seq - 1))
        decay = jnp.exp(-t * jnp.abs(delta_ref[...]))
        row = lax.broadcasted_iota(jnp.int32, (seq, c_), 0)
        for c in range(4):
            kc = dot_hi(h, wout_ref[:, c * c_:(c + 1) * c_]) * decay
            if c % 2 == 1:
                kc = jnp.where(row == 0, 0.0, kc)
            hcat_ref[:, c * c_:(c + 1) * c_] = kc.astype(BF16)

    p = _dot(f_ref[...], hcat_ref[...])
    hb = FREQ_BLOCK // 2
    scale = 1.0 / seq
    for o in range(2):
        pf = p[:, (2 * o) * c_:(2 * o + 1) * c_]
        pb = p[:, (2 * o + 1) * c_:(2 * o + 2) * c_]
        h_ref[o, :hb, :] = (pf[:hb] + pb[:hb]) * scale
        h_ref[o, hb:, :] = (pf[hb:] - pb[hb:]) * scale


def _hy_filter(seq, feat, w1, b1, w2, b2, w3, b3, wout, freq, delta, fmat):
    full = lambda a: _resident(a.shape)
    return pl.pallas_call(
        functools.partial(_filter_kernel, seq),
        grid=(2 * seq // FREQ_BLOCK,),
        in_specs=[full(feat), full(w1), full(b1), full(w2), full(b2), full(w3), full(b3), full(wout),
                  full(freq), full(delta), pl.BlockSpec((FREQ_BLOCK, seq), lambda j: (j, 0))],
        out_specs=pl.BlockSpec((2, FREQ_BLOCK, D_HYENA), lambda j: (0, j, 0)),
        out_shape=jax.ShapeDtypeStruct((2, 2 * seq, D_HYENA), F32),
        scratch_shapes=[pltpu.VMEM((seq, 4 * D_HYENA), BF16)],
        compiler_params=pltpu.CompilerParams(dimension_semantics=("arbitrary",),
                                             vmem_limit_bytes=V7X_VMEM_LIMIT_BYTES),
        name="hy_filter",
    )(feat, w1, b1, w2, b2, w3, b3, wout, freq, delta, fmat)


def _short_conv_rows(u_ref, w_ref, b_ref, t0, seq):
    cur = u_ref[0, t0:t0 + CONV_ROWS, :]
    row = lax.broadcasted_iota(jnp.int32, cur.shape, 0)
    zero = jnp.zeros((1, cur.shape[1]), F32)
    before = u_ref[0, t0 - 1:t0, :] if t0 > 0 else zero
    after = u_ref[0, t0 + CONV_ROWS:t0 + CONV_ROWS + 1, :] if t0 + CONV_ROWS < seq else zero
    prev = jnp.where(row == 0, before, pltpu.roll(cur, 1, axis=0))
    nxt = jnp.where(row == CONV_ROWS - 1, after, pltpu.roll(cur, CONV_ROWS - 1, axis=0))
    return b_ref[0] + prev * w_ref[0, 0:1, :] + cur * w_ref[0, 1:2, :] + nxt * w_ref[0, 2:3, :]


def _hyena_kernel(seq, v_ref, gate_ref, wv_ref, bv_ref, wg_ref, bg_ref, skip_ref, gain_ref,
                  f_ref, ft_ref, h_ref, out_ref, zf_ref, zb_ref, acc_ref):
    o = pl.program_id(1)
    j = pl.program_id(2)
    last = pl.num_programs(2) - 1
    chunks = range(0, seq, CONV_ROWS)

    @pl.when((o == 0) & (j == 0))
    def _():
        for t0 in chunks:
            z = _short_conv_rows(v_ref, wv_ref, bv_ref, t0, seq)
            zf_ref[t0:t0 + CONV_ROWS, :] = z
            zb_ref[t0:t0 + CONV_ROWS, :] = z.astype(BF16)

    hb = FREQ_BLOCK // 2
    zfreq = _dot(f_ref[...], zb_ref[...])
    zr, zi = zfreq[:hb], zfreq[hb:]
    hr, hi = h_ref[0, :hb, :], h_ref[0, hb:, :]
    yr = (zr * hr - zi * hi).astype(BF16)
    yi = (zr * hi + zi * hr).astype(BF16)
    contrib = _dot(ft_ref[:, :hb], yr) + _dot(ft_ref[:, hb:], yi)

    @pl.when(j == 0)
    def _():
        acc_ref[...] = contrib

    @pl.when(j > 0)
    def _():
        acc_ref[...] += contrib

    def gated(t0):
        rows = slice(t0, t0 + CONV_ROWS)
        gate = _short_conv_rows(gate_ref, wg_ref, bg_ref, t0, seq)
        return gate * (acc_ref[rows, :] + zf_ref[rows, :] * skip_ref[0])

    @pl.when((j == last) & (o == 0))
    def _():
        for t0 in chunks:
            z = gated(t0)
            zf_ref[t0:t0 + CONV_ROWS, :] = z
            zb_ref[t0:t0 + CONV_ROWS, :] = z.astype(BF16)

    @pl.when((j == last) & (o == 1))
    def _():
        for t0 in chunks:
            out_ref[0, t0:t0 + CONV_ROWS, :] = _rms(gated(t0), gain_ref[...]).astype(BF16)


def _hyena(hy, conv_w, conv_b, skip, gain, fmat, fmat_t, hspec):
    b, seq, _ = hy.shape
    c = D_HYENA
    nfb = 2 * seq // FREQ_BLOCK
    part = lambda sel: pl.BlockSpec((1, seq, c), lambda bi, o, j: (bi, 0, sel(o)))
    wpart = lambda sel: pl.BlockSpec((1, 3, c), lambda bi, o, j: (sel(o), 0, 0))
    bpart = lambda sel: pl.BlockSpec((1, 1, c), lambda bi, o, j: (sel(o), 0, 0))
    value = lambda o: 0
    gate = lambda o: 1 + o
    return pl.pallas_call(
        functools.partial(_hyena_kernel, seq),
        grid=(b, 2, nfb),
        in_specs=[part(value), part(gate), wpart(value), bpart(value), wpart(gate), bpart(gate),
                  pl.BlockSpec((1, 1, c), lambda bi, o, j: (o, 0, 0)),
                  pl.BlockSpec((1, c), lambda bi, o, j: (0, 0)),
                  pl.BlockSpec((FREQ_BLOCK, seq), lambda bi, o, j: (j, 0)),
                  pl.BlockSpec((seq, FREQ_BLOCK), lambda bi, o, j: (0, j)),
                  pl.BlockSpec((1, FREQ_BLOCK, c), lambda bi, o, j: (o, j, 0))],
        out_specs=pl.BlockSpec((1, seq, c), lambda bi, o, j: (bi, 0, 0)),
        out_shape=jax.ShapeDtypeStruct((b, seq, c), BF16),
        scratch_shapes=[pltpu.VMEM((seq, c), F32), pltpu.VMEM((seq, c), BF16), pltpu.VMEM((seq, c), F32)],
        compiler_params=pltpu.CompilerParams(dimension_semantics=("parallel", "arbitrary", "arbitrary"),
                                             vmem_limit_bytes=V7X_VMEM_LIMIT_BYTES),
        name="hyena",
    )(hy, hy, conv_w, conv_b, conv_w, conv_b, skip, gain, fmat, fmat_t, hspec)


def _attn_kernel(seq, q_ref, k_ref, v_ref, bias_ref, bias16_ref, o_ref,
                 qs_ref, ks_ref, vs_ref, res_ref, nat_ref, s_ref, e_ref):
    lanes = V7X_LANES
    qscale = 1.0 / math.sqrt(HEAD_DIM)

    for p, d in enumerate(DILATIONS):
        ls = seq // d
        first = lax.broadcasted_iota(jnp.int32, (ls, lanes), 1) < HEAD_DIM
        for r in range(d):
            src = pl.ds(r, ls, stride=d) if d > 1 else pl.ds(0, ls)
            rows = slice(r * ls, (r + 1) * ls)
            qq = q_ref[0, src, :] * qscale
            qs_ref[p, 0, rows, :] = jnp.where(first, qq, 0.0).astype(BF16)
            qs_ref[p, 1, rows, :] = jnp.where(first, 0.0, qq).astype(BF16)
            ks_ref[p, rows, :] = k_ref[0, src, :].astype(BF16)
            vs_ref[p, rows, :] = v_ref[0, src, :].astype(BF16)

    first = lax.broadcasted_iota(jnp.int32, (Q_BLOCK, lanes), 1) < HEAD_DIM

    def run_pattern(p, nkeys, placement):
        def body(g, carry):
            blocks = [placement(g * ATTN_GROUP + i) for i in range(ATTN_GROUP)]
            for i, (row0, krow0, bias_of_head) in enumerate(blocks):
                kw = ks_ref[p, pl.ds(krow0, nkeys), :]
                for h in range(2):
                    qh = qs_ref[p, h, pl.ds(row0, Q_BLOCK), :]
                    s = lax.dot_general(qh, kw, (((1,), (1,)), ((), ())), preferred_element_type=F32)
                    s_ref[2 * i + h, :, :nkeys] = s + bias_of_head(h)
            for i, (row0, krow0, bias_of_head) in enumerate(blocks):
                stats = []
                for h in range(2):
                    s = s_ref[2 * i + h, :, :nkeys]
                    m = jnp.max(s, axis=-1, keepdims=True)
                    e = jnp.exp(s - m)
                    e_ref[2 * i + h, :, :nkeys] = e.astype(BF16)
                    stats.append((m, jnp.sum(e, axis=-1, keepdims=True)))
                for kind in range(2):
                    res_ref[p, 1 + kind, pl.ds(row0, Q_BLOCK), :] = jnp.where(first, stats[0][kind], stats[1][kind])
            for i, (row0, krow0, bias_of_head) in enumerate(blocks):
                vw = vs_ref[p, pl.ds(krow0, nkeys), :]
                o0 = _dot(e_ref[2 * i, :, :nkeys], vw)
                o1 = _dot(e_ref[2 * i + 1, :, :nkeys], vw)
                res_ref[p, 0, pl.ds(row0, Q_BLOCK), :] = jnp.where(first, o0, o1)
            return carry

        lax.fori_loop(0, seq // Q_BLOCK // ATTN_GROUP, body, 0)

    for p, d in enumerate(DILATIONS[:2]):
        ls = seq // d
        nblk = ls // Q_BLOCK

        def banded(n, p=p, ls=ls, nblk=nblk):
            r = n // nblk
            ib = n % nblk
            i0 = ib * Q_BLOCK
            k0 = jnp.clip(i0 - HALF_WINDOW, 0, ls - K_WINDOW)
            case = jnp.where(ib == 0, 0, jnp.where(ib == nblk - 1, 2, 1))
            row0 = pl.multiple_of(r * ls + i0, Q_BLOCK)
            krow0 = pl.multiple_of(r * ls + k0, HALF_WINDOW)
            return row0, krow0, lambda h: bias_ref[0, (p * 3 + case) * 2 + h]

        run_pattern(p, K_WINDOW, banded)

    def full(n):
        row0 = pl.multiple_of(n * Q_BLOCK, Q_BLOCK)
        return row0, row0, lambda h: bias16_ref[0, h]

    run_pattern(2, Q_BLOCK, full)

    for p, d in enumerate(DILATIONS[1:], start=1):
        ls = seq // d
        for r in range(d):
            for kind in range(3):
                nat_ref[p - 1, kind, pl.ds(r, ls, stride=d), :] = res_ref[p, kind, r * ls:(r + 1) * ls, :]

    for t0 in range(0, seq, CONV_ROWS):
        rows = slice(t0, t0 + CONV_ROWS)
        parts = [tuple(res_ref[0, kind, rows, :] for kind in range(3))]
        parts += [tuple(nat_ref[p, kind, rows, :] for kind in range(3)) for p in range(2)]
        m = functools.reduce(jnp.maximum, [pt[1] for pt in parts])
        num = 0.0
        den = 0.0
        for out, mp, lp in parts:
            w = jnp.exp(mp - m)
            num = num + w * out
            den = den + w * lp
        o_ref[0, rows, :] = num / den


def _attn_bias_tables():
    slopes = np.array([2.0 ** (-8.0 * (i + 1) / N_HEADS) for i in range(N_HEADS)], np.float32)
    slopes = jnp.asarray(slopes.reshape(N_HEADS // 2, 1, 2, 1, 1))
    qi = lax.broadcasted_iota(jnp.int32, (Q_BLOCK, K_WINDOW), 0)
    kj = lax.broadcasted_iota(jnp.int32, (Q_BLOCK, K_WINDOW), 1)
    offsets = (0, -HALF_WINDOW, -2 * HALF_WINDOW)
    dist = jnp.stack([jnp.abs(kj - qi + off) for off in offsets])
    valid = dist <= HALF_WINDOW
    dil = jnp.asarray(np.array(DILATIONS[:2], np.float32).reshape(2, 1, 1, 1))
    scaled = (dil * dist.astype(F32)[None])[None, :, :, None]
    banded = jnp.where(valid[None, None, :, None], -slopes[:, None] * scaled, NEG_INF)
    banded = banded.reshape(N_HEADS // 2, 12, Q_BLOCK, K_WINDOW)
    d16 = dist[0, :, :Q_BLOCK]
    full = jnp.where(d16 <= HALF_WINDOW, -slopes[:, 0] * (DILATIONS[2] * d16.astype(F32)), NEG_INF)
    return banded.astype(F32), full.astype(F32)


def _dil_attn(q, k, v):
    b, seq, _ = q.shape
    nhp = N_HEADS // 2
    bias, bias16 = _attn_bias_tables()
    head_pair = pl.BlockSpec((1, seq, V7X_LANES), lambda bi, hp: (bi, 0, hp))
    return pl.pallas_call(
        functools.partial(_attn_kernel, seq),
        grid=(b, nhp),
        in_specs=[head_pair, head_pair, head_pair,
                  pl.BlockSpec((1, 12, Q_BLOCK, K_WINDOW), lambda bi, hp: (hp, 0, 0, 0)),
                  pl.BlockSpec((1, 2, Q_BLOCK, Q_BLOCK), lambda bi, hp: (hp, 0, 0, 0))],
        out_specs=head_pair,
        out_shape=jax.ShapeDtypeStruct((b, seq, D_ATTN), F32),
        scratch_shapes=[pltpu.VMEM((3, 2, seq, V7X_LANES), BF16),
                        pltpu.VMEM((3, seq, V7X_LANES), BF16),
                        pltpu.VMEM((3, seq, V7X_LANES), BF16),
                        pltpu.VMEM((3, 3, seq, V7X_LANES), F32),
                        pltpu.VMEM((2, 3, seq, V7X_LANES), F32),
                        pltpu.VMEM((2 * ATTN_GROUP, Q_BLOCK, K_WINDOW), F32),
                        pltpu.VMEM((2 * ATTN_GROUP, Q_BLOCK, K_WINDOW), BF16)],
        compiler_params=pltpu.CompilerParams(dimension_semantics=("parallel", "parallel"),
                                             vmem_limit_bytes=V7X_VMEM_LIMIT_BYTES),
        name="dil_attn",
    )(q, k, v, bias, bias16)


def _out_ffn2_kernel(x1_ref, yh_ref, ya_ref, ga_ref, wo_ref, g3_ref, wg_ref, wu_ref, wd_ref, gf_ref,
                     out_ref, act_ref):
    ya = _rms(ya_ref[...], ga_ref[...]).astype(BF16)
    mix = _dot(yh_ref[...], wo_ref[:D_HYENA, :]) + _dot(ya, wo_ref[D_HYENA:, :])
    x2 = x1_ref[...] + mix
    h = _rms(x2, g3_ref[...]).astype(BF16)
    x3 = x2 + 0.5 * _swiglu(h, wg_ref, wu_ref, wd_ref, act_ref)
    out_ref[...] = _rms(x3, gf_ref[...])


def _out_ffn2(x1, yh, ya, ga, wo, g3, wg, wu, wd, gf):
    n = x1.shape[0]
    row = lambda w: pl.BlockSpec((ROW_TILE, w), lambda i: (i, 0))
    return pl.pallas_call(
        _out_ffn2_kernel,
        grid=(n // ROW_TILE,),
        in_specs=[row(D_MODEL), row(D_HYENA), row(D_ATTN), _resident((1, D_ATTN)), _resident(wo.shape),
                  _resident((1, D_MODEL)), _resident(wg.shape), _resident(wu.shape), _resident(wd.shape),
                  _resident((1, D_MODEL))],
        out_specs=row(D_MODEL),
        out_shape=jax.ShapeDtypeStruct((n, D_MODEL), F32),
        scratch_shapes=[pltpu.VMEM((ROW_TILE, D_FF), BF16)],
        compiler_params=pltpu.CompilerParams(dimension_semantics=("parallel",),
                                             vmem_limit_bytes=V7X_VMEM_LIMIT_BYTES),
        name="out_ffn2",
    )(x1, yh, ya, ga, wo, g3, wg, wu, wd, gf)


def _dft_matrices(seq):
    n = 2 * seq
    split = 64
    k = np.arange(seq, dtype=np.int64)[:, None]
    ang = lambda s: np.pi * (((2 * k + 1) * s) % (2 * n)) / n
    a1 = ang(split * np.arange(seq // split, dtype=np.int64)[None, :])
    a0 = ang(np.arange(split, dtype=np.int64)[None, :])
    tab = lambda x: jnp.asarray(x.astype(np.float32))
    c1, s1 = tab(np.cos(a1))[:, :, None], tab(np.sin(a1))[:, :, None]
    c0, s0 = tab(np.cos(a0))[:, None, :], tab(np.sin(a0))[:, None, :]
    cos = (c1 * c0 - s1 * s0).reshape(seq, seq)
    nsin = -(s1 * c0 + c1 * s0).reshape(seq, seq)
    hb = FREQ_BLOCK // 2
    fmat = jnp.concatenate([cos.reshape(seq // hb, hb, seq), nsin.reshape(seq // hb, hb, seq)], axis=1)
    fmat = fmat.reshape(n, seq)
    return fmat.astype(BF16), fmat.T.astype(BF16)


def _filter_features(seq):
    t = jnp.linspace(0.0, 1.0, seq, dtype=F32)[:, None]
    w = 2.0 * math.pi * jnp.arange(seq, dtype=F32)[:, None] / seq
    f = jnp.linspace(1e-4, FILTER_BANDS - 1, FILTER_BANDS, dtype=F32)[None, :]
    z = jnp.concatenate([t, jnp.cos(f * w), -jnp.sin(f * w)], axis=-1)
    return jnp.pad(z, ((0, 0), (0, FEAT_PAD - FILTER_EMB)))


def _decay_rates():
    max_decay = math.log(DECAY_TARGET) / FAST_DECAY_PCT
    min_decay = math.log(DECAY_TARGET) / SLOW_DECAY_PCT
    return jnp.linspace(min_decay, max_decay, D_HYENA, dtype=F32)[None, :]


def kernel(x, ffn1_norm_g, ffn1_w_gate, ffn1_w_up, ffn1_w_down, mix_norm_g, w_in, hy_conv_w, hy_conv_b, hy_filt_w1, hy_filt_b1, hy_filt_w2, hy_filt_b2, hy_filt_w3, hy_filt_b3, hy_filt_w_out, hy_filt_freq, hy_filt_skip, hy_out_norm_g, attn_out_norm_g, w_out, ffn2_norm_g, ffn2_w_gate, ffn2_w_up, ffn2_w_down, final_norm_g):
    b, seq, d = x.shape
    assert d == D_MODEL and (b * seq) % ROW_TILE == 0 and seq % (DILATIONS[-1] * Q_BLOCK) == 0
    row = lambda a: a.reshape(1, -1).astype(F32)
    bf = lambda a: a.astype(BF16)

    x1, hy, q, k, v = _ffn1_proj(x.reshape(b * seq, d), row(ffn1_norm_g), bf(ffn1_w_gate), bf(ffn1_w_up),
                                 bf(ffn1_w_down), row(mix_norm_g), bf(w_in))

    fmat, fmat_t = _dft_matrices(seq)
    w1 = jnp.pad(hy_filt_w1.astype(F32), ((0, FEAT_PAD - FILTER_EMB), (0, 0)))
    hspec = _hy_filter(seq, _filter_features(seq), w1, row(hy_filt_b1), hy_filt_w2.astype(F32),
                       row(hy_filt_b2), hy_filt_w3.astype(F32), row(hy_filt_b3),
                       hy_filt_w_out.astype(F32), row(hy_filt_freq), _decay_rates(), fmat)

    conv_w = hy_conv_w.astype(F32).reshape(3, 3, D_HYENA).transpose(1, 0, 2)
    conv_b = hy_conv_b.astype(F32).reshape(3, 1, D_HYENA)
    skip = hy_filt_skip.astype(F32).reshape(2, 1, D_HYENA)
    y_hy = _hyena(hy.reshape(b, seq, 3 * D_HYENA), conv_w, conv_b, skip, row(hy_out_norm_g),
                  fmat, fmat_t, hspec)

    shape3 = lambda a: a.reshape(b, seq, D_ATTN)
    y_at = _dil_attn(shape3(q), shape3(k), shape3(v))

    out = _out_ffn2(x1, y_hy.reshape(b * seq, D_HYENA), y_at.reshape(b * seq, D_ATTN),
                    row(attn_out_norm_g), bf(w_out), row(ffn2_norm_g), bf(ffn2_w_gate), bf(ffn2_w_up),
                    bf(ffn2_w_down), row(final_norm_g))
    return out.reshape(b, seq, d)
```

```python
import functools
import math

import numpy as np
import jax
import jax.numpy as jnp
from jax import lax
from jax.experimental import pallas as pl
from jax.experimental.pallas import tpu as pltpu

F32 = jnp.float32
BF16 = jnp.bfloat16

D_MODEL = 1024
D_HYENA = 512
D_ATTN = 512
HEAD_DIM = 64
N_HEADS = D_ATTN // HEAD_DIM
D_FF = 2816
FILTER_EMB = 33
FILTER_BANDS = 16
FILTER_WIDTH = 64
DECAY_TARGET = 1e-2
FAST_DECAY_PCT = 0.3
SLOW_DECAY_PCT = 1.5
DILATIONS = (1, 4, 16)
HALF_WINDOW = 64
RMS_EPS = 1e-6
NEG_INF = -1e30
LOG2_E = math.log2(math.e)

V7X_LANES = 128
V7X_VMEM_LIMIT_BYTES = 56 * 1024 * 1024

ROW_TILE = 512
FF_CHUNK = 256
FREQ_BLOCK = 512
Q_BLOCK = 128
K_WINDOW = 256
ATTN_GROUP = 8
CONV_ROWS = 256
FEAT_PAD = 128


def _dot(a, b):
    return jnp.dot(a, b, preferred_element_type=F32)


def _rms(x, g):
    return x * lax.rsqrt(jnp.mean(x * x, axis=-1, keepdims=True) + RMS_EPS) * g


def _swiglu(h, wg_ref, wu_ref, wd_ref, act_ref):
    for c in range(D_FF // FF_CHUNK):
        cols = slice(c * FF_CHUNK, (c + 1) * FF_CHUNK)
        g = _dot(h, wg_ref[:, cols])
        u = _dot(h, wu_ref[:, cols])
        act_ref[:, cols] = (g * jax.nn.sigmoid(g) * u).astype(BF16)
    return _dot(act_ref[...], wd_ref[...])


def _resident(shape):
    return pl.BlockSpec(shape, lambda *_: (0,) * len(shape), pipeline_mode=pl.Buffered(1))


def _ffn1_proj_kernel(x_ref, g1_ref, wg_ref, wu_ref, wd_ref, g2_ref, win_ref,
                      x1_ref, hy_ref, q_ref, k_ref, v_ref, act_ref):
    x = x_ref[...]
    h = _rms(x, g1_ref[...]).astype(BF16)
    x1 = x + 0.5 * _swiglu(h, wg_ref, wu_ref, wd_ref, act_ref)
    x1_ref[...] = x1
    h2 = _rms(x1, g2_ref[...]).astype(BF16)
    nh = 3 * D_HYENA
    hy_ref[...] = _dot(h2, win_ref[:, :nh])
    q_ref[...] = _dot(h2, win_ref[:, nh:nh + D_ATTN])
    k_ref[...] = _dot(h2, win_ref[:, nh + D_ATTN:nh + 2 * D_ATTN])
    v_ref[...] = _dot(h2, win_ref[:, nh + 2 * D_ATTN:])


def _ffn1_proj(x2d, g1, wg, wu, wd, g2, win):
    n = x2d.shape[0]
    row = lambda w: pl.BlockSpec((ROW_TILE, w), lambda i: (i, 0))
    return pl.pallas_call(
        _ffn1_proj_kernel,
        grid=(n // ROW_TILE,),
        in_specs=[row(D_MODEL), _resident((1, D_MODEL)), _resident(wg.shape), _resident(wu.shape),
                  _resident(wd.shape), _resident((1, D_MODEL)), _resident(win.shape)],
        out_specs=[row(D_MODEL), row(3 * D_HYENA), row(D_ATTN), row(D_ATTN), row(D_ATTN)],
        out_shape=[jax.ShapeDtypeStruct((n, D_MODEL), F32),
                   jax.ShapeDtypeStruct((n, 3 * D_HYENA), F32),
                   jax.ShapeDtypeStruct((n, D_ATTN), F32),
                   jax.ShapeDtypeStruct((n, D_ATTN), F32),
                   jax.ShapeDtypeStruct((n, D_ATTN), F32)],
        scratch_shapes=[pltpu.VMEM((ROW_TILE, D_FF), BF16)],
        compiler_params=pltpu.CompilerParams(dimension_semantics=("parallel",),
                                             vmem_limit_bytes=V7X_VMEM_LIMIT_BYTES),
        name="ffn1_proj",
    )(x2d, g1, wg, wu, wd, g2, win)


def _filter_kernel(seq, feat_ref, w1_ref, b1_ref, w2_ref, b2_ref, w3_ref, b3_ref, wout_ref,
                   freq_ref, delta_ref, f_ref, h_ref, hcat_ref):
    c_ = D_HYENA

    @pl.when(pl.program_id(0) == 0)
    def _():
        hi = lax.Precision.HIGHEST
        dot_hi = lambda a, b: jnp.dot(a, b, precision=hi, preferred_element_type=F32)
        freq = freq_ref[...]
        h = jnp.sin(freq * (dot_hi(feat_ref[...], w1_ref[...]) + b1_ref[...]))
        h = jnp.sin(freq * (dot_hi(h, w2_ref[...]) + b2_ref[...]))
        h = jnp.sin(freq * (dot_hi(h, w3_ref[...]) + b3_ref[...]))
        t = lax.broadcasted_iota(jnp.int32, (seq, 1), 0).astype(F32) * (1.0 / (seq - 1))
        decay = jnp.exp(-t * jnp.abs(delta_ref[...]))
        row = lax.broadcasted_iota(jnp.int32, (seq, c_), 0)
        for c in range(4):
            kc = dot_hi(h, wout_ref[:, c * c_:(c + 1) * c_]) * decay
            if c % 2 == 1:
                kc = jnp.where(row == 0, 0.0, kc)
            hcat_ref[:, c * c_:(c + 1) * c_] = kc.astype(BF16)

    p = _dot(f_ref[...], hcat_ref[...])
    hb = FREQ_BLOCK // 2
    scale = 1.0 / seq
    for o in range(2):
        pf = p[:, (2 * o) * c_:(2 * o + 1) * c_]
        pb = p[:, (2 * o + 1) * c_:(2 * o + 2) * c_]
        h_ref[o, :hb, :] = (pf[:hb] + pb[:hb]) * scale
        h_ref[o, hb:, :] = (pf[hb:] - pb[hb:]) * scale


def _hy_filter(seq, feat, w1, b1, w2, b2, w3, b3, wout, freq, delta, fmat):
    full = lambda a: _resident(a.shape)
    return pl.pallas_call(
        functools.partial(_filter_kernel, seq),
        grid=(2 * seq // FREQ_BLOCK,),
        in_specs=[full(feat), full(w1), full(b1), full(w2), full(b2), full(w3), full(b3), full(wout),
                  full(freq), full(delta), pl.BlockSpec((FREQ_BLOCK, seq), lambda j: (j, 0))],
        out_specs=pl.BlockSpec((2, FREQ_BLOCK, D_HYENA), lambda j: (0, j, 0)),
        out_shape=jax.ShapeDtypeStruct((2, 2 * seq, D_HYENA), F32),
        scratch_shapes=[pltpu.VMEM((seq, 4 * D_HYENA), BF16)],
        compiler_params=pltpu.CompilerParams(dimension_semantics=("arbitrary",),
                                             vmem_limit_bytes=V7X_VMEM_LIMIT_BYTES),
        name="hy_filter",
    )(feat, w1, b1, w2, b2, w3, b3, wout, freq, delta, fmat)


def _short_conv_rows(u_ref, w_ref, b_ref, t0, seq):
    cur = u_ref[0, t0:t0 + CONV_ROWS, :]
    row = lax.broadcasted_iota(jnp.int32, cur.shape, 0)
    zero = jnp.zeros((1, cur.shape[1]), F32)
    before = u_ref[0, t0 - 1:t0, :] if t0 > 0 else zero
    after = u_ref[0, t0 + CONV_ROWS:t0 + CONV_ROWS + 1, :] if t0 + CONV_ROWS < seq else zero
    prev = jnp.where(row == 0, before, pltpu.roll(cur, 1, axis=0))
    nxt = jnp.where(row == CONV_ROWS - 1, after, pltpu.roll(cur, CONV_ROWS - 1, axis=0))
    return b_ref[0] + prev * w_ref[0, 0:1, :] + cur * w_ref[0, 1:2, :] + nxt * w_ref[0, 2:3, :]


def _hyena_kernel(seq, v_ref, gate_ref, wv_ref, bv_ref, wg_ref, bg_ref, skip_ref, gain_ref,
                  f_ref, ft_ref, h_ref, out_ref, zf_ref, zb_ref, acc_ref):
    o = pl.program_id(1)
    j = pl.program_id(2)
    last = pl.num_programs(2) - 1
    chunks = range(0, seq, CONV_ROWS)

    @pl.when((o == 0) & (j == 0))
    def _():
        for t0 in chunks:
            z = _short_conv_rows(v_ref, wv_ref, bv_ref, t0, seq)
            zf_ref[t0:t0 + CONV_ROWS, :] = z
            zb_ref[t0:t0 + CONV_ROWS, :] = z.astype(BF16)

    hb = FREQ_BLOCK // 2
    zfreq = _dot(f_ref[...], zb_ref[...])
    zr, zi = zfreq[:hb], zfreq[hb:]
    hr, hi = h_ref[0, :hb, :], h_ref[0, hb:, :]
    yr = (zr * hr - zi * hi).astype(BF16)
    yi = (zr * hi + zi * hr).astype(BF16)
    contrib = _dot(ft_ref[:, :hb], yr) + _dot(ft_ref[:, hb:], yi)

    @pl.when(j == 0)
    def _():
        acc_ref[...] = contrib

    @pl.when(j > 0)
    def _():
        acc_ref[...] += contrib

    def gated(t0):
        rows = slice(t0, t0 + CONV_ROWS)
        gate = _short_conv_rows(gate_ref, wg_ref, bg_ref, t0, seq)
        return gate * (acc_ref[rows, :] + zf_ref[rows, :] * skip_ref[0])

    @pl.when((j == last) & (o == 0))
    def _():
        for t0 in chunks:
            z = gated(t0)
            zf_ref[t0:t0 + CONV_ROWS, :] = z
            zb_ref[t0:t0 + CONV_ROWS, :] = z.astype(BF16)

    @pl.when((j == last) & (o == 1))
    def _():
        for t0 in chunks:
            out_ref[0, t0:t0 + CONV_ROWS, :] = _rms(gated(t0), gain_ref[...]).astype(BF16)


def _hyena(hy, conv_w, conv_b, skip, gain, fmat, fmat_t, hspec):
    b, seq, _ = hy.shape
    c = D_HYENA
    nfb = 2 * seq // FREQ_BLOCK
    part = lambda sel: pl.BlockSpec((1, seq, c), lambda bi, o, j: (bi, 0, sel(o)))
    wpart = lambda sel: pl.BlockSpec((1, 3, c), lambda bi, o, j: (sel(o), 0, 0))
    bpart = lambda sel: pl.BlockSpec((1, 1, c), lambda bi, o, j: (sel(o), 0, 0))
    value = lambda o: 0
    gate = lambda o: 1 + o
    return pl.pallas_call(
        functools.partial(_hyena_kernel, seq),
        grid=(b, 2, nfb),
        in_specs=[part(value), part(gate), wpart(value), bpart(value), wpart(gate), bpart(gate),
                  pl.BlockSpec((1, 1, c), lambda bi, o, j: (o, 0, 0)),
                  pl.BlockSpec((1, c), lambda bi, o, j: (0, 0)),
                  pl.BlockSpec((FREQ_BLOCK, seq), lambda bi, o, j: (j, 0)),
                  pl.BlockSpec((seq, FREQ_BLOCK), lambda bi, o, j: (0, j)),
                  pl.BlockSpec((1, FREQ_BLOCK, c), lambda bi, o, j: (o, j, 0))],
        out_specs=pl.BlockSpec((1, seq, c), lambda bi, o, j: (bi, 0, 0)),
        out_shape=jax.ShapeDtypeStruct((b, seq, c), BF16),
        scratch_shapes=[pltpu.VMEM((seq, c), F32), pltpu.VMEM((seq, c), BF16), pltpu.VMEM((seq, c), F32)],
        compiler_params=pltpu.CompilerParams(dimension_semantics=("parallel", "arbitrary", "arbitrary"),
                                             vmem_limit_bytes=V7X_VMEM_LIMIT_BYTES),
        name="hyena",
    )(hy, hy, conv_w, conv_b, conv_w, conv_b, skip, gain, fmat, fmat_t, hspec)


def _attn_kernel(seq, q_ref, k_ref, v_ref, bias_ref, bias16_ref, o_ref,
                 qs_ref, ks_ref, vs_ref, res_ref, nat_ref, s_ref, e_ref):
    lanes = V7X_LANES
    qscale = LOG2_E / math.sqrt(HEAD_DIM)

    for p, d in enumerate(DILATIONS):
        ls = seq // d
        first = lax.broadcasted_iota(jnp.int32, (ls, lanes), 1) < HEAD_DIM
        for r in range(d):
            src = pl.ds(r, ls, stride=d) if d > 1 else pl.ds(0, ls)
            rows = slice(r * ls, (r + 1) * ls)
            qq = q_ref[0, src, :] * qscale
            qs_ref[p, 0, rows, :] = jnp.where(first, qq, 0.0).astype(BF16)
            qs_ref[p, 1, rows, :] = jnp.where(first, 0.0, qq).astype(BF16)
            ks_ref[p, rows, :] = k_ref[0, src, :].astype(BF16)
            vv = v_ref[0, src, :]
            vs_ref[p, 0, rows, :] = jnp.where(first, vv, 1.0).astype(BF16)
            vs_ref[p, 1, rows, :] = jnp.where(first, 1.0, vv).astype(BF16)

    first = lax.broadcasted_iota(jnp.int32, (Q_BLOCK, lanes), 1) < HEAD_DIM

    def run_pattern(p, nkeys, placement):
        def body(g, carry):
            blocks = [placement(g * ATTN_GROUP + i) for i in range(ATTN_GROUP)]
            for i, (row0, krow0, bias_of_head) in enumerate(blocks):
                kw = ks_ref[p, pl.ds(krow0, nkeys), :]
                for h in range(2):
                    qh = qs_ref[p, h, pl.ds(row0, Q_BLOCK), :]
                    s = lax.dot_general(qh, kw, (((1,), (1,)), ((), ())), preferred_element_type=F32)
                    s_ref[2 * i + h, :, :nkeys] = s + bias_of_head(h)
            for i, (row0, krow0, bias_of_head) in enumerate(blocks):
                ms = []
                for h in range(2):
                    s = s_ref[2 * i + h, :, :nkeys]
                    m = jnp.max(s, axis=-1, keepdims=True)
                    e_ref[2 * i + h, :, :nkeys] = jnp.exp2(s - m).astype(BF16)
                    ms.append(m)
                res_ref[p, 1, pl.ds(row0, Q_BLOCK), :] = jnp.where(first, ms[0], ms[1])
            for i, (row0, krow0, bias_of_head) in enumerate(blocks):
                o0 = _dot(e_ref[2 * i, :, :nkeys], vs_ref[p, 0, pl.ds(krow0, nkeys), :])
                o1 = _dot(e_ref[2 * i + 1, :, :nkeys], vs_ref[p, 1, pl.ds(krow0, nkeys), :])
                res_ref[p, 0, pl.ds(row0, Q_BLOCK), :] = jnp.where(first, o0, o1)
                res_ref[p, 2, pl.ds(row0, Q_BLOCK), :] = jnp.where(first, o1, o0)
            return carry

        lax.fori_loop(0, seq // Q_BLOCK // ATTN_GROUP, body, 0)

    for p, d in enumerate(DILATIONS[:2]):
        ls = seq // d
        nblk = ls // Q_BLOCK

        def banded(n, p=p, ls=ls, nblk=nblk):
            r = n // nblk
            ib = n % nblk
            i0 = ib * Q_BLOCK
            k0 = jnp.clip(i0 - HALF_WINDOW, 0, ls - K_WINDOW)
            case = jnp.where(ib == 0, 0, jnp.where(ib == nblk - 1, 2, 1))
            row0 = pl.multiple_of(r * ls + i0, Q_BLOCK)
            krow0 = pl.multiple_of(r * ls + k0, HALF_WINDOW)
            return row0, krow0, lambda h: bias_ref[0, (p * 3 + case) * 2 + h]

        run_pattern(p, K_WINDOW, banded)

    def full(n):
        row0 = pl.multiple_of(n * Q_BLOCK, Q_BLOCK)
        return row0, row0, lambda h: bias16_ref[0, h]

    run_pattern(2, Q_BLOCK, full)

    for p, d in enumerate(DILATIONS[1:], start=1):
        ls = seq // d
        for r in range(d):
            for kind in range(3):
                nat_ref[p - 1, kind, pl.ds(r, ls, stride=d), :] = res_ref[p, kind, r * ls:(r + 1) * ls, :]

    for t0 in range(0, seq, CONV_ROWS):
        rows = slice(t0, t0 + CONV_ROWS)
        parts = [tuple(res_ref[0, kind, rows, :] for kind in range(3))]
        parts += [tuple(nat_ref[p, kind, rows, :] for kind in range(3)) for p in range(2)]
        m = functools.reduce(jnp.maximum, [pt[1] for pt in parts])
        num = 0.0
        den = 0.0
        for out, mp, lp in parts:
            w = jnp.exp2(mp - m)
            num = num + w * out
            den = den + w * pltpu.roll(lp, HEAD_DIM, axis=1)
        o_ref[0, rows, :] = num / den


def _attn_bias_tables():
    slopes = np.array([2.0 ** (-8.0 * (i + 1) / N_HEADS) for i in range(N_HEADS)], np.float32)
    slopes = jnp.asarray(slopes.reshape(N_HEADS // 2, 1, 2, 1, 1))
    qi = lax.broadcasted_iota(jnp.int32, (Q_BLOCK, K_WINDOW), 0)
    kj = lax.broadcasted_iota(jnp.int32, (Q_BLOCK, K_WINDOW), 1)
    offsets = (0, -HALF_WINDOW, -2 * HALF_WINDOW)
    dist = jnp.stack([jnp.abs(kj - qi + off) for off in offsets])
    valid = dist <= HALF_WINDOW
    dil = jnp.asarray(np.array(DILATIONS[:2], np.float32).reshape(2, 1, 1, 1))
    scaled = (dil * dist.astype(F32)[None])[None, :, :, None]
    banded = jnp.where(valid[None, None, :, None], -slopes[:, None] * scaled, NEG_INF)
    banded = banded.reshape(N_HEADS // 2, 12, Q_BLOCK, K_WINDOW)
    d16 = dist[0, :, :Q_BLOCK]
    full = jnp.where(d16 <= HALF_WINDOW, -slopes[:, 0] * (DILATIONS[2] * d16.astype(F32)), NEG_INF)
    to_base2 = lambda t: jnp.where(t > 0.5 * NEG_INF, t * LOG2_E, NEG_INF).astype(F32)
    return to_base2(banded), to_base2(full)


def _dil_attn(q, k, v):
    b, seq, _ = q.shape
    nhp = N_HEADS // 2
    bias, bias16 = _attn_bias_tables()
    head_pair = pl.BlockSpec((1, seq, V7X_LANES), lambda bi, hp: (bi, 0, hp))
    return pl.pallas_call(
        functools.partial(_attn_kernel, seq),
        grid=(b, nhp),
        in_specs=[head_pair, head_pair, head_pair,
                  pl.BlockSpec((1, 12, Q_BLOCK, K_WINDOW), lambda bi, hp: (hp, 0, 0, 0)),
                  pl.BlockSpec((1, 2, Q_BLOCK, Q_BLOCK), lambda bi, hp: (hp, 0, 0, 0))],
        out_specs=head_pair,
        out_shape=jax.ShapeDtypeStruct((b, seq, D_ATTN), F32),
        scratch_shapes=[pltpu.VMEM((3, 2, seq, V7X_LANES), BF16),
                        pltpu.VMEM((3, seq, V7X_LANES), BF16),
                        pltpu.VMEM((3, 2, seq, V7X_LANES), BF16),
                        pltpu.VMEM((3, 3, seq, V7X_LANES), F32),
                        pltpu.VMEM((2, 3, seq, V7X_LANES), F32),
                        pltpu.VMEM((2 * ATTN_GROUP, Q_BLOCK, K_WINDOW), F32),
                        pltpu.VMEM((2 * ATTN_GROUP, Q_BLOCK, K_WINDOW), BF16)],
        compiler_params=pltpu.CompilerParams(dimension_semantics=("parallel", "parallel"),
                                             vmem_limit_bytes=V7X_VMEM_LIMIT_BYTES),
        name="dil_attn",
    )(q, k, v, bias, bias16)


def _out_ffn2_kernel(x1_ref, yh_ref, ya_ref, ga_ref, wo_ref, g3_ref, wg_ref, wu_ref, wd_ref, gf_ref,
                     out_ref, act_ref):
    ya = _rms(ya_ref[...], ga_ref[...]).astype(BF16)
    mix = _dot(yh_ref[...], wo_ref[:D_HYENA, :]) + _dot(ya, wo_ref[D_HYENA:, :])
    x2 = x1_ref[...] + mix
    h = _rms(x2, g3_ref[...]).astype(BF16)
    x3 = x2 + 0.5 * _swiglu(h, wg_ref, wu_ref, wd_ref, act_ref)
    out_ref[...] = _rms(x3, gf_ref[...])


def _out_ffn2(x1, yh, ya, ga, wo, g3, wg, wu, wd, gf):
    n = x1.shape[0]
    row = lambda w: pl.BlockSpec((ROW_TILE, w), lambda i: (i, 0))
    return pl.pallas_call(
        _out_ffn2_kernel,
        grid=(n // ROW_TILE,),
        in_specs=[row(D_MODEL), row(D_HYENA), row(D_ATTN), _resident((1, D_ATTN)), _resident(wo.shape),
                  _resident((1, D_MODEL)), _resident(wg.shape), _resident(wu.shape), _resident(wd.shape),
                  _resident((1, D_MODEL))],
        out_specs=row(D_MODEL),
        out_shape=jax.ShapeDtypeStruct((n, D_MODEL), F32),
        scratch_shapes=[pltpu.VMEM((ROW_TILE, D_FF), BF16)],
        compiler_params=pltpu.CompilerParams(dimension_semantics=("parallel",),
                                             vmem_limit_bytes=V7X_VMEM_LIMIT_BYTES),
        name="out_ffn2",
    )(x1, yh, ya, ga, wo, g3, wg, wu, wd, gf)


def _dft_matrices(seq):
    n = 2 * seq
    split = 64
    k = np.arange(seq, dtype=np.int64)[:, None]
    ang = lambda s: np.pi * (((2 * k + 1) * s) % (2 * n)) / n
    a1 = ang(split * np.arange(seq // split, dtype=np.int64)[None, :])
    a0 = ang(np.arange(split, dtype=np.int64)[None, :])
    tab = lambda x: jnp.asarray(x.astype(np.float32))
    hb = FREQ_BLOCK // 2
    c1, s1 = tab(np.cos(a1))[:, :, None], tab(np.sin(a1))[:, :, None]
    c0, s0 = tab(np.cos(a0))[:, None, :], tab(np.sin(a0))[:, None, :]
    cos = (c1 * c0 - s1 * s0).astype(BF16).reshape(seq // hb, hb, seq)
    nsin = (-(s1 * c0 + c1 * s0)).astype(BF16).reshape(seq // hb, hb, seq)
    fmat = jnp.concatenate([cos, nsin], axis=1).reshape(n, seq)
    c1, s1 = tab(np.cos(a1).T)[:, None, :], tab(np.sin(a1).T)[:, None, :]
    c0, s0 = tab(np.cos(a0).T)[None, :, :], tab(np.sin(a0).T)[None, :, :]
    cos = (c1 * c0 - s1 * s0).astype(BF16).reshape(seq, seq // hb, 1, hb)
    nsin = (-(s1 * c0 + c1 * s0)).astype(BF16).reshape(seq, seq // hb, 1, hb)
    fmat_t = jnp.concatenate([cos, nsin], axis=2).reshape(seq, n)
    return fmat, fmat_t


def _filter_features(seq):
    t = jnp.linspace(0.0, 1.0, seq, dtype=F32)[:, None]
    w = 2.0 * math.pi * jnp.arange(seq, dtype=F32)[:, None] / seq
    f = jnp.linspace(1e-4, FILTER_BANDS - 1, FILTER_BANDS, dtype=F32)[None, :]
    z = jnp.concatenate([t, jnp.cos(f * w), -jnp.sin(f * w)], axis=-1)
    return jnp.pad(z, ((0, 0), (0, FEAT_PAD - FILTER_EMB)))


def _decay_rates():
    max_decay = math.log(DECAY_TARGET) / FAST_DECAY_PCT
    min_decay = math.log(DECAY_TARGET) / SLOW_DECAY_PCT
    return jnp.linspace(min_decay, max_decay, D_HYENA, dtype=F32)[None, :]


def kernel(x, ffn1_norm_g, ffn1_w_gate, ffn1_w_up, ffn1_w_down, mix_norm_g, w_in, hy_conv_w, hy_conv_b, hy_filt_w1, hy_filt_b1, hy_filt_w2, hy_filt_b2, hy_filt_w3, hy_filt_b3, hy_filt_w_out, hy_filt_freq, hy_filt_skip, hy_out_norm_g, attn_out_norm_g, w_out, ffn2_norm_g, ffn2_w_gate, ffn2_w_up, ffn2_w_down, final_norm_g):
    b, seq, d = x.shape
    assert d == D_MODEL and (b * seq) % ROW_TILE == 0 and seq % (DILATIONS[-1] * Q_BLOCK) == 0
    row = lambda a: a.reshape(1, -1).astype(F32)
    bf = lambda a: a.astype(BF16)

    x1, hy, q, k, v = _ffn1_proj(x.reshape(b * seq, d), row(ffn1_norm_g), bf(ffn1_w_gate), bf(ffn1_w_up),
                                 bf(ffn1_w_down), row(mix_norm_g), bf(w_in))

    fmat, fmat_t = _dft_matrices(seq)
    w1 = jnp.pad(hy_filt_w1.astype(F32), ((0, FEAT_PAD - FILTER_EMB), (0, 0)))
    hspec = _hy_filter(seq, _filter_features(seq), w1, row(hy_filt_b1), hy_filt_w2.astype(F32),
                       row(hy_filt_b2), hy_filt_w3.astype(F32), row(hy_filt_b3),
                       hy_filt_w_out.astype(F32), row(hy_filt_freq), _decay_rates(), fmat)

    conv_w = hy_conv_w.astype(F32).reshape(3, 3, D_HYENA).transpose(1, 0, 2)
    conv_b = hy_conv_b.astype(F32).reshape(3, 1, D_HYENA)
    skip = hy_filt_skip.astype(F32).reshape(2, 1, D_HYENA)
    y_hy = _hyena(hy.reshape(b, seq, 3 * D_HYENA), conv_w, conv_b, skip, row(hy_out_norm_g),
                  fmat, fmat_t, hspec)

    shape3 = lambda a: a.reshape(b, seq, D_ATTN)
    y_at = _dil_attn(shape3(q), shape3(k), shape3(v))

    out = _out_ffn2(x1, y_hy.reshape(b * seq, D_HYENA), y_at.reshape(b * seq, D_ATTN),
                    row(attn_out_norm_g), bf(w_out), row(ffn2_norm_g), bf(ffn2_w_gate), bf(ffn2_w_up),
                    bf(ffn2_w_down), row(final_norm_g))
    return out.reshape(b, seq, d)
```

```python
import functools
import math

import numpy as np
import jax
import jax.numpy as jnp
from jax import lax
from jax.experimental import pallas as pl
from jax.experimental.pallas import tpu as pltpu

F32 = jnp.float32
BF16 = jnp.bfloat16

D_MODEL = 1024
D_HYENA = 512
D_ATTN = 512
HEAD_DIM = 64
N_HEADS = D_ATTN // HEAD_DIM
D_FF = 2816
FILTER_EMB = 33
FILTER_BANDS = 16
FILTER_WIDTH = 64
DECAY_TARGET = 1e-2
FAST_DECAY_PCT = 0.3
SLOW_DECAY_PCT = 1.5
DILATIONS = (1, 4, 16)
HALF_WINDOW = 64
RMS_EPS = 1e-6
NEG_INF = -1e30
LOG2_E = math.log2(math.e)

V7X_LANES = 128
V7X_SUBLANES = 8
V7X_VMEM_LIMIT_BYTES = 56 * 1024 * 1024

ROW_TILE = 512
FF_CHUNK = 256
FREQ_BLOCK = 512
Q_BLOCK = 128
K_WINDOW = 256
ATTN_GROUP = 8
CONV_ROWS = 256
FEAT_PAD = 128


def _dot(a, b):
    return jnp.dot(a, b, preferred_element_type=F32)


def _rms(x, g):
    return x * lax.rsqrt(jnp.mean(x * x, axis=-1, keepdims=True) + RMS_EPS) * g


def _swiglu(h, wg_ref, wu_ref, wd_ref, act_ref):
    for c in range(D_FF // FF_CHUNK):
        cols = slice(c * FF_CHUNK, (c + 1) * FF_CHUNK)
        g = _dot(h, wg_ref[:, cols])
        u = _dot(h, wu_ref[:, cols])
        act_ref[:, cols] = (g * jax.nn.sigmoid(g) * u).astype(BF16)
    return _dot(act_ref[...], wd_ref[...])


def _resident(shape):
    return pl.BlockSpec(shape, lambda *_: (0,) * len(shape), pipeline_mode=pl.Buffered(1))


def _ffn1_proj_kernel(x_ref, g1_ref, wg_ref, wu_ref, wd_ref, g2_ref, win_ref,
                      x1_ref, hy_ref, q_ref, k_ref, v_ref, act_ref):
    x = x_ref[...]
    h = _rms(x, g1_ref[...]).astype(BF16)
    x1 = x + 0.5 * _swiglu(h, wg_ref, wu_ref, wd_ref, act_ref)
    x1_ref[...] = x1
    h2 = _rms(x1, g2_ref[...]).astype(BF16)
    nh = 3 * D_HYENA
    hy_ref[...] = _dot(h2, win_ref[:, :nh])
    q_ref[...] = _dot(h2, win_ref[:, nh:nh + D_ATTN])
    k_ref[...] = _dot(h2, win_ref[:, nh + D_ATTN:nh + 2 * D_ATTN])
    v_ref[...] = _dot(h2, win_ref[:, nh + 2 * D_ATTN:])


def _ffn1_proj(x2d, g1, wg, wu, wd, g2, win):
    n = x2d.shape[0]
    row = lambda w: pl.BlockSpec((ROW_TILE, w), lambda i: (i, 0))
    return pl.pallas_call(
        _ffn1_proj_kernel,
        grid=(n // ROW_TILE,),
        in_specs=[row(D_MODEL), _resident((1, D_MODEL)), _resident(wg.shape), _resident(wu.shape),
                  _resident(wd.shape), _resident((1, D_MODEL)), _resident(win.shape)],
        out_specs=[row(D_MODEL), row(3 * D_HYENA), row(D_ATTN), row(D_ATTN), row(D_ATTN)],
        out_shape=[jax.ShapeDtypeStruct((n, D_MODEL), F32),
                   jax.ShapeDtypeStruct((n, 3 * D_HYENA), F32),
                   jax.ShapeDtypeStruct((n, D_ATTN), F32),
                   jax.ShapeDtypeStruct((n, D_ATTN), F32),
                   jax.ShapeDtypeStruct((n, D_ATTN), F32)],
        scratch_shapes=[pltpu.VMEM((ROW_TILE, D_FF), BF16)],
        compiler_params=pltpu.CompilerParams(dimension_semantics=("parallel",),
                                             vmem_limit_bytes=V7X_VMEM_LIMIT_BYTES),
        name="ffn1_proj",
    )(x2d, g1, wg, wu, wd, g2, win)


def _filter_kernel(seq, feat_ref, w1_ref, b1_ref, w2_ref, b2_ref, w3_ref, b3_ref, wout_ref,
                   freq_ref, delta_ref, f_ref, h_ref, hcat_ref):
    c_ = D_HYENA

    @pl.when(pl.program_id(0) == 0)
    def _():
        hi = lax.Precision.HIGHEST
        dot_hi = lambda a, b: jnp.dot(a, b, precision=hi, preferred_element_type=F32)
        freq = freq_ref[...]
        h = jnp.sin(freq * (dot_hi(feat_ref[...], w1_ref[...]) + b1_ref[...]))
        h = jnp.sin(freq * (dot_hi(h, w2_ref[...]) + b2_ref[...]))
        h = jnp.sin(freq * (dot_hi(h, w3_ref[...]) + b3_ref[...]))
        t = lax.broadcasted_iota(jnp.int32, (seq, 1), 0).astype(F32) * (1.0 / (seq - 1))
        decay = jnp.exp(-t * jnp.abs(delta_ref[...]))
        row = lax.broadcasted_iota(jnp.int32, (seq, c_), 0)
        for c in range(4):
            kc = dot_hi(h, wout_ref[:, c * c_:(c + 1) * c_]) * decay
            if c % 2 == 1:
                kc = jnp.where(row == 0, 0.0, kc)
            hcat_ref[:, c * c_:(c + 1) * c_] = kc.astype(BF16)

    p = _dot(f_ref[...], hcat_ref[...])
    hb = FREQ_BLOCK // 2
    scale = 1.0 / seq
    for o in range(2):
        pf = p[:, (2 * o) * c_:(2 * o + 1) * c_]
        pb = p[:, (2 * o + 1) * c_:(2 * o + 2) * c_]
        h_ref[o, :hb, :] = (pf[:hb] + pb[:hb]) * scale
        h_ref[o, hb:, :] = (pf[hb:] - pb[hb:]) * scale


def _hy_filter(seq, feat, w1, b1, w2, b2, w3, b3, wout, freq, delta, fmat):
    full = lambda a: _resident(a.shape)
    return pl.pallas_call(
        functools.partial(_filter_kernel, seq),
        grid=(2 * seq // FREQ_BLOCK,),
        in_specs=[full(feat), full(w1), full(b1), full(w2), full(b2), full(w3), full(b3), full(wout),
                  full(freq), full(delta), pl.BlockSpec((FREQ_BLOCK, seq), lambda j: (j, 0))],
        out_specs=pl.BlockSpec((2, FREQ_BLOCK, D_HYENA), lambda j: (0, j, 0)),
        out_shape=jax.ShapeDtypeStruct((2, 2 * seq, D_HYENA), F32),
        scratch_shapes=[pltpu.VMEM((seq, 4 * D_HYENA), BF16)],
        compiler_params=pltpu.CompilerParams(dimension_semantics=("arbitrary",),
                                             vmem_limit_bytes=V7X_VMEM_LIMIT_BYTES),
        name="hy_filter",
    )(feat, w1, b1, w2, b2, w3, b3, wout, freq, delta, fmat)


def _short_conv_rows(u_ref, w_ref, b_ref, t0, seq):
    cur = u_ref[0, pl.ds(t0, CONV_ROWS), :]
    row = lax.broadcasted_iota(jnp.int32, cur.shape, 0)
    if isinstance(t0, int):
        lo, hi = max(t0 - V7X_SUBLANES, 0), min(t0 + CONV_ROWS, seq - V7X_SUBLANES)
    else:
        lo = pl.multiple_of(jnp.maximum(t0 - V7X_SUBLANES, 0), V7X_SUBLANES)
        hi = pl.multiple_of(jnp.minimum(t0 + CONV_ROWS, seq - V7X_SUBLANES), V7X_SUBLANES)
    before = u_ref[0, pl.ds(lo, V7X_SUBLANES), :][V7X_SUBLANES - 1:]
    after = u_ref[0, pl.ds(hi, V7X_SUBLANES), :][:1]
    before = jnp.where(t0 > 0, before, 0.0)
    after = jnp.where(t0 + CONV_ROWS < seq, after, 0.0)
    prev = jnp.where(row == 0, before, pltpu.roll(cur, 1, axis=0))
    nxt = jnp.where(row == CONV_ROWS - 1, after, pltpu.roll(cur, CONV_ROWS - 1, axis=0))
    return b_ref[0] + prev * w_ref[0, 0:1, :] + cur * w_ref[0, 1:2, :] + nxt * w_ref[0, 2:3, :]


def _hyena_kernel(seq, v_ref, gate_ref, wv_ref, bv_ref, wg_ref, bg_ref, skip_ref, gain_ref,
                  f_ref, fct_ref, fst_ref, h_ref, out_ref, zf_ref, zb_ref, acc_ref, g_ref, y_ref):
    o = pl.program_id(1)
    j = pl.program_id(2)
    nfb = 2 * seq // FREQ_BLOCK
    chunks = range(0, seq, CONV_ROWS)
    hb = FREQ_BLOCK // 2

    def forward(slot):
        zfreq = _dot(f_ref[...], zb_ref[...])
        zr, zi = zfreq[:hb], zfreq[hb:]
        hr, hi = h_ref[0, :hb, :], h_ref[0, hb:, :]
        y_ref[slot, :hb, :] = (zr * hr - zi * hi).astype(BF16)
        y_ref[slot, hb:, :] = (zr * hi + zi * hr).astype(BF16)

    def inverse(slot):
        acc_ref[...] += _dot(fct_ref[...], y_ref[slot, :hb, :]) + _dot(fst_ref[...], y_ref[slot, hb:, :])

    def gate_chunk():
        t0 = pl.multiple_of(j * CONV_ROWS, CONV_ROWS)
        g_ref[pl.ds(t0, CONV_ROWS), :] = _short_conv_rows(gate_ref, wg_ref, bg_ref, t0, seq)

    @pl.when((o == 0) & (j == 0))
    def _():
        for t0 in chunks:
            z = _short_conv_rows(v_ref, wv_ref, bv_ref, t0, seq)
            zf_ref[t0:t0 + CONV_ROWS, :] = z
            zb_ref[t0:t0 + CONV_ROWS, :] = z.astype(BF16)

    @pl.when(j == 0)
    def _():
        gate_chunk()
        acc_ref[...] = jnp.zeros_like(acc_ref)
        forward(0)

    @pl.when((j > 0) & (j < nfb))
    def _():
        gate_chunk()
        inverse(1 - j % 2)
        forward(j % 2)

    def gated(rows):
        return g_ref[rows, :] * (acc_ref[rows, :] + zf_ref[rows, :] * skip_ref[0])

    last_slot = (nfb - 1) % 2

    @pl.when((j == nfb) & (o == 0))
    def _():
        inverse(last_slot)
        for t0 in chunks:
            rows = slice(t0, t0 + CONV_ROWS)
            z = gated(rows)
            zf_ref[rows, :] = z
            zb_ref[rows, :] = z.astype(BF16)

    @pl.when((j == nfb) & (o == 1))
    def _():
        inverse(last_slot)
        for t0 in chunks:
            rows = slice(t0, t0 + CONV_ROWS)
            out_ref[0, rows, :] = _rms(gated(rows), gain_ref[...]).astype(BF16)


def _hyena(hy, conv_w, conv_b, skip, gain, fmat, fcos_t, fsin_t, hspec):
    b, seq, _ = hy.shape
    c = D_HYENA
    hb = FREQ_BLOCK // 2
    nfb = 2 * seq // FREQ_BLOCK
    assert seq // CONV_ROWS == nfb
    fwd_blk = lambda j: jnp.minimum(j, nfb - 1)
    inv_blk = lambda j: jnp.maximum(j - 1, 0)
    wpart = lambda sel: pl.BlockSpec((1, 3, c), lambda bi, o, j: (sel(o), 0, 0))
    bpart = lambda sel: pl.BlockSpec((1, 1, c), lambda bi, o, j: (sel(o), 0, 0))
    value = lambda o: 0
    gate = lambda o: 1 + o
    return pl.pallas_call(
        functools.partial(_hyena_kernel, seq),
        grid=(b, 2, nfb + 1),
        in_specs=[pl.BlockSpec((1, seq, c), lambda bi, o, j: (bi, 0, 0), pipeline_mode=pl.Buffered(1)),
                  pl.BlockSpec((1, seq, c), lambda bi, o, j: (bi, 0, 1 + o)),
                  wpart(value), bpart(value), wpart(gate), bpart(gate),
                  pl.BlockSpec((1, 1, c), lambda bi, o, j: (o, 0, 0)),
                  pl.BlockSpec((1, c), lambda bi, o, j: (0, 0)),
                  pl.BlockSpec((FREQ_BLOCK, seq), lambda bi, o, j: (fwd_blk(j), 0)),
                  pl.BlockSpec((seq, hb), lambda bi, o, j: (0, inv_blk(j))),
                  pl.BlockSpec((seq, hb), lambda bi, o, j: (0, inv_blk(j))),
                  pl.BlockSpec((1, FREQ_BLOCK, c), lambda bi, o, j: (o, fwd_blk(j), 0))],
        out_specs=pl.BlockSpec((1, seq, c), lambda bi, o, j: (bi, 0, 0)),
        out_shape=jax.ShapeDtypeStruct((b, seq, c), BF16),
        scratch_shapes=[pltpu.VMEM((seq, c), F32), pltpu.VMEM((seq, c), BF16), pltpu.VMEM((seq, c), F32),
                        pltpu.VMEM((seq, c), F32), pltpu.VMEM((2, FREQ_BLOCK, c), BF16)],
        compiler_params=pltpu.CompilerParams(dimension_semantics=("parallel", "arbitrary", "arbitrary"),
                                             vmem_limit_bytes=V7X_VMEM_LIMIT_BYTES),
        name="hyena",
    )(hy, hy, conv_w, conv_b, conv_w, conv_b, skip, gain, fmat, fcos_t, fsin_t, hspec)


def _attn_kernel(seq, q_ref, k_ref, v_ref, bias_ref, bias16_ref, o_ref,
                 qs_ref, ks_ref, vs_ref, res_ref, nat_ref, s_ref, e_ref):
    lanes = V7X_LANES
    qscale = LOG2_E / math.sqrt(HEAD_DIM)

    for p, d in enumerate(DILATIONS):
        ls = seq // d
        first = lax.broadcasted_iota(jnp.int32, (ls, lanes), 1) < HEAD_DIM
        for r in range(d):
            src = pl.ds(r, ls, stride=d) if d > 1 else pl.ds(0, ls)
            rows = slice(r * ls, (r + 1) * ls)
            qq = q_ref[0, src, :] * qscale
            qs_ref[p, 0, rows, :] = jnp.where(first, qq, 0.0).astype(BF16)
            qs_ref[p, 1, rows, :] = jnp.where(first, 0.0, qq).astype(BF16)
            ks_ref[p, rows, :] = k_ref[0, src, :].astype(BF16)
            vv = v_ref[0, src, :]
            vs_ref[p, 0, rows, :] = jnp.where(first, vv, 1.0).astype(BF16)
            vs_ref[p, 1, rows, :] = jnp.where(first, 1.0, vv).astype(BF16)

    first = lax.broadcasted_iota(jnp.int32, (Q_BLOCK, lanes), 1) < HEAD_DIM

    def run_pattern(p, nkeys, placement):
        def body(g, carry):
            blocks = [placement(g * ATTN_GROUP + i) for i in range(ATTN_GROUP)]
            for i, (row0, krow0, bias_of_head) in enumerate(blocks):
                kw = ks_ref[p, pl.ds(krow0, nkeys), :]
                for h in range(2):
                    qh = qs_ref[p, h, pl.ds(row0, Q_BLOCK), :]
                    s = lax.dot_general(qh, kw, (((1,), (1,)), ((), ())), preferred_element_type=F32)
                    s_ref[2 * i + h, :, :nkeys] = s + bias_of_head(h)
            for i, (row0, krow0, bias_of_head) in enumerate(blocks):
                ms = []
                for h in range(2):
                    s = s_ref[2 * i + h, :, :nkeys]
                    m = jnp.max(s, axis=-1, keepdims=True)
                    e_ref[2 * i + h, :, :nkeys] = jnp.exp2(s - m).astype(BF16)
                    ms.append(m)
                res_ref[p, 1, pl.ds(row0, Q_BLOCK), :] = jnp.where(first, ms[0], ms[1])
            for i, (row0, krow0, bias_of_head) in enumerate(blocks):
                o0 = _dot(e_ref[2 * i, :, :nkeys], vs_ref[p, 0, pl.ds(krow0, nkeys), :])
                o1 = _dot(e_ref[2 * i + 1, :, :nkeys], vs_ref[p, 1, pl.ds(krow0, nkeys), :])
                res_ref[p, 0, pl.ds(row0, Q_BLOCK), :] = jnp.where(first, o0, o1)
                res_ref[p, 2, pl.ds(row0, Q_BLOCK), :] = jnp.where(first, o1, o0)
            return carry

        lax.fori_loop(0, seq // Q_BLOCK // ATTN_GROUP, body, 0)

    for p, d in enumerate(DILATIONS[:2]):
        ls = seq // d
        nblk = ls // Q_BLOCK

        def banded(n, p=p, ls=ls, nblk=nblk):
            r = n // nblk
            ib = n % nblk
            i0 = ib * Q_BLOCK
            k0 = jnp.clip(i0 - HALF_WINDOW, 0, ls - K_WINDOW)
            case = jnp.where(ib == 0, 0, jnp.where(ib == nblk - 1, 2, 1))
            row0 = pl.multiple_of(r * ls + i0, Q_BLOCK)
            krow0 = pl.multiple_of(r * ls + k0, HALF_WINDOW)
            return row0, krow0, lambda h: bias_ref[0, (p * 3 + case) * 2 + h]

        run_pattern(p, K_WINDOW, banded)

    def full(n):
        row0 = pl.multiple_of(n * Q_BLOCK, Q_BLOCK)
        return row0, row0, lambda h: bias16_ref[0, h]

    run_pattern(2, Q_BLOCK, full)

    for p, d in enumerate(DILATIONS[1:], start=1):
        ls = seq // d
        for r in range(d):
            for kind in range(3):
                nat_ref[p - 1, kind, pl.ds(r, ls, stride=d), :] = res_ref[p, kind, r * ls:(r + 1) * ls, :]

    for t0 in range(0, seq, CONV_ROWS):
        rows = slice(t0, t0 + CONV_ROWS)
        parts = [tuple(res_ref[0, kind, rows, :] for kind in range(3))]
        parts += [tuple(nat_ref[p, kind, rows, :] for kind in range(3)) for p in range(2)]
        m = functools.reduce(jnp.maximum, [pt[1] for pt in parts])
        num = 0.0
        den = 0.0
        for out, mp, lp in parts:
            w = jnp.exp2(mp - m)
            num = num + w * out
            den = den + w * pltpu.roll(lp, HEAD_DIM, axis=1)
        o_ref[0, rows, :] = num / den


def _attn_bias_tables():
    slopes = np.array([2.0 ** (-8.0 * (i + 1) / N_HEADS) for i in range(N_HEADS)], np.float32)
    slopes = jnp.asarray(slopes.reshape(N_HEADS // 2, 1, 2, 1, 1))
    qi = lax.broadcasted_iota(jnp.int32, (Q_BLOCK, K_WINDOW), 0)
    kj = lax.broadcasted_iota(jnp.int32, (Q_BLOCK, K_WINDOW), 1)
    offsets = (0, -HALF_WINDOW, -2 * HALF_WINDOW)
    dist = jnp.stack([jnp.abs(kj - qi + off) for off in offsets])
    valid = dist <= HALF_WINDOW
    dil = jnp.asarray(np.array(DILATIONS[:2], np.float32).reshape(2, 1, 1, 1))
    scaled = (dil * dist.astype(F32)[None])[None, :, :, None]
    banded = jnp.where(valid[None, None, :, None], -slopes[:, None] * scaled, NEG_INF)
    banded = banded.reshape(N_HEADS // 2, 12, Q_BLOCK, K_WINDOW)
    d16 = dist[0, :, :Q_BLOCK]
    full = jnp.where(d16 <= HALF_WINDOW, -slopes[:, 0] * (DILATIONS[2] * d16.astype(F32)), NEG_INF)
    to_base2 = lambda t: jnp.where(t > 0.5 * NEG_INF, t * LOG2_E, NEG_INF).astype(F32)
    return to_base2(banded), to_base2(full)


def _dil_attn(q, k, v):
    b, seq, _ = q.shape
    nhp = N_HEADS // 2
    bias, bias16 = _attn_bias_tables()
    head_pair = pl.BlockSpec((1, seq, V7X_LANES), lambda bi, hp: (bi, 0, hp))
    return pl.pallas_call(
        functools.partial(_attn_kernel, seq),
        grid=(b, nhp),
        in_specs=[head_pair, head_pair, head_pair,
                  pl.BlockSpec((1, 12, Q_BLOCK, K_WINDOW), lambda bi, hp: (hp, 0, 0, 0)),
                  pl.BlockSpec((1, 2, Q_BLOCK, Q_BLOCK), lambda bi, hp: (hp, 0, 0, 0))],
        out_specs=head_pair,
        out_shape=jax.ShapeDtypeStruct((b, seq, D_ATTN), F32),
        scratch_shapes=[pltpu.VMEM((3, 2, seq, V7X_LANES), BF16),
                        pltpu.VMEM((3, seq, V7X_LANES), BF16),
                        pltpu.VMEM((3, 2, seq, V7X_LANES), BF16),
                        pltpu.VMEM((3, 3, seq, V7X_LANES), F32),
                        pltpu.VMEM((2, 3, seq, V7X_LANES), F32),
                        pltpu.VMEM((2 * ATTN_GROUP, Q_BLOCK, K_WINDOW), F32),
                        pltpu.VMEM((2 * ATTN_GROUP, Q_BLOCK, K_WINDOW), BF16)],
        compiler_params=pltpu.CompilerParams(dimension_semantics=("parallel", "parallel"),
                                             vmem_limit_bytes=V7X_VMEM_LIMIT_BYTES),
        name="dil_attn",
    )(q, k, v, bias, bias16)


def _out_ffn2_kernel(x1_ref, yh_ref, ya_ref, ga_ref, wo_ref, g3_ref, wg_ref, wu_ref, wd_ref, gf_ref,
                     out_ref, act_ref):
    ya = _rms(ya_ref[...], ga_ref[...]).astype(BF16)
    mix = _dot(yh_ref[...], wo_ref[:D_HYENA, :]) + _dot(ya, wo_ref[D_HYENA:, :])
    x2 = x1_ref[...] + mix
    h = _rms(x2, g3_ref[...]).astype(BF16)
    x3 = x2 + 0.5 * _swiglu(h, wg_ref, wu_ref, wd_ref, act_ref)
    out_ref[...] = _rms(x3, gf_ref[...])


def _out_ffn2(x1, yh, ya, ga, wo, g3, wg, wu, wd, gf):
    n = x1.shape[0]
    row = lambda w: pl.BlockSpec((ROW_TILE, w), lambda i: (i, 0))
    return pl.pallas_call(
        _out_ffn2_kernel,
        grid=(n // ROW_TILE,),
        in_specs=[row(D_MODEL), row(D_HYENA), row(D_ATTN), _resident((1, D_ATTN)), _resident(wo.shape),
                  _resident((1, D_MODEL)), _resident(wg.shape), _resident(wu.shape), _resident(wd.shape),
                  _resident((1, D_MODEL))],
        out_specs=row(D_MODEL),
        out_shape=jax.ShapeDtypeStruct((n, D_MODEL), F32),
        scratch_shapes=[pltpu.VMEM((ROW_TILE, D_FF), BF16)],
        compiler_params=pltpu.CompilerParams(dimension_semantics=("parallel",),
                                             vmem_limit_bytes=V7X_VMEM_LIMIT_BYTES),
        name="out_ffn2",
    )(x1, yh, ya, ga, wo, g3, wg, wu, wd, gf)


def _dft_matrices(seq):
    n = 2 * seq
    split = 64
    k = np.arange(seq, dtype=np.int64)[:, None]
    ang = lambda s: np.pi * (((2 * k + 1) * s) % (2 * n)) / n
    a1 = ang(split * np.arange(seq // split, dtype=np.int64)[None, :])
    a0 = ang(np.arange(split, dtype=np.int64)[None, :])
    tab = lambda x: jnp.asarray(x.astype(np.float32))
    hb = FREQ_BLOCK // 2
    c1, s1 = tab(np.cos(a1))[:, :, None], tab(np.sin(a1))[:, :, None]
    c0, s0 = tab(np.cos(a0))[:, None, :], tab(np.sin(a0))[:, None, :]
    cos = (c1 * c0 - s1 * s0).reshape(seq // hb, hb, seq)
    nsin = (-(s1 * c0 + c1 * s0)).reshape(seq // hb, hb, seq)
    fmat = jnp.concatenate([cos, nsin], axis=1).reshape(n, seq).astype(BF16)
    c1, s1 = tab(np.cos(a1).T)[:, None, :], tab(np.sin(a1).T)[:, None, :]
    c0, s0 = tab(np.cos(a0).T)[None, :, :], tab(np.sin(a0).T)[None, :, :]
    fcos_t = (c1 * c0 - s1 * s0).reshape(seq, seq).astype(BF16)
    fsin_t = (-(s1 * c0 + c1 * s0)).reshape(seq, seq).astype(BF16)
    return fmat, fcos_t, fsin_t


def _filter_features(seq):
    t = jnp.linspace(0.0, 1.0, seq, dtype=F32)[:, None]
    w = 2.0 * math.pi * jnp.arange(seq, dtype=F32)[:, None] / seq
    f = jnp.linspace(1e-4, FILTER_BANDS - 1, FILTER_BANDS, dtype=F32)[None, :]
    z = jnp.concatenate([t, jnp.cos(f * w), -jnp.sin(f * w)], axis=-1)
    return jnp.pad(z, ((0, 0), (0, FEAT_PAD - FILTER_EMB)))


def _decay_rates():
    max_decay = math.log(DECAY_TARGET) / FAST_DECAY_PCT
    min_decay = math.log(DECAY_TARGET) / SLOW_DECAY_PCT
    return jnp.linspace(min_decay, max_decay, D_HYENA, dtype=F32)[None, :]


def kernel(x, ffn1_norm_g, ffn1_w_gate, ffn1_w_up, ffn1_w_down, mix_norm_g, w_in, hy_conv_w, hy_conv_b, hy_filt_w1, hy_filt_b1, hy_filt_w2, hy_filt_b2, hy_filt_w3, hy_filt_b3, hy_filt_w_out, hy_filt_freq, hy_filt_skip, hy_out_norm_g, attn_out_norm_g, w_out, ffn2_norm_g, ffn2_w_gate, ffn2_w_up, ffn2_w_down, final_norm_g):
    b, seq, d = x.shape
    assert d == D_MODEL and (b * seq) % ROW_TILE == 0 and seq % (DILATIONS[-1] * Q_BLOCK) == 0
    row = lambda a: a.reshape(1, -1).astype(F32)
    bf = lambda a: a.astype(BF16)

    x1, hy, q, k, v = _ffn1_proj(x.reshape(b * seq, d), row(ffn1_norm_g), bf(ffn1_w_gate), bf(ffn1_w_up),
                                 bf(ffn1_w_down), row(mix_norm_g), bf(w_in))

    fmat, fcos_t, fsin_t = _dft_matrices(seq)
    w1 = jnp.pad(hy_filt_w1.astype(F32), ((0, FEAT_PAD - FILTER_EMB), (0, 0)))
    hspec = _hy_filter(seq, _filter_features(seq), w1, row(hy_filt_b1), hy_filt_w2.astype(F32),
                       row(hy_filt_b2), hy_filt_w3.astype(F32), row(hy_filt_b3),
                       hy_filt_w_out.astype(F32), row(hy_filt_freq), _decay_rates(), fmat)

    conv_w = hy_conv_w.astype(F32).reshape(3, 3, D_HYENA).transpose(1, 0, 2)
    conv_b = hy_conv_b.astype(F32).reshape(3, 1, D_HYENA)
    skip = hy_filt_skip.astype(F32).reshape(2, 1, D_HYENA)
    y_hy = _hyena(hy.reshape(b, seq, 3 * D_HYENA), conv_w, conv_b, skip, row(hy_out_norm_g),
                  fmat, fcos_t, fsin_t, hspec)

    shape3 = lambda a: a.reshape(b, seq, D_ATTN)
    y_at = _dil_attn(shape3(q), shape3(k), shape3(v))

    out = _out_ffn2(x1, y_hy.reshape(b * seq, D_HYENA), y_at.reshape(b * seq, D_ATTN),
                    row(attn_out_norm_g), bf(w_out), row(ffn2_norm_g), bf(ffn2_w_gate), bf(ffn2_w_up),
                    bf(ffn2_w_down), row(final_norm_g))
    return out.reshape(b, seq, d)
```

```python
import functools
import math

import numpy as np
import jax
import jax.numpy as jnp
from jax import lax
from jax.experimental import pallas as pl
from jax.experimental.pallas import tpu as pltpu

F32 = jnp.float32
BF16 = jnp.bfloat16

D_MODEL = 1024
D_HYENA = 512
D_ATTN = 512
HEAD_DIM = 64
N_HEADS = D_ATTN // HEAD_DIM
D_FF = 2816
FILTER_EMB = 33
FILTER_BANDS = 16
FILTER_WIDTH = 64
DECAY_TARGET = 1e-2
FAST_DECAY_PCT = 0.3
SLOW_DECAY_PCT = 1.5
DILATIONS = (1, 4, 16)
HALF_WINDOW = 64
RMS_EPS = 1e-6
NEG_INF = -1e30
LOG2_E = math.log2(math.e)

V7X_LANES = 128
V7X_SUBLANES = 8
V7X_BF16_SUBLANES = 16
V7X_VMEM_LIMIT_BYTES = 56 * 1024 * 1024

ROW_TILE = 512
FF_CHUNK = 256
FREQ_BLOCK = 512
Q_BLOCK = 128
K_WINDOW = 256
ATTN_GROUP = 8
CONV_ROWS = 256
FEAT_PAD = 128


def _dot(a, b):
    return jnp.dot(a, b, preferred_element_type=F32)


def _rms(x, g):
    return x * lax.rsqrt(jnp.mean(x * x, axis=-1, keepdims=True) + RMS_EPS) * g


def _swiglu(h, wg_ref, wu_ref, wd_ref, act_ref):
    for c in range(D_FF // FF_CHUNK):
        cols = slice(c * FF_CHUNK, (c + 1) * FF_CHUNK)
        g = _dot(h, wg_ref[:, cols])
        u = _dot(h, wu_ref[:, cols])
        act_ref[:, cols] = (g * jax.nn.sigmoid(g) * u).astype(BF16)
    return _dot(act_ref[...], wd_ref[...])


def _resident(shape):
    return pl.BlockSpec(shape, lambda *_: (0,) * len(shape), pipeline_mode=pl.Buffered(1))


def _cast_rider(weights, steps):
    in_specs, out_specs, out_shapes = [], [], []
    for w in weights:
        rows, cols = w.shape
        visits = 1
        while (rows * visits) % steps or (rows * visits // steps) % V7X_BF16_SUBLANES:
            visits *= 2
        slab = pl.BlockSpec((rows * visits // steps, cols), lambda i, visits=visits: (i // visits, 0))
        in_specs.append(slab)
        out_specs.append(slab)
        out_shapes.append(jax.ShapeDtypeStruct(w.shape, BF16))
    return in_specs, out_specs, out_shapes


def _cast_slabs(src_refs, dst_refs):
    for src, dst in zip(src_refs, dst_refs):
        dst[...] = src[...].astype(BF16)


def _ffn1_proj_kernel(x_ref, g1_ref, wg_ref, wu_ref, wd_ref, g2_ref, win_ref, *rest):
    n_cast = (len(rest) - 6) // 2
    x1_ref, hy_ref, q_ref, k_ref, v_ref = rest[n_cast:n_cast + 5]
    act_ref = rest[-1]
    _cast_slabs(rest[:n_cast], rest[n_cast + 5:-1])
    x = x_ref[...]
    h = _rms(x, g1_ref[...]).astype(BF16)
    x1 = x + 0.5 * _swiglu(h, wg_ref, wu_ref, wd_ref, act_ref)
    x1_ref[...] = x1
    h2 = _rms(x1, g2_ref[...]).astype(BF16)
    nh = 3 * D_HYENA
    hy_ref[...] = _dot(h2, win_ref[:, :nh])
    q_ref[...] = _dot(h2, win_ref[:, nh:nh + D_ATTN])
    k_ref[...] = _dot(h2, win_ref[:, nh + D_ATTN:nh + 2 * D_ATTN])
    v_ref[...] = _dot(h2, win_ref[:, nh + 2 * D_ATTN:])


def _ffn1_proj(x2d, g1, wg, wu, wd, g2, win, later_weights):
    n = x2d.shape[0]
    steps = n // ROW_TILE
    row = lambda w: pl.BlockSpec((ROW_TILE, w), lambda i: (i, 0))
    cast_in, cast_out, cast_shapes = _cast_rider(later_weights, steps)
    return pl.pallas_call(
        _ffn1_proj_kernel,
        grid=(steps,),
        in_specs=[row(D_MODEL), _resident((1, D_MODEL)), _resident(wg.shape), _resident(wu.shape),
                  _resident(wd.shape), _resident((1, D_MODEL)), _resident(win.shape)] + cast_in,
        out_specs=[row(D_MODEL), row(3 * D_HYENA), row(D_ATTN), row(D_ATTN), row(D_ATTN)] + cast_out,
        out_shape=[jax.ShapeDtypeStruct((n, D_MODEL), F32),
                   jax.ShapeDtypeStruct((n, 3 * D_HYENA), F32),
                   jax.ShapeDtypeStruct((n, D_ATTN), F32),
                   jax.ShapeDtypeStruct((n, D_ATTN), F32),
                   jax.ShapeDtypeStruct((n, D_ATTN), F32)] + cast_shapes,
        scratch_shapes=[pltpu.VMEM((ROW_TILE, D_FF), BF16)],
        compiler_params=pltpu.CompilerParams(dimension_semantics=("arbitrary",),
                                             vmem_limit_bytes=V7X_VMEM_LIMIT_BYTES),
        name="ffn1_proj",
    )(x2d, g1, wg, wu, wd, g2, win, *later_weights)


def _filter_kernel(seq, feat_ref, w1_ref, b1_ref, w2_ref, b2_ref, w3_ref, b3_ref, wout_ref,
                   freq_ref, delta_ref, f_ref, *rest):
    n_cast = (len(rest) - 2) // 2
    h_ref, hcat_ref = rest[n_cast], rest[-1]
    _cast_slabs(rest[:n_cast], rest[n_cast + 1:-1])
    c_ = D_HYENA

    @pl.when(pl.program_id(0) == 0)
    def _():
        hi = lax.Precision.HIGHEST
        dot_hi = lambda a, b: jnp.dot(a, b, precision=hi, preferred_element_type=F32)
        freq = freq_ref[...]
        h = jnp.sin(freq * (dot_hi(feat_ref[...], w1_ref[...]) + b1_ref[...]))
        h = jnp.sin(freq * (dot_hi(h, w2_ref[...]) + b2_ref[...]))
        h = jnp.sin(freq * (dot_hi(h, w3_ref[...]) + b3_ref[...]))
        t = lax.broadcasted_iota(jnp.int32, (seq, 1), 0).astype(F32) * (1.0 / (seq - 1))
        decay = jnp.exp(-t * jnp.abs(delta_ref[...]))
        row = lax.broadcasted_iota(jnp.int32, (seq, c_), 0)
        for c in range(4):
            kc = dot_hi(h, wout_ref[:, c * c_:(c + 1) * c_]) * decay
            if c % 2 == 1:
                kc = jnp.where(row == 0, 0.0, kc)
            hcat_ref[:, c * c_:(c + 1) * c_] = kc.astype(BF16)

    p = _dot(f_ref[...], hcat_ref[...])
    hb = FREQ_BLOCK // 2
    scale = 1.0 / seq
    for o in range(2):
        pf = p[:, (2 * o) * c_:(2 * o + 1) * c_]
        pb = p[:, (2 * o + 1) * c_:(2 * o + 2) * c_]
        h_ref[o, :hb, :] = (pf[:hb] + pb[:hb]) * scale
        h_ref[o, hb:, :] = (pf[hb:] - pb[hb:]) * scale


def _hy_filter(seq, feat, w1, b1, w2, b2, w3, b3, wout, freq, delta, fmat, later_weights):
    full = lambda a: _resident(a.shape)
    steps = 2 * seq // FREQ_BLOCK
    cast_in, cast_out, cast_shapes = _cast_rider(later_weights, steps)
    return pl.pallas_call(
        functools.partial(_filter_kernel, seq),
        grid=(steps,),
        in_specs=[full(feat), full(w1), full(b1), full(w2), full(b2), full(w3), full(b3), full(wout),
                  full(freq), full(delta), pl.BlockSpec((FREQ_BLOCK, seq), lambda j: (j, 0))] + cast_in,
        out_specs=[pl.BlockSpec((2, FREQ_BLOCK, D_HYENA), lambda j: (0, j, 0))] + cast_out,
        out_shape=[jax.ShapeDtypeStruct((2, 2 * seq, D_HYENA), F32)] + cast_shapes,
        scratch_shapes=[pltpu.VMEM((seq, 4 * D_HYENA), BF16)],
        compiler_params=pltpu.CompilerParams(dimension_semantics=("arbitrary",),
                                             vmem_limit_bytes=V7X_VMEM_LIMIT_BYTES),
        name="hy_filter",
    )(feat, w1, b1, w2, b2, w3, b3, wout, freq, delta, fmat, *later_weights)


def _short_conv_rows(u_ref, w_ref, b_ref, t0, seq):
    cur = u_ref[0, pl.ds(t0, CONV_ROWS), :]
    row = lax.broadcasted_iota(jnp.int32, cur.shape, 0)
    if isinstance(t0, int):
        lo, hi = max(t0 - V7X_SUBLANES, 0), min(t0 + CONV_ROWS, seq - V7X_SUBLANES)
    else:
        lo = pl.multiple_of(jnp.maximum(t0 - V7X_SUBLANES, 0), V7X_SUBLANES)
        hi = pl.multiple_of(jnp.minimum(t0 + CONV_ROWS, seq - V7X_SUBLANES), V7X_SUBLANES)
    before = u_ref[0, pl.ds(lo, V7X_SUBLANES), :][V7X_SUBLANES - 1:]
    after = u_ref[0, pl.ds(hi, V7X_SUBLANES), :][:1]
    before = jnp.where(t0 > 0, before, 0.0)
    after = jnp.where(t0 + CONV_ROWS < seq, after, 0.0)
    prev = jnp.where(row == 0, before, pltpu.roll(cur, 1, axis=0))
    nxt = jnp.where(row == CONV_ROWS - 1, after, pltpu.roll(cur, CONV_ROWS - 1, axis=0))
    return b_ref[0] + prev * w_ref[0, 0:1, :] + cur * w_ref[0, 1:2, :] + nxt * w_ref[0, 2:3, :]


def _hyena_kernel(seq, v_ref, gate_ref, wv_ref, bv_ref, wg_ref, bg_ref, skip_ref, gain_ref,
                  f_ref, fct_ref, fst_ref, h_ref, out_ref, zf_ref, zb_ref, acc_ref, g_ref, y_ref):
    o = pl.program_id(1)
    j = pl.program_id(2)
    nfb = 2 * seq // FREQ_BLOCK
    chunks = range(0, seq, CONV_ROWS)
    hb = FREQ_BLOCK // 2

    def forward(slot):
        zfreq = _dot(f_ref[...], zb_ref[...])
        zr, zi = zfreq[:hb], zfreq[hb:]
        hr, hi = h_ref[0, :hb, :], h_ref[0, hb:, :]
        y_ref[slot, :hb, :] = (zr * hr - zi * hi).astype(BF16)
        y_ref[slot, hb:, :] = (zr * hi + zi * hr).astype(BF16)

    def inverse(slot):
        acc_ref[...] += _dot(fct_ref[0], y_ref[slot, :hb, :]) + _dot(fst_ref[0], y_ref[slot, hb:, :])

    def gate_chunk():
        t0 = pl.multiple_of(j * CONV_ROWS, CONV_ROWS)
        g_ref[pl.ds(t0, CONV_ROWS), :] = _short_conv_rows(gate_ref, wg_ref, bg_ref, t0, seq)

    @pl.when((o == 0) & (j == 0))
    def _():
        for t0 in chunks:
            z = _short_conv_rows(v_ref, wv_ref, bv_ref, t0, seq)
            zf_ref[t0:t0 + CONV_ROWS, :] = z
            zb_ref[t0:t0 + CONV_ROWS, :] = z.astype(BF16)

    @pl.when(j == 0)
    def _():
        gate_chunk()
        acc_ref[...] = jnp.zeros_like(acc_ref)
        forward(0)

    @pl.when((j > 0) & (j < nfb))
    def _():
        gate_chunk()
        inverse(1 - j % 2)
        forward(j % 2)

    def gated(rows):
        return g_ref[rows, :] * (acc_ref[rows, :] + zf_ref[rows, :] * skip_ref[0])

    last_slot = (nfb - 1) % 2

    @pl.when((j == nfb) & (o == 0))
    def _():
        inverse(last_slot)
        for t0 in chunks:
            rows = slice(t0, t0 + CONV_ROWS)
            z = gated(rows)
            zf_ref[rows, :] = z
            zb_ref[rows, :] = z.astype(BF16)

    @pl.when((j == nfb) & (o == 1))
    def _():
        inverse(last_slot)
        for t0 in chunks:
            rows = slice(t0, t0 + CONV_ROWS)
            out_ref[0, rows, :] = _rms(gated(rows), gain_ref[...]).astype(BF16)


def _hyena(hy, conv_w, conv_b, skip, gain, fmat, fcos_t, fsin_t, hspec):
    b, seq, _ = hy.shape
    c = D_HYENA
    hb = FREQ_BLOCK // 2
    nfb = 2 * seq // FREQ_BLOCK
    assert seq // CONV_ROWS == nfb
    fwd_blk = lambda j: jnp.minimum(j, nfb - 1)
    inv_blk = lambda j: jnp.maximum(j - 1, 0)
    wpart = lambda sel: pl.BlockSpec((1, 3, c), lambda bi, o, j: (sel(o), 0, 0))
    bpart = lambda sel: pl.BlockSpec((1, 1, c), lambda bi, o, j: (sel(o), 0, 0))
    value = lambda o: 0
    gate = lambda o: 1 + o
    return pl.pallas_call(
        functools.partial(_hyena_kernel, seq),
        grid=(b, 2, nfb + 1),
        in_specs=[pl.BlockSpec((1, seq, c), lambda bi, o, j: (bi, 0, 0), pipeline_mode=pl.Buffered(1)),
                  pl.BlockSpec((1, seq, c), lambda bi, o, j: (bi, 0, 1 + o)),
                  wpart(value), bpart(value), wpart(gate), bpart(gate),
                  pl.BlockSpec((1, 1, c), lambda bi, o, j: (o, 0, 0)),
                  pl.BlockSpec((1, c), lambda bi, o, j: (0, 0)),
                  pl.BlockSpec((FREQ_BLOCK, seq), lambda bi, o, j: (fwd_blk(j), 0)),
                  pl.BlockSpec((1, seq, hb), lambda bi, o, j: (inv_blk(j), 0, 0)),
                  pl.BlockSpec((1, seq, hb), lambda bi, o, j: (inv_blk(j), 0, 0)),
                  pl.BlockSpec((1, FREQ_BLOCK, c), lambda bi, o, j: (o, fwd_blk(j), 0))],
        out_specs=pl.BlockSpec((1, seq, c), lambda bi, o, j: (bi, 0, 0)),
        out_shape=jax.ShapeDtypeStruct((b, seq, c), BF16),
        scratch_shapes=[pltpu.VMEM((seq, c), F32), pltpu.VMEM((seq, c), BF16), pltpu.VMEM((seq, c), F32),
                        pltpu.VMEM((seq, c), F32), pltpu.VMEM((2, FREQ_BLOCK, c), BF16)],
        compiler_params=pltpu.CompilerParams(dimension_semantics=("parallel", "arbitrary", "arbitrary"),
                                             vmem_limit_bytes=V7X_VMEM_LIMIT_BYTES),
        name="hyena",
    )(hy, hy, conv_w, conv_b, conv_w, conv_b, skip, gain, fmat, fcos_t, fsin_t, hspec)


def _attn_kernel(seq, q_ref, k_ref, v_ref, bias_ref, bias16_ref, o_ref,
                 qs_ref, ks_ref, vs_ref, res_ref, nat_ref, s_ref, e_ref):
    lanes = V7X_LANES
    qscale = LOG2_E / math.sqrt(HEAD_DIM)

    for p, d in enumerate(DILATIONS):
        ls = seq // d
        first = lax.broadcasted_iota(jnp.int32, (ls, lanes), 1) < HEAD_DIM
        for r in range(d):
            src = pl.ds(r, ls, stride=d) if d > 1 else pl.ds(0, ls)
            rows = slice(r * ls, (r + 1) * ls)
            qq = q_ref[0, src, :] * qscale
            qs_ref[p, 0, rows, :] = jnp.where(first, qq, 0.0).astype(BF16)
            qs_ref[p, 1, rows, :] = jnp.where(first, 0.0, qq).astype(BF16)
            ks_ref[p, rows, :] = k_ref[0, src, :].astype(BF16)
            vv = v_ref[0, src, :]
            vs_ref[p, 0, rows, :] = jnp.where(first, vv, 1.0).astype(BF16)
            vs_ref[p, 1, rows, :] = jnp.where(first, 1.0, vv).astype(BF16)

    first = lax.broadcasted_iota(jnp.int32, (Q_BLOCK, lanes), 1) < HEAD_DIM

    def run_pattern(p, nkeys, placement):
        def body(g, carry):
            blocks = [placement(g * ATTN_GROUP + i) for i in range(ATTN_GROUP)]
            for i, (row0, krow0, bias_of_head) in enumerate(blocks):
                kw = ks_ref[p, pl.ds(krow0, nkeys), :]
                for h in range(2):
                    qh = qs_ref[p, h, pl.ds(row0, Q_BLOCK), :]
                    s = lax.dot_general(qh, kw, (((1,), (1,)), ((), ())), preferred_element_type=F32)
                    s_ref[2 * i + h, :, :nkeys] = s + bias_of_head(h)
            for i, (row0, krow0, bias_of_head) in enumerate(blocks):
                ms = []
                for h in range(2):
                    s = s_ref[2 * i + h, :, :nkeys]
                    m = jnp.max(s, axis=-1, keepdims=True)
                    e_ref[2 * i + h, :, :nkeys] = jnp.exp2(s - m).astype(BF16)
                    ms.append(m)
                res_ref[p, 1, pl.ds(row0, Q_BLOCK), :] = jnp.where(first, ms[0], ms[1])
            for i, (row0, krow0, bias_of_head) in enumerate(blocks):
                o0 = _dot(e_ref[2 * i, :, :nkeys], vs_ref[p, 0, pl.ds(krow0, nkeys), :])
                o1 = _dot(e_ref[2 * i + 1, :, :nkeys], vs_ref[p, 1, pl.ds(krow0, nkeys), :])
                res_ref[p, 0, pl.ds(row0, Q_BLOCK), :] = jnp.where(first, o0, o1)
                res_ref[p, 2, pl.ds(row0, Q_BLOCK), :] = jnp.where(first, o1, o0)
            return carry

        lax.fori_loop(0, seq // Q_BLOCK // ATTN_GROUP, body, 0)

    for p, d in enumerate(DILATIONS[:2]):
        ls = seq // d
        nblk = ls // Q_BLOCK

        def banded(n, p=p, ls=ls, nblk=nblk):
            r = n // nblk
            ib = n % nblk
            i0 = ib * Q_BLOCK
            k0 = jnp.clip(i0 - HALF_WINDOW, 0, ls - K_WINDOW)
            case = jnp.where(ib == 0, 0, jnp.where(ib == nblk - 1, 2, 1))
            row0 = pl.multiple_of(r * ls + i0, Q_BLOCK)
            krow0 = pl.multiple_of(r * ls + k0, HALF_WINDOW)
            return row0, krow0, lambda h: bias_ref[0, (p * 3 + case) * 2 + h]

        run_pattern(p, K_WINDOW, banded)

    def full(n):
        row0 = pl.multiple_of(n * Q_BLOCK, Q_BLOCK)
        return row0, row0, lambda h: bias16_ref[0, h]

    run_pattern(2, Q_BLOCK, full)

    for p, d in enumerate(DILATIONS[1:], start=1):
        ls = seq // d
        for r in range(d):
            for kind in range(3):
                nat_ref[p - 1, kind, pl.ds(r, ls, stride=d), :] = res_ref[p, kind, r * ls:(r + 1) * ls, :]

    for t0 in range(0, seq, CONV_ROWS):
        rows = slice(t0, t0 + CONV_ROWS)
        parts = [tuple(res_ref[0, kind, rows, :] for kind in range(3))]
        parts += [tuple(nat_ref[p, kind, rows, :] for kind in range(3)) for p in range(2)]
        m = functools.reduce(jnp.maximum, [pt[1] for pt in parts])
        num = 0.0
        den = 0.0
        for out, mp, lp in parts:
            w = jnp.exp2(mp - m)
            num = num + w * out
            den = den + w * pltpu.roll(lp, HEAD_DIM, axis=1)
        o_ref[0, rows, :] = num / den


def _attn_bias_tables():
    slopes = np.array([2.0 ** (-8.0 * (i + 1) / N_HEADS) for i in range(N_HEADS)], np.float32)
    slopes = jnp.asarray(slopes.reshape(N_HEADS // 2, 1, 2, 1, 1))
    qi = lax.broadcasted_iota(jnp.int32, (Q_BLOCK, K_WINDOW), 0)
    kj = lax.broadcasted_iota(jnp.int32, (Q_BLOCK, K_WINDOW), 1)
    offsets = (0, -HALF_WINDOW, -2 * HALF_WINDOW)
    dist = jnp.stack([jnp.abs(kj - qi + off) for off in offsets])
    valid = dist <= HALF_WINDOW
    dil = jnp.asarray(np.array(DILATIONS[:2], np.float32).reshape(2, 1, 1, 1))
    scaled = (dil * dist.astype(F32)[None])[None, :, :, None]
    banded = jnp.where(valid[None, None, :, None], -slopes[:, None] * scaled, NEG_INF)
    banded = banded.reshape(N_HEADS // 2, 12, Q_BLOCK, K_WINDOW)
    d16 = dist[0, :, :Q_BLOCK]
    full = jnp.where(d16 <= HALF_WINDOW, -slopes[:, 0] * (DILATIONS[2] * d16.astype(F32)), NEG_INF)
    to_base2 = lambda t: jnp.where(t > 0.5 * NEG_INF, t * LOG2_E, NEG_INF).astype(F32)
    return to_base2(banded), to_base2(full)


def _dil_attn(q, k, v):
    b, seq, _ = q.shape
    nhp = N_HEADS // 2
    bias, bias16 = _attn_bias_tables()
    head_pair = pl.BlockSpec((1, seq, V7X_LANES), lambda bi, hp: (bi, 0, hp))
    return pl.pallas_call(
        functools.partial(_attn_kernel, seq),
        grid=(b, nhp),
        in_specs=[head_pair, head_pair, head_pair,
                  pl.BlockSpec((1, 12, Q_BLOCK, K_WINDOW), lambda bi, hp: (hp, 0, 0, 0)),
                  pl.BlockSpec((1, 2, Q_BLOCK, Q_BLOCK), lambda bi, hp: (hp, 0, 0, 0))],
        out_specs=head_pair,
        out_shape=jax.ShapeDtypeStruct((b, seq, D_ATTN), F32),
        scratch_shapes=[pltpu.VMEM((3, 2, seq, V7X_LANES), BF16),
                        pltpu.VMEM((3, seq, V7X_LANES), BF16),
                        pltpu.VMEM((3, 2, seq, V7X_LANES), BF16),
                        pltpu.VMEM((3, 3, seq, V7X_LANES), F32),
                        pltpu.VMEM((2, 3, seq, V7X_LANES), F32),
                        pltpu.VMEM((2 * ATTN_GROUP, Q_BLOCK, K_WINDOW), F32),
                        pltpu.VMEM((2 * ATTN_GROUP, Q_BLOCK, K_WINDOW), BF16)],
        compiler_params=pltpu.CompilerParams(dimension_semantics=("parallel", "parallel"),
                                             vmem_limit_bytes=V7X_VMEM_LIMIT_BYTES),
        name="dil_attn",
    )(q, k, v, bias, bias16)


def _out_ffn2_kernel(x1_ref, yh_ref, ya_ref, ga_ref, wo_ref, g3_ref, wg_ref, wu_ref, wd_ref, gf_ref,
                     out_ref, act_ref):
    ya = _rms(ya_ref[...], ga_ref[...]).astype(BF16)
    mix = _dot(yh_ref[...], wo_ref[:D_HYENA, :]) + _dot(ya, wo_ref[D_HYENA:, :])
    x2 = x1_ref[...] + mix
    h = _rms(x2, g3_ref[...]).astype(BF16)
    x3 = x2 + 0.5 * _swiglu(h, wg_ref, wu_ref, wd_ref, act_ref)
    out_ref[...] = _rms(x3, gf_ref[...])


def _out_ffn2(x1, yh, ya, ga, wo, g3, wg, wu, wd, gf):
    n = x1.shape[0]
    row = lambda w: pl.BlockSpec((ROW_TILE, w), lambda i: (i, 0))
    return pl.pallas_call(
        _out_ffn2_kernel,
        grid=(n // ROW_TILE,),
        in_specs=[row(D_MODEL), row(D_HYENA), row(D_ATTN), _resident((1, D_ATTN)), _resident(wo.shape),
                  _resident((1, D_MODEL)), _resident(wg.shape), _resident(wu.shape), _resident(wd.shape),
                  _resident((1, D_MODEL))],
        out_specs=row(D_MODEL),
        out_shape=jax.ShapeDtypeStruct((n, D_MODEL), F32),
        scratch_shapes=[pltpu.VMEM((ROW_TILE, D_FF), BF16)],
        compiler_params=pltpu.CompilerParams(dimension_semantics=("parallel",),
                                             vmem_limit_bytes=V7X_VMEM_LIMIT_BYTES),
        name="out_ffn2",
    )(x1, yh, ya, ga, wo, g3, wg, wu, wd, gf)


def _dft_matrices(seq):
    n = 2 * seq
    split = 64
    k = np.arange(seq, dtype=np.int64)[:, None]
    ang = lambda s: np.pi * (((2 * k + 1) * s) % (2 * n)) / n
    a1 = ang(split * np.arange(seq // split, dtype=np.int64)[None, :])
    a0 = ang(np.arange(split, dtype=np.int64)[None, :])
    tab = lambda x: jnp.asarray(x.astype(np.float32))
    hb = FREQ_BLOCK // 2
    c1, s1 = tab(np.cos(a1))[:, :, None], tab(np.sin(a1))[:, :, None]
    c0, s0 = tab(np.cos(a0))[:, None, :], tab(np.sin(a0))[:, None, :]
    cos = (c1 * c0 - s1 * s0).reshape(seq // hb, hb, seq)
    nsin = (-(s1 * c0 + c1 * s0)).reshape(seq // hb, hb, seq)
    fmat = jnp.concatenate([cos, nsin], axis=1).reshape(n, seq).astype(BF16)
    blocked = lambda x: tab(x.T.reshape(x.shape[1], seq // hb, hb).transpose(1, 0, 2))
    c1, s1 = blocked(np.cos(a1))[:, :, None, :], blocked(np.sin(a1))[:, :, None, :]
    c0, s0 = blocked(np.cos(a0))[:, None, :, :], blocked(np.sin(a0))[:, None, :, :]
    fcos_t = (c1 * c0 - s1 * s0).reshape(seq // hb, seq, hb).astype(BF16)
    fsin_t = (-(s1 * c0 + c1 * s0)).reshape(seq // hb, seq, hb).astype(BF16)
    return fmat, fcos_t, fsin_t


def _filter_features(seq):
    t = jnp.linspace(0.0, 1.0, seq, dtype=F32)[:, None]
    w = 2.0 * math.pi * jnp.arange(seq, dtype=F32)[:, None] / seq
    f = jnp.linspace(1e-4, FILTER_BANDS - 1, FILTER_BANDS, dtype=F32)[None, :]
    z = jnp.concatenate([t, jnp.cos(f * w), -jnp.sin(f * w)], axis=-1)
    return jnp.pad(z, ((0, 0), (0, FEAT_PAD - FILTER_EMB)))


def _decay_rates():
    max_decay = math.log(DECAY_TARGET) / FAST_DECAY_PCT
    min_decay = math.log(DECAY_TARGET) / SLOW_DECAY_PCT
    return jnp.linspace(min_decay, max_decay, D_HYENA, dtype=F32)[None, :]


def kernel(x, ffn1_norm_g, ffn1_w_gate, ffn1_w_up, ffn1_w_down, mix_norm_g, w_in, hy_conv_w, hy_conv_b, hy_filt_w1, hy_filt_b1, hy_filt_w2, hy_filt_b2, hy_filt_w3, hy_filt_b3, hy_filt_w_out, hy_filt_freq, hy_filt_skip, hy_out_norm_g, attn_out_norm_g, w_out, ffn2_norm_g, ffn2_w_gate, ffn2_w_up, ffn2_w_down, final_norm_g):
    b, seq, d = x.shape
    assert d == D_MODEL and (b * seq) % ROW_TILE == 0 and seq % (DILATIONS[-1] * Q_BLOCK) == 0
    row = lambda a: a.reshape(1, -1).astype(F32)
    f32 = lambda a: a.astype(F32)

    fmat, fcos_t, fsin_t = _dft_matrices(seq)
    w1 = jnp.pad(f32(hy_filt_w1), ((0, FEAT_PAD - FILTER_EMB), (0, 0)))
    hspec, wg1, wu1, wd1, win = _hy_filter(
        seq, _filter_features(seq), w1, row(hy_filt_b1), f32(hy_filt_w2), row(hy_filt_b2), f32(hy_filt_w3),
        row(hy_filt_b3), f32(hy_filt_w_out), row(hy_filt_freq), _decay_rates(), fmat,
        [f32(ffn1_w_gate), f32(ffn1_w_up), f32(ffn1_w_down), f32(w_in)])

    x1, hy, q, k, v, wo, wg2, wu2, wd2 = _ffn1_proj(
        x.reshape(b * seq, d), row(ffn1_norm_g), wg1, wu1, wd1, row(mix_norm_g), win,
        [f32(w_out), f32(ffn2_w_gate), f32(ffn2_w_up), f32(ffn2_w_down)])

    conv_w = hy_conv_w.astype(F32).reshape(3, 3, D_HYENA).transpose(1, 0, 2)
    conv_b = hy_conv_b.astype(F32).reshape(3, 1, D_HYENA)
    skip = hy_filt_skip.astype(F32).reshape(2, 1, D_HYENA)
    y_hy = _hyena(hy.reshape(b, seq, 3 * D_HYENA), conv_w, conv_b, skip, row(hy_out_norm_g),
                  fmat, fcos_t, fsin_t, hspec)

    shape3 = lambda a: a.reshape(b, seq, D_ATTN)
    y_at = _dil_attn(shape3(q), shape3(k), shape3(v))

    out = _out_ffn2(x1, y_hy.reshape(b * seq, D_HYENA), y_at.reshape(b * seq, D_ATTN),
                    row(attn_out_norm_g), wo, row(ffn2_norm_g), wg2, wu2, wd2, row(final_norm_g))
    return out.reshape(b, seq, d)
```

```python
import functools
import math

import numpy as np
import jax
import jax.numpy as jnp
from jax import lax
from jax.experimental import pallas as pl
from jax.experimental.pallas import tpu as pltpu

F32 = jnp.float32
BF16 = jnp.bfloat16

D_MODEL = 1024
D_HYENA = 512
D_ATTN = 512
HEAD_DIM = 64
N_HEADS = D_ATTN // HEAD_DIM
D_FF = 2816
FILTER_EMB = 33
FILTER_BANDS = 16
FILTER_WIDTH = 64
DECAY_TARGET = 1e-2
FAST_DECAY_PCT = 0.3
SLOW_DECAY_PCT = 1.5
DILATIONS = (1, 4, 16)
HALF_WINDOW = 64
RMS_EPS = 1e-6
NEG_INF = -1e30
LOG2_E = math.log2(math.e)

V7X_LANES = 128
V7X_SUBLANES = 8
V7X_BF16_SUBLANES = 16
V7X_VMEM_LIMIT_BYTES = 56 * 1024 * 1024

ROW_TILE = 512
FF_CHUNK = 256
FREQ_BLOCK = 1024
FILTER_ROWS = 512
Q_BLOCK = 128
K_WINDOW = 256
ATTN_GROUP = 8
CONV_ROWS = 256
FEAT_PAD = 128


def _dot(a, b):
    return jnp.dot(a, b, preferred_element_type=F32)


def _rms(x, g):
    return x * lax.rsqrt(jnp.mean(x * x, axis=-1, keepdims=True) + RMS_EPS) * g


def _swiglu(h, wg_ref, wu_ref, wd_ref, act_ref):
    for c in range(D_FF // FF_CHUNK):
        cols = slice(c * FF_CHUNK, (c + 1) * FF_CHUNK)
        g = _dot(h, wg_ref[:, cols])
        u = _dot(h, wu_ref[:, cols])
        act_ref[:, cols] = (g * jax.nn.sigmoid(g) * u).astype(BF16)
    return _dot(act_ref[...], wd_ref[...])


def _resident(shape):
    return pl.BlockSpec(shape, lambda *_: (0,) * len(shape), pipeline_mode=pl.Buffered(1))


def _cast_rider(weights, steps):
    in_specs, out_specs, out_shapes = [], [], []
    for w in weights:
        rows, cols = w.shape
        visits = 1
        while (rows * visits) % steps or (rows * visits // steps) % V7X_BF16_SUBLANES:
            visits *= 2
        slab = pl.BlockSpec((rows * visits // steps, cols), lambda i, visits=visits: (i // visits, 0))
        in_specs.append(slab)
        out_specs.append(slab)
        out_shapes.append(jax.ShapeDtypeStruct(w.shape, BF16))
    return in_specs, out_specs, out_shapes


def _cast_slabs(src_refs, dst_refs):
    for src, dst in zip(src_refs, dst_refs):
        dst[...] = src[...].astype(BF16)


def _ffn1_proj_kernel(x_ref, g1_ref, wg_ref, wu_ref, wd_ref, g2_ref, win_ref, *rest):
    n_cast = (len(rest) - 6) // 2
    x1_ref, hy_ref, q_ref, k_ref, v_ref = rest[n_cast:n_cast + 5]
    act_ref = rest[-1]
    _cast_slabs(rest[:n_cast], rest[n_cast + 5:-1])
    x = x_ref[...]
    h = _rms(x, g1_ref[...]).astype(BF16)
    x1 = x + 0.5 * _swiglu(h, wg_ref, wu_ref, wd_ref, act_ref)
    x1_ref[...] = x1
    h2 = _rms(x1, g2_ref[...]).astype(BF16)
    nh = 3 * D_HYENA
    hy_ref[...] = _dot(h2, win_ref[:, :nh])
    q_ref[...] = _dot(h2, win_ref[:, nh:nh + D_ATTN])
    k_ref[...] = _dot(h2, win_ref[:, nh + D_ATTN:nh + 2 * D_ATTN])
    v_ref[...] = _dot(h2, win_ref[:, nh + 2 * D_ATTN:])


def _ffn1_proj(x2d, g1, wg, wu, wd, g2, win, later_weights):
    n = x2d.shape[0]
    steps = n // ROW_TILE
    row = lambda w: pl.BlockSpec((ROW_TILE, w), lambda i: (i, 0))
    cast_in, cast_out, cast_shapes = _cast_rider(later_weights, steps)
    return pl.pallas_call(
        _ffn1_proj_kernel,
        grid=(steps,),
        in_specs=[row(D_MODEL), _resident((1, D_MODEL)), _resident(wg.shape), _resident(wu.shape),
                  _resident(wd.shape), _resident((1, D_MODEL)), _resident(win.shape)] + cast_in,
        out_specs=[row(D_MODEL), row(3 * D_HYENA), row(D_ATTN), row(D_ATTN), row(D_ATTN)] + cast_out,
        out_shape=[jax.ShapeDtypeStruct((n, D_MODEL), F32),
                   jax.ShapeDtypeStruct((n, 3 * D_HYENA), F32),
                   jax.ShapeDtypeStruct((n, D_ATTN), F32),
                   jax.ShapeDtypeStruct((n, D_ATTN), F32),
                   jax.ShapeDtypeStruct((n, D_ATTN), F32)] + cast_shapes,
        scratch_shapes=[pltpu.VMEM((ROW_TILE, D_FF), BF16)],
        compiler_params=pltpu.CompilerParams(dimension_semantics=("arbitrary",),
                                             vmem_limit_bytes=V7X_VMEM_LIMIT_BYTES),
        name="ffn1_proj",
    )(x2d, g1, wg, wu, wd, g2, win, *later_weights)


def _filter_kernel(seq, feat_ref, w1_ref, b1_ref, w2_ref, b2_ref, w3_ref, b3_ref, wout_ref,
                   freq_ref, delta_ref, f_ref, *rest):
    n_cast = (len(rest) - 2) // 2
    h_ref, hcat_ref = rest[n_cast], rest[-1]
    _cast_slabs(rest[:n_cast], rest[n_cast + 1:-1])
    c_ = D_HYENA

    @pl.when(pl.program_id(0) == 0)
    def _():
        hi = lax.Precision.HIGHEST
        dot_hi = lambda a, b: jnp.dot(a, b, precision=hi, preferred_element_type=F32)
        freq = freq_ref[...]
        h = jnp.sin(freq * (dot_hi(feat_ref[...], w1_ref[...]) + b1_ref[...]))
        h = jnp.sin(freq * (dot_hi(h, w2_ref[...]) + b2_ref[...]))
        h = jnp.sin(freq * (dot_hi(h, w3_ref[...]) + b3_ref[...]))
        t = lax.broadcasted_iota(jnp.int32, (seq, 1), 0).astype(F32) * (1.0 / (seq - 1))
        decay = jnp.exp(-t * jnp.abs(delta_ref[...]))
        row = lax.broadcasted_iota(jnp.int32, (seq, c_), 0)
        for c in range(4):
            kc = dot_hi(h, wout_ref[:, c * c_:(c + 1) * c_]) * decay
            if c % 2 == 1:
                kc = jnp.where(row == 0, 0.0, kc)
            hcat_ref[:, c * c_:(c + 1) * c_] = kc.astype(BF16)

    p = _dot(f_ref[...], hcat_ref[...])
    is_sin = (pl.program_id(0) * FILTER_ROWS // (FREQ_BLOCK // 2)) % 2
    sign = (1 - 2 * is_sin).astype(F32)
    scale = 1.0 / seq
    for o in range(2):
        pf = p[:, (2 * o) * c_:(2 * o + 1) * c_]
        pb = p[:, (2 * o + 1) * c_:(2 * o + 2) * c_]
        h_ref[o] = (pf + sign * pb) * scale


def _hy_filter(seq, feat, w1, b1, w2, b2, w3, b3, wout, freq, delta, fmat, later_weights):
    full = lambda a: _resident(a.shape)
    assert (FREQ_BLOCK // 2) % FILTER_ROWS == 0
    steps = 2 * seq // FILTER_ROWS
    cast_in, cast_out, cast_shapes = _cast_rider(later_weights, steps)
    return pl.pallas_call(
        functools.partial(_filter_kernel, seq),
        grid=(steps,),
        in_specs=[full(feat), full(w1), full(b1), full(w2), full(b2), full(w3), full(b3), full(wout),
                  full(freq), full(delta), pl.BlockSpec((FILTER_ROWS, seq), lambda j: (j, 0))] + cast_in,
        out_specs=[pl.BlockSpec((2, FILTER_ROWS, D_HYENA), lambda j: (0, j, 0))] + cast_out,
        out_shape=[jax.ShapeDtypeStruct((2, 2 * seq, D_HYENA), F32)] + cast_shapes,
        scratch_shapes=[pltpu.VMEM((seq, 4 * D_HYENA), BF16)],
        compiler_params=pltpu.CompilerParams(dimension_semantics=("arbitrary",),
                                             vmem_limit_bytes=V7X_VMEM_LIMIT_BYTES),
        name="hy_filter",
    )(feat, w1, b1, w2, b2, w3, b3, wout, freq, delta, fmat, *later_weights)


def _short_conv(cur, before, after, w_ref, b_ref):
    rows = cur.shape[0]
    row = lax.broadcasted_iota(jnp.int32, cur.shape, 0)
    prev = jnp.where(row == 0, before, pltpu.roll(cur, 1, axis=0))
    nxt = jnp.where(row == rows - 1, after, pltpu.roll(cur, rows - 1, axis=0))
    return b_ref[0] + prev * w_ref[0, 0:1, :] + cur * w_ref[0, 1:2, :] + nxt * w_ref[0, 2:3, :]


def _short_conv_rows(u_ref, w_ref, b_ref, t0, seq):
    zero = jnp.zeros((1, u_ref.shape[2]), F32)
    before = u_ref[0, t0 - V7X_SUBLANES:t0, :][V7X_SUBLANES - 1:] if t0 > 0 else zero
    end = t0 + CONV_ROWS
    after = u_ref[0, end:end + V7X_SUBLANES, :][:1] if end < seq else zero
    return _short_conv(u_ref[0, t0:end, :], before, after, w_ref, b_ref)


def _hyena_kernel(seq, v_ref, gate_ref, gate_lo_ref, gate_hi_ref, wv_ref, bv_ref, wg_ref, bg_ref, skip_ref,
                  gain_ref, f_ref, fct_ref, fst_ref, h_ref, out_ref, zf_ref, zb_ref, acc_ref, g_ref, y_ref):
    o = pl.program_id(1)
    j = pl.program_id(2)
    nfb = 2 * seq // FREQ_BLOCK
    gate_rows = seq // nfb
    chunks = range(0, seq, CONV_ROWS)
    hb = FREQ_BLOCK // 2

    def forward(slot):
        zfreq = _dot(f_ref[...], zb_ref[...])
        zr, zi = zfreq[:hb], zfreq[hb:]
        hr, hi = h_ref[0, :hb, :], h_ref[0, hb:, :]
        y_ref[slot, :hb, :] = (zr * hr - zi * hi).astype(BF16)
        y_ref[slot, hb:, :] = (zr * hi + zi * hr).astype(BF16)

    def inverse(slot):
        acc_ref[...] += _dot(fct_ref[0], y_ref[slot, :hb, :]) + _dot(fst_ref[0], y_ref[slot, hb:, :])

    def gate_chunk():
        before = jnp.where(j > 0, gate_lo_ref[0, V7X_SUBLANES - 1:, :], 0.0)
        after = jnp.where(j < nfb - 1, gate_hi_ref[0, :1, :], 0.0)
        t0 = pl.multiple_of(j * gate_rows, gate_rows)
        g_ref[pl.ds(t0, gate_rows), :] = _short_conv(gate_ref[0], before, after, wg_ref, bg_ref)

    @pl.when((o == 0) & (j == 0))
    def _():
        for t0 in chunks:
            z = _short_conv_rows(v_ref, wv_ref, bv_ref, t0, seq)
            zf_ref[t0:t0 + CONV_ROWS, :] = z
            zb_ref[t0:t0 + CONV_ROWS, :] = z.astype(BF16)

    @pl.when(j == 0)
    def _():
        gate_chunk()
        acc_ref[...] = jnp.zeros_like(acc_ref)
        forward(0)

    @pl.when((j > 0) & (j < nfb))
    def _():
        gate_chunk()
        inverse(1 - j % 2)
        forward(j % 2)

    def gated(rows):
        return g_ref[rows, :] * (acc_ref[rows, :] + zf_ref[rows, :] * skip_ref[0])

    last_slot = (nfb - 1) % 2

    @pl.when((j == nfb) & (o == 0))
    def _():
        inverse(last_slot)
        for t0 in chunks:
            rows = slice(t0, t0 + CONV_ROWS)
            z = gated(rows)
            zf_ref[rows, :] = z
            zb_ref[rows, :] = z.astype(BF16)

    @pl.when((j == nfb) & (o == 1))
    def _():
        inverse(last_slot)
        for t0 in chunks:
            rows = slice(t0, t0 + CONV_ROWS)
            out_ref[0, rows, :] = _rms(gated(rows), gain_ref[...]).astype(BF16)


def _hyena(hy, conv_w, conv_b, skip, gain, fmat, fcos_t, fsin_t, hspec):
    b, seq, _ = hy.shape
    c = D_HYENA
    hb = FREQ_BLOCK // 2
    nfb = 2 * seq // FREQ_BLOCK
    gate_rows = seq // nfb
    halo_per_chunk = gate_rows // V7X_SUBLANES
    fwd_blk = lambda j: jnp.minimum(j, nfb - 1)
    inv_blk = lambda j: jnp.maximum(j - 1, 0)
    wpart = lambda sel: pl.BlockSpec((1, 3, c), lambda bi, o, j: (sel(o), 0, 0))
    bpart = lambda sel: pl.BlockSpec((1, 1, c), lambda bi, o, j: (sel(o), 0, 0))
    value = lambda o: 0
    gate = lambda o: 1 + o
    halo = lambda first: pl.BlockSpec((1, V7X_SUBLANES, c), lambda bi, o, j: (bi, first(fwd_blk(j)), 1 + o))
    return pl.pallas_call(
        functools.partial(_hyena_kernel, seq),
        grid=(b, 2, nfb + 1),
        in_specs=[pl.BlockSpec((1, seq, c), lambda bi, o, j: (bi, 0, 0), pipeline_mode=pl.Buffered(1)),
                  pl.BlockSpec((1, gate_rows, c), lambda bi, o, j: (bi, fwd_blk(j), 1 + o)),
                  halo(lambda jc: jnp.maximum(jc * halo_per_chunk - 1, 0)),
                  halo(lambda jc: jnp.minimum((jc + 1) * halo_per_chunk, seq // V7X_SUBLANES - 1)),
                  wpart(value), bpart(value), wpart(gate), bpart(gate),
                  pl.BlockSpec((1, 1, c), lambda bi, o, j: (o, 0, 0)),
                  pl.BlockSpec((1, c), lambda bi, o, j: (0, 0)),
                  pl.BlockSpec((FREQ_BLOCK, seq), lambda bi, o, j: (fwd_blk(j), 0)),
                  pl.BlockSpec((1, seq, hb), lambda bi, o, j: (inv_blk(j), 0, 0)),
                  pl.BlockSpec((1, seq, hb), lambda bi, o, j: (inv_blk(j), 0, 0)),
                  pl.BlockSpec((1, FREQ_BLOCK, c), lambda bi, o, j: (o, fwd_blk(j), 0))],
        out_specs=pl.BlockSpec((1, seq, c), lambda bi, o, j: (bi, 0, 0), pipeline_mode=pl.Buffered(1)),
        out_shape=jax.ShapeDtypeStruct((b, seq, c), BF16),
        scratch_shapes=[pltpu.VMEM((seq, c), F32), pltpu.VMEM((seq, c), BF16), pltpu.VMEM((seq, c), F32),
                        pltpu.VMEM((seq, c), F32), pltpu.VMEM((2, FREQ_BLOCK, c), BF16)],
        compiler_params=pltpu.CompilerParams(dimension_semantics=("parallel", "arbitrary", "arbitrary"),
                                             vmem_limit_bytes=V7X_VMEM_LIMIT_BYTES),
        name="hyena",
    )(hy, hy, hy, hy, conv_w, conv_b, conv_w, conv_b, skip, gain, fmat, fcos_t, fsin_t, hspec)


def _attn_kernel(seq, q_ref, k_ref, v_ref, bias_ref, bias16_ref, o_ref,
                 qs_ref, ks_ref, vs_ref, res_ref, nat_ref, s_ref, e_ref):
    lanes = V7X_LANES
    qscale = LOG2_E / math.sqrt(HEAD_DIM)

    for p, d in enumerate(DILATIONS):
        ls = seq // d
        first = lax.broadcasted_iota(jnp.int32, (ls, lanes), 1) < HEAD_DIM
        for r in range(d):
            src = pl.ds(r, ls, stride=d) if d > 1 else pl.ds(0, ls)
            rows = slice(r * ls, (r + 1) * ls)
            qq = q_ref[0, src, :] * qscale
            qs_ref[p, 0, rows, :] = jnp.where(first, qq, 0.0).astype(BF16)
            qs_ref[p, 1, rows, :] = jnp.where(first, 0.0, qq).astype(BF16)
            ks_ref[p, rows, :] = k_ref[0, src, :].astype(BF16)
            vv = v_ref[0, src, :]
            vs_ref[p, 0, rows, :] = jnp.where(first, vv, 1.0).astype(BF16)
            vs_ref[p, 1, rows, :] = jnp.where(first, 1.0, vv).astype(BF16)

    first = lax.broadcasted_iota(jnp.int32, (Q_BLOCK, lanes), 1) < HEAD_DIM

    def run_pattern(p, nkeys, placement):
        def body(g, carry):
            blocks = [placement(g * ATTN_GROUP + i) for i in range(ATTN_GROUP)]
            for i, (row0, krow0, bias_of_head) in enumerate(blocks):
                kw = ks_ref[p, pl.ds(krow0, nkeys), :]
                for h in range(2):
                    qh = qs_ref[p, h, pl.ds(row0, Q_BLOCK), :]
                    s = lax.dot_general(qh, kw, (((1,), (1,)), ((), ())), preferred_element_type=F32)
                    s_ref[2 * i + h, :, :nkeys] = s + bias_of_head(h)
            for i, (row0, krow0, bias_of_head) in enumerate(blocks):
                ms = []
                for h in range(2):
                    s = s_ref[2 * i + h, :, :nkeys]
                    m = jnp.max(s, axis=-1, keepdims=True)
                    e_ref[2 * i + h, :, :nkeys] = jnp.exp2(s - m).astype(BF16)
                    ms.append(m)
                res_ref[p, 1, pl.ds(row0, Q_BLOCK), :] = jnp.where(first, ms[0], ms[1])
            for i, (row0, krow0, bias_of_head) in enumerate(blocks):
                o0 = _dot(e_ref[2 * i, :, :nkeys], vs_ref[p, 0, pl.ds(krow0, nkeys), :])
                o1 = _dot(e_ref[2 * i + 1, :, :nkeys], vs_ref[p, 1, pl.ds(krow0, nkeys), :])
                res_ref[p, 0, pl.ds(row0, Q_BLOCK), :] = jnp.where(first, o0, o1)
                res_ref[p, 2, pl.ds(row0, Q_BLOCK), :] = jnp.where(first, o1, o0)
            return carry

        lax.fori_loop(0, seq // Q_BLOCK // ATTN_GROUP, body, 0)

    for p, d in enumerate(DILATIONS[:2]):
        ls = seq // d
        nblk = ls // Q_BLOCK

        def banded(n, p=p, ls=ls, nblk=nblk):
            r = n // nblk
            ib = n % nblk
            i0 = ib * Q_BLOCK
            k0 = jnp.clip(i0 - HALF_WINDOW, 0, ls - K_WINDOW)
            case = jnp.where(ib == 0, 0, jnp.where(ib == nblk - 1, 2, 1))
            row0 = pl.multiple_of(r * ls + i0, Q_BLOCK)
            krow0 = pl.multiple_of(r * ls + k0, HALF_WINDOW)
            return row0, krow0, lambda h: bias_ref[0, (p * 3 + case) * 2 + h]

        run_pattern(p, K_WINDOW, banded)

    def full(n):
        row0 = pl.multiple_of(n * Q_BLOCK, Q_BLOCK)
        return row0, row0, lambda h: bias16_ref[0, h]

    run_pattern(2, Q_BLOCK, full)

    for p, d in enumerate(DILATIONS[1:], start=1):
        ls = seq // d
        for r in range(d):
            for kind in range(3):
                nat_ref[p - 1, kind, pl.ds(r, ls, stride=d), :] = res_ref[p, kind, r * ls:(r + 1) * ls, :]

    for t0 in range(0, seq, CONV_ROWS):
        rows = slice(t0, t0 + CONV_ROWS)
        parts = [tuple(res_ref[0, kind, rows, :] for kind in range(3))]
        parts += [tuple(nat_ref[p, kind, rows, :] for kind in range(3)) for p in range(2)]
        m = functools.reduce(jnp.maximum, [pt[1] for pt in parts])
        num = 0.0
        den = 0.0
        for out, mp, lp in parts:
            w = jnp.exp2(mp - m)
            num = num + w * out
            den = den + w * pltpu.roll(lp, HEAD_DIM, axis=1)
        o_ref[0, rows, :] = num / den


def _attn_bias_tables():
    slopes = np.array([2.0 ** (-8.0 * (i + 1) / N_HEADS) for i in range(N_HEADS)], np.float32)
    slopes = jnp.asarray(slopes.reshape(N_HEADS // 2, 1, 2, 1, 1))
    qi = lax.broadcasted_iota(jnp.int32, (Q_BLOCK, K_WINDOW), 0)
    kj = lax.broadcasted_iota(jnp.int32, (Q_BLOCK, K_WINDOW), 1)
    offsets = (0, -HALF_WINDOW, -2 * HALF_WINDOW)
    dist = jnp.stack([jnp.abs(kj - qi + off) for off in offsets])
    valid = dist <= HALF_WINDOW
    dil = jnp.asarray(np.array(DILATIONS[:2], np.float32).reshape(2, 1, 1, 1))
    scaled = (dil * dist.astype(F32)[None])[None, :, :, None]
    banded = jnp.where(valid[None, None, :, None], -slopes[:, None] * scaled, NEG_INF)
    banded = banded.reshape(N_HEADS // 2, 12, Q_BLOCK, K_WINDOW)
    d16 = dist[0, :, :Q_BLOCK]
    full = jnp.where(d16 <= HALF_WINDOW, -slopes[:, 0] * (DILATIONS[2] * d16.astype(F32)), NEG_INF)
    to_base2 = lambda t: jnp.where(t > 0.5 * NEG_INF, t * LOG2_E, NEG_INF).astype(F32)
    return to_base2(banded), to_base2(full)


def _dil_attn(q, k, v):
    b, seq, _ = q.shape
    nhp = N_HEADS // 2
    bias, bias16 = _attn_bias_tables()
    head_pair = pl.BlockSpec((1, seq, V7X_LANES), lambda bi, hp: (bi, 0, hp))
    return pl.pallas_call(
        functools.partial(_attn_kernel, seq),
        grid=(b, nhp),
        in_specs=[head_pair, head_pair, head_pair,
                  pl.BlockSpec((1, 12, Q_BLOCK, K_WINDOW), lambda bi, hp: (hp, 0, 0, 0)),
                  pl.BlockSpec((1, 2, Q_BLOCK, Q_BLOCK), lambda bi, hp: (hp, 0, 0, 0))],
        out_specs=head_pair,
        out_shape=jax.ShapeDtypeStruct((b, seq, D_ATTN), F32),
        scratch_shapes=[pltpu.VMEM((3, 2, seq, V7X_LANES), BF16),
                        pltpu.VMEM((3, seq, V7X_LANES), BF16),
                        pltpu.VMEM((3, 2, seq, V7X_LANES), BF16),
                        pltpu.VMEM((3, 3, seq, V7X_LANES), F32),
                        pltpu.VMEM((2, 3, seq, V7X_LANES), F32),
                        pltpu.VMEM((2 * ATTN_GROUP, Q_BLOCK, K_WINDOW), F32),
                        pltpu.VMEM((2 * ATTN_GROUP, Q_BLOCK, K_WINDOW), BF16)],
        compiler_params=pltpu.CompilerParams(dimension_semantics=("parallel", "parallel"),
                                             vmem_limit_bytes=V7X_VMEM_LIMIT_BYTES),
        name="dil_attn",
    )(q, k, v, bias, bias16)


def _out_ffn2_kernel(x1_ref, yh_ref, ya_ref, ga_ref, wo_ref, g3_ref, wg_ref, wu_ref, wd_ref, gf_ref,
                     out_ref, act_ref):
    ya = _rms(ya_ref[...], ga_ref[...]).astype(BF16)
    mix = _dot(yh_ref[...], wo_ref[:D_HYENA, :]) + _dot(ya, wo_ref[D_HYENA:, :])
    x2 = x1_ref[...] + mix
    h = _rms(x2, g3_ref[...]).astype(BF16)
    x3 = x2 + 0.5 * _swiglu(h, wg_ref, wu_ref, wd_ref, act_ref)
    out_ref[...] = _rms(x3, gf_ref[...])


def _out_ffn2(x1, yh, ya, ga, wo, g3, wg, wu, wd, gf):
    n = x1.shape[0]
    row = lambda w: pl.BlockSpec((ROW_TILE, w), lambda i: (i, 0))
    return pl.pallas_call(
        _out_ffn2_kernel,
        grid=(n // ROW_TILE,),
        in_specs=[row(D_MODEL), row(D_HYENA), row(D_ATTN), _resident((1, D_ATTN)), _resident(wo.shape),
                  _resident((1, D_MODEL)), _resident(wg.shape), _resident(wu.shape), _resident(wd.shape),
                  _resident((1, D_MODEL))],
        out_specs=row(D_MODEL),
        out_shape=jax.ShapeDtypeStruct((n, D_MODEL), F32),
        scratch_shapes=[pltpu.VMEM((ROW_TILE, D_FF), BF16)],
        compiler_params=pltpu.CompilerParams(dimension_semantics=("parallel",),
                                             vmem_limit_bytes=V7X_VMEM_LIMIT_BYTES),
        name="out_ffn2",
    )(x1, yh, ya, ga, wo, g3, wg, wu, wd, gf)


def _dft_matrices(seq):
    n = 2 * seq
    split = 64
    k = np.arange(seq, dtype=np.int64)[:, None]
    ang = lambda s: np.pi * (((2 * k + 1) * s) % (2 * n)) / n
    a1 = ang(split * np.arange(seq // split, dtype=np.int64)[None, :])
    a0 = ang(np.arange(split, dtype=np.int64)[None, :])
    tab = lambda x: jnp.asarray(x.astype(np.float32))
    hb = FREQ_BLOCK // 2
    c1, s1 = tab(np.cos(a1))[:, :, None], tab(np.sin(a1))[:, :, None]
    c0, s0 = tab(np.cos(a0))[:, None, :], tab(np.sin(a0))[:, None, :]
    cos = (c1 * c0 - s1 * s0).reshape(seq // hb, hb, seq)
    nsin = (-(s1 * c0 + c1 * s0)).reshape(seq // hb, hb, seq)
    fmat = jnp.concatenate([cos, nsin], axis=1).reshape(n, seq).astype(BF16)
    blocked = lambda x: tab(x.T.reshape(x.shape[1], seq // hb, hb).transpose(1, 0, 2))
    c1, s1 = blocked(np.cos(a1))[:, :, None, :], blocked(np.sin(a1))[:, :, None, :]
    c0, s0 = blocked(np.cos(a0))[:, None, :, :], blocked(np.sin(a0))[:, None, :, :]
    fcos_t = (c1 * c0 - s1 * s0).reshape(seq // hb, seq, hb).astype(BF16)
    fsin_t = (-(s1 * c0 + c1 * s0)).reshape(seq // hb, seq, hb).astype(BF16)
    return fmat, fcos_t, fsin_t


def _filter_features(seq):
    t = jnp.linspace(0.0, 1.0, seq, dtype=F32)[:, None]
    w = 2.0 * math.pi * jnp.arange(seq, dtype=F32)[:, None] / seq
    f = jnp.linspace(1e-4, FILTER_BANDS - 1, FILTER_BANDS, dtype=F32)[None, :]
    z = jnp.concatenate([t, jnp.cos(f * w), -jnp.sin(f * w)], axis=-1)
    return jnp.pad(z, ((0, 0), (0, FEAT_PAD - FILTER_EMB)))


def _decay_rates():
    max_decay = math.log(DECAY_TARGET) / FAST_DECAY_PCT
    min_decay = math.log(DECAY_TARGET) / SLOW_DECAY_PCT
    return jnp.linspace(min_decay, max_decay, D_HYENA, dtype=F32)[None, :]


def kernel(x, ffn1_norm_g, ffn1_w_gate, ffn1_w_up, ffn1_w_down, mix_norm_g, w_in, hy_conv_w, hy_conv_b, hy_filt_w1, hy_filt_b1, hy_filt_w2, hy_filt_b2, hy_filt_w3, hy_filt_b3, hy_filt_w_out, hy_filt_freq, hy_filt_skip, hy_out_norm_g, attn_out_norm_g, w_out, ffn2_norm_g, ffn2_w_gate, ffn2_w_up, ffn2_w_down, final_norm_g):
    b, seq, d = x.shape
    assert d == D_MODEL and (b * seq) % ROW_TILE == 0 and seq % (DILATIONS[-1] * Q_BLOCK) == 0
    row = lambda a: a.reshape(1, -1).astype(F32)
    f32 = lambda a: a.astype(F32)

    fmat, fcos_t, fsin_t = _dft_matrices(seq)
    w1 = jnp.pad(f32(hy_filt_w1), ((0, FEAT_PAD - FILTER_EMB), (0, 0)))
    hspec, wg1, wu1, wd1, win = _hy_filter(
        seq, _filter_features(seq), w1, row(hy_filt_b1), f32(hy_filt_w2), row(hy_filt_b2), f32(hy_filt_w3),
        row(hy_filt_b3), f32(hy_filt_w_out), row(hy_filt_freq), _decay_rates(), fmat,
        [f32(ffn1_w_gate), f32(ffn1_w_up), f32(ffn1_w_down), f32(w_in)])

    x1, hy, q, k, v, wo, wg2, wu2, wd2 = _ffn1_proj(
        x.reshape(b * seq, d), row(ffn1_norm_g), wg1, wu1, wd1, row(mix_norm_g), win,
        [f32(w_out), f32(ffn2_w_gate), f32(ffn2_w_up), f32(ffn2_w_down)])

    conv_w = hy_conv_w.astype(F32).reshape(3, 3, D_HYENA).transpose(1, 0, 2)
    conv_b = hy_conv_b.astype(F32).reshape(3, 1, D_HYENA)
    skip = hy_filt_skip.astype(F32).reshape(2, 1, D_HYENA)
    y_hy = _hyena(hy.reshape(b, seq, 3 * D_HYENA), conv_w, conv_b, skip, row(hy_out_norm_g),
                  fmat, fcos_t, fsin_t, hspec)

    shape3 = lambda a: a.reshape(b, seq, D_ATTN)
    y_at = _dil_attn(shape3(q), shape3(k), shape3(v))

    out = _out_ffn2(x1, y_hy.reshape(b * seq, D_HYENA), y_at.reshape(b * seq, D_ATTN),
                    row(attn_out_norm_g), wo, row(ffn2_norm_g), wg2, wu2, wd2, row(final_norm_g))
    return out.reshape(b, seq, d)
```

```python
import functools
import math

import numpy as np
import jax
import jax.numpy as jnp
from jax import lax
from jax.experimental import pallas as pl
from jax.experimental.pallas import tpu as pltpu

F32 = jnp.float32
BF16 = jnp.bfloat16

D_MODEL = 1024
D_HYENA = 512
D_ATTN = 512
HEAD_DIM = 64
N_HEADS = D_ATTN // HEAD_DIM
D_FF = 2816
FILTER_EMB = 33
FILTER_BANDS = 16
FILTER_WIDTH = 64
DECAY_TARGET = 1e-2
FAST_DECAY_PCT = 0.3
SLOW_DECAY_PCT = 1.5
DILATIONS = (1, 4, 16)
HALF_WINDOW = 64
RMS_EPS = 1e-6
NEG_INF = -1e30
LOG2_E = math.log2(math.e)

V7X_LANES = 128
V7X_SUBLANES = 8
V7X_BF16_SUBLANES = 16
V7X_VMEM_LIMIT_BYTES = 56 * 1024 * 1024

ROW_TILE = 512
FF_CHUNK = 256
FREQ_BLOCK = 1024
FILTER_ROWS = 512
Q_BLOCK = 128
K_WINDOW = 256
ATTN_GROUP = 8
CONV_ROWS = 256
FEAT_PAD = 128


def _dot(a, b):
    return jnp.dot(a, b, preferred_element_type=F32)


def _rms(x, g):
    return x * lax.rsqrt(jnp.mean(x * x, axis=-1, keepdims=True) + RMS_EPS) * g


def _swiglu(h, wg_ref, wu_ref, wd_ref, act_ref):
    for c in range(D_FF // FF_CHUNK):
        cols = slice(c * FF_CHUNK, (c + 1) * FF_CHUNK)
        g = _dot(h, wg_ref[:, cols])
        u = _dot(h, wu_ref[:, cols])
        act_ref[:, cols] = (g * jax.nn.sigmoid(g) * u).astype(BF16)
    return _dot(act_ref[...], wd_ref[...])


def _resident(shape):
    return pl.BlockSpec(shape, lambda *_: (0,) * len(shape), pipeline_mode=pl.Buffered(1))


def _cast_rider(weights, steps):
    in_specs, out_specs, out_shapes = [], [], []
    for w in weights:
        rows, cols = w.shape
        visits = 1
        while (rows * visits) % steps or (rows * visits // steps) % V7X_BF16_SUBLANES:
            visits *= 2
        slab = pl.BlockSpec((rows * visits // steps, cols), lambda i, visits=visits: (i // visits, 0))
        in_specs.append(slab)
        out_specs.append(slab)
        out_shapes.append(jax.ShapeDtypeStruct(w.shape, BF16))
    return in_specs, out_specs, out_shapes


def _cast_slabs(src_refs, dst_refs):
    for src, dst in zip(src_refs, dst_refs):
        dst[...] = src[...].astype(BF16)


def _ffn1_proj_kernel(x_ref, g1_ref, wg_ref, wu_ref, wd_ref, g2_ref, win_ref, *rest):
    n_cast = (len(rest) - 6) // 2
    x1_ref, hy_ref, q_ref, k_ref, v_ref = rest[n_cast:n_cast + 5]
    act_ref = rest[-1]
    _cast_slabs(rest[:n_cast], rest[n_cast + 5:-1])
    x = x_ref[...]
    h = _rms(x, g1_ref[...]).astype(BF16)
    x1 = x + 0.5 * _swiglu(h, wg_ref, wu_ref, wd_ref, act_ref)
    x1_ref[...] = x1
    h2 = _rms(x1, g2_ref[...]).astype(BF16)
    nh = 3 * D_HYENA
    hy_ref[...] = _dot(h2, win_ref[:, :nh])
    q_ref[...] = _dot(h2, win_ref[:, nh:nh + D_ATTN])
    k_ref[...] = _dot(h2, win_ref[:, nh + D_ATTN:nh + 2 * D_ATTN])
    v_ref[...] = _dot(h2, win_ref[:, nh + 2 * D_ATTN:])


def _ffn1_proj(x2d, g1, wg, wu, wd, g2, win, later_weights):
    n = x2d.shape[0]
    steps = n // ROW_TILE
    row = lambda w: pl.BlockSpec((ROW_TILE, w), lambda i: (i, 0))
    cast_in, cast_out, cast_shapes = _cast_rider(later_weights, steps)
    return pl.pallas_call(
        _ffn1_proj_kernel,
        grid=(steps,),
        in_specs=[row(D_MODEL), _resident((1, D_MODEL)), _resident(wg.shape), _resident(wu.shape),
                  _resident(wd.shape), _resident((1, D_MODEL)), _resident(win.shape)] + cast_in,
        out_specs=[row(D_MODEL), row(3 * D_HYENA), row(D_ATTN), row(D_ATTN), row(D_ATTN)] + cast_out,
        out_shape=[jax.ShapeDtypeStruct((n, D_MODEL), F32),
                   jax.ShapeDtypeStruct((n, 3 * D_HYENA), F32),
                   jax.ShapeDtypeStruct((n, D_ATTN), F32),
                   jax.ShapeDtypeStruct((n, D_ATTN), F32),
                   jax.ShapeDtypeStruct((n, D_ATTN), F32)] + cast_shapes,
        scratch_shapes=[pltpu.VMEM((ROW_TILE, D_FF), BF16)],
        compiler_params=pltpu.CompilerParams(dimension_semantics=("arbitrary",),
                                             vmem_limit_bytes=V7X_VMEM_LIMIT_BYTES),
        name="ffn1_proj",
    )(x2d, g1, wg, wu, wd, g2, win, *later_weights)


def _filter_kernel(seq, feat_ref, w1_ref, b1_ref, w2_ref, b2_ref, w3_ref, b3_ref, wout_ref,
                   freq_ref, delta_ref, f_ref, *rest):
    n_cast = (len(rest) - 2) // 2
    h_ref, hcat_ref = rest[n_cast], rest[-1]
    _cast_slabs(rest[:n_cast], rest[n_cast + 1:-1])
    c_ = D_HYENA

    @pl.when(pl.program_id(0) == 0)
    def _():
        hi = lax.Precision.HIGHEST
        dot_hi = lambda a, b: jnp.dot(a, b, precision=hi, preferred_element_type=F32)
        freq = freq_ref[...]
        h = jnp.sin(freq * (dot_hi(feat_ref[...], w1_ref[...]) + b1_ref[...]))
        h = jnp.sin(freq * (dot_hi(h, w2_ref[...]) + b2_ref[...]))
        h = jnp.sin(freq * (dot_hi(h, w3_ref[...]) + b3_ref[...]))
        t = lax.broadcasted_iota(jnp.int32, (seq, 1), 0).astype(F32) * (1.0 / (seq - 1))
        decay = jnp.exp(-t * jnp.abs(delta_ref[...]))
        row = lax.broadcasted_iota(jnp.int32, (seq, c_), 0)
        for c in range(4):
            kc = dot_hi(h, wout_ref[:, c * c_:(c + 1) * c_]) * decay
            if c % 2 == 1:
                kc = jnp.where(row == 0, 0.0, kc)
            hcat_ref[:, c * c_:(c + 1) * c_] = kc.astype(BF16)

    p = _dot(f_ref[...], hcat_ref[...])
    is_sin = (pl.program_id(0) * FILTER_ROWS // (FREQ_BLOCK // 2)) % 2
    sign = (1 - 2 * is_sin).astype(F32)
    scale = 1.0 / seq
    for o in range(2):
        pf = p[:, (2 * o) * c_:(2 * o + 1) * c_]
        pb = p[:, (2 * o + 1) * c_:(2 * o + 2) * c_]
        h_ref[o] = (pf + sign * pb) * scale


def _hy_filter(seq, feat, w1, b1, w2, b2, w3, b3, wout, freq, delta, fmat, later_weights):
    full = lambda a: _resident(a.shape)
    assert (FREQ_BLOCK // 2) % FILTER_ROWS == 0
    steps = 2 * seq // FILTER_ROWS
    cast_in, cast_out, cast_shapes = _cast_rider(later_weights, steps)
    return pl.pallas_call(
        functools.partial(_filter_kernel, seq),
        grid=(steps,),
        in_specs=[full(feat), full(w1), full(b1), full(w2), full(b2), full(w3), full(b3), full(wout),
                  full(freq), full(delta), pl.BlockSpec((FILTER_ROWS, seq), lambda j: (j, 0))] + cast_in,
        out_specs=[pl.BlockSpec((2, FILTER_ROWS, D_HYENA), lambda j: (0, j, 0))] + cast_out,
        out_shape=[jax.ShapeDtypeStruct((2, 2 * seq, D_HYENA), F32)] + cast_shapes,
        scratch_shapes=[pltpu.VMEM((seq, 4 * D_HYENA), BF16)],
        compiler_params=pltpu.CompilerParams(dimension_semantics=("arbitrary",),
                                             vmem_limit_bytes=V7X_VMEM_LIMIT_BYTES),
        name="hy_filter",
    )(feat, w1, b1, w2, b2, w3, b3, wout, freq, delta, fmat, *later_weights)


def _short_conv(cur, before, after, w_ref, b_ref):
    rows = cur.shape[0]
    row = lax.broadcasted_iota(jnp.int32, cur.shape, 0)
    prev = jnp.where(row == 0, before, pltpu.roll(cur, 1, axis=0))
    nxt = jnp.where(row == rows - 1, after, pltpu.roll(cur, rows - 1, axis=0))
    return b_ref[0] + prev * w_ref[0, 0:1, :] + cur * w_ref[0, 1:2, :] + nxt * w_ref[0, 2:3, :]


def _short_conv_rows(u_ref, w_ref, b_ref, t0, seq):
    zero = jnp.zeros((1, u_ref.shape[2]), F32)
    before = u_ref[0, t0 - V7X_SUBLANES:t0, :][V7X_SUBLANES - 1:] if t0 > 0 else zero
    end = t0 + CONV_ROWS
    after = u_ref[0, end:end + V7X_SUBLANES, :][:1] if end < seq else zero
    return _short_conv(u_ref[0, t0:end, :], before, after, w_ref, b_ref)


def _hyena_kernel(seq, v_ref, gate_ref, gate_lo_ref, gate_hi_ref, wv_ref, bv_ref, wg_ref, bg_ref, skip_ref,
                  gain_ref, f_ref, h_ref, out_ref, zf_ref, zb_ref, acc_ref, g_ref, y_ref):
    o = pl.program_id(1)
    j = pl.program_id(2)
    nfb = 2 * seq // FREQ_BLOCK
    gate_rows = seq // nfb
    chunks = range(0, seq, CONV_ROWS)
    hb = FREQ_BLOCK // 2

    def f_block(jb):
        return f_ref[pl.ds(pl.multiple_of(jb * FREQ_BLOCK, FREQ_BLOCK), FREQ_BLOCK), :]

    def forward(jb):
        zfreq = _dot(f_block(jb), zb_ref[...])
        zr, zi = zfreq[:hb], zfreq[hb:]
        hr, hi = h_ref[0, :hb, :], h_ref[0, hb:, :]
        y_ref[jb % 2, :, :hb] = (zr * hr - zi * hi).T.astype(BF16)
        y_ref[jb % 2, :, hb:] = (zr * hi + zi * hr).T.astype(BF16)

    def inverse(jb):
        acc_ref[...] += _dot(y_ref[jb % 2], f_block(jb))

    def gate_chunk():
        before = jnp.where(j > 0, gate_lo_ref[0, V7X_SUBLANES - 1:, :], 0.0)
        after = jnp.where(j < nfb - 1, gate_hi_ref[0, :1, :], 0.0)
        t0 = pl.multiple_of(j * gate_rows, gate_rows)
        g_ref[pl.ds(t0, gate_rows), :] = _short_conv(gate_ref[0], before, after, wg_ref, bg_ref)

    @pl.when((o == 0) & (j == 0))
    def _():
        for t0 in chunks:
            z = _short_conv_rows(v_ref, wv_ref, bv_ref, t0, seq)
            zf_ref[t0:t0 + CONV_ROWS, :] = z
            zb_ref[t0:t0 + CONV_ROWS, :] = z.astype(BF16)

    @pl.when(j == 0)
    def _():
        gate_chunk()
        acc_ref[...] = jnp.zeros_like(acc_ref)
        forward(0)

    @pl.when((j > 0) & (j < nfb))
    def _():
        gate_chunk()
        inverse(j - 1)
        forward(j)

    def gated(rows):
        return g_ref[rows, :] * (acc_ref[:, rows].T + zf_ref[rows, :] * skip_ref[0])

    @pl.when((j == nfb) & (o == 0))
    def _():
        inverse(nfb - 1)
        for t0 in chunks:
            rows = slice(t0, t0 + CONV_ROWS)
            z = gated(rows)
            zf_ref[rows, :] = z
            zb_ref[rows, :] = z.astype(BF16)

    @pl.when((j == nfb) & (o == 1))
    def _():
        inverse(nfb - 1)
        for t0 in chunks:
            rows = slice(t0, t0 + CONV_ROWS)
            out_ref[0, rows, :] = _rms(gated(rows), gain_ref[...]).astype(BF16)


def _hyena(hy, conv_w, conv_b, skip, gain, fmat, hspec):
    b, seq, _ = hy.shape
    c = D_HYENA
    nfb = 2 * seq // FREQ_BLOCK
    gate_rows = seq // nfb
    halo_per_chunk = gate_rows // V7X_SUBLANES
    fwd_blk = lambda j: jnp.minimum(j, nfb - 1)
    wpart = lambda sel: pl.BlockSpec((1, 3, c), lambda bi, o, j: (sel(o), 0, 0))
    bpart = lambda sel: pl.BlockSpec((1, 1, c), lambda bi, o, j: (sel(o), 0, 0))
    value = lambda o: 0
    gate = lambda o: 1 + o
    halo = lambda first: pl.BlockSpec((1, V7X_SUBLANES, c), lambda bi, o, j: (bi, first(fwd_blk(j)), 1 + o))
    return pl.pallas_call(
        functools.partial(_hyena_kernel, seq),
        grid=(b, 2, nfb + 1),
        in_specs=[pl.BlockSpec((1, seq, c), lambda bi, o, j: (bi, 0, 0), pipeline_mode=pl.Buffered(1)),
                  pl.BlockSpec((1, gate_rows, c), lambda bi, o, j: (bi, fwd_blk(j), 1 + o)),
                  halo(lambda jc: jnp.maximum(jc * halo_per_chunk - 1, 0)),
                  halo(lambda jc: jnp.minimum((jc + 1) * halo_per_chunk, seq // V7X_SUBLANES - 1)),
                  wpart(value), bpart(value), wpart(gate), bpart(gate),
                  pl.BlockSpec((1, 1, c), lambda bi, o, j: (o, 0, 0)),
                  pl.BlockSpec((1, c), lambda bi, o, j: (0, 0)),
                  _resident(fmat.shape),
                  pl.BlockSpec((1, FREQ_BLOCK, c), lambda bi, o, j: (o, fwd_blk(j), 0))],
        out_specs=pl.BlockSpec((1, seq, c), lambda bi, o, j: (bi, 0, 0), pipeline_mode=pl.Buffered(1)),
        out_shape=jax.ShapeDtypeStruct((b, seq, c), BF16),
        scratch_shapes=[pltpu.VMEM((seq, c), F32), pltpu.VMEM((seq, c), BF16), pltpu.VMEM((c, seq), F32),
                        pltpu.VMEM((seq, c), F32), pltpu.VMEM((2, c, FREQ_BLOCK), BF16)],
        compiler_params=pltpu.CompilerParams(dimension_semantics=("parallel", "arbitrary", "arbitrary"),
                                             vmem_limit_bytes=V7X_VMEM_LIMIT_BYTES),
        name="hyena",
    )(hy, hy, hy, hy, conv_w, conv_b, conv_w, conv_b, skip, gain, fmat, hspec)


def _attn_kernel(seq, q_ref, k_ref, v_ref, bias_ref, bias16_ref, o_ref,
                 qs_ref, ks_ref, vs_ref, res_ref, nat_ref, s_ref, e_ref):
    lanes = V7X_LANES
    qscale = LOG2_E / math.sqrt(HEAD_DIM)

    for p, d in enumerate(DILATIONS):
        ls = seq // d
        first = lax.broadcasted_iota(jnp.int32, (ls, lanes), 1) < HEAD_DIM
        for r in range(d):
            src = pl.ds(r, ls, stride=d) if d > 1 else pl.ds(0, ls)
            rows = slice(r * ls, (r + 1) * ls)
            qq = q_ref[0, src, :] * qscale
            qs_ref[p, 0, rows, :] = jnp.where(first, qq, 0.0).astype(BF16)
            qs_ref[p, 1, rows, :] = jnp.where(first, 0.0, qq).astype(BF16)
            ks_ref[p, rows, :] = k_ref[0, src, :].astype(BF16)
            vv = v_ref[0, src, :]
            vs_ref[p, 0, rows, :] = jnp.where(first, vv, 1.0).astype(BF16)
            vs_ref[p, 1, rows, :] = jnp.where(first, 1.0, vv).astype(BF16)

    first = lax.broadcasted_iota(jnp.int32, (Q_BLOCK, lanes), 1) < HEAD_DIM

    def run_pattern(p, nkeys, placement):
        def body(g, carry):
            blocks = [placement(g * ATTN_GROUP + i) for i in range(ATTN_GROUP)]
            for i, (row0, krow0, bias_of_head) in enumerate(blocks):
                kw = ks_ref[p, pl.ds(krow0, nkeys), :]
                for h in range(2):
                    qh = qs_ref[p, h, pl.ds(row0, Q_BLOCK), :]
                    s = lax.dot_general(qh, kw, (((1,), (1,)), ((), ())), preferred_element_type=F32)
                    s_ref[2 * i + h, :, :nkeys] = s + bias_of_head(h)
            for i, (row0, krow0, bias_of_head) in enumerate(blocks):
                ms = []
                for h in range(2):
                    s = s_ref[2 * i + h, :, :nkeys]
                    m = jnp.max(s, axis=-1, keepdims=True)
                    e_ref[2 * i + h, :, :nkeys] = jnp.exp2(s - m).astype(BF16)
                    ms.append(m)
                res_ref[p, 1, pl.ds(row0, Q_BLOCK), :] = jnp.where(first, ms[0], ms[1])
            for i, (row0, krow0, bias_of_head) in enumerate(blocks):
                o0 = _dot(e_ref[2 * i, :, :nkeys], vs_ref[p, 0, pl.ds(krow0, nkeys), :])
                o1 = _dot(e_ref[2 * i + 1, :, :nkeys], vs_ref[p, 1, pl.ds(krow0, nkeys), :])
                res_ref[p, 0, pl.ds(row0, Q_BLOCK), :] = jnp.where(first, o0, o1)
                res_ref[p, 2, pl.ds(row0, Q_BLOCK), :] = jnp.where(first, o1, o0)
            return carry

        lax.fori_loop(0, seq // Q_BLOCK // ATTN_GROUP, body, 0)

    for p, d in enumerate(DILATIONS[:2]):
        ls = seq // d
        nblk = ls // Q_BLOCK

        def banded(n, p=p, ls=ls, nblk=nblk):
            r = n // nblk
            ib = n % nblk
            i0 = ib * Q_BLOCK
            k0 = jnp.clip(i0 - HALF_WINDOW, 0, ls - K_WINDOW)
            case = jnp.where(ib == 0, 0, jnp.where(ib == nblk - 1, 2, 1))
            row0 = pl.multiple_of(r * ls + i0, Q_BLOCK)
            krow0 = pl.multiple_of(r * ls + k0, HALF_WINDOW)
            return row0, krow0, lambda h: bias_ref[0, (p * 3 + case) * 2 + h]

        run_pattern(p, K_WINDOW, banded)

    def full(n):
        row0 = pl.multiple_of(n * Q_BLOCK, Q_BLOCK)
        return row0, row0, lambda h: bias16_ref[0, h]

    run_pattern(2, Q_BLOCK, full)

    for p, d in enumerate(DILATIONS[1:], start=1):
        ls = seq // d
        for r in range(d):
            for kind in range(3):
                nat_ref[p - 1, kind, pl.ds(r, ls, stride=d), :] = res_ref[p, kind, r * ls:(r + 1) * ls, :]

    for t0 in range(0, seq, CONV_ROWS):
        rows = slice(t0, t0 + CONV_ROWS)
        parts = [tuple(res_ref[0, kind, rows, :] for kind in range(3))]
        parts += [tuple(nat_ref[p, kind, rows, :] for kind in range(3)) for p in range(2)]
        m = functools.reduce(jnp.maximum, [pt[1] for pt in parts])
        num = 0.0
        den = 0.0
        for out, mp, lp in parts:
            w = jnp.exp2(mp - m)
            num = num + w * out
            den = den + w * pltpu.roll(lp, HEAD_DIM, axis=1)
        o_ref[0, rows, :] = num / den


def _attn_bias_tables():
    slopes = np.array([2.0 ** (-8.0 * (i + 1) / N_HEADS) for i in range(N_HEADS)], np.float32)
    slopes = jnp.asarray(slopes.reshape(N_HEADS // 2, 1, 2, 1, 1))
    qi = lax.broadcasted_iota(jnp.int32, (Q_BLOCK, K_WINDOW), 0)
    kj = lax.broadcasted_iota(jnp.int32, (Q_BLOCK, K_WINDOW), 1)
    offsets = (0, -HALF_WINDOW, -2 * HALF_WINDOW)
    dist = jnp.stack([jnp.abs(kj - qi + off) for off in offsets])
    valid = dist <= HALF_WINDOW
    dil = jnp.asarray(np.array(DILATIONS[:2], np.float32).reshape(2, 1, 1, 1))
    scaled = (dil * dist.astype(F32)[None])[None, :, :, None]
    banded = jnp.where(valid[None, None, :, None], -slopes[:, None] * scaled, NEG_INF)
    banded = banded.reshape(N_HEADS // 2, 12, Q_BLOCK, K_WINDOW)
    d16 = dist[0, :, :Q_BLOCK]
    full = jnp.where(d16 <= HALF_WINDOW, -slopes[:, 0] * (DILATIONS[2] * d16.astype(F32)), NEG_INF)
    to_base2 = lambda t: jnp.where(t > 0.5 * NEG_INF, t * LOG2_E, NEG_INF).astype(F32)
    return to_base2(banded), to_base2(full)


def _dil_attn(q, k, v):
    b, seq, _ = q.shape
    nhp = N_HEADS // 2
    bias, bias16 = _attn_bias_tables()
    head_pair = pl.BlockSpec((1, seq, V7X_LANES), lambda bi, hp: (bi, 0, hp))
    return pl.pallas_call(
        functools.partial(_attn_kernel, seq),
        grid=(b, nhp),
        in_specs=[head_pair, head_pair, head_pair,
                  pl.BlockSpec((1, 12, Q_BLOCK, K_WINDOW), lambda bi, hp: (hp, 0, 0, 0)),
                  pl.BlockSpec((1, 2, Q_BLOCK, Q_BLOCK), lambda bi, hp: (hp, 0, 0, 0))],
        out_specs=head_pair,
        out_shape=jax.ShapeDtypeStruct((b, seq, D_ATTN), F32),
        scratch_shapes=[pltpu.VMEM((3, 2, seq, V7X_LANES), BF16),
                        pltpu.VMEM((3, seq, V7X_LANES), BF16),
                        pltpu.VMEM((3, 2, seq, V7X_LANES), BF16),
                        pltpu.VMEM((3, 3, seq, V7X_LANES), F32),
                        pltpu.VMEM((2, 3, seq, V7X_LANES), F32),
                        pltpu.VMEM((2 * ATTN_GROUP, Q_BLOCK, K_WINDOW), F32),
                        pltpu.VMEM((2 * ATTN_GROUP, Q_BLOCK, K_WINDOW), BF16)],
        compiler_params=pltpu.CompilerParams(dimension_semantics=("parallel", "parallel"),
                                             vmem_limit_bytes=V7X_VMEM_LIMIT_BYTES),
        name="dil_attn",
    )(q, k, v, bias, bias16)


def _out_ffn2_kernel(x1_ref, yh_ref, ya_ref, ga_ref, wo_ref, g3_ref, wg_ref, wu_ref, wd_ref, gf_ref,
                     out_ref, act_ref):
    ya = _rms(ya_ref[...], ga_ref[...]).astype(BF16)
    mix = _dot(yh_ref[...], wo_ref[:D_HYENA, :]) + _dot(ya, wo_ref[D_HYENA:, :])
    x2 = x1_ref[...] + mix
    h = _rms(x2, g3_ref[...]).astype(BF16)
    x3 = x2 + 0.5 * _swiglu(h, wg_ref, wu_ref, wd_ref, act_ref)
    out_ref[...] = _rms(x3, gf_ref[...])


def _out_ffn2(x1, yh, ya, ga, wo, g3, wg, wu, wd, gf):
    n = x1.shape[0]
    row = lambda w: pl.BlockSpec((ROW_TILE, w), lambda i: (i, 0))
    return pl.pallas_call(
        _out_ffn2_kernel,
        grid=(n // ROW_TILE,),
        in_specs=[row(D_MODEL), row(D_HYENA), row(D_ATTN), _resident((1, D_ATTN)), _resident(wo.shape),
                  _resident((1, D_MODEL)), _resident(wg.shape), _resident(wu.shape), _resident(wd.shape),
                  _resident((1, D_MODEL))],
        out_specs=row(D_MODEL),
        out_shape=jax.ShapeDtypeStruct((n, D_MODEL), F32),
        scratch_shapes=[pltpu.VMEM((ROW_TILE, D_FF), BF16)],
        compiler_params=pltpu.CompilerParams(dimension_semantics=("parallel",),
                                             vmem_limit_bytes=V7X_VMEM_LIMIT_BYTES),
        name="out_ffn2",
    )(x1, yh, ya, ga, wo, g3, wg, wu, wd, gf)


def _dft_matrices(seq):
    n = 2 * seq
    split = 64
    k = np.arange(seq, dtype=np.int64)[:, None]
    ang = lambda s: np.pi * (((2 * k + 1) * s) % (2 * n)) / n
    a1 = ang(split * np.arange(seq // split, dtype=np.int64)[None, :])
    a0 = ang(np.arange(split, dtype=np.int64)[None, :])
    tab = lambda x: jnp.asarray(x.astype(np.float32))
    hb = FREQ_BLOCK // 2
    c1, s1 = tab(np.cos(a1))[:, :, None], tab(np.sin(a1))[:, :, None]
    c0, s0 = tab(np.cos(a0))[:, None, :], tab(np.sin(a0))[:, None, :]
    cos = (c1 * c0 - s1 * s0).reshape(seq // hb, hb, seq)
    nsin = (-(s1 * c0 + c1 * s0)).reshape(seq // hb, hb, seq)
    return jnp.concatenate([cos, nsin], axis=1).reshape(n, seq).astype(BF16)


def _filter_features(seq):
    t = jnp.linspace(0.0, 1.0, seq, dtype=F32)[:, None]
    w = 2.0 * math.pi * jnp.arange(seq, dtype=F32)[:, None] / seq
    f = jnp.linspace(1e-4, FILTER_BANDS - 1, FILTER_BANDS, dtype=F32)[None, :]
    z = jnp.concatenate([t, jnp.cos(f * w), -jnp.sin(f * w)], axis=-1)
    return jnp.pad(z, ((0, 0), (0, FEAT_PAD - FILTER_EMB)))


def _decay_rates():
    max_decay = math.log(DECAY_TARGET) / FAST_DECAY_PCT
    min_decay = math.log(DECAY_TARGET) / SLOW_DECAY_PCT
    return jnp.linspace(min_decay, max_decay, D_HYENA, dtype=F32)[None, :]


def kernel(x, ffn1_norm_g, ffn1_w_gate, ffn1_w_up, ffn1_w_down, mix_norm_g, w_in, hy_conv_w, hy_conv_b, hy_filt_w1, hy_filt_b1, hy_filt_w2, hy_filt_b2, hy_filt_w3, hy_filt_b3, hy_filt_w_out, hy_filt_freq, hy_filt_skip, hy_out_norm_g, attn_out_norm_g, w_out, ffn2_norm_g, ffn2_w_gate, ffn2_w_up, ffn2_w_down, final_norm_g):
    b, seq, d = x.shape
    assert d == D_MODEL and (b * seq) % ROW_TILE == 0 and seq % (DILATIONS[-1] * Q_BLOCK) == 0
    row = lambda a: a.reshape(1, -1).astype(F32)
    f32 = lambda a: a.astype(F32)

    fmat = _dft_matrices(seq)
    w1 = jnp.pad(f32(hy_filt_w1), ((0, FEAT_PAD - FILTER_EMB), (0, 0)))
    hspec, wg1, wu1, wd1, win = _hy_filter(
        seq, _filter_features(seq), w1, row(hy_filt_b1), f32(hy_filt_w2), row(hy_filt_b2), f32(hy_filt_w3),
        row(hy_filt_b3), f32(hy_filt_w_out), row(hy_filt_freq), _decay_rates(), fmat,
        [f32(ffn1_w_gate), f32(ffn1_w_up), f32(ffn1_w_down), f32(w_in)])

    x1, hy, q, k, v, wo, wg2, wu2, wd2 = _ffn1_proj(
        x.reshape(b * seq, d), row(ffn1_norm_g), wg1, wu1, wd1, row(mix_norm_g), win,
        [f32(w_out), f32(ffn2_w_gate), f32(ffn2_w_up), f32(ffn2_w_down)])

    conv_w = hy_conv_w.astype(F32).reshape(3, 3, D_HYENA).transpose(1, 0, 2)
    conv_b = hy_conv_b.astype(F32).reshape(3, 1, D_HYENA)
    skip = hy_filt_skip.astype(F32).reshape(2, 1, D_HYENA)
    y_hy = _hyena(hy.reshape(b, seq, 3 * D_HYENA), conv_w, conv_b, skip, row(hy_out_norm_g),
                  fmat, hspec)

    shape3 = lambda a: a.reshape(b, seq, D_ATTN)
    y_at = _dil_attn(shape3(q), shape3(k), shape3(v))

    out = _out_ffn2(x1, y_hy.reshape(b * seq, D_HYENA), y_at.reshape(b * seq, D_ATTN),
                    row(attn_out_norm_g), wo, row(ffn2_norm_g), wg2, wu2, wd2, row(final_norm_g))
    return out.reshape(b, seq, d)
```

```python
import functools
import math

import numpy as np
import jax
import jax.numpy as jnp
from jax import lax
from jax.experimental import pallas as pl
from jax.experimental.pallas import tpu as pltpu

F32 = jnp.float32
BF16 = jnp.bfloat16

D_MODEL = 1024
D_HYENA = 512
D_ATTN = 512
HEAD_DIM = 64
N_HEADS = D_ATTN // HEAD_DIM
D_FF = 2816
FILTER_EMB = 33
FILTER_BANDS = 16
FILTER_WIDTH = 64
DECAY_TARGET = 1e-2
FAST_DECAY_PCT = 0.3
SLOW_DECAY_PCT = 1.5
DILATIONS = (1, 4, 16)
HALF_WINDOW = 64
RMS_EPS = 1e-6
NEG_INF = -1e30
LOG2_E = math.log2(math.e)

V7X_LANES = 128
V7X_SUBLANES = 8
V7X_BF16_SUBLANES = 16
V7X_VMEM_LIMIT_BYTES = 56 * 1024 * 1024

ROW_TILE = 512
FF_CHUNK = 256
FREQ_BLOCK = 1024
FILTER_ROWS = 512
Q_BLOCK = 128
K_WINDOW = 256
ATTN_GROUP = 8
CONV_ROWS = 256
FEAT_PAD = 128


def _dot(a, b):
    return jnp.dot(a, b, preferred_element_type=F32)


def _rms(x, g):
    return x * lax.rsqrt(jnp.mean(x * x, axis=-1, keepdims=True) + RMS_EPS) * g


def _swiglu(h, wg_ref, wu_ref, wd_ref, act_ref):
    for c in range(D_FF // FF_CHUNK):
        cols = slice(c * FF_CHUNK, (c + 1) * FF_CHUNK)
        g = _dot(h, wg_ref[:, cols])
        u = _dot(h, wu_ref[:, cols])
        act_ref[:, cols] = (g * jax.nn.sigmoid(g) * u).astype(BF16)
    return _dot(act_ref[...], wd_ref[...])


def _resident(shape):
    return pl.BlockSpec(shape, lambda *_: (0,) * len(shape), pipeline_mode=pl.Buffered(1))


def _cast_rider(weights, steps):
    in_specs, out_specs, out_shapes = [], [], []
    for w in weights:
        rows, cols = w.shape
        visits = 1
        while (rows * visits) % steps or (rows * visits // steps) % V7X_BF16_SUBLANES:
            visits *= 2
        slab = pl.BlockSpec((rows * visits // steps, cols), lambda i, visits=visits: (i // visits, 0))
        in_specs.append(slab)
        out_specs.append(slab)
        out_shapes.append(jax.ShapeDtypeStruct(w.shape, BF16))
    return in_specs, out_specs, out_shapes


def _cast_slabs(src_refs, dst_refs):
    for src, dst in zip(src_refs, dst_refs):
        dst[...] = src[...].astype(BF16)


N_QKV_OUT = 3 * len(DILATIONS)


MAX_DIL = DILATIONS[-1]
TILE_TIMES = ROW_TILE // MAX_DIL


def _residue_view(d, width):
    return lambda b, seq: (b, d, seq // MAX_DIL, (MAX_DIL // d) * width)


def _tile_spec(d, width, per_seq):
    return pl.BlockSpec((1, d, TILE_TIMES, (MAX_DIL // d) * width),
                        lambda i: (i // per_seq, 0, i % per_seq, 0))


def _store_rows(val, out_ref, d):
    width = val.shape[1]
    for r in range(MAX_DIL):
        piece = val[r * TILE_TIMES:(r + 1) * TILE_TIMES, :].astype(out_ref.dtype)
        out_ref[0, r % d, :, (r // d) * width:(r // d + 1) * width] = piece


def _ffn1_proj_kernel(x_ref, g1_ref, wg_ref, wu_ref, wd_ref, g2_ref, win_ref, *rest):
    n_cast = (len(rest) - 3 - N_QKV_OUT) // 2
    x1_ref, hy_ref = rest[n_cast:n_cast + 2]
    qkv_refs = rest[n_cast + 2:n_cast + 2 + N_QKV_OUT]
    act_ref = rest[-1]
    _cast_slabs(rest[:n_cast], rest[n_cast + 2 + N_QKV_OUT:-1])
    x = jnp.concatenate([x_ref[0, 0, :, r * D_MODEL:(r + 1) * D_MODEL] for r in range(MAX_DIL)], axis=0)
    h = _rms(x, g1_ref[...]).astype(BF16)
    x1 = x + 0.5 * _swiglu(h, wg_ref, wu_ref, wd_ref, act_ref)
    _store_rows(x1, x1_ref, 1)
    h2 = _rms(x1, g2_ref[...]).astype(BF16)
    nh = 3 * D_HYENA
    _store_rows(_dot(h2, win_ref[:, :nh]), hy_ref, 1)
    scales = (LOG2_E / math.sqrt(HEAD_DIM), 1.0, 1.0)
    for part, scale in enumerate(scales):
        cols = slice(nh + part * D_ATTN, nh + (part + 1) * D_ATTN)
        proj = (_dot(h2, win_ref[:, cols]) * scale).astype(BF16)
        for out, d in zip(qkv_refs[part * len(DILATIONS):(part + 1) * len(DILATIONS)], DILATIONS):
            _store_rows(proj, out, d)


def _ffn1_proj(x3d, g1, wg, wu, wd, g2, win, later_weights):
    b, seq, _ = x3d.shape
    steps = b * seq // ROW_TILE
    per_seq = seq // ROW_TILE
    cast_in, cast_out, cast_shapes = _cast_rider(later_weights, steps)
    outs = [(1, D_MODEL, F32), (1, 3 * D_HYENA, F32)] + [(d, D_ATTN, BF16) for _ in range(3) for d in DILATIONS]
    results = pl.pallas_call(
        _ffn1_proj_kernel,
        grid=(steps,),
        in_specs=[_tile_spec(1, D_MODEL, per_seq), _resident((1, D_MODEL)), _resident(wg.shape),
                  _resident(wu.shape), _resident(wd.shape), _resident((1, D_MODEL)),
                  _resident(win.shape)] + cast_in,
        out_specs=[_tile_spec(d, w, per_seq) for d, w, _ in outs] + cast_out,
        out_shape=[jax.ShapeDtypeStruct(_residue_view(d, w)(b, seq), dt) for d, w, dt in outs] + cast_shapes,
        scratch_shapes=[pltpu.VMEM((ROW_TILE, D_FF), BF16)],
        compiler_params=pltpu.CompilerParams(dimension_semantics=("arbitrary",),
                                             vmem_limit_bytes=V7X_VMEM_LIMIT_BYTES),
        name="ffn1_proj",
    )(x3d.reshape(_residue_view(1, D_MODEL)(b, seq)), g1, wg, wu, wd, g2, win, *later_weights)
    rows = [r.reshape(b, seq, w) for r, (_, w, _) in zip(results, outs)]
    return [rows[0].reshape(b * seq, D_MODEL)] + rows[1:] + list(results[len(outs):])


def _filter_kernel(seq, feat_ref, w1_ref, b1_ref, w2_ref, b2_ref, w3_ref, b3_ref, wout_ref,
                   freq_ref, delta_ref, f_ref, *rest):
    n_cast = (len(rest) - 2) // 2
    h_ref, hcat_ref = rest[n_cast], rest[-1]
    _cast_slabs(rest[:n_cast], rest[n_cast + 1:-1])
    c_ = D_HYENA

    @pl.when(pl.program_id(0) == 0)
    def _():
        hi = lax.Precision.HIGHEST
        dot_hi = lambda a, b: jnp.dot(a, b, precision=hi, preferred_element_type=F32)
        freq = freq_ref[...]
        h = jnp.sin(freq * (dot_hi(feat_ref[...], w1_ref[...]) + b1_ref[...]))
        h = jnp.sin(freq * (dot_hi(h, w2_ref[...]) + b2_ref[...]))
        h = jnp.sin(freq * (dot_hi(h, w3_ref[...]) + b3_ref[...]))
        t = lax.broadcasted_iota(jnp.int32, (seq, 1), 0).astype(F32) * (1.0 / (seq - 1))
        decay = jnp.exp(-t * jnp.abs(delta_ref[...]))
        row = lax.broadcasted_iota(jnp.int32, (seq, c_), 0)
        for c in range(4):
            kc = dot_hi(h, wout_ref[:, c * c_:(c + 1) * c_]) * decay
            if c % 2 == 1:
                kc = jnp.where(row == 0, 0.0, kc)
            hcat_ref[:, c * c_:(c + 1) * c_] = kc.astype(BF16)

    p = _dot(f_ref[...], hcat_ref[...])
    is_sin = (pl.program_id(0) * FILTER_ROWS // (FREQ_BLOCK // 2)) % 2
    sign = (1 - 2 * is_sin).astype(F32)
    scale = 1.0 / seq
    for o in range(2):
        pf = p[:, (2 * o) * c_:(2 * o + 1) * c_]
        pb = p[:, (2 * o + 1) * c_:(2 * o + 2) * c_]
        h_ref[o] = (pf + sign * pb) * scale


def _hy_filter(seq, feat, w1, b1, w2, b2, w3, b3, wout, freq, delta, fmat, later_weights):
    full = lambda a: _resident(a.shape)
    assert (FREQ_BLOCK // 2) % FILTER_ROWS == 0
    steps = 2 * seq // FILTER_ROWS
    cast_in, cast_out, cast_shapes = _cast_rider(later_weights, steps)
    return pl.pallas_call(
        functools.partial(_filter_kernel, seq),
        grid=(steps,),
        in_specs=[full(feat), full(w1), full(b1), full(w2), full(b2), full(w3), full(b3), full(wout),
                  full(freq), full(delta), pl.BlockSpec((FILTER_ROWS, seq), lambda j: (j, 0))] + cast_in,
        out_specs=[pl.BlockSpec((2, FILTER_ROWS, D_HYENA), lambda j: (0, j, 0))] + cast_out,
        out_shape=[jax.ShapeDtypeStruct((2, 2 * seq, D_HYENA), F32)] + cast_shapes,
        scratch_shapes=[pltpu.VMEM((seq, 4 * D_HYENA), BF16)],
        compiler_params=pltpu.CompilerParams(dimension_semantics=("arbitrary",),
                                             vmem_limit_bytes=V7X_VMEM_LIMIT_BYTES),
        name="hy_filter",
    )(feat, w1, b1, w2, b2, w3, b3, wout, freq, delta, fmat, *later_weights)


def _short_conv(cur, before, after, w_ref, b_ref):
    rows = cur.shape[0]
    row = lax.broadcasted_iota(jnp.int32, cur.shape, 0)
    prev = jnp.where(row == 0, before, pltpu.roll(cur, 1, axis=0))
    nxt = jnp.where(row == rows - 1, after, pltpu.roll(cur, rows - 1, axis=0))
    return b_ref[0] + prev * w_ref[0, 0:1, :] + cur * w_ref[0, 1:2, :] + nxt * w_ref[0, 2:3, :]


def _short_conv_rows(u_ref, w_ref, b_ref, t0, seq):
    zero = jnp.zeros((1, u_ref.shape[2]), F32)
    before = u_ref[0, t0 - V7X_SUBLANES:t0, :][V7X_SUBLANES - 1:] if t0 > 0 else zero
    end = t0 + CONV_ROWS
    after = u_ref[0, end:end + V7X_SUBLANES, :][:1] if end < seq else zero
    return _short_conv(u_ref[0, t0:end, :], before, after, w_ref, b_ref)


def _hyena_kernel(seq, v_ref, gate_ref, gate_lo_ref, gate_hi_ref, wv_ref, bv_ref, wg_ref, bg_ref, skip_ref,
                  gain_ref, f_ref, h_ref, out_ref, zf_ref, zb_ref, acc_ref, g_ref, y_ref):
    o = pl.program_id(1)
    j = pl.program_id(2)
    nfb = 2 * seq // FREQ_BLOCK
    gate_rows = seq // nfb
    chunks = range(0, seq, CONV_ROWS)
    hb = FREQ_BLOCK // 2

    def f_block(jb):
        return f_ref[pl.ds(pl.multiple_of(jb * FREQ_BLOCK, FREQ_BLOCK), FREQ_BLOCK), :]

    def forward(jb):
        zfreq = _dot(f_block(jb), zb_ref[...])
        zr, zi = zfreq[:hb], zfreq[hb:]
        hr, hi = h_ref[0, :hb, :], h_ref[0, hb:, :]
        y_ref[jb % 2, :, :hb] = (zr * hr - zi * hi).T.astype(BF16)
        y_ref[jb % 2, :, hb:] = (zr * hi + zi * hr).T.astype(BF16)

    def inverse(jb):
        acc_ref[...] += _dot(y_ref[jb % 2], f_block(jb))

    def gate_chunk():
        before = jnp.where(j > 0, gate_lo_ref[0, V7X_SUBLANES - 1:, :], 0.0)
        after = jnp.where(j < nfb - 1, gate_hi_ref[0, :1, :], 0.0)
        t0 = pl.multiple_of(j * gate_rows, gate_rows)
        g_ref[pl.ds(t0, gate_rows), :] = _short_conv(gate_ref[0], before, after, wg_ref, bg_ref)

    @pl.when((o == 0) & (j == 0))
    def _():
        for t0 in chunks:
            z = _short_conv_rows(v_ref, wv_ref, bv_ref, t0, seq)
            zf_ref[t0:t0 + CONV_ROWS, :] = z
            zb_ref[t0:t0 + CONV_ROWS, :] = z.astype(BF16)

    @pl.when(j == 0)
    def _():
        gate_chunk()
        acc_ref[...] = jnp.zeros_like(acc_ref)
        forward(0)

    @pl.when((j > 0) & (j < nfb))
    def _():
        gate_chunk()
        inverse(j - 1)
        forward(j)

    def gated(rows):
        return g_ref[rows, :] * (acc_ref[:, rows].T + zf_ref[rows, :] * skip_ref[0])

    @pl.when((j == nfb) & (o == 0))
    def _():
        inverse(nfb - 1)
        for t0 in chunks:
            rows = slice(t0, t0 + CONV_ROWS)
            z = gated(rows)
            zf_ref[rows, :] = z
            zb_ref[rows, :] = z.astype(BF16)

    @pl.when((j == nfb) & (o == 1))
    def _():
        inverse(nfb - 1)
        for t0 in chunks:
            rows = slice(t0, t0 + CONV_ROWS)
            out_ref[0, rows, :] = _rms(gated(rows), gain_ref[...]).astype(BF16)


def _hyena(hy, conv_w, conv_b, skip, gain, fmat, hspec):
    b, seq, _ = hy.shape
    c = D_HYENA
    nfb = 2 * seq // FREQ_BLOCK
    gate_rows = seq // nfb
    halo_per_chunk = gate_rows // V7X_SUBLANES
    fwd_blk = lambda j: jnp.minimum(j, nfb - 1)
    wpart = lambda sel: pl.BlockSpec((1, 3, c), lambda bi, o, j: (sel(o), 0, 0))
    bpart = lambda sel: pl.BlockSpec((1, 1, c), lambda bi, o, j: (sel(o), 0, 0))
    value = lambda o: 0
    gate = lambda o: 1 + o
    halo = lambda first: pl.BlockSpec((1, V7X_SUBLANES, c), lambda bi, o, j: (bi, first(fwd_blk(j)), 1 + o))
    return pl.pallas_call(
        functools.partial(_hyena_kernel, seq),
        grid=(b, 2, nfb + 1),
        in_specs=[pl.BlockSpec((1, seq, c), lambda bi, o, j: (bi, 0, 0), pipeline_mode=pl.Buffered(1)),
                  pl.BlockSpec((1, gate_rows, c), lambda bi, o, j: (bi, fwd_blk(j), 1 + o)),
                  halo(lambda jc: jnp.maximum(jc * halo_per_chunk - 1, 0)),
                  halo(lambda jc: jnp.minimum((jc + 1) * halo_per_chunk, seq // V7X_SUBLANES - 1)),
                  wpart(value), bpart(value), wpart(gate), bpart(gate),
                  pl.BlockSpec((1, 1, c), lambda bi, o, j: (o, 0, 0)),
                  pl.BlockSpec((1, c), lambda bi, o, j: (0, 0)),
                  _resident(fmat.shape),
                  pl.BlockSpec((1, FREQ_BLOCK, c), lambda bi, o, j: (o, fwd_blk(j), 0))],
        out_specs=pl.BlockSpec((1, seq, c), lambda bi, o, j: (bi, 0, 0), pipeline_mode=pl.Buffered(1)),
        out_shape=jax.ShapeDtypeStruct((b, seq, c), BF16),
        scratch_shapes=[pltpu.VMEM((seq, c), F32), pltpu.VMEM((seq, c), BF16), pltpu.VMEM((c, seq), F32),
                        pltpu.VMEM((seq, c), F32), pltpu.VMEM((2, c, FREQ_BLOCK), BF16)],
        compiler_params=pltpu.CompilerParams(dimension_semantics=("parallel", "arbitrary", "arbitrary"),
                                             vmem_limit_bytes=V7X_VMEM_LIMIT_BYTES),
        name="hyena",
    )(hy, hy, hy, hy, conv_w, conv_b, conv_w, conv_b, skip, gain, fmat, hspec)


def _attn_kernel(seq, *refs):
    npat = len(DILATIONS)
    q_refs, k_refs, v_refs = refs[:npat], refs[npat:2 * npat], refs[2 * npat:3 * npat]
    bias_ref, bias16_ref, o_ref, res_ref, nat_ref, s_ref, e_ref = refs[3 * npat:]
    lanes = V7X_LANES
    first = lax.broadcasted_iota(jnp.int32, (Q_BLOCK, lanes), 1) < HEAD_DIM

    def run_pattern(p, nkeys, placement):
        first_k = lax.broadcasted_iota(jnp.int32, (nkeys, lanes), 1) < HEAD_DIM

        def body(g, carry):
            blocks = [placement(g * ATTN_GROUP + i) for i in range(ATTN_GROUP)]
            for i, (row0, krow0, bias_of_head) in enumerate(blocks):
                kw = k_refs[p][0, pl.ds(krow0, nkeys), :]
                q = q_refs[p][0, pl.ds(row0, Q_BLOCK), :]
                for h, qh in enumerate((jnp.where(first, q, 0), jnp.where(first, 0, q))):
                    s = lax.dot_general(qh, kw, (((1,), (1,)), ((), ())), preferred_element_type=F32)
                    s_ref[2 * i + h, :, :nkeys] = s + bias_of_head(h)
            for i, (row0, krow0, bias_of_head) in enumerate(blocks):
                ms = []
                for h in range(2):
                    s = s_ref[2 * i + h, :, :nkeys]
                    m = jnp.max(s, axis=-1, keepdims=True)
                    e_ref[2 * i + h, :, :nkeys] = jnp.exp2(s - m).astype(BF16)
                    ms.append(m)
                res_ref[p, 1, pl.ds(row0, Q_BLOCK), :] = jnp.where(first, ms[0], ms[1])
            for i, (row0, krow0, bias_of_head) in enumerate(blocks):
                vw = v_refs[p][0, pl.ds(krow0, nkeys), :]
                o0 = _dot(e_ref[2 * i, :, :nkeys], jnp.where(first_k, vw, 1))
                o1 = _dot(e_ref[2 * i + 1, :, :nkeys], jnp.where(first_k, 1, vw))
                res_ref[p, 0, pl.ds(row0, Q_BLOCK), :] = jnp.where(first, o0, o1)
                res_ref[p, 2, pl.ds(row0, Q_BLOCK), :] = jnp.where(first, o1, o0)
            return carry

        lax.fori_loop(0, seq // Q_BLOCK // ATTN_GROUP, body, 0)

    for p, d in enumerate(DILATIONS[:2]):
        ls = seq // d
        nblk = ls // Q_BLOCK

        def banded(n, p=p, ls=ls, nblk=nblk):
            r = n // nblk
            ib = n % nblk
            i0 = ib * Q_BLOCK
            k0 = jnp.clip(i0 - HALF_WINDOW, 0, ls - K_WINDOW)
            case = jnp.where(ib == 0, 0, jnp.where(ib == nblk - 1, 2, 1))
            row0 = pl.multiple_of(r * ls + i0, Q_BLOCK)
            krow0 = pl.multiple_of(r * ls + k0, HALF_WINDOW)
            return row0, krow0, lambda h: bias_ref[0, (p * 3 + case) * 2 + h]

        run_pattern(p, K_WINDOW, banded)

    def full(n):
        row0 = pl.multiple_of(n * Q_BLOCK, Q_BLOCK)
        return row0, row0, lambda h: bias16_ref[0, h]

    run_pattern(2, Q_BLOCK, full)

    for p, d in enumerate(DILATIONS[1:], start=1):
        ls = seq // d
        for r in range(d):
            for kind in range(3):
                nat_ref[p - 1, kind, pl.ds(r, ls, stride=d), :] = res_ref[p, kind, r * ls:(r + 1) * ls, :]

    for t0 in range(0, seq, CONV_ROWS):
        rows = slice(t0, t0 + CONV_ROWS)
        parts = [tuple(res_ref[0, kind, rows, :] for kind in range(3))]
        parts += [tuple(nat_ref[p, kind, rows, :] for kind in range(3)) for p in range(2)]
        m = functools.reduce(jnp.maximum, [pt[1] for pt in parts])
        num = 0.0
        den = 0.0
        for out, mp, lp in parts:
            w = jnp.exp2(mp - m)
            num = num + w * out
            den = den + w * pltpu.roll(lp, HEAD_DIM, axis=1)
        o_ref[0, rows, :] = num / den


def _attn_bias_tables():
    slopes = np.array([2.0 ** (-8.0 * (i + 1) / N_HEADS) for i in range(N_HEADS)], np.float32)
    slopes = jnp.asarray(slopes.reshape(N_HEADS // 2, 1, 2, 1, 1))
    qi = lax.broadcasted_iota(jnp.int32, (Q_BLOCK, K_WINDOW), 0)
    kj = lax.broadcasted_iota(jnp.int32, (Q_BLOCK, K_WINDOW), 1)
    offsets = (0, -HALF_WINDOW, -2 * HALF_WINDOW)
    dist = jnp.stack([jnp.abs(kj - qi + off) for off in offsets])
    valid = dist <= HALF_WINDOW
    dil = jnp.asarray(np.array(DILATIONS[:2], np.float32).reshape(2, 1, 1, 1))
    scaled = (dil * dist.astype(F32)[None])[None, :, :, None]
    banded = jnp.where(valid[None, None, :, None], -slopes[:, None] * scaled, NEG_INF)
    banded = banded.reshape(N_HEADS // 2, 12, Q_BLOCK, K_WINDOW)
    d16 = dist[0, :, :Q_BLOCK]
    full = jnp.where(d16 <= HALF_WINDOW, -slopes[:, 0] * (DILATIONS[2] * d16.astype(F32)), NEG_INF)
    to_base2 = lambda t: jnp.where(t > 0.5 * NEG_INF, t * LOG2_E, NEG_INF).astype(F32)
    return to_base2(banded), to_base2(full)


def _dil_attn(qkv):
    b, seq, _ = qkv[0].shape
    nhp = N_HEADS // 2
    bias, bias16 = _attn_bias_tables()
    head_pair = pl.BlockSpec((1, seq, V7X_LANES), lambda bi, hp: (bi, 0, hp))
    return pl.pallas_call(
        functools.partial(_attn_kernel, seq),
        grid=(b, nhp),
        in_specs=[head_pair] * len(qkv) + [
            pl.BlockSpec((1, 12, Q_BLOCK, K_WINDOW), lambda bi, hp: (hp, 0, 0, 0)),
            pl.BlockSpec((1, 2, Q_BLOCK, Q_BLOCK), lambda bi, hp: (hp, 0, 0, 0))],
        out_specs=head_pair,
        out_shape=jax.ShapeDtypeStruct((b, seq, D_ATTN), F32),
        scratch_shapes=[pltpu.VMEM((3, 3, seq, V7X_LANES), F32),
                        pltpu.VMEM((2, 3, seq, V7X_LANES), F32),
                        pltpu.VMEM((2 * ATTN_GROUP, Q_BLOCK, K_WINDOW), F32),
                        pltpu.VMEM((2 * ATTN_GROUP, Q_BLOCK, K_WINDOW), BF16)],
        compiler_params=pltpu.CompilerParams(dimension_semantics=("parallel", "parallel"),
                                             vmem_limit_bytes=V7X_VMEM_LIMIT_BYTES),
        name="dil_attn",
    )(*qkv, bias, bias16)


def _out_ffn2_kernel(x1_ref, yh_ref, ya_ref, ga_ref, wo_ref, g3_ref, wg_ref, wu_ref, wd_ref, gf_ref,
                     out_ref, act_ref):
    ya = _rms(ya_ref[...], ga_ref[...]).astype(BF16)
    mix = _dot(yh_ref[...], wo_ref[:D_HYENA, :]) + _dot(ya, wo_ref[D_HYENA:, :])
    x2 = x1_ref[...] + mix
    h = _rms(x2, g3_ref[...]).astype(BF16)
    x3 = x2 + 0.5 * _swiglu(h, wg_ref, wu_ref, wd_ref, act_ref)
    out_ref[...] = _rms(x3, gf_ref[...])


def _out_ffn2(x1, yh, ya, ga, wo, g3, wg, wu, wd, gf):
    n = x1.shape[0]
    row = lambda w: pl.BlockSpec((ROW_TILE, w), lambda i: (i, 0))
    return pl.pallas_call(
        _out_ffn2_kernel,
        grid=(n // ROW_TILE,),
        in_specs=[row(D_MODEL), row(D_HYENA), row(D_ATTN), _resident((1, D_ATTN)), _resident(wo.shape),
                  _resident((1, D_MODEL)), _resident(wg.shape), _resident(wu.shape), _resident(wd.shape),
                  _resident((1, D_MODEL))],
        out_specs=row(D_MODEL),
        out_shape=jax.ShapeDtypeStruct((n, D_MODEL), F32),
        scratch_shapes=[pltpu.VMEM((ROW_TILE, D_FF), BF16)],
        compiler_params=pltpu.CompilerParams(dimension_semantics=("parallel",),
                                             vmem_limit_bytes=V7X_VMEM_LIMIT_BYTES),
        name="out_ffn2",
    )(x1, yh, ya, ga, wo, g3, wg, wu, wd, gf)


def _dft_matrices(seq):
    n = 2 * seq
    split = 64
    k = np.arange(seq, dtype=np.int64)[:, None]
    ang = lambda s: np.pi * (((2 * k + 1) * s) % (2 * n)) / n
    a1 = ang(split * np.arange(seq // split, dtype=np.int64)[None, :])
    a0 = ang(np.arange(split, dtype=np.int64)[None, :])
    hb = FREQ_BLOCK // 2

    def rows_of_f(x, quarter_turn):
        cos_rows = x.reshape(seq // hb, 1, hb, -1)
        return jnp.asarray(np.concatenate([cos_rows, quarter_turn.reshape(cos_rows.shape)], axis=1)
                           .reshape(n, -1).astype(np.float32))

    c1 = rows_of_f(np.cos(a1), -np.sin(a1))[:, :, None]
    s1 = rows_of_f(np.sin(a1), np.cos(a1))[:, :, None]
    c0 = rows_of_f(np.cos(a0), np.cos(a0))[:, None, :]
    s0 = rows_of_f(np.sin(a0), np.sin(a0))[:, None, :]
    return (c1 * c0 - s1 * s0).reshape(n, seq).astype(BF16)


def _filter_features(seq):
    t = jnp.linspace(0.0, 1.0, seq, dtype=F32)[:, None]
    w = 2.0 * math.pi * jnp.arange(seq, dtype=F32)[:, None] / seq
    f = jnp.linspace(1e-4, FILTER_BANDS - 1, FILTER_BANDS, dtype=F32)[None, :]
    z = jnp.concatenate([t, jnp.cos(f * w), -jnp.sin(f * w)], axis=-1)
    return jnp.pad(z, ((0, 0), (0, FEAT_PAD - FILTER_EMB)))


def _decay_rates():
    max_decay = math.log(DECAY_TARGET) / FAST_DECAY_PCT
    min_decay = math.log(DECAY_TARGET) / SLOW_DECAY_PCT
    return jnp.linspace(min_decay, max_decay, D_HYENA, dtype=F32)[None, :]


def kernel(x, ffn1_norm_g, ffn1_w_gate, ffn1_w_up, ffn1_w_down, mix_norm_g, w_in, hy_conv_w, hy_conv_b, hy_filt_w1, hy_filt_b1, hy_filt_w2, hy_filt_b2, hy_filt_w3, hy_filt_b3, hy_filt_w_out, hy_filt_freq, hy_filt_skip, hy_out_norm_g, attn_out_norm_g, w_out, ffn2_norm_g, ffn2_w_gate, ffn2_w_up, ffn2_w_down, final_norm_g):
    b, seq, d = x.shape
    assert d == D_MODEL and (b * seq) % ROW_TILE == 0 and seq % (DILATIONS[-1] * Q_BLOCK) == 0
    row = lambda a: a.reshape(1, -1).astype(F32)
    f32 = lambda a: a.astype(F32)

    fmat = _dft_matrices(seq)
    w1 = jnp.pad(f32(hy_filt_w1), ((0, FEAT_PAD - FILTER_EMB), (0, 0)))
    hspec, wg1, wu1, wd1, win = _hy_filter(
        seq, _filter_features(seq), w1, row(hy_filt_b1), f32(hy_filt_w2), row(hy_filt_b2), f32(hy_filt_w3),
        row(hy_filt_b3), f32(hy_filt_w_out), row(hy_filt_freq), _decay_rates(), fmat,
        [f32(ffn1_w_gate), f32(ffn1_w_up), f32(ffn1_w_down), f32(w_in)])

    x1, hy, *rest = _ffn1_proj(
        x, row(ffn1_norm_g), wg1, wu1, wd1, row(mix_norm_g), win,
        [f32(w_out), f32(ffn2_w_gate), f32(ffn2_w_up), f32(ffn2_w_down)])
    qkv, (wo, wg2, wu2, wd2) = rest[:N_QKV_OUT], rest[N_QKV_OUT:]

    conv_w = hy_conv_w.astype(F32).reshape(3, 3, D_HYENA).transpose(1, 0, 2)
    conv_b = hy_conv_b.astype(F32).reshape(3, 1, D_HYENA)
    skip = hy_filt_skip.astype(F32).reshape(2, 1, D_HYENA)
    y_hy = _hyena(hy.reshape(b, seq, 3 * D_HYENA), conv_w, conv_b, skip, row(hy_out_norm_g),
                  fmat, hspec)

    y_at = _dil_attn([a.reshape(b, seq, D_ATTN) for a in qkv])

    out = _out_ffn2(x1, y_hy.reshape(b * seq, D_HYENA), y_at.reshape(b * seq, D_ATTN),
                    row(attn_out_norm_g), wo, row(ffn2_norm_g), wg2, wu2, wd2, row(final_norm_g))
    return out.reshape(b, seq, d)
```

```python
import functools
import math

import numpy as np
import jax
import jax.numpy as jnp
from jax import lax
from jax.experimental import pallas as pl
from jax.experimental.pallas import tpu as pltpu

F32 = jnp.float32
BF16 = jnp.bfloat16

D_MODEL = 1024
D_HYENA = 512
D_ATTN = 512
HEAD_DIM = 64
N_HEADS = D_ATTN // HEAD_DIM
D_FF = 2816
FILTER_EMB = 33
FILTER_BANDS = 16
FILTER_WIDTH = 64
DECAY_TARGET = 1e-2
FAST_DECAY_PCT = 0.3
SLOW_DECAY_PCT = 1.5
DILATIONS = (1, 4, 16)
HALF_WINDOW = 64
RMS_EPS = 1e-6
NEG_INF = -1e30
LOG2_E = math.log2(math.e)

V7X_LANES = 128
V7X_SUBLANES = 8
V7X_BF16_SUBLANES = 16
V7X_VMEM_LIMIT_BYTES = 56 * 1024 * 1024

ROW_TILE = 512
FF_CHUNK = 256
FREQ_BLOCK = 1024
FILTER_ROWS = 512
Q_BLOCK = 128
K_WINDOW = 256
ATTN_GROUP = 8
CONV_ROWS = 256
FEAT_PAD = 128


def _dot(a, b):
    return jnp.dot(a, b, preferred_element_type=F32)


def _rms(x, g):
    return x * lax.rsqrt(jnp.mean(x * x, axis=-1, keepdims=True) + RMS_EPS) * g


def _swiglu(h, wg_ref, wu_ref, wd_ref, act_ref):
    for c in range(D_FF // FF_CHUNK):
        cols = slice(c * FF_CHUNK, (c + 1) * FF_CHUNK)
        g = _dot(h, wg_ref[:, cols])
        u = _dot(h, wu_ref[:, cols])
        act_ref[:, cols] = (g * jax.nn.sigmoid(g) * u).astype(BF16)
    return _dot(act_ref[...], wd_ref[...])


def _resident(shape):
    return pl.BlockSpec(shape, lambda *_: (0,) * len(shape), pipeline_mode=pl.Buffered(1))


def _cast_rider(weights, steps):
    in_specs, out_specs, out_shapes = [], [], []
    for w in weights:
        rows, cols = w.shape
        visits = 1
        while (rows * visits) % steps or (rows * visits // steps) % V7X_BF16_SUBLANES:
            visits *= 2
        slab = pl.BlockSpec((rows * visits // steps, cols), lambda i, visits=visits: (i // visits, 0))
        in_specs.append(slab)
        out_specs.append(slab)
        out_shapes.append(jax.ShapeDtypeStruct(w.shape, BF16))
    return in_specs, out_specs, out_shapes


def _cast_slabs(src_refs, dst_refs):
    for src, dst in zip(src_refs, dst_refs):
        dst[...] = src[...].astype(BF16)


def _ffn1_proj_kernel(x_ref, g1_ref, wg_ref, wu_ref, wd_ref, g2_ref, win_ref, *rest):
    n_cast = (len(rest) - 6) // 2
    x1_ref, hy_ref, q_ref, k_ref, v_ref = rest[n_cast:n_cast + 5]
    act_ref = rest[-1]
    _cast_slabs(rest[:n_cast], rest[n_cast + 5:-1])
    x = x_ref[...]
    h = _rms(x, g1_ref[...]).astype(BF16)
    x1 = x + 0.5 * _swiglu(h, wg_ref, wu_ref, wd_ref, act_ref)
    x1_ref[...] = x1
    h2 = _rms(x1, g2_ref[...]).astype(BF16)
    nh = 3 * D_HYENA
    hy_ref[...] = _dot(h2, win_ref[:, :nh])
    q_ref[...] = _dot(h2, win_ref[:, nh:nh + D_ATTN])
    k_ref[...] = _dot(h2, win_ref[:, nh + D_ATTN:nh + 2 * D_ATTN])
    v_ref[...] = _dot(h2, win_ref[:, nh + 2 * D_ATTN:])


def _ffn1_proj(x2d, g1, wg, wu, wd, g2, win, later_weights):
    n = x2d.shape[0]
    steps = n // ROW_TILE
    row = lambda w: pl.BlockSpec((ROW_TILE, w), lambda i: (i, 0))
    cast_in, cast_out, cast_shapes = _cast_rider(later_weights, steps)
    return pl.pallas_call(
        _ffn1_proj_kernel,
        grid=(steps,),
        in_specs=[row(D_MODEL), _resident((1, D_MODEL)), _resident(wg.shape), _resident(wu.shape),
                  _resident(wd.shape), _resident((1, D_MODEL)), _resident(win.shape)] + cast_in,
        out_specs=[row(D_MODEL), row(3 * D_HYENA), row(D_ATTN), row(D_ATTN), row(D_ATTN)] + cast_out,
        out_shape=[jax.ShapeDtypeStruct((n, D_MODEL), F32),
                   jax.ShapeDtypeStruct((n, 3 * D_HYENA), F32),
                   jax.ShapeDtypeStruct((n, D_ATTN), F32),
                   jax.ShapeDtypeStruct((n, D_ATTN), F32),
                   jax.ShapeDtypeStruct((n, D_ATTN), F32)] + cast_shapes,
        scratch_shapes=[pltpu.VMEM((ROW_TILE, D_FF), BF16)],
        compiler_params=pltpu.CompilerParams(dimension_semantics=("arbitrary",),
                                             vmem_limit_bytes=V7X_VMEM_LIMIT_BYTES),
        name="ffn1_proj",
    )(x2d, g1, wg, wu, wd, g2, win, *later_weights)


def _filter_kernel(seq, feat_ref, w1_ref, b1_ref, w2_ref, b2_ref, w3_ref, b3_ref, wout_ref,
                   freq_ref, delta_ref, f_ref, *rest):
    n_cast = (len(rest) - 2) // 2
    h_ref, hcat_ref = rest[n_cast], rest[-1]
    _cast_slabs(rest[:n_cast], rest[n_cast + 1:-1])
    c_ = D_HYENA

    @pl.when(pl.program_id(0) == 0)
    def _():
        hi = lax.Precision.HIGHEST
        dot_hi = lambda a, b: jnp.dot(a, b, precision=hi, preferred_element_type=F32)
        freq = freq_ref[...]
        h = jnp.sin(freq * (dot_hi(feat_ref[...], w1_ref[...]) + b1_ref[...]))
        h = jnp.sin(freq * (dot_hi(h, w2_ref[...]) + b2_ref[...]))
        h = jnp.sin(freq * (dot_hi(h, w3_ref[...]) + b3_ref[...]))
        t = lax.broadcasted_iota(jnp.int32, (seq, 1), 0).astype(F32) * (1.0 / (seq - 1))
        decay = jnp.exp(-t * jnp.abs(delta_ref[...]))
        row = lax.broadcasted_iota(jnp.int32, (seq, c_), 0)
        for c in range(4):
            kc = dot_hi(h, wout_ref[:, c * c_:(c + 1) * c_]) * decay
            if c % 2 == 1:
                kc = jnp.where(row == 0, 0.0, kc)
            hcat_ref[:, c * c_:(c + 1) * c_] = kc.astype(BF16)

    is_sin = (pl.program_id(0) * FILTER_ROWS // (FREQ_BLOCK // 2)) % 2
    sign = (1 - 2 * is_sin).astype(F32)
    scale = 1.0 / seq
    for o in range(2):
        p = _dot(f_ref[...], hcat_ref[:, (2 * o) * c_:(2 * o + 2) * c_])
        h_ref[o] = (p[:, :c_] + sign * p[:, c_:]) * scale


def _hy_filter(seq, feat, w1, b1, w2, b2, w3, b3, wout, freq, delta, fmat, later_weights):
    full = lambda a: _resident(a.shape)
    assert (FREQ_BLOCK // 2) % FILTER_ROWS == 0
    steps = 2 * seq // FILTER_ROWS
    cast_in, cast_out, cast_shapes = _cast_rider(later_weights, steps)
    return pl.pallas_call(
        functools.partial(_filter_kernel, seq),
        grid=(steps,),
        in_specs=[full(feat), full(w1), full(b1), full(w2), full(b2), full(w3), full(b3), full(wout),
                  full(freq), full(delta), pl.BlockSpec((FILTER_ROWS, seq), lambda j: (j, 0))] + cast_in,
        out_specs=[pl.BlockSpec((2, FILTER_ROWS, D_HYENA), lambda j: (0, j, 0))] + cast_out,
        out_shape=[jax.ShapeDtypeStruct((2, 2 * seq, D_HYENA), F32)] + cast_shapes,
        scratch_shapes=[pltpu.VMEM((seq, 4 * D_HYENA), BF16)],
        compiler_params=pltpu.CompilerParams(dimension_semantics=("arbitrary",),
                                             vmem_limit_bytes=V7X_VMEM_LIMIT_BYTES),
        name="hy_filter",
    )(feat, w1, b1, w2, b2, w3, b3, wout, freq, delta, fmat, *later_weights)


def _short_conv(cur, before, after, w_ref, b_ref):
    rows = cur.shape[0]
    row = lax.broadcasted_iota(jnp.int32, cur.shape, 0)
    prev = jnp.where(row == 0, before, pltpu.roll(cur, 1, axis=0))
    nxt = jnp.where(row == rows - 1, after, pltpu.roll(cur, rows - 1, axis=0))
    return b_ref[0] + prev * w_ref[0, 0:1, :] + cur * w_ref[0, 1:2, :] + nxt * w_ref[0, 2:3, :]


def _short_conv_rows(u_ref, w_ref, b_ref, t0, seq):
    zero = jnp.zeros((1, u_ref.shape[2]), F32)
    before = u_ref[0, t0 - V7X_SUBLANES:t0, :][V7X_SUBLANES - 1:] if t0 > 0 else zero
    end = t0 + CONV_ROWS
    after = u_ref[0, end:end + V7X_SUBLANES, :][:1] if end < seq else zero
    return _short_conv(u_ref[0, t0:end, :], before, after, w_ref, b_ref)


def _hyena_kernel(seq, v_ref, gate_ref, gate_lo_ref, gate_hi_ref, wv_ref, bv_ref, wg_ref, bg_ref, skip_ref,
                  gain_ref, f_ref, h_ref, out_ref, zf_ref, zb_ref, acc_ref, g_ref, y_ref):
    o = pl.program_id(1)
    j = pl.program_id(2)
    nfb = 2 * seq // FREQ_BLOCK
    gate_rows = seq // nfb
    chunks = range(0, seq, CONV_ROWS)
    hb = FREQ_BLOCK // 2

    def f_block(jb):
        return f_ref[pl.ds(pl.multiple_of(jb * FREQ_BLOCK, FREQ_BLOCK), FREQ_BLOCK), :]

    def forward(jb):
        zfreq = _dot(f_block(jb), zb_ref[...])
        zr, zi = zfreq[:hb], zfreq[hb:]
        hr, hi = h_ref[0, :hb, :], h_ref[0, hb:, :]
        y_ref[jb % 2, :, :hb] = (zr * hr - zi * hi).T.astype(BF16)
        y_ref[jb % 2, :, hb:] = (zr * hi + zi * hr).T.astype(BF16)

    def inverse(jb):
        acc_ref[...] += _dot(y_ref[jb % 2], f_block(jb))

    def gate_chunk():
        before = jnp.where(j > 0, gate_lo_ref[0, V7X_SUBLANES - 1:, :], 0.0)
        after = jnp.where(j < nfb - 1, gate_hi_ref[0, :1, :], 0.0)
        t0 = pl.multiple_of(j * gate_rows, gate_rows)
        g_ref[pl.ds(t0, gate_rows), :] = _short_conv(gate_ref[0], before, after, wg_ref, bg_ref)

    @pl.when((o == 0) & (j == 0))
    def _():
        for t0 in chunks:
            z = _short_conv_rows(v_ref, wv_ref, bv_ref, t0, seq)
            zf_ref[t0:t0 + CONV_ROWS, :] = z
            zb_ref[t0:t0 + CONV_ROWS, :] = z.astype(BF16)

    @pl.when(j == 0)
    def _():
        gate_chunk()
        acc_ref[...] = jnp.zeros_like(acc_ref)
        forward(0)

    @pl.when((j > 0) & (j < nfb))
    def _():
        gate_chunk()
        inverse(j - 1)
        forward(j)

    def gated(rows):
        return g_ref[rows, :] * (acc_ref[:, rows].T + zf_ref[rows, :] * skip_ref[0])

    @pl.when((j == nfb) & (o == 0))
    def _():
        inverse(nfb - 1)
        for t0 in chunks:
            rows = slice(t0, t0 + CONV_ROWS)
            z = gated(rows)
            zf_ref[rows, :] = z
            zb_ref[rows, :] = z.astype(BF16)

    @pl.when((j == nfb) & (o == 1))
    def _():
        inverse(nfb - 1)
        for t0 in chunks:
            rows = slice(t0, t0 + CONV_ROWS)
            out_ref[0, rows, :] = _rms(gated(rows), gain_ref[...]).astype(BF16)


def _hyena(hy, conv_w, conv_b, skip, gain, fmat, hspec):
    b, seq, _ = hy.shape
    c = D_HYENA
    nfb = 2 * seq // FREQ_BLOCK
    gate_rows = seq // nfb
    halo_per_chunk = gate_rows // V7X_SUBLANES
    fwd_blk = lambda j: jnp.minimum(j, nfb - 1)
    wpart = lambda sel: pl.BlockSpec((1, 3, c), lambda bi, o, j: (sel(o), 0, 0))
    bpart = lambda sel: pl.BlockSpec((1, 1, c), lambda bi, o, j: (sel(o), 0, 0))
    value = lambda o: 0
    gate = lambda o: 1 + o
    halo = lambda first: pl.BlockSpec((1, V7X_SUBLANES, c), lambda bi, o, j: (bi, first(fwd_blk(j)), 1 + o))
    return pl.pallas_call(
        functools.partial(_hyena_kernel, seq),
        grid=(b, 2, nfb + 1),
        in_specs=[pl.BlockSpec((1, seq, c), lambda bi, o, j: (bi, 0, 0), pipeline_mode=pl.Buffered(1)),
                  pl.BlockSpec((1, gate_rows, c), lambda bi, o, j: (bi, fwd_blk(j), 1 + o)),
                  halo(lambda jc: jnp.maximum(jc * halo_per_chunk - 1, 0)),
                  halo(lambda jc: jnp.minimum((jc + 1) * halo_per_chunk, seq // V7X_SUBLANES - 1)),
                  wpart(value), bpart(value), wpart(gate), bpart(gate),
                  pl.BlockSpec((1, 1, c), lambda bi, o, j: (o, 0, 0)),
                  pl.BlockSpec((1, c), lambda bi, o, j: (0, 0)),
                  _resident(fmat.shape),
                  pl.BlockSpec((1, FREQ_BLOCK, c), lambda bi, o, j: (o, fwd_blk(j), 0))],
        out_specs=pl.BlockSpec((1, seq, c), lambda bi, o, j: (bi, 0, 0), pipeline_mode=pl.Buffered(1)),
        out_shape=jax.ShapeDtypeStruct((b, seq, c), BF16),
        scratch_shapes=[pltpu.VMEM((seq, c), F32), pltpu.VMEM((seq, c), BF16), pltpu.VMEM((c, seq), F32),
                        pltpu.VMEM((seq, c), F32), pltpu.VMEM((2, c, FREQ_BLOCK), BF16)],
        compiler_params=pltpu.CompilerParams(dimension_semantics=("parallel", "arbitrary", "arbitrary"),
                                             vmem_limit_bytes=V7X_VMEM_LIMIT_BYTES),
        name="hyena",
    )(hy, hy, hy, hy, conv_w, conv_b, conv_w, conv_b, skip, gain, fmat, hspec)


def _attn_kernel(seq, q_ref, k_ref, v_ref, bias_ref, bias16_ref, o_ref,
                 qs_ref, ks_ref, vs_ref, res_ref, nat_ref, s_ref, e_ref):
    lanes = V7X_LANES
    qscale = LOG2_E / math.sqrt(HEAD_DIM)

    for p, d in enumerate(DILATIONS):
        ls = seq // d
        first = lax.broadcasted_iota(jnp.int32, (ls, lanes), 1) < HEAD_DIM
        for r in range(d):
            src = pl.ds(r, ls, stride=d) if d > 1 else pl.ds(0, ls)
            rows = slice(r * ls, (r + 1) * ls)
            qq = q_ref[0, src, :] * qscale
            qs_ref[p, 0, rows, :] = jnp.where(first, qq, 0.0).astype(BF16)
            qs_ref[p, 1, rows, :] = jnp.where(first, 0.0, qq).astype(BF16)
            ks_ref[p, rows, :] = k_ref[0, src, :].astype(BF16)
            vv = v_ref[0, src, :]
            vs_ref[p, 0, rows, :] = jnp.where(first, vv, 1.0).astype(BF16)
            vs_ref[p, 1, rows, :] = jnp.where(first, 1.0, vv).astype(BF16)

    first = lax.broadcasted_iota(jnp.int32, (Q_BLOCK, lanes), 1) < HEAD_DIM

    def run_pattern(p, nkeys, placement):
        def body(g, carry):
            blocks = [placement(g * ATTN_GROUP + i) for i in range(ATTN_GROUP)]
            for i, (row0, krow0, bias_of_head) in enumerate(blocks):
                kw = ks_ref[p, pl.ds(krow0, nkeys), :]
                for h in range(2):
                    qh = qs_ref[p, h, pl.ds(row0, Q_BLOCK), :]
                    s = lax.dot_general(qh, kw, (((1,), (1,)), ((), ())), preferred_element_type=F32)
                    s_ref[2 * i + h, :, :nkeys] = s + bias_of_head(h)
            for i, (row0, krow0, bias_of_head) in enumerate(blocks):
                ms = []
                for h in range(2):
                    s = s_ref[2 * i + h, :, :nkeys]
                    m = jnp.max(s, axis=-1, keepdims=True)
                    e_ref[2 * i + h, :, :nkeys] = jnp.exp2(s - m).astype(BF16)
                    ms.append(m)
                res_ref[p, 1, pl.ds(row0, Q_BLOCK), :] = jnp.where(first, ms[0], ms[1])
            for i, (row0, krow0, bias_of_head) in enumerate(blocks):
                o0 = _dot(e_ref[2 * i, :, :nkeys], vs_ref[p, 0, pl.ds(krow0, nkeys), :])
                o1 = _dot(e_ref[2 * i + 1, :, :nkeys], vs_ref[p, 1, pl.ds(krow0, nkeys), :])
                res_ref[p, 0, pl.ds(row0, Q_BLOCK), :] = jnp.where(first, o0, o1)
                res_ref[p, 2, pl.ds(row0, Q_BLOCK), :] = jnp.where(first, o1, o0)
            return carry

        lax.fori_loop(0, seq // Q_BLOCK // ATTN_GROUP, body, 0)

    for p, d in enumerate(DILATIONS[:2]):
        ls = seq // d
        nblk = ls // Q_BLOCK

        def banded(n, p=p, ls=ls, nblk=nblk):
            r = n // nblk
            ib = n % nblk
            i0 = ib * Q_BLOCK
            k0 = jnp.clip(i0 - HALF_WINDOW, 0, ls - K_WINDOW)
            case = jnp.where(ib == 0, 0, jnp.where(ib == nblk - 1, 2, 1))
            row0 = pl.multiple_of(r * ls + i0, Q_BLOCK)
            krow0 = pl.multiple_of(r * ls + k0, HALF_WINDOW)
            return row0, krow0, lambda h: bias_ref[0, (p * 3 + case) * 2 + h]

        run_pattern(p, K_WINDOW, banded)

    def full(n):
        row0 = pl.multiple_of(n * Q_BLOCK, Q_BLOCK)
        return row0, row0, lambda h: bias16_ref[0, h]

    run_pattern(2, Q_BLOCK, full)

    for p, d in enumerate(DILATIONS[1:], start=1):
        ls = seq // d
        for r in range(d):
            for kind in range(3):
                nat_ref[p - 1, kind, pl.ds(r, ls, stride=d), :] = res_ref[p, kind, r * ls:(r + 1) * ls, :]

    for t0 in range(0, seq, CONV_ROWS):
        rows = slice(t0, t0 + CONV_ROWS)
        parts = [tuple(res_ref[0, kind, rows, :] for kind in range(3))]
        parts += [tuple(nat_ref[p, kind, rows, :] for kind in range(3)) for p in range(2)]
        m = functools.reduce(jnp.maximum, [pt[1] for pt in parts])
        num = 0.0
        den = 0.0
        for out, mp, lp in parts:
            w = jnp.exp2(mp - m)
            num = num + w * out
            den = den + w * pltpu.roll(lp, HEAD_DIM, axis=1)
        o_ref[0, rows, :] = num / den


def _attn_bias_tables():
    slopes = np.array([2.0 ** (-8.0 * (i + 1) / N_HEADS) for i in range(N_HEADS)], np.float32)
    slopes = jnp.asarray(slopes.reshape(N_HEADS // 2, 1, 2, 1, 1))
    qi = lax.broadcasted_iota(jnp.int32, (Q_BLOCK, K_WINDOW), 0)
    kj = lax.broadcasted_iota(jnp.int32, (Q_BLOCK, K_WINDOW), 1)
    offsets = (0, -HALF_WINDOW, -2 * HALF_WINDOW)
    dist = jnp.stack([jnp.abs(kj - qi + off) for off in offsets])
    valid = dist <= HALF_WINDOW
    dil = jnp.asarray(np.array(DILATIONS[:2], np.float32).reshape(2, 1, 1, 1))
    scaled = (dil * dist.astype(F32)[None])[None, :, :, None]
    banded = jnp.where(valid[None, None, :, None], -slopes[:, None] * scaled, NEG_INF)
    banded = banded.reshape(N_HEADS // 2, 12, Q_BLOCK, K_WINDOW)
    d16 = dist[0, :, :Q_BLOCK]
    full = jnp.where(d16 <= HALF_WINDOW, -slopes[:, 0] * (DILATIONS[2] * d16.astype(F32)), NEG_INF)
    to_base2 = lambda t: jnp.where(t > 0.5 * NEG_INF, t * LOG2_E, NEG_INF).astype(F32)
    return to_base2(banded), to_base2(full)


def _dil_attn(q, k, v):
    b, seq, _ = q.shape
    nhp = N_HEADS // 2
    bias, bias16 = _attn_bias_tables()
    head_pair = pl.BlockSpec((1, seq, V7X_LANES), lambda bi, hp: (bi, 0, hp))
    return pl.pallas_call(
        functools.partial(_attn_kernel, seq),
        grid=(b, nhp),
        in_specs=[head_pair, head_pair, head_pair,
                  pl.BlockSpec((1, 12, Q_BLOCK, K_WINDOW), lambda bi, hp: (hp, 0, 0, 0)),
                  pl.BlockSpec((1, 2, Q_BLOCK, Q_BLOCK), lambda bi, hp: (hp, 0, 0, 0))],
        out_specs=head_pair,
        out_shape=jax.ShapeDtypeStruct((b, seq, D_ATTN), F32),
        scratch_shapes=[pltpu.VMEM((3, 2, seq, V7X_LANES), BF16),
                        pltpu.VMEM((3, seq, V7X_LANES), BF16),
                        pltpu.VMEM((3, 2, seq, V7X_LANES), BF16),
                        pltpu.VMEM((3, 3, seq, V7X_LANES), F32),
                        pltpu.VMEM((2, 3, seq, V7X_LANES), F32),
                        pltpu.VMEM((2 * ATTN_GROUP, Q_BLOCK, K_WINDOW), F32),
                        pltpu.VMEM((2 * ATTN_GROUP, Q_BLOCK, K_WINDOW), BF16)],
        compiler_params=pltpu.CompilerParams(dimension_semantics=("parallel", "parallel"),
                                             vmem_limit_bytes=V7X_VMEM_LIMIT_BYTES),
        name="dil_attn",
    )(q, k, v, bias, bias16)


def _out_ffn2_kernel(x1_ref, yh_ref, ya_ref, ga_ref, wo_ref, g3_ref, wg_ref, wu_ref, wd_ref, gf_ref,
                     out_ref, act_ref):
    ya = _rms(ya_ref[...], ga_ref[...]).astype(BF16)
    mix = _dot(yh_ref[...], wo_ref[:D_HYENA, :]) + _dot(ya, wo_ref[D_HYENA:, :])
    x2 = x1_ref[...] + mix
    h = _rms(x2, g3_ref[...]).astype(BF16)
    x3 = x2 + 0.5 * _swiglu(h, wg_ref, wu_ref, wd_ref, act_ref)
    out_ref[...] = _rms(x3, gf_ref[...])


def _out_ffn2(x1, yh, ya, ga, wo, g3, wg, wu, wd, gf):
    n = x1.shape[0]
    row = lambda w: pl.BlockSpec((ROW_TILE, w), lambda i: (i, 0))
    return pl.pallas_call(
        _out_ffn2_kernel,
        grid=(n // ROW_TILE,),
        in_specs=[row(D_MODEL), row(D_HYENA), row(D_ATTN), _resident((1, D_ATTN)), _resident(wo.shape),
                  _resident((1, D_MODEL)), _resident(wg.shape), _resident(wu.shape), _resident(wd.shape),
                  _resident((1, D_MODEL))],
        out_specs=row(D_MODEL),
        out_shape=jax.ShapeDtypeStruct((n, D_MODEL), F32),
        scratch_shapes=[pltpu.VMEM((ROW_TILE, D_FF), BF16)],
        compiler_params=pltpu.CompilerParams(dimension_semantics=("parallel",),
                                             vmem_limit_bytes=V7X_VMEM_LIMIT_BYTES),
        name="out_ffn2",
    )(x1, yh, ya, ga, wo, g3, wg, wu, wd, gf)


def _dft_matrices(seq):
    n = 2 * seq
    split = 64
    k = np.arange(seq, dtype=np.int64)[:, None]
    ang = lambda s: np.pi * (((2 * k + 1) * s) % (2 * n)) / n
    a1 = ang(split * np.arange(seq // split, dtype=np.int64)[None, :])
    a0 = ang(np.arange(split, dtype=np.int64)[None, :])
    hb = FREQ_BLOCK // 2

    def rows_of_f(x, quarter_turn):
        cos_rows = x.reshape(seq // hb, 1, hb, -1)
        return jnp.asarray(np.concatenate([cos_rows, quarter_turn.reshape(cos_rows.shape)], axis=1)
                           .reshape(n, -1).astype(np.float32))

    c1 = rows_of_f(np.cos(a1), -np.sin(a1))[:, :, None]
    s1 = rows_of_f(np.sin(a1), np.cos(a1))[:, :, None]
    c0 = rows_of_f(np.cos(a0), np.cos(a0))[:, None, :]
    s0 = rows_of_f(np.sin(a0), np.sin(a0))[:, None, :]
    return (c1 * c0 - s1 * s0).reshape(n, seq).astype(BF16)


def _filter_features(seq):
    t = jnp.linspace(0.0, 1.0, seq, dtype=F32)[:, None]
    w = 2.0 * math.pi * jnp.arange(seq, dtype=F32)[:, None] / seq
    f = jnp.linspace(1e-4, FILTER_BANDS - 1, FILTER_BANDS, dtype=F32)[None, :]
    z = jnp.concatenate([t, jnp.cos(f * w), -jnp.sin(f * w)], axis=-1)
    return jnp.pad(z, ((0, 0), (0, FEAT_PAD - FILTER_EMB)))


def _decay_rates():
    max_decay = math.log(DECAY_TARGET) / FAST_DECAY_PCT
    min_decay = math.log(DECAY_TARGET) / SLOW_DECAY_PCT
    return jnp.linspace(min_decay, max_decay, D_HYENA, dtype=F32)[None, :]


def kernel(x, ffn1_norm_g, ffn1_w_gate, ffn1_w_up, ffn1_w_down, mix_norm_g, w_in, hy_conv_w, hy_conv_b, hy_filt_w1, hy_filt_b1, hy_filt_w2, hy_filt_b2, hy_filt_w3, hy_filt_b3, hy_filt_w_out, hy_filt_freq, hy_filt_skip, hy_out_norm_g, attn_out_norm_g, w_out, ffn2_norm_g, ffn2_w_gate, ffn2_w_up, ffn2_w_down, final_norm_g):
    b, seq, d = x.shape
    assert d == D_MODEL and (b * seq) % ROW_TILE == 0 and seq % (DILATIONS[-1] * Q_BLOCK) == 0
    row = lambda a: a.reshape(1, -1).astype(F32)
    f32 = lambda a: a.astype(F32)

    fmat = _dft_matrices(seq)
    w1 = jnp.pad(f32(hy_filt_w1), ((0, FEAT_PAD - FILTER_EMB), (0, 0)))
    hspec, wg1, wu1, wd1, win = _hy_filter(
        seq, _filter_features(seq), w1, row(hy_filt_b1), f32(hy_filt_w2), row(hy_filt_b2), f32(hy_filt_w3),
        row(hy_filt_b3), f32(hy_filt_w_out), row(hy_filt_freq), _decay_rates(), fmat,
        [f32(ffn1_w_gate), f32(ffn1_w_up), f32(ffn1_w_down), f32(w_in)])

    x1, hy, q, k, v, wo, wg2, wu2, wd2 = _ffn1_proj(
        x.reshape(b * seq, d), row(ffn1_norm_g), wg1, wu1, wd1, row(mix_norm_g), win,
        [f32(w_out), f32(ffn2_w_gate), f32(ffn2_w_up), f32(ffn2_w_down)])

    conv_w = hy_conv_w.astype(F32).reshape(3, 3, D_HYENA).transpose(1, 0, 2)
    conv_b = hy_conv_b.astype(F32).reshape(3, 1, D_HYENA)
    skip = hy_filt_skip.astype(F32).reshape(2, 1, D_HYENA)
    y_hy = _hyena(hy.reshape(b, seq, 3 * D_HYENA), conv_w, conv_b, skip, row(hy_out_norm_g),
                  fmat, hspec)

    shape3 = lambda a: a.reshape(b, seq, D_ATTN)
    y_at = _dil_attn(shape3(q), shape3(k), shape3(v))

    out = _out_ffn2(x1, y_hy.reshape(b * seq, D_HYENA), y_at.reshape(b * seq, D_ATTN),
                    row(attn_out_norm_g), wo, row(ffn2_norm_g), wg2, wu2, wd2, row(final_norm_g))
    return out.reshape(b, seq, d)
```

```python
import functools
import math

import numpy as np
import jax
import jax.numpy as jnp
from jax import lax
from jax.experimental import pallas as pl
from jax.experimental.pallas import tpu as pltpu

F32 = jnp.float32
BF16 = jnp.bfloat16

D_MODEL = 1024
D_HYENA = 512
D_ATTN = 512
HEAD_DIM = 64
N_HEADS = D_ATTN // HEAD_DIM
D_FF = 2816
FILTER_EMB = 33
FILTER_BANDS = 16
FILTER_WIDTH = 64
DECAY_TARGET = 1e-2
FAST_DECAY_PCT = 0.3
SLOW_DECAY_PCT = 1.5
DILATIONS = (1, 4, 16)
HALF_WINDOW = 64
RMS_EPS = 1e-6
NEG_INF = -1e30
LOG2_E = math.log2(math.e)

V7X_LANES = 128
V7X_SUBLANES = 8
V7X_BF16_SUBLANES = 16
V7X_VMEM_LIMIT_BYTES = 56 * 1024 * 1024

ROW_TILE = 512
FF_CHUNK = 256
FREQ_BLOCK = 1024
FILTER_ROWS = 512
Q_BLOCK = 128
K_WINDOW = 256
ATTN_GROUP = 16
CONV_ROWS = 256
GATE_SUB_ROWS = 64
FEAT_PAD = 128


def _dot(a, b):
    return jnp.dot(a, b, preferred_element_type=F32)


def _rms(x, g):
    return x * lax.rsqrt(jnp.mean(x * x, axis=-1, keepdims=True) + RMS_EPS) * g


def _swiglu(h, wg_ref, wu_ref, wd_ref, act_ref):
    for c in range(D_FF // FF_CHUNK):
        cols = slice(c * FF_CHUNK, (c + 1) * FF_CHUNK)
        g = _dot(h, wg_ref[:, cols])
        u = _dot(h, wu_ref[:, cols])
        act_ref[:, cols] = (g * jax.nn.sigmoid(g) * u).astype(BF16)
    return _dot(act_ref[...], wd_ref[...])


def _resident(shape):
    return pl.BlockSpec(shape, lambda *_: (0,) * len(shape), pipeline_mode=pl.Buffered(1))


def _cast_rider(weights, steps):
    in_specs, out_specs, out_shapes = [], [], []
    for w in weights:
        rows, cols = w.shape
        visits = 1
        while (rows * visits) % steps or (rows * visits // steps) % V7X_BF16_SUBLANES:
            visits *= 2
        slab = pl.BlockSpec((rows * visits // steps, cols), lambda i, visits=visits: (i // visits, 0))
        in_specs.append(slab)
        out_specs.append(slab)
        out_shapes.append(jax.ShapeDtypeStruct(w.shape, BF16))
    return in_specs, out_specs, out_shapes


def _cast_slabs(src_refs, dst_refs):
    for src, dst in zip(src_refs, dst_refs):
        dst[...] = src[...].astype(BF16)


def _ffn1_proj_kernel(x_ref, g1_ref, wg_ref, wu_ref, wd_ref, g2_ref, win_ref, *rest):
    n_cast = (len(rest) - 6) // 2
    x1_ref, hy_ref, q_ref, k_ref, v_ref = rest[n_cast:n_cast + 5]
    act_ref = rest[-1]
    _cast_slabs(rest[:n_cast], rest[n_cast + 5:-1])
    x = x_ref[...]
    h = _rms(x, g1_ref[...]).astype(BF16)
    x1 = x + 0.5 * _swiglu(h, wg_ref, wu_ref, wd_ref, act_ref)
    x1_ref[...] = x1
    h2 = _rms(x1, g2_ref[...]).astype(BF16)
    nh = 3 * D_HYENA
    hy_ref[...] = _dot(h2, win_ref[:, :nh])
    q_ref[...] = _dot(h2, win_ref[:, nh:nh + D_ATTN])
    k_ref[...] = _dot(h2, win_ref[:, nh + D_ATTN:nh + 2 * D_ATTN])
    v_ref[...] = _dot(h2, win_ref[:, nh + 2 * D_ATTN:])


def _ffn1_proj(x2d, g1, wg, wu, wd, g2, win, later_weights):
    n = x2d.shape[0]
    steps = n // ROW_TILE
    row = lambda w: pl.BlockSpec((ROW_TILE, w), lambda i: (i, 0))
    cast_in, cast_out, cast_shapes = _cast_rider(later_weights, steps)
    return pl.pallas_call(
        _ffn1_proj_kernel,
        grid=(steps,),
        in_specs=[row(D_MODEL), _resident((1, D_MODEL)), _resident(wg.shape), _resident(wu.shape),
                  _resident(wd.shape), _resident((1, D_MODEL)), _resident(win.shape)] + cast_in,
        out_specs=[row(D_MODEL), row(3 * D_HYENA), row(D_ATTN), row(D_ATTN), row(D_ATTN)] + cast_out,
        out_shape=[jax.ShapeDtypeStruct((n, D_MODEL), F32),
                   jax.ShapeDtypeStruct((n, 3 * D_HYENA), F32),
                   jax.ShapeDtypeStruct((n, D_ATTN), F32),
                   jax.ShapeDtypeStruct((n, D_ATTN), F32),
                   jax.ShapeDtypeStruct((n, D_ATTN), F32)] + cast_shapes,
        scratch_shapes=[pltpu.VMEM((ROW_TILE, D_FF), BF16)],
        compiler_params=pltpu.CompilerParams(dimension_semantics=("arbitrary",),
                                             vmem_limit_bytes=V7X_VMEM_LIMIT_BYTES),
        name="ffn1_proj",
    )(x2d, g1, wg, wu, wd, g2, win, *later_weights)


def _filter_kernel(seq, feat_ref, w1_ref, b1_ref, w2_ref, b2_ref, w3_ref, b3_ref, wout_ref,
                   freq_ref, delta_ref, f_ref, *rest):
    n_cast = (len(rest) - 2) // 2
    h_ref, hcat_ref = rest[n_cast], rest[-1]
    _cast_slabs(rest[:n_cast], rest[n_cast + 1:-1])
    c_ = D_HYENA

    @pl.when(pl.program_id(0) == 0)
    def _():
        hi = lax.Precision.HIGHEST
        dot_hi = lambda a, b: jnp.dot(a, b, precision=hi, preferred_element_type=F32)
        freq = freq_ref[...]
        h = jnp.sin(freq * (dot_hi(feat_ref[...], w1_ref[...]) + b1_ref[...]))
        h = jnp.sin(freq * (dot_hi(h, w2_ref[...]) + b2_ref[...]))
        h = jnp.sin(freq * (dot_hi(h, w3_ref[...]) + b3_ref[...]))
        t = lax.broadcasted_iota(jnp.int32, (seq, 1), 0).astype(F32) * (1.0 / (seq - 1))
        decay = jnp.exp(-t * jnp.abs(delta_ref[...]))
        row = lax.broadcasted_iota(jnp.int32, (seq, c_), 0)
        for c in range(4):
            kc = dot_hi(h, wout_ref[:, c * c_:(c + 1) * c_]) * decay
            if c % 2 == 1:
                kc = jnp.where(row == 0, 0.0, kc)
            hcat_ref[:, c * c_:(c + 1) * c_] = kc.astype(BF16)

    is_sin = (pl.program_id(0) * FILTER_ROWS // (FREQ_BLOCK // 2)) % 2
    sign = (1 - 2 * is_sin).astype(F32)
    scale = 1.0 / seq
    for o in range(2):
        p = _dot(f_ref[...], hcat_ref[:, (2 * o) * c_:(2 * o + 2) * c_])
        h_ref[o] = (p[:, :c_] + sign * p[:, c_:]) * scale


def _hy_filter(seq, feat, w1, b1, w2, b2, w3, b3, wout, freq, delta, fmat, later_weights):
    full = lambda a: _resident(a.shape)
    assert (FREQ_BLOCK // 2) % FILTER_ROWS == 0
    steps = 2 * seq // FILTER_ROWS
    cast_in, cast_out, cast_shapes = _cast_rider(later_weights, steps)
    return pl.pallas_call(
        functools.partial(_filter_kernel, seq),
        grid=(steps,),
        in_specs=[full(feat), full(w1), full(b1), full(w2), full(b2), full(w3), full(b3), full(wout),
                  full(freq), full(delta), pl.BlockSpec((FILTER_ROWS, seq), lambda j: (j, 0))] + cast_in,
        out_specs=[pl.BlockSpec((2, FILTER_ROWS, D_HYENA), lambda j: (0, j, 0))] + cast_out,
        out_shape=[jax.ShapeDtypeStruct((2, 2 * seq, D_HYENA), F32)] + cast_shapes,
        scratch_shapes=[pltpu.VMEM((seq, 4 * D_HYENA), BF16)],
        compiler_params=pltpu.CompilerParams(dimension_semantics=("arbitrary",),
                                             vmem_limit_bytes=V7X_VMEM_LIMIT_BYTES),
        name="hy_filter",
    )(feat, w1, b1, w2, b2, w3, b3, wout, freq, delta, fmat, *later_weights)


def _short_conv(cur, before, after, w_ref, b_ref):
    rows = cur.shape[0]
    row = lax.broadcasted_iota(jnp.int32, cur.shape, 0)
    prev = jnp.where(row == 0, before, pltpu.roll(cur, 1, axis=0))
    nxt = jnp.where(row == rows - 1, after, pltpu.roll(cur, rows - 1, axis=0))
    return b_ref[0] + prev * w_ref[0, 0:1, :] + cur * w_ref[0, 1:2, :] + nxt * w_ref[0, 2:3, :]


def _short_conv_rows(u_ref, w_ref, b_ref, t0, seq):
    zero = jnp.zeros((1, u_ref.shape[2]), F32)
    before = u_ref[0, t0 - V7X_SUBLANES:t0, :][V7X_SUBLANES - 1:] if t0 > 0 else zero
    end = t0 + CONV_ROWS
    after = u_ref[0, end:end + V7X_SUBLANES, :][:1] if end < seq else zero
    return _short_conv(u_ref[0, t0:end, :], before, after, w_ref, b_ref)


def _hyena_kernel(seq, v_ref, gate_ref, gate_lo_ref, gate_hi_ref, wv_ref, bv_ref, wg_ref, bg_ref, skip_ref,
                  gain_ref, f_ref, h_ref, out_ref, zf_ref, zb_ref, acc_ref, g_ref, y_ref):
    o = pl.program_id(1)
    j = pl.program_id(2)
    nfb = 2 * seq // FREQ_BLOCK
    gate_rows = seq // nfb
    chunks = range(0, seq, CONV_ROWS)
    hb = FREQ_BLOCK // 2

    def f_block(jb):
        return f_ref[pl.ds(pl.multiple_of(jb * FREQ_BLOCK, FREQ_BLOCK), FREQ_BLOCK), :]

    def forward(jb, slot):
        zfreq = _dot(f_block(jb), zb_ref[...])
        zr, zi = zfreq[:hb], zfreq[hb:]
        hr, hi = h_ref[0, :hb, :], h_ref[0, hb:, :]
        y_ref[slot, :, :hb] = (zr * hr - zi * hi).T.astype(BF16)
        y_ref[slot, :, hb:] = (zr * hi + zi * hr).T.astype(BF16)

    def inverse(jb, slot):
        acc_ref[...] += _dot(y_ref[slot], f_block(jb))

    def gate_chunk():
        t0 = pl.multiple_of(j * gate_rows, gate_rows)
        for r0 in range(0, gate_rows, GATE_SUB_ROWS):
            r1 = r0 + GATE_SUB_ROWS
            if r0 == 0:
                before = jnp.where(j > 0, gate_lo_ref[0, V7X_SUBLANES - 1:, :], 0.0)
            else:
                before = gate_ref[0, r0 - V7X_SUBLANES:r0, :][V7X_SUBLANES - 1:]
            if r1 == gate_rows:
                after = jnp.where(j < nfb - 1, gate_hi_ref[0, :1, :], 0.0)
            else:
                after = gate_ref[0, r1:r1 + V7X_SUBLANES, :][:1]
            g_ref[pl.ds(t0 + r0, GATE_SUB_ROWS), :] = _short_conv(gate_ref[0, r0:r1, :], before, after,
                                                                   wg_ref, bg_ref)

    @pl.when((o == 0) & (j == 0))
    def _():
        for t0 in chunks:
            z = _short_conv_rows(v_ref, wv_ref, bv_ref, t0, seq)
            zf_ref[t0:t0 + CONV_ROWS, :] = z
            zb_ref[t0:t0 + CONV_ROWS, :] = z.astype(BF16)

    @pl.when(j == 0)
    def _():
        gate_chunk()
        acc_ref[...] = jnp.zeros_like(acc_ref)
        forward(0, 0)

    for parity in range(2):
        @pl.when((j > 0) & (j < nfb) & (j % 2 == parity))
        def _(parity=parity):
            gate_chunk()
            forward(j, parity)
            inverse(j - 1, 1 - parity)

    def gated(rows):
        return g_ref[rows, :] * (acc_ref[:, rows].T + zf_ref[rows, :] * skip_ref[0])

    last_slot = (nfb - 1) % 2

    @pl.when((j == nfb) & (o == 0))
    def _():
        inverse(nfb - 1, last_slot)
        for t0 in chunks:
            rows = slice(t0, t0 + CONV_ROWS)
            z = gated(rows)
            zf_ref[rows, :] = z
            zb_ref[rows, :] = z.astype(BF16)

    @pl.when((j == nfb) & (o == 1))
    def _():
        inverse(nfb - 1, last_slot)
        for t0 in chunks:
            rows = slice(t0, t0 + CONV_ROWS)
            out_ref[0, rows, :] = _rms(gated(rows), gain_ref[...]).astype(BF16)


def _hyena(hy, conv_w, conv_b, skip, gain, fmat, hspec):
    b, seq, _ = hy.shape
    c = D_HYENA
    nfb = 2 * seq // FREQ_BLOCK
    gate_rows = seq // nfb
    halo_per_chunk = gate_rows // V7X_SUBLANES
    fwd_blk = lambda j: jnp.minimum(j, nfb - 1)
    wpart = lambda sel: pl.BlockSpec((1, 3, c), lambda bi, o, j: (sel(o), 0, 0))
    bpart = lambda sel: pl.BlockSpec((1, 1, c), lambda bi, o, j: (sel(o), 0, 0))
    value = lambda o: 0
    gate = lambda o: 1 + o
    halo = lambda first: pl.BlockSpec((1, V7X_SUBLANES, c), lambda bi, o, j: (bi, first(fwd_blk(j)), 1 + o))
    return pl.pallas_call(
        functools.partial(_hyena_kernel, seq),
        grid=(b, 2, nfb + 1),
        in_specs=[pl.BlockSpec((1, seq, c), lambda bi, o, j: (bi, 0, 0), pipeline_mode=pl.Buffered(1)),
                  pl.BlockSpec((1, gate_rows, c), lambda bi, o, j: (bi, fwd_blk(j), 1 + o)),
                  halo(lambda jc: jnp.maximum(jc * halo_per_chunk - 1, 0)),
                  halo(lambda jc: jnp.minimum((jc + 1) * halo_per_chunk, seq // V7X_SUBLANES - 1)),
                  wpart(value), bpart(value), wpart(gate), bpart(gate),
                  pl.BlockSpec((1, 1, c), lambda bi, o, j: (o, 0, 0)),
                  pl.BlockSpec((1, c), lambda bi, o, j: (0, 0)),
                  _resident(fmat.shape),
                  pl.BlockSpec((1, FREQ_BLOCK, c), lambda bi, o, j: (o, fwd_blk(j), 0))],
        out_specs=pl.BlockSpec((1, seq, c), lambda bi, o, j: (bi, 0, 0), pipeline_mode=pl.Buffered(1)),
        out_shape=jax.ShapeDtypeStruct((b, seq, c), BF16),
        scratch_shapes=[pltpu.VMEM((seq, c), F32), pltpu.VMEM((seq, c), BF16), pltpu.VMEM((c, seq), F32),
                        pltpu.VMEM((seq, c), F32), pltpu.VMEM((2, c, FREQ_BLOCK), BF16)],
        compiler_params=pltpu.CompilerParams(dimension_semantics=("parallel", "arbitrary", "arbitrary"),
                                             vmem_limit_bytes=V7X_VMEM_LIMIT_BYTES),
        name="hyena",
    )(hy, hy, hy, hy, conv_w, conv_b, conv_w, conv_b, skip, gain, fmat, hspec)


def _attn_kernel(seq, q_ref, k_ref, v_ref, bias_ref, bias16_ref, o_ref,
                 qs_ref, ks_ref, vs_ref, res_ref, nat_ref, s_ref, e_ref):
    lanes = V7X_LANES
    qscale = LOG2_E / math.sqrt(HEAD_DIM)

    for p, d in enumerate(DILATIONS):
        ls = seq // d
        first = lax.broadcasted_iota(jnp.int32, (ls, lanes), 1) < HEAD_DIM
        for r in range(d):
            src = pl.ds(r, ls, stride=d) if d > 1 else pl.ds(0, ls)
            rows = slice(r * ls, (r + 1) * ls)
            qq = q_ref[0, src, :] * qscale
            qs_ref[p, 0, rows, :] = jnp.where(first, qq, 0.0).astype(BF16)
            qs_ref[p, 1, rows, :] = jnp.where(first, 0.0, qq).astype(BF16)
            ks_ref[p, rows, :] = k_ref[0, src, :].astype(BF16)
            vv = v_ref[0, src, :]
            vs_ref[p, 0, rows, :] = jnp.where(first, vv, 1.0).astype(BF16)
            vs_ref[p, 1, rows, :] = jnp.where(first, 1.0, vv).astype(BF16)

    first = lax.broadcasted_iota(jnp.int32, (Q_BLOCK, lanes), 1) < HEAD_DIM

    def run_pattern(p, nkeys, placement):
        def body(g, carry):
            blocks = [placement(g * ATTN_GROUP + i) for i in range(ATTN_GROUP)]
            for i, (row0, krow0, bias_of_head) in enumerate(blocks):
                kw = ks_ref[p, pl.ds(krow0, nkeys), :]
                for h in range(2):
                    qh = qs_ref[p, h, pl.ds(row0, Q_BLOCK), :]
                    s = lax.dot_general(qh, kw, (((1,), (1,)), ((), ())), preferred_element_type=F32)
                    s_ref[2 * i + h, :, :nkeys] = s + bias_of_head(h)
            for i, (row0, krow0, bias_of_head) in enumerate(blocks):
                ms = []
                for h in range(2):
                    s = s_ref[2 * i + h, :, :nkeys]
                    m = jnp.max(s, axis=-1, keepdims=True)
                    e_ref[2 * i + h, :, :nkeys] = jnp.exp2(s - m).astype(BF16)
                    ms.append(m)
                res_ref[p, 1, pl.ds(row0, Q_BLOCK), :] = jnp.where(first, ms[0], ms[1])
            for i, (row0, krow0, bias_of_head) in enumerate(blocks):
                o0 = _dot(e_ref[2 * i, :, :nkeys], vs_ref[p, 0, pl.ds(krow0, nkeys), :])
                o1 = _dot(e_ref[2 * i + 1, :, :nkeys], vs_ref[p, 1, pl.ds(krow0, nkeys), :])
                res_ref[p, 0, pl.ds(row0, Q_BLOCK), :] = jnp.where(first, o0, o1)
                res_ref[p, 2, pl.ds(row0, Q_BLOCK), :] = jnp.where(first, o1, o0)
            return carry

        lax.fori_loop(0, seq // Q_BLOCK // ATTN_GROUP, body, 0)

    for p, d in enumerate(DILATIONS[:2]):
        ls = seq // d
        nblk = ls // Q_BLOCK

        def banded(n, p=p, ls=ls, nblk=nblk):
            r = n // nblk
            ib = n % nblk
            i0 = ib * Q_BLOCK
            k0 = jnp.clip(i0 - HALF_WINDOW, 0, ls - K_WINDOW)
            case = jnp.where(ib == 0, 0, jnp.where(ib == nblk - 1, 2, 1))
            row0 = pl.multiple_of(r * ls + i0, Q_BLOCK)
            krow0 = pl.multiple_of(r * ls + k0, HALF_WINDOW)
            return row0, krow0, lambda h: bias_ref[0, (p * 3 + case) * 2 + h]

        run_pattern(p, K_WINDOW, banded)

    def full(n):
        row0 = pl.multiple_of(n * Q_BLOCK, Q_BLOCK)
        return row0, row0, lambda h: bias16_ref[0, h]

    run_pattern(2, Q_BLOCK, full)

    for p, d in enumerate(DILATIONS[1:], start=1):
        ls = seq // d
        for r in range(d):
            for kind in range(3):
                nat_ref[p - 1, kind, pl.ds(r, ls, stride=d), :] = res_ref[p, kind, r * ls:(r + 1) * ls, :]

    for t0 in range(0, seq, CONV_ROWS):
        rows = slice(t0, t0 + CONV_ROWS)
        parts = [tuple(res_ref[0, kind, rows, :] for kind in range(3))]
        parts += [tuple(nat_ref[p, kind, rows, :] for kind in range(3)) for p in range(2)]
        m = functools.reduce(jnp.maximum, [pt[1] for pt in parts])
        num = 0.0
        den = 0.0
        for out, mp, lp in parts:
            w = jnp.exp2(mp - m)
            num = num + w * out
            den = den + w * pltpu.roll(lp, HEAD_DIM, axis=1)
        o_ref[0, rows, :] = num / den


def _attn_bias_tables():
    slopes = np.array([2.0 ** (-8.0 * (i + 1) / N_HEADS) for i in range(N_HEADS)], np.float32)
    slopes = jnp.asarray(slopes.reshape(N_HEADS // 2, 1, 2, 1, 1))
    qi = lax.broadcasted_iota(jnp.int32, (Q_BLOCK, K_WINDOW), 0)
    kj = lax.broadcasted_iota(jnp.int32, (Q_BLOCK, K_WINDOW), 1)
    offsets = (0, -HALF_WINDOW, -2 * HALF_WINDOW)
    dist = jnp.stack([jnp.abs(kj - qi + off) for off in offsets])
    valid = dist <= HALF_WINDOW
    dil = jnp.asarray(np.array(DILATIONS[:2], np.float32).reshape(2, 1, 1, 1))
    scaled = (dil * dist.astype(F32)[None])[None, :, :, None]
    banded = jnp.where(valid[None, None, :, None], -slopes[:, None] * scaled, NEG_INF)
    banded = banded.reshape(N_HEADS // 2, 12, Q_BLOCK, K_WINDOW)
    d16 = dist[0, :, :Q_BLOCK]
    full = jnp.where(d16 <= HALF_WINDOW, -slopes[:, 0] * (DILATIONS[2] * d16.astype(F32)), NEG_INF)
    to_base2 = lambda t: jnp.where(t > 0.5 * NEG_INF, t * LOG2_E, NEG_INF).astype(F32)
    return to_base2(banded), to_base2(full)


def _dil_attn(q, k, v):
    b, seq, _ = q.shape
    nhp = N_HEADS // 2
    bias, bias16 = _attn_bias_tables()
    head_pair = pl.BlockSpec((1, seq, V7X_LANES), lambda bi, hp: (bi, 0, hp))
    return pl.pallas_call(
        functools.partial(_attn_kernel, seq),
        grid=(b, nhp),
        in_specs=[head_pair, head_pair, head_pair,
                  pl.BlockSpec((1, 12, Q_BLOCK, K_WINDOW), lambda bi, hp: (hp, 0, 0, 0)),
                  pl.BlockSpec((1, 2, Q_BLOCK, Q_BLOCK), lambda bi, hp: (hp, 0, 0, 0))],
        out_specs=head_pair,
        out_shape=jax.ShapeDtypeStruct((b, seq, D_ATTN), F32),
        scratch_shapes=[pltpu.VMEM((3, 2, seq, V7X_LANES), BF16),
                        pltpu.VMEM((3, seq, V7X_LANES), BF16),
                        pltpu.VMEM((3, 2, seq, V7X_LANES), BF16),
                        pltpu.VMEM((3, 3, seq, V7X_LANES), F32),
                        pltpu.VMEM((2, 3, seq, V7X_LANES), F32),
                        pltpu.VMEM((2 * ATTN_GROUP, Q_BLOCK, K_WINDOW), F32),
                        pltpu.VMEM((2 * ATTN_GROUP, Q_BLOCK, K_WINDOW), BF16)],
        compiler_params=pltpu.CompilerParams(dimension_semantics=("parallel", "parallel"),
                                             vmem_limit_bytes=V7X_VMEM_LIMIT_BYTES),
        name="dil_attn",
    )(q, k, v, bias, bias16)


def _out_ffn2_kernel(x1_ref, yh_ref, ya_ref, ga_ref, wo_ref, g3_ref, wg_ref, wu_ref, wd_ref, gf_ref,
                     out_ref, act_ref):
    ya = _rms(ya_ref[...], ga_ref[...]).astype(BF16)
    mix = _dot(yh_ref[...], wo_ref[:D_HYENA, :]) + _dot(ya, wo_ref[D_HYENA:, :])
    x2 = x1_ref[...] + mix
    h = _rms(x2, g3_ref[...]).astype(BF16)
    x3 = x2 + 0.5 * _swiglu(h, wg_ref, wu_ref, wd_ref, act_ref)
    out_ref[...] = _rms(x3, gf_ref[...])


def _out_ffn2(x1, yh, ya, ga, wo, g3, wg, wu, wd, gf):
    n = x1.shape[0]
    row = lambda w: pl.BlockSpec((ROW_TILE, w), lambda i: (i, 0))
    return pl.pallas_call(
        _out_ffn2_kernel,
        grid=(n // ROW_TILE,),
        in_specs=[row(D_MODEL), row(D_HYENA), row(D_ATTN), _resident((1, D_ATTN)), _resident(wo.shape),
                  _resident((1, D_MODEL)), _resident(wg.shape), _resident(wu.shape), _resident(wd.shape),
                  _resident((1, D_MODEL))],
        out_specs=row(D_MODEL),
        out_shape=jax.ShapeDtypeStruct((n, D_MODEL), F32),
        scratch_shapes=[pltpu.VMEM((ROW_TILE, D_FF), BF16)],
        compiler_params=pltpu.CompilerParams(dimension_semantics=("parallel",),
                                             vmem_limit_bytes=V7X_VMEM_LIMIT_BYTES),
        name="out_ffn2",
    )(x1, yh, ya, ga, wo, g3, wg, wu, wd, gf)


def _dft_matrices(seq):
    n = 2 * seq
    split = 64
    k = np.arange(seq, dtype=np.int64)[:, None]
    ang = lambda s: np.pi * (((2 * k + 1) * s) % (2 * n)) / n
    a1 = ang(split * np.arange(seq // split, dtype=np.int64)[None, :])
    a0 = ang(np.arange(split, dtype=np.int64)[None, :])
    hb = FREQ_BLOCK // 2

    def rows_of_f(x, quarter_turn):
        cos_rows = x.reshape(seq // hb, 1, hb, -1)
        return jnp.asarray(np.concatenate([cos_rows, quarter_turn.reshape(cos_rows.shape)], axis=1)
                           .reshape(n, -1).astype(np.float32))

    c1 = rows_of_f(np.cos(a1), -np.sin(a1))[:, :, None]
    s1 = rows_of_f(np.sin(a1), np.cos(a1))[:, :, None]
    c0 = rows_of_f(np.cos(a0), np.cos(a0))[:, None, :]
    s0 = rows_of_f(np.sin(a0), np.sin(a0))[:, None, :]
    return (c1 * c0 - s1 * s0).reshape(n, seq).astype(BF16)


def _filter_features(seq):
    t = jnp.linspace(0.0, 1.0, seq, dtype=F32)[:, None]
    w = 2.0 * math.pi * jnp.arange(seq, dtype=F32)[:, None] / seq
    f = jnp.linspace(1e-4, FILTER_BANDS - 1, FILTER_BANDS, dtype=F32)[None, :]
    z = jnp.concatenate([t, jnp.cos(f * w), -jnp.sin(f * w)], axis=-1)
    return jnp.pad(z, ((0, 0), (0, FEAT_PAD - FILTER_EMB)))


def _decay_rates():
    max_decay = math.log(DECAY_TARGET) / FAST_DECAY_PCT
    min_decay = math.log(DECAY_TARGET) / SLOW_DECAY_PCT
    return jnp.linspace(min_decay, max_decay, D_HYENA, dtype=F32)[None, :]


def kernel(x, ffn1_norm_g, ffn1_w_gate, ffn1_w_up, ffn1_w_down, mix_norm_g, w_in, hy_conv_w, hy_conv_b, hy_filt_w1, hy_filt_b1, hy_filt_w2, hy_filt_b2, hy_filt_w3, hy_filt_b3, hy_filt_w_out, hy_filt_freq, hy_filt_skip, hy_out_norm_g, attn_out_norm_g, w_out, ffn2_norm_g, ffn2_w_gate, ffn2_w_up, ffn2_w_down, final_norm_g):
    b, seq, d = x.shape
    assert d == D_MODEL and (b * seq) % ROW_TILE == 0 and seq % (DILATIONS[-1] * Q_BLOCK) == 0
    row = lambda a: a.reshape(1, -1).astype(F32)
    f32 = lambda a: a.astype(F32)

    fmat = _dft_matrices(seq)
    w1 = jnp.pad(f32(hy_filt_w1), ((0, FEAT_PAD - FILTER_EMB), (0, 0)))
    hspec, wg1, wu1, wd1, win = _hy_filter(
        seq, _filter_features(seq), w1, row(hy_filt_b1), f32(hy_filt_w2), row(hy_filt_b2), f32(hy_filt_w3),
        row(hy_filt_b3), f32(hy_filt_w_out), row(hy_filt_freq), _decay_rates(), fmat,
        [f32(ffn1_w_gate), f32(ffn1_w_up), f32(ffn1_w_down), f32(w_in)])

    x1, hy, q, k, v, wo, wg2, wu2, wd2 = _ffn1_proj(
        x.reshape(b * seq, d), row(ffn1_norm_g), wg1, wu1, wd1, row(mix_norm_g), win,
        [f32(w_out), f32(ffn2_w_gate), f32(ffn2_w_up), f32(ffn2_w_down)])

    conv_w = hy_conv_w.astype(F32).reshape(3, 3, D_HYENA).transpose(1, 0, 2)
    conv_b = hy_conv_b.astype(F32).reshape(3, 1, D_HYENA)
    skip = hy_filt_skip.astype(F32).reshape(2, 1, D_HYENA)
    y_hy = _hyena(hy.reshape(b, seq, 3 * D_HYENA), conv_w, conv_b, skip, row(hy_out_norm_g),
                  fmat, hspec)

    shape3 = lambda a: a.reshape(b, seq, D_ATTN)
    y_at = _dil_attn(shape3(q), shape3(k), shape3(v))

    out = _out_ffn2(x1, y_hy.reshape(b * seq, D_HYENA), y_at.reshape(b * seq, D_ATTN),
                    row(attn_out_norm_g), wo, row(ffn2_norm_g), wg2, wu2, wd2, row(final_norm_g))
    return out.reshape(b, seq, d)
```

```python
import functools
import math

import numpy as np
import jax
import jax.numpy as jnp
from jax import lax
from jax.experimental import pallas as pl
from jax.experimental.pallas import tpu as pltpu

F32 = jnp.float32
BF16 = jnp.bfloat16

D_MODEL = 1024
D_HYENA = 512
D_ATTN = 512
HEAD_DIM = 64
N_HEADS = D_ATTN // HEAD_DIM
D_FF = 2816
FILTER_EMB = 33
FILTER_BANDS = 16
FILTER_WIDTH = 64
DECAY_TARGET = 1e-2
FAST_DECAY_PCT = 0.3
SLOW_DECAY_PCT = 1.5
DILATIONS = (1, 4, 16)
HALF_WINDOW = 64
RMS_EPS = 1e-6
NEG_INF = -1e30
LOG2_E = math.log2(math.e)

V7X_LANES = 128
V7X_SUBLANES = 8
V7X_BF16_SUBLANES = 16
V7X_VMEM_LIMIT_BYTES = 56 * 1024 * 1024

ROW_TILE = 512
FF_CHUNK = 256
FREQ_BLOCK = 1024
FILTER_ROWS = 512
DFT_BASE_ROWS = 256
Q_BLOCK = 128
K_WINDOW = 256
ATTN_GROUP = 16
CONV_ROWS = 256
GATE_SUB_ROWS = 64
FEAT_PAD = 128


def _dot(a, b):
    return jnp.dot(a, b, preferred_element_type=F32)


def _rms(x, g):
    return x * lax.rsqrt(jnp.mean(x * x, axis=-1, keepdims=True) + RMS_EPS) * g


def _swiglu(h, wg_ref, wu_ref, wd_ref, act_ref):
    for c in range(D_FF // FF_CHUNK):
        cols = slice(c * FF_CHUNK, (c + 1) * FF_CHUNK)
        g = _dot(h, wg_ref[:, cols])
        u = _dot(h, wu_ref[:, cols])
        act_ref[:, cols] = (g * jax.nn.sigmoid(g) * u).astype(BF16)
    return _dot(act_ref[...], wd_ref[...])


def _resident(shape):
    return pl.BlockSpec(shape, lambda *_: (0,) * len(shape), pipeline_mode=pl.Buffered(1))


def _cast_rider(weights, steps):
    in_specs, out_specs, out_shapes = [], [], []
    for w in weights:
        rows, cols = w.shape
        visits = 1
        while (rows * visits) % steps or (rows * visits // steps) % V7X_BF16_SUBLANES:
            visits *= 2
        slab = pl.BlockSpec((rows * visits // steps, cols), lambda i, visits=visits: (i // visits, 0))
        in_specs.append(slab)
        out_specs.append(slab)
        out_shapes.append(jax.ShapeDtypeStruct(w.shape, BF16))
    return in_specs, out_specs, out_shapes


def _cast_slabs(src_refs, dst_refs):
    for src, dst in zip(src_refs, dst_refs):
        dst[...] = src[...].astype(BF16)


def _ffn1_proj_kernel(x_ref, g1_ref, wg_ref, wu_ref, wd_ref, g2_ref, win_ref, *rest):
    n_cast = (len(rest) - 6) // 2
    x1_ref, hy_ref, q_ref, k_ref, v_ref = rest[n_cast:n_cast + 5]
    act_ref = rest[-1]
    _cast_slabs(rest[:n_cast], rest[n_cast + 5:-1])
    x = x_ref[...]
    h = _rms(x, g1_ref[...]).astype(BF16)
    x1 = x + 0.5 * _swiglu(h, wg_ref, wu_ref, wd_ref, act_ref)
    x1_ref[...] = x1
    h2 = _rms(x1, g2_ref[...]).astype(BF16)
    nh = 3 * D_HYENA
    hy_ref[...] = _dot(h2, win_ref[:, :nh])
    q_ref[...] = _dot(h2, win_ref[:, nh:nh + D_ATTN])
    k_ref[...] = _dot(h2, win_ref[:, nh + D_ATTN:nh + 2 * D_ATTN])
    v_ref[...] = _dot(h2, win_ref[:, nh + 2 * D_ATTN:])


def _ffn1_proj(x2d, g1, wg, wu, wd, g2, win, later_weights):
    n = x2d.shape[0]
    steps = n // ROW_TILE
    row = lambda w: pl.BlockSpec((ROW_TILE, w), lambda i: (i, 0))
    cast_in, cast_out, cast_shapes = _cast_rider(later_weights, steps)
    return pl.pallas_call(
        _ffn1_proj_kernel,
        grid=(steps,),
        in_specs=[row(D_MODEL), _resident((1, D_MODEL)), _resident(wg.shape), _resident(wu.shape),
                  _resident(wd.shape), _resident((1, D_MODEL)), _resident(win.shape)] + cast_in,
        out_specs=[row(D_MODEL), row(3 * D_HYENA), row(D_ATTN), row(D_ATTN), row(D_ATTN)] + cast_out,
        out_shape=[jax.ShapeDtypeStruct((n, D_MODEL), F32),
                   jax.ShapeDtypeStruct((n, 3 * D_HYENA), F32),
                   jax.ShapeDtypeStruct((n, D_ATTN), F32),
                   jax.ShapeDtypeStruct((n, D_ATTN), F32),
                   jax.ShapeDtypeStruct((n, D_ATTN), F32)] + cast_shapes,
        scratch_shapes=[pltpu.VMEM((ROW_TILE, D_FF), BF16)],
        compiler_params=pltpu.CompilerParams(dimension_semantics=("arbitrary",),
                                             vmem_limit_bytes=V7X_VMEM_LIMIT_BYTES),
        name="ffn1_proj",
    )(x2d, g1, wg, wu, wd, g2, win, *later_weights)


def _filter_kernel(seq, feat_ref, w1_ref, b1_ref, w2_ref, b2_ref, w3_ref, b3_ref, wout_ref,
                   freq_ref, delta_ref, f_ref, *rest):
    n_cast = (len(rest) - 2) // 2
    h_ref, hcat_ref = rest[n_cast], rest[-1]
    _cast_slabs(rest[:n_cast], rest[n_cast + 1:-1])
    c_ = D_HYENA

    @pl.when(pl.program_id(0) == 0)
    def _():
        hi = lax.Precision.HIGHEST
        dot_hi = lambda a, b: jnp.dot(a, b, precision=hi, preferred_element_type=F32)
        freq = freq_ref[...]
        h = jnp.sin(freq * (dot_hi(feat_ref[...], w1_ref[...]) + b1_ref[...]))
        h = jnp.sin(freq * (dot_hi(h, w2_ref[...]) + b2_ref[...]))
        h = jnp.sin(freq * (dot_hi(h, w3_ref[...]) + b3_ref[...]))
        t = lax.broadcasted_iota(jnp.int32, (seq, 1), 0).astype(F32) * (1.0 / (seq - 1))
        decay = jnp.exp(-t * jnp.abs(delta_ref[...]))
        row = lax.broadcasted_iota(jnp.int32, (seq, c_), 0)
        for c in range(4):
            kc = dot_hi(h, wout_ref[:, c * c_:(c + 1) * c_]) * decay
            if c % 2 == 1:
                kc = jnp.where(row == 0, 0.0, kc)
            hcat_ref[:, c * c_:(c + 1) * c_] = kc.astype(BF16)

    is_sin = (pl.program_id(0) * FILTER_ROWS // (FREQ_BLOCK // 2)) % 2
    sign = (1 - 2 * is_sin).astype(F32)
    scale = 1.0 / seq
    for o in range(2):
        p = _dot(f_ref[...], hcat_ref[:, (2 * o) * c_:(2 * o + 2) * c_])
        h_ref[o] = (p[:, :c_] + sign * p[:, c_:]) * scale


def _hy_filter(seq, feat, w1, b1, w2, b2, w3, b3, wout, freq, delta, fmat, later_weights):
    full = lambda a: _resident(a.shape)
    assert (FREQ_BLOCK // 2) % FILTER_ROWS == 0
    steps = 2 * seq // FILTER_ROWS
    cast_in, cast_out, cast_shapes = _cast_rider(later_weights, steps)
    return pl.pallas_call(
        functools.partial(_filter_kernel, seq),
        grid=(steps,),
        in_specs=[full(feat), full(w1), full(b1), full(w2), full(b2), full(w3), full(b3), full(wout),
                  full(freq), full(delta), pl.BlockSpec((FILTER_ROWS, seq), lambda j: (j, 0))] + cast_in,
        out_specs=[pl.BlockSpec((2, FILTER_ROWS, D_HYENA), lambda j: (0, j, 0))] + cast_out,
        out_shape=[jax.ShapeDtypeStruct((2, 2 * seq, D_HYENA), F32)] + cast_shapes,
        scratch_shapes=[pltpu.VMEM((seq, 4 * D_HYENA), BF16)],
        compiler_params=pltpu.CompilerParams(dimension_semantics=("arbitrary",),
                                             vmem_limit_bytes=V7X_VMEM_LIMIT_BYTES),
        name="hy_filter",
    )(feat, w1, b1, w2, b2, w3, b3, wout, freq, delta, fmat, *later_weights)


def _short_conv(cur, before, after, w_ref, b_ref):
    rows = cur.shape[0]
    row = lax.broadcasted_iota(jnp.int32, cur.shape, 0)
    prev = jnp.where(row == 0, before, pltpu.roll(cur, 1, axis=0))
    nxt = jnp.where(row == rows - 1, after, pltpu.roll(cur, rows - 1, axis=0))
    return b_ref[0] + prev * w_ref[0, 0:1, :] + cur * w_ref[0, 1:2, :] + nxt * w_ref[0, 2:3, :]


def _short_conv_rows(u_ref, w_ref, b_ref, t0, seq):
    zero = jnp.zeros((1, u_ref.shape[2]), F32)
    before = u_ref[0, t0 - V7X_SUBLANES:t0, :][V7X_SUBLANES - 1:] if t0 > 0 else zero
    end = t0 + CONV_ROWS
    after = u_ref[0, end:end + V7X_SUBLANES, :][:1] if end < seq else zero
    return _short_conv(u_ref[0, t0:end, :], before, after, w_ref, b_ref)


def _hyena_kernel(seq, v_ref, gate_ref, gate_lo_ref, gate_hi_ref, wv_ref, bv_ref, wg_ref, bg_ref, skip_ref,
                  gain_ref, f_ref, h_ref, out_ref, zf_ref, zb_ref, acc_ref, g_ref, y_ref):
    o = pl.program_id(1)
    j = pl.program_id(2)
    nfb = 2 * seq // FREQ_BLOCK
    gate_rows = seq // nfb
    chunks = range(0, seq, CONV_ROWS)
    hb = FREQ_BLOCK // 2

    def f_block(jb):
        return f_ref[pl.ds(pl.multiple_of(jb * FREQ_BLOCK, FREQ_BLOCK), FREQ_BLOCK), :]

    def forward(jb, slot):
        zfreq = _dot(f_block(jb), zb_ref[...])
        zr, zi = zfreq[:hb], zfreq[hb:]
        hr, hi = h_ref[0, :hb, :], h_ref[0, hb:, :]
        y_ref[slot, :, :hb] = (zr * hr - zi * hi).T.astype(BF16)
        y_ref[slot, :, hb:] = (zr * hi + zi * hr).T.astype(BF16)

    def inverse(jb, slot):
        acc_ref[...] += _dot(y_ref[slot], f_block(jb))

    def gate_chunk():
        t0 = pl.multiple_of(j * gate_rows, gate_rows)
        for r0 in range(0, gate_rows, GATE_SUB_ROWS):
            r1 = r0 + GATE_SUB_ROWS
            if r0 == 0:
                before = jnp.where(j > 0, gate_lo_ref[0, V7X_SUBLANES - 1:, :], 0.0)
            else:
                before = gate_ref[0, r0 - V7X_SUBLANES:r0, :][V7X_SUBLANES - 1:]
            if r1 == gate_rows:
                after = jnp.where(j < nfb - 1, gate_hi_ref[0, :1, :], 0.0)
            else:
                after = gate_ref[0, r1:r1 + V7X_SUBLANES, :][:1]
            g_ref[pl.ds(t0 + r0, GATE_SUB_ROWS), :] = _short_conv(gate_ref[0, r0:r1, :], before, after,
                                                                   wg_ref, bg_ref)

    @pl.when((o == 0) & (j == 0))
    def _():
        for t0 in chunks:
            z = _short_conv_rows(v_ref, wv_ref, bv_ref, t0, seq)
            zf_ref[t0:t0 + CONV_ROWS, :] = z
            zb_ref[t0:t0 + CONV_ROWS, :] = z.astype(BF16)

    @pl.when(j == 0)
    def _():
        gate_chunk()
        acc_ref[...] = jnp.zeros_like(acc_ref)
        forward(0, 0)

    for parity in range(2):
        @pl.when((j > 0) & (j < nfb) & (j % 2 == parity))
        def _(parity=parity):
            gate_chunk()
            forward(j, parity)
            inverse(j - 1, 1 - parity)

    def gated(rows):
        return g_ref[rows, :] * (acc_ref[:, rows].T + zf_ref[rows, :] * skip_ref[0])

    last_slot = (nfb - 1) % 2

    @pl.when((j == nfb) & (o == 0))
    def _():
        inverse(nfb - 1, last_slot)
        for t0 in chunks:
            rows = slice(t0, t0 + CONV_ROWS)
            z = gated(rows)
            zf_ref[rows, :] = z
            zb_ref[rows, :] = z.astype(BF16)

    @pl.when((j == nfb) & (o == 1))
    def _():
        inverse(nfb - 1, last_slot)
        for t0 in chunks:
            rows = slice(t0, t0 + CONV_ROWS)
            out_ref[0, rows, :] = _rms(gated(rows), gain_ref[...]).astype(BF16)


def _hyena(hy, conv_w, conv_b, skip, gain, fmat, hspec):
    b, seq, _ = hy.shape
    c = D_HYENA
    nfb = 2 * seq // FREQ_BLOCK
    gate_rows = seq // nfb
    halo_per_chunk = gate_rows // V7X_SUBLANES
    fwd_blk = lambda j: jnp.minimum(j, nfb - 1)
    wpart = lambda sel: pl.BlockSpec((1, 3, c), lambda bi, o, j: (sel(o), 0, 0))
    bpart = lambda sel: pl.BlockSpec((1, 1, c), lambda bi, o, j: (sel(o), 0, 0))
    value = lambda o: 0
    gate = lambda o: 1 + o
    halo = lambda first: pl.BlockSpec((1, V7X_SUBLANES, c), lambda bi, o, j: (bi, first(fwd_blk(j)), 1 + o))
    return pl.pallas_call(
        functools.partial(_hyena_kernel, seq),
        grid=(b, 2, nfb + 1),
        in_specs=[pl.BlockSpec((1, seq, c), lambda bi, o, j: (bi, 0, 0), pipeline_mode=pl.Buffered(1)),
                  pl.BlockSpec((1, gate_rows, c), lambda bi, o, j: (bi, fwd_blk(j), 1 + o)),
                  halo(lambda jc: jnp.maximum(jc * halo_per_chunk - 1, 0)),
                  halo(lambda jc: jnp.minimum((jc + 1) * halo_per_chunk, seq // V7X_SUBLANES - 1)),
                  wpart(value), bpart(value), wpart(gate), bpart(gate),
                  pl.BlockSpec((1, 1, c), lambda bi, o, j: (o, 0, 0)),
                  pl.BlockSpec((1, c), lambda bi, o, j: (0, 0)),
                  _resident(fmat.shape),
                  pl.BlockSpec((1, FREQ_BLOCK, c), lambda bi, o, j: (o, fwd_blk(j), 0))],
        out_specs=pl.BlockSpec((1, seq, c), lambda bi, o, j: (bi, 0, 0), pipeline_mode=pl.Buffered(1)),
        out_shape=jax.ShapeDtypeStruct((b, seq, c), BF16),
        scratch_shapes=[pltpu.VMEM((seq, c), F32), pltpu.VMEM((seq, c), BF16), pltpu.VMEM((c, seq), F32),
                        pltpu.VMEM((seq, c), F32), pltpu.VMEM((2, c, FREQ_BLOCK), BF16)],
        compiler_params=pltpu.CompilerParams(dimension_semantics=("parallel", "arbitrary", "arbitrary"),
                                             vmem_limit_bytes=V7X_VMEM_LIMIT_BYTES),
        name="hyena",
    )(hy, hy, hy, hy, conv_w, conv_b, conv_w, conv_b, skip, gain, fmat, hspec)


def _attn_kernel(seq, q_ref, k_ref, v_ref, bias_ref, bias16_ref, o_ref,
                 qs_ref, ks_ref, vs_ref, res_ref, nat_ref, s_ref, e_ref):
    lanes = V7X_LANES
    qscale = LOG2_E / math.sqrt(HEAD_DIM)

    for p, d in enumerate(DILATIONS):
        ls = seq // d
        first = lax.broadcasted_iota(jnp.int32, (ls, lanes), 1) < HEAD_DIM
        for r in range(d):
            src = pl.ds(r, ls, stride=d) if d > 1 else pl.ds(0, ls)
            rows = slice(r * ls, (r + 1) * ls)
            qq = q_ref[0, src, :] * qscale
            qs_ref[p, 0, rows, :] = jnp.where(first, qq, 0.0).astype(BF16)
            qs_ref[p, 1, rows, :] = jnp.where(first, 0.0, qq).astype(BF16)
            ks_ref[p, rows, :] = k_ref[0, src, :].astype(BF16)
            vv = v_ref[0, src, :]
            vs_ref[p, 0, rows, :] = jnp.where(first, vv, 1.0).astype(BF16)
            vs_ref[p, 1, rows, :] = jnp.where(first, 1.0, vv).astype(BF16)

    first = lax.broadcasted_iota(jnp.int32, (Q_BLOCK, lanes), 1) < HEAD_DIM

    def run_pattern(p, nkeys, placement):
        def body(g, carry):
            blocks = [placement(g * ATTN_GROUP + i) for i in range(ATTN_GROUP)]
            for i, (row0, krow0, bias_of_head) in enumerate(blocks):
                kw = ks_ref[p, pl.ds(krow0, nkeys), :]
                for h in range(2):
                    qh = qs_ref[p, h, pl.ds(row0, Q_BLOCK), :]
                    s = lax.dot_general(qh, kw, (((1,), (1,)), ((), ())), preferred_element_type=F32)
                    s_ref[2 * i + h, :, :nkeys] = s + bias_of_head(h)
            for i, (row0, krow0, bias_of_head) in enumerate(blocks):
                ms = []
                for h in range(2):
                    s = s_ref[2 * i + h, :, :nkeys]
                    m = jnp.max(s, axis=-1, keepdims=True)
                    e_ref[2 * i + h, :, :nkeys] = jnp.exp2(s - m).astype(BF16)
                    ms.append(m)
                res_ref[p, 1, pl.ds(row0, Q_BLOCK), :] = jnp.where(first, ms[0], ms[1])
            for i, (row0, krow0, bias_of_head) in enumerate(blocks):
                o0 = _dot(e_ref[2 * i, :, :nkeys], vs_ref[p, 0, pl.ds(krow0, nkeys), :])
                o1 = _dot(e_ref[2 * i + 1, :, :nkeys], vs_ref[p, 1, pl.ds(krow0, nkeys), :])
                res_ref[p, 0, pl.ds(row0, Q_BLOCK), :] = jnp.where(first, o0, o1)
                res_ref[p, 2, pl.ds(row0, Q_BLOCK), :] = jnp.where(first, o1, o0)
            return carry

        lax.fori_loop(0, seq // Q_BLOCK // ATTN_GROUP, body, 0)

    for p, d in enumerate(DILATIONS[:2]):
        ls = seq // d
        nblk = ls // Q_BLOCK

        def banded(n, p=p, ls=ls, nblk=nblk):
            r = n // nblk
            ib = n % nblk
            i0 = ib * Q_BLOCK
            k0 = jnp.clip(i0 - HALF_WINDOW, 0, ls - K_WINDOW)
            case = jnp.where(ib == 0, 0, jnp.where(ib == nblk - 1, 2, 1))
            row0 = pl.multiple_of(r * ls + i0, Q_BLOCK)
            krow0 = pl.multiple_of(r * ls + k0, HALF_WINDOW)
            return row0, krow0, lambda h: bias_ref[0, (p * 3 + case) * 2 + h]

        run_pattern(p, K_WINDOW, banded)

    def full(n):
        row0 = pl.multiple_of(n * Q_BLOCK, Q_BLOCK)
        return row0, row0, lambda h: bias16_ref[0, h]

    run_pattern(2, Q_BLOCK, full)

    for p, d in enumerate(DILATIONS[1:], start=1):
        ls = seq // d
        for r in range(d):
            for kind in range(3):
                nat_ref[p - 1, kind, pl.ds(r, ls, stride=d), :] = res_ref[p, kind, r * ls:(r + 1) * ls, :]

    for t0 in range(0, seq, CONV_ROWS):
        rows = slice(t0, t0 + CONV_ROWS)
        parts = [tuple(res_ref[0, kind, rows, :] for kind in range(3))]
        parts += [tuple(nat_ref[p, kind, rows, :] for kind in range(3)) for p in range(2)]
        m = functools.reduce(jnp.maximum, [pt[1] for pt in parts])
        num = 0.0
        den = 0.0
        for out, mp, lp in parts:
            w = jnp.exp2(mp - m)
            num = num + w * out
            den = den + w * pltpu.roll(lp, HEAD_DIM, axis=1)
        o_ref[0, rows, :] = num / den


def _attn_bias_tables():
    slopes = np.array([2.0 ** (-8.0 * (i + 1) / N_HEADS) for i in range(N_HEADS)], np.float32)
    slopes = jnp.asarray(slopes.reshape(N_HEADS // 2, 1, 2, 1, 1))
    qi = lax.broadcasted_iota(jnp.int32, (Q_BLOCK, K_WINDOW), 0)
    kj = lax.broadcasted_iota(jnp.int32, (Q_BLOCK, K_WINDOW), 1)
    offsets = (0, -HALF_WINDOW, -2 * HALF_WINDOW)
    dist = jnp.stack([jnp.abs(kj - qi + off) for off in offsets])
    valid = dist <= HALF_WINDOW
    dil = jnp.asarray(np.array(DILATIONS[:2], np.float32).reshape(2, 1, 1, 1))
    scaled = (dil * dist.astype(F32)[None])[None, :, :, None]
    banded = jnp.where(valid[None, None, :, None], -slopes[:, None] * scaled, NEG_INF)
    banded = banded.reshape(N_HEADS // 2, 12, Q_BLOCK, K_WINDOW)
    d16 = dist[0, :, :Q_BLOCK]
    full = jnp.where(d16 <= HALF_WINDOW, -slopes[:, 0] * (DILATIONS[2] * d16.astype(F32)), NEG_INF)
    to_base2 = lambda t: jnp.where(t > 0.5 * NEG_INF, t * LOG2_E, NEG_INF).astype(F32)
    return to_base2(banded), to_base2(full)


def _dil_attn(q, k, v):
    b, seq, _ = q.shape
    nhp = N_HEADS // 2
    bias, bias16 = _attn_bias_tables()
    head_pair = pl.BlockSpec((1, seq, V7X_LANES), lambda bi, hp: (bi, 0, hp))
    return pl.pallas_call(
        functools.partial(_attn_kernel, seq),
        grid=(b, nhp),
        in_specs=[head_pair, head_pair, head_pair,
                  pl.BlockSpec((1, 12, Q_BLOCK, K_WINDOW), lambda bi, hp: (hp, 0, 0, 0)),
                  pl.BlockSpec((1, 2, Q_BLOCK, Q_BLOCK), lambda bi, hp: (hp, 0, 0, 0))],
        out_specs=head_pair,
        out_shape=jax.ShapeDtypeStruct((b, seq, D_ATTN), F32),
        scratch_shapes=[pltpu.VMEM((3, 2, seq, V7X_LANES), BF16),
                        pltpu.VMEM((3, seq, V7X_LANES), BF16),
                        pltpu.VMEM((3, 2, seq, V7X_LANES), BF16),
                        pltpu.VMEM((3, 3, seq, V7X_LANES), F32),
                        pltpu.VMEM((2, 3, seq, V7X_LANES), F32),
                        pltpu.VMEM((2 * ATTN_GROUP, Q_BLOCK, K_WINDOW), F32),
                        pltpu.VMEM((2 * ATTN_GROUP, Q_BLOCK, K_WINDOW), BF16)],
        compiler_params=pltpu.CompilerParams(dimension_semantics=("parallel", "parallel"),
                                             vmem_limit_bytes=V7X_VMEM_LIMIT_BYTES),
        name="dil_attn",
    )(q, k, v, bias, bias16)


def _out_ffn2_kernel(x1_ref, yh_ref, ya_ref, ga_ref, wo_ref, g3_ref, wg_ref, wu_ref, wd_ref, gf_ref,
                     out_ref, act_ref):
    ya = _rms(ya_ref[...], ga_ref[...]).astype(BF16)
    mix = _dot(yh_ref[...], wo_ref[:D_HYENA, :]) + _dot(ya, wo_ref[D_HYENA:, :])
    x2 = x1_ref[...] + mix
    h = _rms(x2, g3_ref[...]).astype(BF16)
    x3 = x2 + 0.5 * _swiglu(h, wg_ref, wu_ref, wd_ref, act_ref)
    out_ref[...] = _rms(x3, gf_ref[...])


def _out_ffn2(x1, yh, ya, ga, wo, g3, wg, wu, wd, gf):
    n = x1.shape[0]
    row = lambda w: pl.BlockSpec((ROW_TILE, w), lambda i: (i, 0))
    return pl.pallas_call(
        _out_ffn2_kernel,
        grid=(n // ROW_TILE,),
        in_specs=[row(D_MODEL), row(D_HYENA), row(D_ATTN), _resident((1, D_ATTN)), _resident(wo.shape),
                  _resident((1, D_MODEL)), _resident(wg.shape), _resident(wu.shape), _resident(wd.shape),
                  _resident((1, D_MODEL))],
        out_specs=row(D_MODEL),
        out_shape=jax.ShapeDtypeStruct((n, D_MODEL), F32),
        scratch_shapes=[pltpu.VMEM((ROW_TILE, D_FF), BF16)],
        compiler_params=pltpu.CompilerParams(dimension_semantics=("parallel",),
                                             vmem_limit_bytes=V7X_VMEM_LIMIT_BYTES),
        name="out_ffn2",
    )(x1, yh, ya, ga, wo, g3, wg, wu, wd, gf)


def _dft_matrices(seq):
    n = 2 * seq
    hb = FREQ_BLOCK // 2
    s = np.arange(seq, dtype=np.int64)[None, :]
    turn = lambda steps: 2.0 * np.pi * (steps % (2 * n)) / (2 * n)
    theta = turn((2 * np.arange(DFT_BASE_ROWS, dtype=np.int64)[:, None] + 1) * s)
    phase = turn(2 * DFT_BASE_ROWS * np.arange(seq // DFT_BASE_ROWS, dtype=np.int64)[:, None] * s)
    tab = lambda x: jnp.asarray(x.astype(np.float32))
    c0, s0 = np.cos(theta), np.sin(theta)
    lead = tab(np.stack([c0, -s0]))[None, :, None]
    lag = tab(np.stack([s0, c0]))[None, :, None]
    group = lambda x: tab(x).reshape(seq // hb, 1, hb // DFT_BASE_ROWS, 1, seq)
    return (lead * group(np.cos(phase)) - lag * group(np.sin(phase))).reshape(n, seq).astype(BF16)


def _filter_features(seq):
    t = jnp.linspace(0.0, 1.0, seq, dtype=F32)[:, None]
    w = 2.0 * math.pi * jnp.arange(seq, dtype=F32)[:, None] / seq
    f = jnp.linspace(1e-4, FILTER_BANDS - 1, FILTER_BANDS, dtype=F32)[None, :]
    z = jnp.concatenate([t, jnp.cos(f * w), -jnp.sin(f * w)], axis=-1)
    return jnp.pad(z, ((0, 0), (0, FEAT_PAD - FILTER_EMB)))


def _decay_rates():
    max_decay = math.log(DECAY_TARGET) / FAST_DECAY_PCT
    min_decay = math.log(DECAY_TARGET) / SLOW_DECAY_PCT
    return jnp.linspace(min_decay, max_decay, D_HYENA, dtype=F32)[None, :]


def kernel(x, ffn1_norm_g, ffn1_w_gate, ffn1_w_up, ffn1_w_down, mix_norm_g, w_in, hy_conv_w, hy_conv_b, hy_filt_w1, hy_filt_b1, hy_filt_w2, hy_filt_b2, hy_filt_w3, hy_filt_b3, hy_filt_w_out, hy_filt_freq, hy_filt_skip, hy_out_norm_g, attn_out_norm_g, w_out, ffn2_norm_g, ffn2_w_gate, ffn2_w_up, ffn2_w_down, final_norm_g):
    b, seq, d = x.shape
    assert d == D_MODEL and (b * seq) % ROW_TILE == 0 and seq % (DILATIONS[-1] * Q_BLOCK) == 0
    row = lambda a: a.reshape(1, -1).astype(F32)
    f32 = lambda a: a.astype(F32)

    fmat = _dft_matrices(seq)
    w1 = jnp.pad(f32(hy_filt_w1), ((0, FEAT_PAD - FILTER_EMB), (0, 0)))
    hspec, wg1, wu1, wd1, win = _hy_filter(
        seq, _filter_features(seq), w1, row(hy_filt_b1), f32(hy_filt_w2), row(hy_filt_b2), f32(hy_filt_w3),
        row(hy_filt_b3), f32(hy_filt_w_out), row(hy_filt_freq), _decay_rates(), fmat,
        [f32(ffn1_w_gate), f32(ffn1_w_up), f32(ffn1_w_down), f32(w_in)])

    x1, hy, q, k, v, wo, wg2, wu2, wd2 = _ffn1_proj(
        x.reshape(b * seq, d), row(ffn1_norm_g), wg1, wu1, wd1, row(mix_norm_g), win,
        [f32(w_out), f32(ffn2_w_gate), f32(ffn2_w_up), f32(ffn2_w_down)])

    conv_w = hy_conv_w.astype(F32).reshape(3, 3, D_HYENA).transpose(1, 0, 2)
    conv_b = hy_conv_b.astype(F32).reshape(3, 1, D_HYENA)
    skip = hy_filt_skip.astype(F32).reshape(2, 1, D_HYENA)
    y_hy = _hyena(hy.reshape(b, seq, 3 * D_HYENA), conv_w, conv_b, skip, row(hy_out_norm_g),
                  fmat, hspec)

    shape3 = lambda a: a.reshape(b, seq, D_ATTN)
    y_at = _dil_attn(shape3(q), shape3(k), shape3(v))

    out = _out_ffn2(x1, y_hy.reshape(b * seq, D_HYENA), y_at.reshape(b * seq, D_ATTN),
                    row(attn_out_norm_g), wo, row(ffn2_norm_g), wg2, wu2, wd2, row(final_norm_g))
    return out.reshape(b, seq, d)
```

```python
import functools
import math

import numpy as np
import jax
import jax.numpy as jnp
from jax import lax
from jax.experimental import pallas as pl
from jax.experimental.pallas import tpu as pltpu

F32 = jnp.float32
BF16 = jnp.bfloat16

D_MODEL = 1024
D_HYENA = 512
D_ATTN = 512
HEAD_DIM = 64
N_HEADS = D_ATTN // HEAD_DIM
D_FF = 2816
FILTER_EMB = 33
FILTER_BANDS = 16
FILTER_WIDTH = 64
DECAY_TARGET = 1e-2
FAST_DECAY_PCT = 0.3
SLOW_DECAY_PCT = 1.5
DILATIONS = (1, 4, 16)
HALF_WINDOW = 64
RMS_EPS = 1e-6
NEG_INF = -1e30
LOG2_E = math.log2(math.e)

V7X_LANES = 128
V7X_SUBLANES = 8
V7X_BF16_SUBLANES = 16
V7X_VMEM_LIMIT_BYTES = 56 * 1024 * 1024

ROW_TILE = 512
OUT_ROW_TILE = 1024
FF_CHUNK = 256
FREQ_BLOCK = 1024
FILTER_ROWS = 512
DFT_BASE_ROWS = 256
Q_BLOCK = 128
K_WINDOW = 256
ATTN_GROUP = 16
CONV_ROWS = 256
GATE_SUB_ROWS = 64
FEAT_PAD = 128


def _dot(a, b):
    return jnp.dot(a, b, preferred_element_type=F32)


def _rms(x, g):
    return x * lax.rsqrt(jnp.mean(x * x, axis=-1, keepdims=True) + RMS_EPS) * g


def _swiglu(h, wg_ref, wu_ref, wd_ref, act_ref):
    for c in range(D_FF // FF_CHUNK):
        cols = slice(c * FF_CHUNK, (c + 1) * FF_CHUNK)
        g = _dot(h, wg_ref[:, cols])
        u = _dot(h, wu_ref[:, cols])
        act_ref[:, cols] = (g * jax.nn.sigmoid(g) * u).astype(BF16)
    return _dot(act_ref[...], wd_ref[...])


def _resident(shape):
    return pl.BlockSpec(shape, lambda *_: (0,) * len(shape), pipeline_mode=pl.Buffered(1))


def _cast_rider(weights, steps):
    in_specs, out_specs, out_shapes = [], [], []
    for w in weights:
        rows, cols = w.shape
        visits = 1
        while (rows * visits) % steps or (rows * visits // steps) % V7X_BF16_SUBLANES:
            visits *= 2
        slab = pl.BlockSpec((rows * visits // steps, cols), lambda i, visits=visits: (i // visits, 0))
        in_specs.append(slab)
        out_specs.append(slab)
        out_shapes.append(jax.ShapeDtypeStruct(w.shape, BF16))
    return in_specs, out_specs, out_shapes


def _cast_slabs(src_refs, dst_refs):
    for src, dst in zip(src_refs, dst_refs):
        dst[...] = src[...].astype(BF16)


def _ffn1_proj_kernel(x_ref, g1_ref, wg_ref, wu_ref, wd_ref, g2_ref, win_ref, *rest):
    n_cast = (len(rest) - 6) // 2
    x1_ref, hy_ref, q_ref, k_ref, v_ref = rest[n_cast:n_cast + 5]
    act_ref = rest[-1]
    _cast_slabs(rest[:n_cast], rest[n_cast + 5:-1])
    x = x_ref[...]
    h = _rms(x, g1_ref[...]).astype(BF16)
    x1 = x + 0.5 * _swiglu(h, wg_ref, wu_ref, wd_ref, act_ref)
    x1_ref[...] = x1
    h2 = _rms(x1, g2_ref[...]).astype(BF16)
    nh = 3 * D_HYENA
    hy_ref[...] = _dot(h2, win_ref[:, :nh])
    q_ref[...] = _dot(h2, win_ref[:, nh:nh + D_ATTN])
    k_ref[...] = _dot(h2, win_ref[:, nh + D_ATTN:nh + 2 * D_ATTN])
    v_ref[...] = _dot(h2, win_ref[:, nh + 2 * D_ATTN:])


def _ffn1_proj(x2d, g1, wg, wu, wd, g2, win, later_weights):
    n = x2d.shape[0]
    steps = n // ROW_TILE
    row = lambda w: pl.BlockSpec((ROW_TILE, w), lambda i: (i, 0))
    cast_in, cast_out, cast_shapes = _cast_rider(later_weights, steps)
    return pl.pallas_call(
        _ffn1_proj_kernel,
        grid=(steps,),
        in_specs=[row(D_MODEL), _resident((1, D_MODEL)), _resident(wg.shape), _resident(wu.shape),
                  _resident(wd.shape), _resident((1, D_MODEL)), _resident(win.shape)] + cast_in,
        out_specs=[row(D_MODEL), row(3 * D_HYENA), row(D_ATTN), row(D_ATTN), row(D_ATTN)] + cast_out,
        out_shape=[jax.ShapeDtypeStruct((n, D_MODEL), F32),
                   jax.ShapeDtypeStruct((n, 3 * D_HYENA), F32),
                   jax.ShapeDtypeStruct((n, D_ATTN), F32),
                   jax.ShapeDtypeStruct((n, D_ATTN), F32),
                   jax.ShapeDtypeStruct((n, D_ATTN), F32)] + cast_shapes,
        scratch_shapes=[pltpu.VMEM((ROW_TILE, D_FF), BF16)],
        compiler_params=pltpu.CompilerParams(dimension_semantics=("arbitrary",),
                                             vmem_limit_bytes=V7X_VMEM_LIMIT_BYTES),
        name="ffn1_proj",
    )(x2d, g1, wg, wu, wd, g2, win, *later_weights)


def _split_bf16(a):
    hi = a.astype(BF16)
    return hi, (a - hi.astype(F32)).astype(BF16)


def _dot_split(a, b):
    a_hi, a_lo = a
    b_hi, b_lo = b
    return _dot(a_hi, b_hi) + (_dot(a_hi, b_lo) + _dot(a_lo, b_hi))


def _filter_kernel(seq, feat_ref, w1_ref, b1_ref, w2_ref, b2_ref, w3_ref, b3_ref, wtop_ref, wbot_ref,
                   freq_ref, delta_ref, f_ref, *rest):
    n_cast = (len(rest) - 2) // 2
    h_ref, hcat_ref = rest[n_cast], rest[-1]
    _cast_slabs(rest[:n_cast], rest[n_cast + 1:-1])
    c_ = D_HYENA
    half = seq // 2

    @pl.when(pl.program_id(0) == 0)
    def _():
        hi = lax.Precision.HIGHEST
        dot_hi = lambda a, b: jnp.dot(a, b, precision=hi, preferred_element_type=F32)
        freq = freq_ref[...]
        h = jnp.sin(freq * (dot_hi(feat_ref[...], w1_ref[...]) + b1_ref[...]))
        h = jnp.sin(freq * (dot_hi(h, w2_ref[...]) + b2_ref[...]))
        h = _split_bf16(jnp.sin(freq * (dot_hi(h, w3_ref[...]) + b3_ref[...])))
        row = lax.broadcasted_iota(jnp.int32, (half, c_), 0)
        for part, w_ref in enumerate((wtop_ref, wbot_ref)):
            t = (row[:, :1] + part * half).astype(F32) * (1.0 / (seq - 1))
            decay = jnp.exp(-t * jnp.abs(delta_ref[...]))
            for c in range(4):
                kc = _dot_split(h, _split_bf16(w_ref[:, c * c_:(c + 1) * c_])) * decay
                if c % 2 == 1 and part == 0:
                    kc = jnp.where(row == 0, 0.0, kc)
                hcat_ref[part * half:(part + 1) * half, c * c_:(c + 1) * c_] = kc.astype(BF16)

    is_sin = (pl.program_id(0) * FILTER_ROWS // (FREQ_BLOCK // 2)) % 2
    sign = (1 - 2 * is_sin).astype(F32)
    scale = 1.0 / seq
    for o in range(2):
        p = _dot(f_ref[...], hcat_ref[:, (2 * o) * c_:(2 * o + 2) * c_])
        h_ref[o] = (p[:, :c_] + sign * p[:, c_:]) * scale


def _hy_filter(seq, feat, w1, b1, w2, b2, w3, b3, wout, freq, delta, fmat, later_weights):
    full = lambda a: _resident(a.shape)
    assert (FREQ_BLOCK // 2) % FILTER_ROWS == 0
    steps = 2 * seq // FILTER_ROWS
    cast_in, cast_out, cast_shapes = _cast_rider(later_weights, steps)
    twice = lambda r: jnp.concatenate([r, r], axis=1)
    diag2 = lambda w: jnp.concatenate([jnp.concatenate([w, jnp.zeros_like(w)], axis=1),
                                       jnp.concatenate([jnp.zeros_like(w), w], axis=1)], axis=0)
    feat = jnp.concatenate([feat[:seq // 2], feat[seq // 2:]], axis=1)
    w1, w2, w3 = diag2(w1), diag2(w2), diag2(w3)
    b1, b2, b3, freq = twice(b1), twice(b2), twice(b3), twice(freq)
    wtop = jnp.concatenate([wout, jnp.zeros_like(wout)], axis=0)
    wbot = jnp.concatenate([jnp.zeros_like(wout), wout], axis=0)
    return pl.pallas_call(
        functools.partial(_filter_kernel, seq),
        grid=(steps,),
        in_specs=[full(feat), full(w1), full(b1), full(w2), full(b2), full(w3), full(b3), full(wtop),
                  full(wbot), full(freq), full(delta),
                  pl.BlockSpec((FILTER_ROWS, seq), lambda j: (j, 0))] + cast_in,
        out_specs=[pl.BlockSpec((2, FILTER_ROWS, D_HYENA), lambda j: (0, j, 0))] + cast_out,
        out_shape=[jax.ShapeDtypeStruct((2, 2 * seq, D_HYENA), F32)] + cast_shapes,
        scratch_shapes=[pltpu.VMEM((seq, 4 * D_HYENA), BF16)],
        compiler_params=pltpu.CompilerParams(dimension_semantics=("arbitrary",),
                                             vmem_limit_bytes=V7X_VMEM_LIMIT_BYTES),
        name="hy_filter",
    )(feat, w1, b1, w2, b2, w3, b3, wtop, wbot, freq, delta, fmat, *later_weights)


def _short_conv(cur, before, after, w_ref, b_ref):
    rows = cur.shape[0]
    row = lax.broadcasted_iota(jnp.int32, cur.shape, 0)
    prev = jnp.where(row == 0, before, pltpu.roll(cur, 1, axis=0))
    nxt = jnp.where(row == rows - 1, after, pltpu.roll(cur, rows - 1, axis=0))
    return b_ref[0] + prev * w_ref[0, 0:1, :] + cur * w_ref[0, 1:2, :] + nxt * w_ref[0, 2:3, :]


def _short_conv_rows(u_ref, w_ref, b_ref, t0, seq):
    zero = jnp.zeros((1, u_ref.shape[2]), F32)
    before = u_ref[0, t0 - V7X_SUBLANES:t0, :][V7X_SUBLANES - 1:] if t0 > 0 else zero
    end = t0 + CONV_ROWS
    after = u_ref[0, end:end + V7X_SUBLANES, :][:1] if end < seq else zero
    return _short_conv(u_ref[0, t0:end, :], before, after, w_ref, b_ref)


def _hyena_kernel(seq, v_ref, gate_ref, gate_lo_ref, gate_hi_ref, wv_ref, bv_ref, wg_ref, bg_ref, skip_ref,
                  gain_ref, f_ref, h_ref, out_ref, zf_ref, zb_ref, acc_ref, g_ref, y_ref):
    o = pl.program_id(1)
    j = pl.program_id(2)
    nfb = 2 * seq // FREQ_BLOCK
    gate_rows = seq // nfb
    chunks = range(0, seq, CONV_ROWS)
    hb = FREQ_BLOCK // 2

    def f_block(jb):
        return f_ref[pl.ds(pl.multiple_of(jb * FREQ_BLOCK, FREQ_BLOCK), FREQ_BLOCK), :]

    def forward(jb, slot):
        zfreq = _dot(f_block(jb), zb_ref[...])
        zr, zi = zfreq[:hb], zfreq[hb:]
        hr, hi = h_ref[0, :hb, :], h_ref[0, hb:, :]
        y_ref[slot, :, :hb] = (zr * hr - zi * hi).T.astype(BF16)
        y_ref[slot, :, hb:] = (zr * hi + zi * hr).T.astype(BF16)

    def inverse(jb, slot):
        acc_ref[...] += _dot(y_ref[slot], f_block(jb))

    def gate_chunk():
        t0 = pl.multiple_of(j * gate_rows, gate_rows)
        for r0 in range(0, gate_rows, GATE_SUB_ROWS):
            r1 = r0 + GATE_SUB_ROWS
            if r0 == 0:
                before = jnp.where(j > 0, gate_lo_ref[0, V7X_SUBLANES - 1:, :], 0.0)
            else:
                before = gate_ref[0, r0 - V7X_SUBLANES:r0, :][V7X_SUBLANES - 1:]
            if r1 == gate_rows:
                after = jnp.where(j < nfb - 1, gate_hi_ref[0, :1, :], 0.0)
            else:
                after = gate_ref[0, r1:r1 + V7X_SUBLANES, :][:1]
            g_ref[pl.ds(t0 + r0, GATE_SUB_ROWS), :] = _short_conv(gate_ref[0, r0:r1, :], before, after,
                                                                   wg_ref, bg_ref)

    @pl.when((o == 0) & (j == 0))
    def _():
        for t0 in chunks:
            z = _short_conv_rows(v_ref, wv_ref, bv_ref, t0, seq)
            zf_ref[t0:t0 + CONV_ROWS, :] = z
            zb_ref[t0:t0 + CONV_ROWS, :] = z.astype(BF16)

    @pl.when(j == 0)
    def _():
        gate_chunk()
        acc_ref[...] = jnp.zeros_like(acc_ref)
        forward(0, 0)

    for parity in range(2):
        @pl.when((j > 0) & (j < nfb) & (j % 2 == parity))
        def _(parity=parity):
            gate_chunk()
            forward(j, parity)
            inverse(j - 1, 1 - parity)

    def gated(rows):
        return g_ref[rows, :] * (acc_ref[:, rows].T + zf_ref[rows, :] * skip_ref[0])

    last_slot = (nfb - 1) % 2

    @pl.when((j == nfb) & (o == 0))
    def _():
        inverse(nfb - 1, last_slot)
        for t0 in chunks:
            rows = slice(t0, t0 + CONV_ROWS)
            z = gated(rows)
            zf_ref[rows, :] = z
            zb_ref[rows, :] = z.astype(BF16)

    @pl.when((j == nfb) & (o == 1))
    def _():
        inverse(nfb - 1, last_slot)
        for t0 in chunks:
            rows = slice(t0, t0 + CONV_ROWS)
            out_ref[0, rows, :] = _rms(gated(rows), gain_ref[...]).astype(BF16)


def _hyena(hy, conv_w, conv_b, skip, gain, fmat, hspec):
    b, seq, _ = hy.shape
    c = D_HYENA
    nfb = 2 * seq // FREQ_BLOCK
    gate_rows = seq // nfb
    halo_per_chunk = gate_rows // V7X_SUBLANES
    fwd_blk = lambda j: jnp.minimum(j, nfb - 1)
    wpart = lambda sel: pl.BlockSpec((1, 3, c), lambda bi, o, j: (sel(o), 0, 0))
    bpart = lambda sel: pl.BlockSpec((1, 1, c), lambda bi, o, j: (sel(o), 0, 0))
    value = lambda o: 0
    gate = lambda o: 1 + o
    halo = lambda first: pl.BlockSpec((1, V7X_SUBLANES, c), lambda bi, o, j: (bi, first(fwd_blk(j)), 1 + o))
    return pl.pallas_call(
        functools.partial(_hyena_kernel, seq),
        grid=(b, 2, nfb + 1),
        in_specs=[pl.BlockSpec((1, seq, c), lambda bi, o, j: (bi, 0, 0), pipeline_mode=pl.Buffered(1)),
                  pl.BlockSpec((1, gate_rows, c), lambda bi, o, j: (bi, fwd_blk(j), 1 + o)),
                  halo(lambda jc: jnp.maximum(jc * halo_per_chunk - 1, 0)),
                  halo(lambda jc: jnp.minimum((jc + 1) * halo_per_chunk, seq // V7X_SUBLANES - 1)),
                  wpart(value), bpart(value), wpart(gate), bpart(gate),
                  pl.BlockSpec((1, 1, c), lambda bi, o, j: (o, 0, 0)),
                  pl.BlockSpec((1, c), lambda bi, o, j: (0, 0)),
                  _resident(fmat.shape),
                  pl.BlockSpec((1, FREQ_BLOCK, c), lambda bi, o, j: (o, fwd_blk(j), 0))],
        out_specs=pl.BlockSpec((1, seq, c), lambda bi, o, j: (bi, 0, 0), pipeline_mode=pl.Buffered(1)),
        out_shape=jax.ShapeDtypeStruct((b, seq, c), BF16),
        scratch_shapes=[pltpu.VMEM((seq, c), F32), pltpu.VMEM((seq, c), BF16), pltpu.VMEM((c, seq), F32),
                        pltpu.VMEM((seq, c), F32), pltpu.VMEM((2, c, FREQ_BLOCK), BF16)],
        compiler_params=pltpu.CompilerParams(dimension_semantics=("parallel", "arbitrary", "arbitrary"),
                                             vmem_limit_bytes=V7X_VMEM_LIMIT_BYTES),
        name="hyena",
    )(hy, hy, hy, hy, conv_w, conv_b, conv_w, conv_b, skip, gain, fmat, hspec)


def _attn_kernel(seq, q_ref, k_ref, v_ref, bias_ref, bias16_ref, o_ref,
                 qs_ref, ks_ref, vs_ref, res_ref, nat_ref, s_ref, e_ref):
    lanes = V7X_LANES
    qscale = LOG2_E / math.sqrt(HEAD_DIM)

    for p, d in enumerate(DILATIONS):
        ls = seq // d
        first = lax.broadcasted_iota(jnp.int32, (ls, lanes), 1) < HEAD_DIM
        for r in range(d):
            src = pl.ds(r, ls, stride=d) if d > 1 else pl.ds(0, ls)
            rows = slice(r * ls, (r + 1) * ls)
            qq = q_ref[0, src, :] * qscale
            qs_ref[p, 0, rows, :] = jnp.where(first, qq, 0.0).astype(BF16)
            qs_ref[p, 1, rows, :] = jnp.where(first, 0.0, qq).astype(BF16)
            ks_ref[p, rows, :] = k_ref[0, src, :].astype(BF16)
            vv = v_ref[0, src, :]
            vs_ref[p, 0, rows, :] = jnp.where(first, vv, 1.0).astype(BF16)
            vs_ref[p, 1, rows, :] = jnp.where(first, 1.0, vv).astype(BF16)

    first = lax.broadcasted_iota(jnp.int32, (Q_BLOCK, lanes), 1) < HEAD_DIM

    def run_pattern(p, nkeys, placement):
        def body(g, carry):
            blocks = [placement(g * ATTN_GROUP + i) for i in range(ATTN_GROUP)]
            for i, (row0, krow0, bias_of_head) in enumerate(blocks):
                kw = ks_ref[p, pl.ds(krow0, nkeys), :]
                for h in range(2):
                    qh = qs_ref[p, h, pl.ds(row0, Q_BLOCK), :]
                    s = lax.dot_general(qh, kw, (((1,), (1,)), ((), ())), preferred_element_type=F32)
                    s_ref[2 * i + h, :, :nkeys] = s + bias_of_head(h)
            for i, (row0, krow0, bias_of_head) in enumerate(blocks):
                ms = []
                for h in range(2):
                    s = s_ref[2 * i + h, :, :nkeys]
                    m = jnp.max(s, axis=-1, keepdims=True)
                    e_ref[2 * i + h, :, :nkeys] = jnp.exp2(s - m).astype(BF16)
                    ms.append(m)
                res_ref[p, 1, pl.ds(row0, Q_BLOCK), :] = jnp.where(first, ms[0], ms[1])
            for i, (row0, krow0, bias_of_head) in enumerate(blocks):
                o0 = _dot(e_ref[2 * i, :, :nkeys], vs_ref[p, 0, pl.ds(krow0, nkeys), :])
                o1 = _dot(e_ref[2 * i + 1, :, :nkeys], vs_ref[p, 1, pl.ds(krow0, nkeys), :])
                res_ref[p, 0, pl.ds(row0, Q_BLOCK), :] = jnp.where(first, o0, o1)
                res_ref[p, 2, pl.ds(row0, Q_BLOCK), :] = jnp.where(first, o1, o0)
            return carry

        lax.fori_loop(0, seq // Q_BLOCK // ATTN_GROUP, body, 0)

    for p, d in enumerate(DILATIONS[:2]):
        ls = seq // d
        nblk = ls // Q_BLOCK

        def banded(n, p=p, ls=ls, nblk=nblk):
            r = n // nblk
            ib = n % nblk
            i0 = ib * Q_BLOCK
            k0 = jnp.clip(i0 - HALF_WINDOW, 0, ls - K_WINDOW)
            case = jnp.where(ib == 0, 0, jnp.where(ib == nblk - 1, 2, 1))
            row0 = pl.multiple_of(r * ls + i0, Q_BLOCK)
            krow0 = pl.multiple_of(r * ls + k0, HALF_WINDOW)
            return row0, krow0, lambda h: bias_ref[0, (p * 3 + case) * 2 + h]

        run_pattern(p, K_WINDOW, banded)

    def full(n):
        row0 = pl.multiple_of(n * Q_BLOCK, Q_BLOCK)
        return row0, row0, lambda h: bias16_ref[0, h]

    run_pattern(2, Q_BLOCK, full)

    for p, d in enumerate(DILATIONS[1:], start=1):
        ls = seq // d
        for r in range(d):
            for kind in range(3):
                nat_ref[p - 1, kind, pl.ds(r, ls, stride=d), :] = res_ref[p, kind, r * ls:(r + 1) * ls, :]

    for t0 in range(0, seq, CONV_ROWS):
        rows = slice(t0, t0 + CONV_ROWS)
        parts = [tuple(res_ref[0, kind, rows, :] for kind in range(3))]
        parts += [tuple(nat_ref[p, kind, rows, :] for kind in range(3)) for p in range(2)]
        m = functools.reduce(jnp.maximum, [pt[1] for pt in parts])
        num = 0.0
        den = 0.0
        for out, mp, lp in parts:
            w = jnp.exp2(mp - m)
            num = num + w * out
            den = den + w * pltpu.roll(lp, HEAD_DIM, axis=1)
        o_ref[0, rows, :] = num / den


def _attn_bias_tables():
    slopes = np.array([2.0 ** (-8.0 * (i + 1) / N_HEADS) for i in range(N_HEADS)], np.float32)
    slopes = jnp.asarray(slopes.reshape(N_HEADS // 2, 1, 2, 1, 1))
    qi = lax.broadcasted_iota(jnp.int32, (Q_BLOCK, K_WINDOW), 0)
    kj = lax.broadcasted_iota(jnp.int32, (Q_BLOCK, K_WINDOW), 1)
    offsets = (0, -HALF_WINDOW, -2 * HALF_WINDOW)
    dist = jnp.stack([jnp.abs(kj - qi + off) for off in offsets])
    valid = dist <= HALF_WINDOW
    dil = jnp.asarray(np.array(DILATIONS[:2], np.float32).reshape(2, 1, 1, 1))
    scaled = (dil * dist.astype(F32)[None])[None, :, :, None]
    banded = jnp.where(valid[None, None, :, None], -slopes[:, None] * scaled, NEG_INF)
    banded = banded.reshape(N_HEADS // 2, 12, Q_BLOCK, K_WINDOW)
    d16 = dist[0, :, :Q_BLOCK]
    full = jnp.where(d16 <= HALF_WINDOW, -slopes[:, 0] * (DILATIONS[2] * d16.astype(F32)), NEG_INF)
    to_base2 = lambda t: jnp.where(t > 0.5 * NEG_INF, t * LOG2_E, NEG_INF).astype(F32)
    return to_base2(banded), to_base2(full)


def _dil_attn(q, k, v):
    b, seq, _ = q.shape
    nhp = N_HEADS // 2
    bias, bias16 = _attn_bias_tables()
    head_pair = pl.BlockSpec((1, seq, V7X_LANES), lambda bi, hp: (bi, 0, hp))
    return pl.pallas_call(
        functools.partial(_attn_kernel, seq),
        grid=(b, nhp),
        in_specs=[head_pair, head_pair, head_pair,
                  pl.BlockSpec((1, 12, Q_BLOCK, K_WINDOW), lambda bi, hp: (hp, 0, 0, 0)),
                  pl.BlockSpec((1, 2, Q_BLOCK, Q_BLOCK), lambda bi, hp: (hp, 0, 0, 0))],
        out_specs=head_pair,
        out_shape=jax.ShapeDtypeStruct((b, seq, D_ATTN), F32),
        scratch_shapes=[pltpu.VMEM((3, 2, seq, V7X_LANES), BF16),
                        pltpu.VMEM((3, seq, V7X_LANES), BF16),
                        pltpu.VMEM((3, 2, seq, V7X_LANES), BF16),
                        pltpu.VMEM((3, 3, seq, V7X_LANES), F32),
                        pltpu.VMEM((2, 3, seq, V7X_LANES), F32),
                        pltpu.VMEM((2 * ATTN_GROUP, Q_BLOCK, K_WINDOW), F32),
                        pltpu.VMEM((2 * ATTN_GROUP, Q_BLOCK, K_WINDOW), BF16)],
        compiler_params=pltpu.CompilerParams(dimension_semantics=("parallel", "parallel"),
                                             vmem_limit_bytes=V7X_VMEM_LIMIT_BYTES),
        name="dil_attn",
    )(q, k, v, bias, bias16)


def _out_ffn2_kernel(x1_ref, yh_ref, ya_ref, ga_ref, wo_ref, g3_ref, wg_ref, wu_ref, wd_ref, gf_ref,
                     out_ref, act_ref):
    ya = _rms(ya_ref[...], ga_ref[...]).astype(BF16)
    mix = _dot(yh_ref[...], wo_ref[:D_HYENA, :]) + _dot(ya, wo_ref[D_HYENA:, :])
    x2 = x1_ref[...] + mix
    h = _rms(x2, g3_ref[...]).astype(BF16)
    x3 = x2 + 0.5 * _swiglu(h, wg_ref, wu_ref, wd_ref, act_ref)
    out_ref[...] = _rms(x3, gf_ref[...])


def _out_ffn2(x1, yh, ya, ga, wo, g3, wg, wu, wd, gf):
    n = x1.shape[0]
    row = lambda w: pl.BlockSpec((OUT_ROW_TILE, w), lambda i: (i, 0))
    return pl.pallas_call(
        _out_ffn2_kernel,
        grid=(n // OUT_ROW_TILE,),
        in_specs=[row(D_MODEL), row(D_HYENA), row(D_ATTN), _resident((1, D_ATTN)), _resident(wo.shape),
                  _resident((1, D_MODEL)), _resident(wg.shape), _resident(wu.shape), _resident(wd.shape),
                  _resident((1, D_MODEL))],
        out_specs=row(D_MODEL),
        out_shape=jax.ShapeDtypeStruct((n, D_MODEL), F32),
        scratch_shapes=[pltpu.VMEM((OUT_ROW_TILE, D_FF), BF16)],
        compiler_params=pltpu.CompilerParams(dimension_semantics=("parallel",),
                                             vmem_limit_bytes=V7X_VMEM_LIMIT_BYTES),
        name="out_ffn2",
    )(x1, yh, ya, ga, wo, g3, wg, wu, wd, gf)


def _dft_matrices(seq):
    n = 2 * seq
    hb = FREQ_BLOCK // 2
    s = np.arange(seq, dtype=np.int64)[None, :]
    turn = lambda steps: 2.0 * np.pi * (steps % (2 * n)) / (2 * n)
    theta = turn((2 * np.arange(DFT_BASE_ROWS, dtype=np.int64)[:, None] + 1) * s)
    phase = turn(2 * DFT_BASE_ROWS * np.arange(seq // DFT_BASE_ROWS, dtype=np.int64)[:, None] * s)
    tab = lambda x: jnp.asarray(x.astype(np.float32))
    c0, s0 = np.cos(theta), np.sin(theta)
    lead = tab(np.stack([c0, -s0]))[None, :, None]
    lag = tab(np.stack([s0, c0]))[None, :, None]
    group = lambda x: tab(x).reshape(seq // hb, 1, hb // DFT_BASE_ROWS, 1, seq)
    return (lead * group(np.cos(phase)) - lag * group(np.sin(phase))).reshape(n, seq).astype(BF16)


def _filter_features(seq):
    t = jnp.linspace(0.0, 1.0, seq, dtype=F32)[:, None]
    w = 2.0 * math.pi * jnp.arange(seq, dtype=F32)[:, None] / seq
    f = jnp.linspace(1e-4, FILTER_BANDS - 1, FILTER_BANDS, dtype=F32)[None, :]
    z = jnp.concatenate([t, jnp.cos(f * w), -jnp.sin(f * w)], axis=-1)
    return jnp.pad(z, ((0, 0), (0, FEAT_PAD - FILTER_EMB)))


def _decay_rates():
    max_decay = math.log(DECAY_TARGET) / FAST_DECAY_PCT
    min_decay = math.log(DECAY_TARGET) / SLOW_DECAY_PCT
    return jnp.linspace(min_decay, max_decay, D_HYENA, dtype=F32)[None, :]


def kernel(x, ffn1_norm_g, ffn1_w_gate, ffn1_w_up, ffn1_w_down, mix_norm_g, w_in, hy_conv_w, hy_conv_b, hy_filt_w1, hy_filt_b1, hy_filt_w2, hy_filt_b2, hy_filt_w3, hy_filt_b3, hy_filt_w_out, hy_filt_freq, hy_filt_skip, hy_out_norm_g, attn_out_norm_g, w_out, ffn2_norm_g, ffn2_w_gate, ffn2_w_up, ffn2_w_down, final_norm_g):
    b, seq, d = x.shape
    assert d == D_MODEL and (b * seq) % OUT_ROW_TILE == 0 and seq % (DILATIONS[-1] * Q_BLOCK) == 0
    row = lambda a: a.reshape(1, -1).astype(F32)
    f32 = lambda a: a.astype(F32)

    fmat = _dft_matrices(seq)
    w1 = jnp.pad(f32(hy_filt_w1), ((0, FEAT_PAD - FILTER_EMB), (0, 0)))
    hspec, wg1, wu1, wd1, win = _hy_filter(
        seq, _filter_features(seq), w1, row(hy_filt_b1), f32(hy_filt_w2), row(hy_filt_b2), f32(hy_filt_w3),
        row(hy_filt_b3), f32(hy_filt_w_out), row(hy_filt_freq), _decay_rates(), fmat,
        [f32(ffn1_w_gate), f32(ffn1_w_up), f32(ffn1_w_down), f32(w_in)])

    x1, hy, q, k, v, wo, wg2, wu2, wd2 = _ffn1_proj(
        x.reshape(b * seq, d), row(ffn1_norm_g), wg1, wu1, wd1, row(mix_norm_g), win,
        [f32(w_out), f32(ffn2_w_gate), f32(ffn2_w_up), f32(ffn2_w_down)])

    conv_w = hy_conv_w.astype(F32).reshape(3, 3, D_HYENA).transpose(1, 0, 2)
    conv_b = hy_conv_b.astype(F32).reshape(3, 1, D_HYENA)
    skip = hy_filt_skip.astype(F32).reshape(2, 1, D_HYENA)
    y_hy = _hyena(hy.reshape(b, seq, 3 * D_HYENA), conv_w, conv_b, skip, row(hy_out_norm_g),
                  fmat, hspec)

    shape3 = lambda a: a.reshape(b, seq, D_ATTN)
    y_at = _dil_attn(shape3(q), shape3(k), shape3(v))

    out = _out_ffn2(x1, y_hy.reshape(b * seq, D_HYENA), y_at.reshape(b * seq, D_ATTN),
                    row(attn_out_norm_g), wo, row(ffn2_norm_g), wg2, wu2, wd2, row(final_norm_g))
    return out.reshape(b, seq, d)
```

```python
import functools
import math

import numpy as np
import jax
import jax.numpy as jnp
from jax import lax
from jax.experimental import pallas as pl
from jax.experimental.pallas import tpu as pltpu

F32 = jnp.float32
BF16 = jnp.bfloat16

D_MODEL = 1024
D_HYENA = 512
D_ATTN = 512
HEAD_DIM = 64
N_HEADS = D_ATTN // HEAD_DIM
D_FF = 2816
FILTER_EMB = 33
FILTER_BANDS = 16
FILTER_WIDTH = 64
DECAY_TARGET = 1e-2
FAST_DECAY_PCT = 0.3
SLOW_DECAY_PCT = 1.5
DILATIONS = (1, 4, 16)
HALF_WINDOW = 64
RMS_EPS = 1e-6
NEG_INF = -1e30
LOG2_E = math.log2(math.e)

V7X_LANES = 128
V7X_SUBLANES = 8
V7X_BF16_SUBLANES = 16
V7X_VMEM_LIMIT_BYTES = 56 * 1024 * 1024

ROW_TILE = 512
OUT_ROW_TILE = 1024
FF_CHUNK = 256
FREQ_BLOCK = 1024
FILTER_ROWS = 512
DFT_BASE_ROWS = 256
Q_BLOCK = 128
K_WINDOW = 256
ATTN_GROUP = 16
CONV_ROWS = 256
GATE_SUB_ROWS = 64
FEAT_PAD = 128


def _dot(a, b):
    return jnp.dot(a, b, preferred_element_type=F32)


def _rms(x, g):
    return x * lax.rsqrt(jnp.mean(x * x, axis=-1, keepdims=True) + RMS_EPS) * g


def _swiglu(h, wg_ref, wu_ref, wd_ref, act_ref):
    for c in range(D_FF // FF_CHUNK):
        cols = slice(c * FF_CHUNK, (c + 1) * FF_CHUNK)
        g = _dot(h, wg_ref[:, cols])
        u = _dot(h, wu_ref[:, cols])
        act_ref[:, cols] = (g * jax.nn.sigmoid(g) * u).astype(BF16)
    return _dot(act_ref[...], wd_ref[...])


def _resident(shape):
    return pl.BlockSpec(shape, lambda *_: (0,) * len(shape), pipeline_mode=pl.Buffered(1))


def _cast_rider(weights, steps):
    in_specs, out_specs, out_shapes = [], [], []
    for w in weights:
        rows, cols = w.shape
        visits = 1
        while (rows * visits) % steps or (rows * visits // steps) % V7X_BF16_SUBLANES:
            visits *= 2
        slab = pl.BlockSpec((rows * visits // steps, cols), lambda i, visits=visits: (i // visits, 0))
        in_specs.append(slab)
        out_specs.append(slab)
        out_shapes.append(jax.ShapeDtypeStruct(w.shape, BF16))
    return in_specs, out_specs, out_shapes


def _cast_slabs(src_refs, dst_refs):
    for src, dst in zip(src_refs, dst_refs):
        dst[...] = src[...].astype(BF16)


def _ffn1_proj_kernel(x_ref, g1_ref, wg_ref, wu_ref, wd_ref, g2_ref, win_ref, *rest):
    n_cast = (len(rest) - 6) // 2
    x1_ref, hy_ref, q_ref, k_ref, v_ref = rest[n_cast:n_cast + 5]
    act_ref = rest[-1]
    _cast_slabs(rest[:n_cast], rest[n_cast + 5:-1])
    x = x_ref[...]
    h = _rms(x, g1_ref[...]).astype(BF16)
    x1 = x + 0.5 * _swiglu(h, wg_ref, wu_ref, wd_ref, act_ref)
    x1_ref[...] = x1
    h2 = _rms(x1, g2_ref[...]).astype(BF16)
    nh = 3 * D_HYENA
    hy_ref[...] = _dot(h2, win_ref[:, :nh])
    q_ref[...] = _dot(h2, win_ref[:, nh:nh + D_ATTN])
    k_ref[...] = _dot(h2, win_ref[:, nh + D_ATTN:nh + 2 * D_ATTN])
    v_ref[...] = _dot(h2, win_ref[:, nh + 2 * D_ATTN:])


def _ffn1_proj(x2d, g1, wg, wu, wd, g2, win, later_weights):
    n = x2d.shape[0]
    steps = n // ROW_TILE
    row = lambda w: pl.BlockSpec((ROW_TILE, w), lambda i: (i, 0))
    cast_in, cast_out, cast_shapes = _cast_rider(later_weights, steps)
    return pl.pallas_call(
        _ffn1_proj_kernel,
        grid=(steps,),
        in_specs=[row(D_MODEL), _resident((1, D_MODEL)), _resident(wg.shape), _resident(wu.shape),
                  _resident(wd.shape), _resident((1, D_MODEL)), _resident(win.shape)] + cast_in,
        out_specs=[row(D_MODEL), row(3 * D_HYENA), row(D_ATTN), row(D_ATTN), row(D_ATTN)] + cast_out,
        out_shape=[jax.ShapeDtypeStruct((n, D_MODEL), F32),
                   jax.ShapeDtypeStruct((n, 3 * D_HYENA), F32),
                   jax.ShapeDtypeStruct((n, D_ATTN), F32),
                   jax.ShapeDtypeStruct((n, D_ATTN), F32),
                   jax.ShapeDtypeStruct((n, D_ATTN), F32)] + cast_shapes,
        scratch_shapes=[pltpu.VMEM((ROW_TILE, D_FF), BF16)],
        compiler_params=pltpu.CompilerParams(dimension_semantics=("arbitrary",),
                                             vmem_limit_bytes=V7X_VMEM_LIMIT_BYTES),
        name="ffn1_proj",
    )(x2d, g1, wg, wu, wd, g2, win, *later_weights)


def _split_bf16(a):
    hi = a.astype(BF16)
    return hi, (a - hi.astype(F32)).astype(BF16)


def _dot_split(a, b):
    a_hi, a_lo = a
    b_hi, b_lo = b
    return _dot(a_hi, b_hi) + (_dot(a_hi, b_lo) + _dot(a_lo, b_hi))


def _filter_kernel(seq, feat_ref, w1_ref, b1_ref, w2_ref, b2_ref, w3_ref, b3_ref, wtop_ref, wbot_ref,
                   freq_ref, delta_ref, f_ref, *rest):
    n_cast = (len(rest) - 2) // 2
    h_ref, hcat_ref = rest[n_cast], rest[-1]
    _cast_slabs(rest[:n_cast], rest[n_cast + 1:-1])
    c_ = D_HYENA
    half = seq // 2

    @pl.when(pl.program_id(0) == 0)
    def _():
        hi = lax.Precision.HIGHEST
        dot_hi = lambda a, b: jnp.dot(a, b, precision=hi, preferred_element_type=F32)
        freq = freq_ref[...]
        h = jnp.sin(freq * (dot_hi(feat_ref[...], w1_ref[...]) + b1_ref[...]))
        h = jnp.sin(freq * (dot_hi(h, w2_ref[...]) + b2_ref[...]))
        h = _split_bf16(jnp.sin(freq * (dot_hi(h, w3_ref[...]) + b3_ref[...])))
        row = lax.broadcasted_iota(jnp.int32, (half, c_), 0)
        for part, w_ref in enumerate((wtop_ref, wbot_ref)):
            t = (row[:, :1] + part * half).astype(F32) * (1.0 / (seq - 1))
            decay = jnp.exp(-t * jnp.abs(delta_ref[...]))
            rows = slice(part * half, (part + 1) * half)
            for o in range(2):
                fwd = _dot_split(h, _split_bf16(w_ref[:, (2 * o) * c_:(2 * o + 1) * c_])) * decay
                bwd = _dot_split(h, _split_bf16(w_ref[:, (2 * o + 1) * c_:(2 * o + 2) * c_])) * decay
                if part == 0:
                    bwd = jnp.where(row == 0, 0.0, bwd)
                hcat_ref[0, rows, o * c_:(o + 1) * c_] = (fwd + bwd).astype(BF16)
                hcat_ref[1, rows, o * c_:(o + 1) * c_] = (fwd - bwd).astype(BF16)

    is_sin = (pl.program_id(0) * FILTER_ROWS // (FREQ_BLOCK // 2)) % 2
    p = _dot(f_ref[...], hcat_ref[is_sin]) * (1.0 / seq)
    for o in range(2):
        h_ref[o] = p[:, o * c_:(o + 1) * c_]


def _hy_filter(seq, feat, w1, b1, w2, b2, w3, b3, wout, freq, delta, fmat, later_weights):
    full = lambda a: _resident(a.shape)
    assert (FREQ_BLOCK // 2) % FILTER_ROWS == 0
    steps = 2 * seq // FILTER_ROWS
    cast_in, cast_out, cast_shapes = _cast_rider(later_weights, steps)
    twice = lambda r: jnp.concatenate([r, r], axis=1)
    diag2 = lambda w: jnp.concatenate([jnp.concatenate([w, jnp.zeros_like(w)], axis=1),
                                       jnp.concatenate([jnp.zeros_like(w), w], axis=1)], axis=0)
    feat = jnp.concatenate([feat[:seq // 2], feat[seq // 2:]], axis=1)
    w1, w2, w3 = diag2(w1), diag2(w2), diag2(w3)
    b1, b2, b3, freq = twice(b1), twice(b2), twice(b3), twice(freq)
    wtop = jnp.concatenate([wout, jnp.zeros_like(wout)], axis=0)
    wbot = jnp.concatenate([jnp.zeros_like(wout), wout], axis=0)
    return pl.pallas_call(
        functools.partial(_filter_kernel, seq),
        grid=(steps,),
        in_specs=[full(feat), full(w1), full(b1), full(w2), full(b2), full(w3), full(b3), full(wtop),
                  full(wbot), full(freq), full(delta),
                  pl.BlockSpec((FILTER_ROWS, seq), lambda j: (j, 0))] + cast_in,
        out_specs=[pl.BlockSpec((2, FILTER_ROWS, D_HYENA), lambda j: (0, j, 0))] + cast_out,
        out_shape=[jax.ShapeDtypeStruct((2, 2 * seq, D_HYENA), F32)] + cast_shapes,
        scratch_shapes=[pltpu.VMEM((2, seq, 2 * D_HYENA), BF16)],
        compiler_params=pltpu.CompilerParams(dimension_semantics=("arbitrary",),
                                             vmem_limit_bytes=V7X_VMEM_LIMIT_BYTES),
        name="hy_filter",
    )(feat, w1, b1, w2, b2, w3, b3, wtop, wbot, freq, delta, fmat, *later_weights)


def _short_conv(cur, before, after, w_ref, b_ref):
    rows = cur.shape[0]
    row = lax.broadcasted_iota(jnp.int32, cur.shape, 0)
    prev = jnp.where(row == 0, before, pltpu.roll(cur, 1, axis=0))
    nxt = jnp.where(row == rows - 1, after, pltpu.roll(cur, rows - 1, axis=0))
    return b_ref[0] + prev * w_ref[0, 0:1, :] + cur * w_ref[0, 1:2, :] + nxt * w_ref[0, 2:3, :]


def _short_conv_rows(u_ref, w_ref, b_ref, t0, seq):
    zero = jnp.zeros((1, u_ref.shape[2]), F32)
    before = u_ref[0, t0 - V7X_SUBLANES:t0, :][V7X_SUBLANES - 1:] if t0 > 0 else zero
    end = t0 + CONV_ROWS
    after = u_ref[0, end:end + V7X_SUBLANES, :][:1] if end < seq else zero
    return _short_conv(u_ref[0, t0:end, :], before, after, w_ref, b_ref)


def _hyena_kernel(seq, v_ref, gate_ref, gate_lo_ref, gate_hi_ref, wv_ref, bv_ref, wg_ref, bg_ref, skip_ref,
                  gain_ref, f_ref, h_ref, out_ref, zf_ref, zb_ref, acc_ref, g_ref, y_ref):
    o = pl.program_id(1)
    j = pl.program_id(2)
    nfb = 2 * seq // FREQ_BLOCK
    gate_rows = seq // nfb
    chunks = range(0, seq, CONV_ROWS)
    hb = FREQ_BLOCK // 2

    def f_block(jb):
        return f_ref[pl.ds(pl.multiple_of(jb * FREQ_BLOCK, FREQ_BLOCK), FREQ_BLOCK), :]

    def forward(jb, slot):
        zfreq = _dot(f_block(jb), zb_ref[...])
        zr, zi = zfreq[:hb], zfreq[hb:]
        hr, hi = h_ref[0, :hb, :], h_ref[0, hb:, :]
        y_ref[slot, :, :hb] = (zr * hr - zi * hi).T.astype(BF16)
        y_ref[slot, :, hb:] = (zr * hi + zi * hr).T.astype(BF16)

    def inverse(jb, slot):
        acc_ref[...] += _dot(y_ref[slot], f_block(jb))

    def gate_chunk():
        t0 = pl.multiple_of(j * gate_rows, gate_rows)
        for r0 in range(0, gate_rows, GATE_SUB_ROWS):
            r1 = r0 + GATE_SUB_ROWS
            if r0 == 0:
                before = jnp.where(j > 0, gate_lo_ref[0, V7X_SUBLANES - 1:, :], 0.0)
            else:
                before = gate_ref[0, r0 - V7X_SUBLANES:r0, :][V7X_SUBLANES - 1:]
            if r1 == gate_rows:
                after = jnp.where(j < nfb - 1, gate_hi_ref[0, :1, :], 0.0)
            else:
                after = gate_ref[0, r1:r1 + V7X_SUBLANES, :][:1]
            g_ref[pl.ds(t0 + r0, GATE_SUB_ROWS), :] = _short_conv(gate_ref[0, r0:r1, :], before, after,
                                                                   wg_ref, bg_ref)

    @pl.when((o == 0) & (j == 0))
    def _():
        for t0 in chunks:
            z = _short_conv_rows(v_ref, wv_ref, bv_ref, t0, seq)
            zf_ref[t0:t0 + CONV_ROWS, :] = z
            zb_ref[t0:t0 + CONV_ROWS, :] = z.astype(BF16)

    @pl.when(j == 0)
    def _():
        gate_chunk()
        acc_ref[...] = jnp.zeros_like(acc_ref)
        forward(0, 0)

    for parity in range(2):
        @pl.when((j > 0) & (j < nfb) & (j % 2 == parity))
        def _(parity=parity):
            gate_chunk()
            forward(j, parity)
            inverse(j - 1, 1 - parity)

    def gated(rows):
        return g_ref[rows, :] * (acc_ref[:, rows].T + zf_ref[rows, :] * skip_ref[0])

    last_slot = (nfb - 1) % 2

    @pl.when((j == nfb) & (o == 0))
    def _():
        inverse(nfb - 1, last_slot)
        for t0 in chunks:
            rows = slice(t0, t0 + CONV_ROWS)
            z = gated(rows)
            zf_ref[rows, :] = z
            zb_ref[rows, :] = z.astype(BF16)

    @pl.when((j == nfb) & (o == 1))
    def _():
        inverse(nfb - 1, last_slot)
        for t0 in chunks:
            rows = slice(t0, t0 + CONV_ROWS)
            out_ref[0, rows, :] = _rms(gated(rows), gain_ref[...]).astype(BF16)


def _hyena(hy, conv_w, conv_b, skip, gain, fmat, hspec):
    b, seq, _ = hy.shape
    c = D_HYENA
    nfb = 2 * seq // FREQ_BLOCK
    gate_rows = seq // nfb
    halo_per_chunk = gate_rows // V7X_SUBLANES
    fwd_blk = lambda j: jnp.minimum(j, nfb - 1)
    wpart = lambda sel: pl.BlockSpec((1, 3, c), lambda bi, o, j: (sel(o), 0, 0))
    bpart = lambda sel: pl.BlockSpec((1, 1, c), lambda bi, o, j: (sel(o), 0, 0))
    value = lambda o: 0
    gate = lambda o: 1 + o
    halo = lambda first: pl.BlockSpec((1, V7X_SUBLANES, c), lambda bi, o, j: (bi, first(fwd_blk(j)), 1 + o))
    return pl.pallas_call(
        functools.partial(_hyena_kernel, seq),
        grid=(b, 2, nfb + 1),
        in_specs=[pl.BlockSpec((1, seq, c), lambda bi, o, j: (bi, 0, 0), pipeline_mode=pl.Buffered(1)),
                  pl.BlockSpec((1, gate_rows, c), lambda bi, o, j: (bi, fwd_blk(j), 1 + o)),
                  halo(lambda jc: jnp.maximum(jc * halo_per_chunk - 1, 0)),
                  halo(lambda jc: jnp.minimum((jc + 1) * halo_per_chunk, seq // V7X_SUBLANES - 1)),
                  wpart(value), bpart(value), wpart(gate), bpart(gate),
                  pl.BlockSpec((1, 1, c), lambda bi, o, j: (o, 0, 0)),
                  pl.BlockSpec((1, c), lambda bi, o, j: (0, 0)),
                  _resident(fmat.shape),
                  pl.BlockSpec((1, FREQ_BLOCK, c), lambda bi, o, j: (o, fwd_blk(j), 0))],
        out_specs=pl.BlockSpec((1, seq, c), lambda bi, o, j: (bi, 0, 0), pipeline_mode=pl.Buffered(1)),
        out_shape=jax.ShapeDtypeStruct((b, seq, c), BF16),
        scratch_shapes=[pltpu.VMEM((seq, c), F32), pltpu.VMEM((seq, c), BF16), pltpu.VMEM((c, seq), F32),
                        pltpu.VMEM((seq, c), F32), pltpu.VMEM((2, c, FREQ_BLOCK), BF16)],
        compiler_params=pltpu.CompilerParams(dimension_semantics=("parallel", "arbitrary", "arbitrary"),
                                             vmem_limit_bytes=V7X_VMEM_LIMIT_BYTES),
        name="hyena",
    )(hy, hy, hy, hy, conv_w, conv_b, conv_w, conv_b, skip, gain, fmat, hspec)


def _attn_kernel(seq, q_ref, k_ref, v_ref, bias_ref, bias16_ref, o_ref,
                 qs_ref, ks_ref, vs_ref, res_ref, nat_ref, s_ref, e_ref):
    lanes = V7X_LANES
    qscale = LOG2_E / math.sqrt(HEAD_DIM)

    for p, d in enumerate(DILATIONS):
        ls = seq // d
        first = lax.broadcasted_iota(jnp.int32, (ls, lanes), 1) < HEAD_DIM
        for r in range(d):
            src = pl.ds(r, ls, stride=d) if d > 1 else pl.ds(0, ls)
            rows = slice(r * ls, (r + 1) * ls)
            qq = q_ref[0, src, :] * qscale
            qs_ref[p, 0, rows, :] = jnp.where(first, qq, 0.0).astype(BF16)
            qs_ref[p, 1, rows, :] = jnp.where(first, 0.0, qq).astype(BF16)
            ks_ref[p, rows, :] = k_ref[0, src, :].astype(BF16)
            vv = v_ref[0, src, :]
            vs_ref[p, 0, rows, :] = jnp.where(first, vv, 1.0).astype(BF16)
            vs_ref[p, 1, rows, :] = jnp.where(first, 1.0, vv).astype(BF16)

    first = lax.broadcasted_iota(jnp.int32, (Q_BLOCK, lanes), 1) < HEAD_DIM

    def run_pattern(p, nkeys, placement):
        def body(g, carry):
            blocks = [placement(g * ATTN_GROUP + i) for i in range(ATTN_GROUP)]
            for i, (row0, krow0, bias_of_head) in enumerate(blocks):
                kw = ks_ref[p, pl.ds(krow0, nkeys), :]
                for h in range(2):
                    qh = qs_ref[p, h, pl.ds(row0, Q_BLOCK), :]
                    s = lax.dot_general(qh, kw, (((1,), (1,)), ((), ())), preferred_element_type=F32)
                    s_ref[2 * i + h, :, :nkeys] = s + bias_of_head(h)
            for i, (row0, krow0, bias_of_head) in enumerate(blocks):
                ms = []
                for h in range(2):
                    s = s_ref[2 * i + h, :, :nkeys]
                    m = jnp.max(s, axis=-1, keepdims=True)
                    e_ref[2 * i + h, :, :nkeys] = jnp.exp2(s - m).astype(BF16)
                    ms.append(m)
                res_ref[p, 1, pl.ds(row0, Q_BLOCK), :] = jnp.where(first, ms[0], ms[1])
            for i, (row0, krow0, bias_of_head) in enumerate(blocks):
                o0 = _dot(e_ref[2 * i, :, :nkeys], vs_ref[p, 0, pl.ds(krow0, nkeys), :])
                o1 = _dot(e_ref[2 * i + 1, :, :nkeys], vs_ref[p, 1, pl.ds(krow0, nkeys), :])
                res_ref[p, 0, pl.ds(row0, Q_BLOCK), :] = jnp.where(first, o0, o1)
                res_ref[p, 2, pl.ds(row0, Q_BLOCK), :] = jnp.where(first, o1, o0)
            return carry

        lax.fori_loop(0, seq // Q_BLOCK // ATTN_GROUP, body, 0)

    for p, d in enumerate(DILATIONS[:2]):
        ls = seq // d
        nblk = ls // Q_BLOCK

        def banded(n, p=p, ls=ls, nblk=nblk):
            r = n // nblk
            ib = n % nblk
            i0 = ib * Q_BLOCK
            k0 = jnp.clip(i0 - HALF_WINDOW, 0, ls - K_WINDOW)
            case = jnp.where(ib == 0, 0, jnp.where(ib == nblk - 1, 2, 1))
            row0 = pl.multiple_of(r * ls + i0, Q_BLOCK)
            krow0 = pl.multiple_of(r * ls + k0, HALF_WINDOW)
            return row0, krow0, lambda h: bias_ref[0, (p * 3 + case) * 2 + h]

        run_pattern(p, K_WINDOW, banded)

    def full(n):
        row0 = pl.multiple_of(n * Q_BLOCK, Q_BLOCK)
        return row0, row0, lambda h: bias16_ref[0, h]

    run_pattern(2, Q_BLOCK, full)

    for p, d in enumerate(DILATIONS[1:], start=1):
        ls = seq // d
        for r in range(d):
            for kind in range(3):
                nat_ref[p - 1, kind, pl.ds(r, ls, stride=d), :] = res_ref[p, kind, r * ls:(r + 1) * ls, :]

    for t0 in range(0, seq, CONV_ROWS):
        rows = slice(t0, t0 + CONV_ROWS)
        parts = [tuple(res_ref[0, kind, rows, :] for kind in range(3))]
        parts += [tuple(nat_ref[p, kind, rows, :] for kind in range(3)) for p in range(2)]
        m = functools.reduce(jnp.maximum, [pt[1] for pt in parts])
        num = 0.0
        den = 0.0
        for out, mp, lp in parts:
            w = jnp.exp2(mp - m)
            num = num + w * out
            den = den + w * pltpu.roll(lp, HEAD_DIM, axis=1)
        o_ref[0, rows, :] = num / den


def _attn_bias_tables():
    slopes = np.array([2.0 ** (-8.0 * (i + 1) / N_HEADS) for i in range(N_HEADS)], np.float32)
    slopes = jnp.asarray(slopes.reshape(N_HEADS // 2, 1, 2, 1, 1))
    qi = lax.broadcasted_iota(jnp.int32, (Q_BLOCK, K_WINDOW), 0)
    kj = lax.broadcasted_iota(jnp.int32, (Q_BLOCK, K_WINDOW), 1)
    offsets = (0, -HALF_WINDOW, -2 * HALF_WINDOW)
    dist = jnp.stack([jnp.abs(kj - qi + off) for off in offsets])
    valid = dist <= HALF_WINDOW
    dil = jnp.asarray(np.array(DILATIONS[:2], np.float32).reshape(2, 1, 1, 1))
    scaled = (dil * dist.astype(F32)[None])[None, :, :, None]
    banded = jnp.where(valid[None, None, :, None], -slopes[:, None] * scaled, NEG_INF)
    banded = banded.reshape(N_HEADS // 2, 12, Q_BLOCK, K_WINDOW)
    d16 = dist[0, :, :Q_BLOCK]
    full = jnp.where(d16 <= HALF_WINDOW, -slopes[:, 0] * (DILATIONS[2] * d16.astype(F32)), NEG_INF)
    to_base2 = lambda t: jnp.where(t > 0.5 * NEG_INF, t * LOG2_E, NEG_INF).astype(F32)
    return to_base2(banded), to_base2(full)


def _dil_attn(q, k, v):
    b, seq, _ = q.shape
    nhp = N_HEADS // 2
    bias, bias16 = _attn_bias_tables()
    head_pair = pl.BlockSpec((1, seq, V7X_LANES), lambda bi, hp: (bi, 0, hp))
    return pl.pallas_call(
        functools.partial(_attn_kernel, seq),
        grid=(b, nhp),
        in_specs=[head_pair, head_pair, head_pair,
                  pl.BlockSpec((1, 12, Q_BLOCK, K_WINDOW), lambda bi, hp: (hp, 0, 0, 0)),
                  pl.BlockSpec((1, 2, Q_BLOCK, Q_BLOCK), lambda bi, hp: (hp, 0, 0, 0))],
        out_specs=head_pair,
        out_shape=jax.ShapeDtypeStruct((b, seq, D_ATTN), F32),
        scratch_shapes=[pltpu.VMEM((3, 2, seq, V7X_LANES), BF16),
                        pltpu.VMEM((3, seq, V7X_LANES), BF16),
                        pltpu.VMEM((3, 2, seq, V7X_LANES), BF16),
                        pltpu.VMEM((3, 3, seq, V7X_LANES), F32),
                        pltpu.VMEM((2, 3, seq, V7X_LANES), F32),
                        pltpu.VMEM((2 * ATTN_GROUP, Q_BLOCK, K_WINDOW), F32),
                        pltpu.VMEM((2 * ATTN_GROUP, Q_BLOCK, K_WINDOW), BF16)],
        compiler_params=pltpu.CompilerParams(dimension_semantics=("parallel", "parallel"),
                                             vmem_limit_bytes=V7X_VMEM_LIMIT_BYTES),
        name="dil_attn",
    )(q, k, v, bias, bias16)


def _out_ffn2_kernel(x1_ref, yh_ref, ya_ref, ga_ref, wo_ref, g3_ref, wg_ref, wu_ref, wd_ref, gf_ref,
                     out_ref, act_ref):
    ya = _rms(ya_ref[...], ga_ref[...]).astype(BF16)
    mix = _dot(yh_ref[...], wo_ref[:D_HYENA, :]) + _dot(ya, wo_ref[D_HYENA:, :])
    x2 = x1_ref[...] + mix
    h = _rms(x2, g3_ref[...]).astype(BF16)
    x3 = x2 + 0.5 * _swiglu(h, wg_ref, wu_ref, wd_ref, act_ref)
    out_ref[...] = _rms(x3, gf_ref[...])


def _out_ffn2(x1, yh, ya, ga, wo, g3, wg, wu, wd, gf):
    n = x1.shape[0]
    row = lambda w: pl.BlockSpec((OUT_ROW_TILE, w), lambda i: (i, 0))
    return pl.pallas_call(
        _out_ffn2_kernel,
        grid=(n // OUT_ROW_TILE,),
        in_specs=[row(D_MODEL), row(D_HYENA), row(D_ATTN), _resident((1, D_ATTN)), _resident(wo.shape),
                  _resident((1, D_MODEL)), _resident(wg.shape), _resident(wu.shape), _resident(wd.shape),
                  _resident((1, D_MODEL))],
        out_specs=row(D_MODEL),
        out_shape=jax.ShapeDtypeStruct((n, D_MODEL), F32),
        scratch_shapes=[pltpu.VMEM((OUT_ROW_TILE, D_FF), BF16)],
        compiler_params=pltpu.CompilerParams(dimension_semantics=("parallel",),
                                             vmem_limit_bytes=V7X_VMEM_LIMIT_BYTES),
        name="out_ffn2",
    )(x1, yh, ya, ga, wo, g3, wg, wu, wd, gf)


def _dft_matrices(seq):
    n = 2 * seq
    hb = FREQ_BLOCK // 2
    s = np.arange(seq, dtype=np.int64)[None, :]
    turn = lambda steps: 2.0 * np.pi * (steps % (2 * n)) / (2 * n)
    theta = turn((2 * np.arange(DFT_BASE_ROWS, dtype=np.int64)[:, None] + 1) * s)
    phase = turn(2 * DFT_BASE_ROWS * np.arange(seq // DFT_BASE_ROWS, dtype=np.int64)[:, None] * s)
    tab = lambda x: jnp.asarray(x.astype(np.float32))
    c0, s0 = np.cos(theta), np.sin(theta)
    lead = tab(np.stack([c0, -s0]))[None, :, None]
    lag = tab(np.stack([s0, c0]))[None, :, None]
    group = lambda x: tab(x).reshape(seq // hb, 1, hb // DFT_BASE_ROWS, 1, seq)
    return (lead * group(np.cos(phase)) - lag * group(np.sin(phase))).reshape(n, seq).astype(BF16)


def _filter_features(seq):
    t = jnp.linspace(0.0, 1.0, seq, dtype=F32)[:, None]
    w = 2.0 * math.pi * jnp.arange(seq, dtype=F32)[:, None] / seq
    f = jnp.linspace(1e-4, FILTER_BANDS - 1, FILTER_BANDS, dtype=F32)[None, :]
    z = jnp.concatenate([t, jnp.cos(f * w), -jnp.sin(f * w)], axis=-1)
    return jnp.pad(z, ((0, 0), (0, FEAT_PAD - FILTER_EMB)))


def _decay_rates():
    max_decay = math.log(DECAY_TARGET) / FAST_DECAY_PCT
    min_decay = math.log(DECAY_TARGET) / SLOW_DECAY_PCT
    return jnp.linspace(min_decay, max_decay, D_HYENA, dtype=F32)[None, :]


def kernel(x, ffn1_norm_g, ffn1_w_gate, ffn1_w_up, ffn1_w_down, mix_norm_g, w_in, hy_conv_w, hy_conv_b, hy_filt_w1, hy_filt_b1, hy_filt_w2, hy_filt_b2, hy_filt_w3, hy_filt_b3, hy_filt_w_out, hy_filt_freq, hy_filt_skip, hy_out_norm_g, attn_out_norm_g, w_out, ffn2_norm_g, ffn2_w_gate, ffn2_w_up, ffn2_w_down, final_norm_g):
    b, seq, d = x.shape
    assert d == D_MODEL and (b * seq) % OUT_ROW_TILE == 0 and seq % (DILATIONS[-1] * Q_BLOCK) == 0
    row = lambda a: a.reshape(1, -1).astype(F32)
    f32 = lambda a: a.astype(F32)

    fmat = _dft_matrices(seq)
    w1 = jnp.pad(f32(hy_filt_w1), ((0, FEAT_PAD - FILTER_EMB), (0, 0)))
    hspec, wg1, wu1, wd1, win = _hy_filter(
        seq, _filter_features(seq), w1, row(hy_filt_b1), f32(hy_filt_w2), row(hy_filt_b2), f32(hy_filt_w3),
        row(hy_filt_b3), f32(hy_filt_w_out), row(hy_filt_freq), _decay_rates(), fmat,
        [f32(ffn1_w_gate), f32(ffn1_w_up), f32(ffn1_w_down), f32(w_in)])

    x1, hy, q, k, v, wo, wg2, wu2, wd2 = _ffn1_proj(
        x.reshape(b * seq, d), row(ffn1_norm_g), wg1, wu1, wd1, row(mix_norm_g), win,
        [f32(w_out), f32(ffn2_w_gate), f32(ffn2_w_up), f32(ffn2_w_down)])

    conv_w = hy_conv_w.astype(F32).reshape(3, 3, D_HYENA).transpose(1, 0, 2)
    conv_b = hy_conv_b.astype(F32).reshape(3, 1, D_HYENA)
    skip = hy_filt_skip.astype(F32).reshape(2, 1, D_HYENA)
    y_hy = _hyena(hy.reshape(b, seq, 3 * D_HYENA), conv_w, conv_b, skip, row(hy_out_norm_g),
                  fmat, hspec)

    shape3 = lambda a: a.reshape(b, seq, D_ATTN)
    y_at = _dil_attn(shape3(q), shape3(k), shape3(v))

    out = _out_ffn2(x1, y_hy.reshape(b * seq, D_HYENA), y_at.reshape(b * seq, D_ATTN),
                    row(attn_out_norm_g), wo, row(ffn2_norm_g), wg2, wu2, wd2, row(final_norm_g))
    return out.reshape(b, seq, d)
```

```python
import functools
import math

import numpy as np
import jax
import jax.numpy as jnp
from jax import lax
from jax.experimental import pallas as pl
from jax.experimental.pallas import tpu as pltpu

F32 = jnp.float32
BF16 = jnp.bfloat16

D_MODEL = 1024
D_HYENA = 512
D_ATTN = 512
HEAD_DIM = 64
N_HEADS = D_ATTN // HEAD_DIM
D_FF = 2816
FILTER_EMB = 33
FILTER_BANDS = 16
FILTER_WIDTH = 64
DECAY_TARGET = 1e-2
FAST_DECAY_PCT = 0.3
SLOW_DECAY_PCT = 1.5
DILATIONS = (1, 4, 16)
HALF_WINDOW = 64
RMS_EPS = 1e-6
NEG_INF = -1e30
LOG2_E = math.log2(math.e)

V7X_LANES = 128
V7X_SUBLANES = 8
V7X_BF16_SUBLANES = 16
V7X_VMEM_LIMIT_BYTES = 56 * 1024 * 1024

ROW_TILE = 512
OUT_ROW_TILE = 1024
FF_CHUNK = 256
FREQ_BLOCK = 1024
FILTER_ROWS = 512
DFT_BASE_ROWS = 256
Q_BLOCK = 128
K_WINDOW = 256
ATTN_GROUP = 16
CONV_ROWS = 256
GATE_SUB_ROWS = 64
FEAT_PAD = 128


def _dot(a, b):
    return jnp.dot(a, b, preferred_element_type=F32)


def _rms(x, g):
    return x * lax.rsqrt(jnp.mean(x * x, axis=-1, keepdims=True) + RMS_EPS) * g


def _swiglu(h, wg_ref, wu_ref, wd_ref, act_ref):
    for c in range(D_FF // FF_CHUNK):
        cols = slice(c * FF_CHUNK, (c + 1) * FF_CHUNK)
        g = _dot(h, wg_ref[:, cols])
        u = _dot(h, wu_ref[:, cols])
        act_ref[:, cols] = (g * jax.nn.sigmoid(g) * u).astype(BF16)
    return _dot(act_ref[...], wd_ref[...])


def _resident(shape):
    return pl.BlockSpec(shape, lambda *_: (0,) * len(shape), pipeline_mode=pl.Buffered(1))


def _cast_rider(weights, steps):
    in_specs, out_specs, out_shapes = [], [], []
    for w in weights:
        rows, cols = w.shape
        visits = 1
        while (rows * visits) % steps or (rows * visits // steps) % V7X_BF16_SUBLANES:
            visits *= 2
        slab = pl.BlockSpec((rows * visits // steps, cols), lambda i, visits=visits: (i // visits, 0))
        in_specs.append(slab)
        out_specs.append(slab)
        out_shapes.append(jax.ShapeDtypeStruct(w.shape, BF16))
    return in_specs, out_specs, out_shapes


def _cast_slabs(src_refs, dst_refs):
    for src, dst in zip(src_refs, dst_refs):
        dst[...] = src[...].astype(BF16)


def _ffn1_proj_kernel(x_ref, g1_ref, wg_ref, wu_ref, wd_ref, g2_ref, win_ref, *rest):
    n_cast = (len(rest) - 6) // 2
    x1_ref, hy_ref, q_ref, k_ref, v_ref = rest[n_cast:n_cast + 5]
    act_ref = rest[-1]
    _cast_slabs(rest[:n_cast], rest[n_cast + 5:-1])
    x = x_ref[...]
    h = _rms(x, g1_ref[...]).astype(BF16)
    x1 = x + 0.5 * _swiglu(h, wg_ref, wu_ref, wd_ref, act_ref)
    x1_ref[...] = x1
    h2 = _rms(x1, g2_ref[...]).astype(BF16)
    nh = 3 * D_HYENA
    hy_ref[...] = _dot(h2, win_ref[:, :nh])
    q_ref[...] = _dot(h2, win_ref[:, nh:nh + D_ATTN])
    k_ref[...] = _dot(h2, win_ref[:, nh + D_ATTN:nh + 2 * D_ATTN])
    v_ref[...] = _dot(h2, win_ref[:, nh + 2 * D_ATTN:])


def _ffn1_proj(x2d, g1, wg, wu, wd, g2, win, later_weights):
    n = x2d.shape[0]
    steps = n // ROW_TILE
    row = lambda w: pl.BlockSpec((ROW_TILE, w), lambda i: (i, 0))
    cast_in, cast_out, cast_shapes = _cast_rider(later_weights, steps)
    return pl.pallas_call(
        _ffn1_proj_kernel,
        grid=(steps,),
        in_specs=[row(D_MODEL), _resident((1, D_MODEL)), _resident(wg.shape), _resident(wu.shape),
                  _resident(wd.shape), _resident((1, D_MODEL)), _resident(win.shape)] + cast_in,
        out_specs=[row(D_MODEL), row(3 * D_HYENA), row(D_ATTN), row(D_ATTN), row(D_ATTN)] + cast_out,
        out_shape=[jax.ShapeDtypeStruct((n, D_MODEL), F32),
                   jax.ShapeDtypeStruct((n, 3 * D_HYENA), F32),
                   jax.ShapeDtypeStruct((n, D_ATTN), F32),
                   jax.ShapeDtypeStruct((n, D_ATTN), F32),
                   jax.ShapeDtypeStruct((n, D_ATTN), F32)] + cast_shapes,
        scratch_shapes=[pltpu.VMEM((ROW_TILE, D_FF), BF16)],
        compiler_params=pltpu.CompilerParams(dimension_semantics=("arbitrary",),
                                             vmem_limit_bytes=V7X_VMEM_LIMIT_BYTES),
        name="ffn1_proj",
    )(x2d, g1, wg, wu, wd, g2, win, *later_weights)


def _split_bf16(a):
    hi = a.astype(BF16)
    return hi, (a - hi.astype(F32)).astype(BF16)


def _dot_split(a, b):
    a_hi, a_lo = a
    b_hi, b_lo = b
    return _dot(a_hi, b_hi) + (_dot(a_hi, b_lo) + _dot(a_lo, b_hi))


def _filter_kernel(seq, feat_ref, w1_ref, b1_ref, w2_ref, b2_ref, w3_ref, b3_ref, wtop_ref, wbot_ref,
                   freq_ref, delta_ref, f_ref, *rest):
    n_cast = (len(rest) - 2) // 2
    h_ref, hcat_ref = rest[n_cast], rest[-1]
    _cast_slabs(rest[:n_cast], rest[n_cast + 1:-1])
    c_ = D_HYENA
    half = seq // 2

    @pl.when(pl.program_id(0) == 0)
    def _():
        hi = lax.Precision.HIGHEST
        dot_hi = lambda a, b: jnp.dot(a, b, precision=hi, preferred_element_type=F32)
        freq = freq_ref[...]
        h = jnp.sin(freq * (dot_hi(feat_ref[...], w1_ref[...]) + b1_ref[...]))
        h = jnp.sin(freq * (dot_hi(h, w2_ref[...]) + b2_ref[...]))
        h = _split_bf16(jnp.sin(freq * (dot_hi(h, w3_ref[...]) + b3_ref[...])))
        row = lax.broadcasted_iota(jnp.int32, (half, c_), 0)
        for part, w_ref in enumerate((wtop_ref, wbot_ref)):
            t = (row[:, :1] + part * half).astype(F32) * (1.0 / (seq - 1))
            decay = jnp.exp(-t * jnp.abs(delta_ref[...]))
            rows = slice(part * half, (part + 1) * half)
            for o in range(2):
                fwd = _dot_split(h, _split_bf16(w_ref[:, (2 * o) * c_:(2 * o + 1) * c_])) * decay
                bwd = _dot_split(h, _split_bf16(w_ref[:, (2 * o + 1) * c_:(2 * o + 2) * c_])) * decay
                if part == 0:
                    bwd = jnp.where(row == 0, 0.0, bwd)
                hcat_ref[0, rows, o * c_:(o + 1) * c_] = (fwd + bwd).astype(BF16)
                hcat_ref[1, rows, o * c_:(o + 1) * c_] = (fwd - bwd).astype(BF16)

    is_sin = (pl.program_id(0) * FILTER_ROWS // (FREQ_BLOCK // 2)) % 2
    p = _dot(f_ref[...], hcat_ref[is_sin]) * (1.0 / seq)
    for o in range(2):
        h_ref[o] = p[:, o * c_:(o + 1) * c_]


def _hy_filter(seq, feat, w1, b1, w2, b2, w3, b3, wout, freq, delta, fmat, later_weights):
    full = lambda a: _resident(a.shape)
    assert (FREQ_BLOCK // 2) % FILTER_ROWS == 0
    steps = 2 * seq // FILTER_ROWS
    cast_in, cast_out, cast_shapes = _cast_rider(later_weights, steps)
    twice = lambda r: jnp.concatenate([r, r], axis=1)
    diag2 = lambda w: jnp.concatenate([jnp.concatenate([w, jnp.zeros_like(w)], axis=1),
                                       jnp.concatenate([jnp.zeros_like(w), w], axis=1)], axis=0)
    feat = jnp.concatenate([feat[:seq // 2], feat[seq // 2:]], axis=1)
    w1, w2, w3 = diag2(w1), diag2(w2), diag2(w3)
    b1, b2, b3, freq = twice(b1), twice(b2), twice(b3), twice(freq)
    wtop = jnp.concatenate([wout, jnp.zeros_like(wout)], axis=0)
    wbot = jnp.concatenate([jnp.zeros_like(wout), wout], axis=0)
    return pl.pallas_call(
        functools.partial(_filter_kernel, seq),
        grid=(steps,),
        in_specs=[full(feat), full(w1), full(b1), full(w2), full(b2), full(w3), full(b3), full(wtop),
                  full(wbot), full(freq), full(delta),
                  pl.BlockSpec((FILTER_ROWS, seq), lambda j: (j, 0))] + cast_in,
        out_specs=[pl.BlockSpec((2, FILTER_ROWS, D_HYENA), lambda j: (0, j, 0))] + cast_out,
        out_shape=[jax.ShapeDtypeStruct((2, 2 * seq, D_HYENA), F32)] + cast_shapes,
        scratch_shapes=[pltpu.VMEM((2, seq, 2 * D_HYENA), BF16)],
        compiler_params=pltpu.CompilerParams(dimension_semantics=("arbitrary",),
                                             vmem_limit_bytes=V7X_VMEM_LIMIT_BYTES),
        name="hy_filter",
    )(feat, w1, b1, w2, b2, w3, b3, wtop, wbot, freq, delta, fmat, *later_weights)


def _short_conv(cur, before, after, w_ref, b_ref):
    rows = cur.shape[0]
    row = lax.broadcasted_iota(jnp.int32, cur.shape, 0)
    prev = jnp.where(row == 0, before, pltpu.roll(cur, 1, axis=0))
    nxt = jnp.where(row == rows - 1, after, pltpu.roll(cur, rows - 1, axis=0))
    return b_ref[0] + prev * w_ref[0, 0:1, :] + cur * w_ref[0, 1:2, :] + nxt * w_ref[0, 2:3, :]


def _short_conv_rows(u_ref, w_ref, b_ref, t0, seq):
    zero = jnp.zeros((1, u_ref.shape[2]), F32)
    before = u_ref[0, t0 - V7X_SUBLANES:t0, :][V7X_SUBLANES - 1:] if t0 > 0 else zero
    end = t0 + CONV_ROWS
    after = u_ref[0, end:end + V7X_SUBLANES, :][:1] if end < seq else zero
    return _short_conv(u_ref[0, t0:end, :], before, after, w_ref, b_ref)


def _hyena_kernel(seq, v_ref, gate_ref, gate_lo_ref, gate_hi_ref, wv_ref, bv_ref, wg_ref, bg_ref, skip_ref,
                  gain_ref, f_ref, h_ref, out_ref, zf_ref, zb_ref, acc_ref, g_ref, y_ref):
    o = pl.program_id(1)
    j = pl.program_id(2)
    nfb = 2 * seq // FREQ_BLOCK
    gate_rows = seq // nfb
    chunks = range(0, seq, CONV_ROWS)
    hb = FREQ_BLOCK // 2

    def f_block(jb):
        return f_ref[pl.ds(pl.multiple_of(jb * FREQ_BLOCK, FREQ_BLOCK), FREQ_BLOCK), :]

    def forward(jb, slot):
        zfreq = _dot(f_block(jb), zb_ref[...])
        zr, zi = zfreq[:hb], zfreq[hb:]
        hr, hi = h_ref[0, :hb, :], h_ref[0, hb:, :]
        y_ref[slot, :, :hb] = (zr * hr - zi * hi).T.astype(BF16)
        y_ref[slot, :, hb:] = (zr * hi + zi * hr).T.astype(BF16)

    def inverse(jb, slot):
        acc_ref[...] += _dot(y_ref[slot], f_block(jb))

    def gate_chunk():
        t0 = pl.multiple_of(j * gate_rows, gate_rows)
        for r0 in range(0, gate_rows, GATE_SUB_ROWS):
            r1 = r0 + GATE_SUB_ROWS
            if r0 == 0:
                before = jnp.where(j > 0, gate_lo_ref[0, V7X_SUBLANES - 1:, :], 0.0)
            else:
                before = gate_ref[0, r0 - V7X_SUBLANES:r0, :][V7X_SUBLANES - 1:]
            if r1 == gate_rows:
                after = jnp.where(j < nfb - 1, gate_hi_ref[0, :1, :], 0.0)
            else:
                after = gate_ref[0, r1:r1 + V7X_SUBLANES, :][:1]
            g_ref[pl.ds(t0 + r0, GATE_SUB_ROWS), :] = _short_conv(gate_ref[0, r0:r1, :], before, after,
                                                                   wg_ref, bg_ref)

    @pl.when((o == 0) & (j == 0))
    def _():
        for t0 in chunks:
            z = _short_conv_rows(v_ref, wv_ref, bv_ref, t0, seq)
            zf_ref[t0:t0 + CONV_ROWS, :] = z
            zb_ref[t0:t0 + CONV_ROWS, :] = z.astype(BF16)

    @pl.when(j == 0)
    def _():
        gate_chunk()
        acc_ref[...] = jnp.zeros_like(acc_ref)
        forward(0, 0)

    for parity in range(2):
        @pl.when((j > 0) & (j < nfb) & (j % 2 == parity))
        def _(parity=parity):
            gate_chunk()
            forward(j, parity)
            inverse(j - 1, 1 - parity)

    def gated(rows):
        return g_ref[rows, :] * (acc_ref[:, rows].T + zf_ref[rows, :] * skip_ref[0])

    last_slot = (nfb - 1) % 2

    @pl.when((j == nfb) & (o == 0))
    def _():
        inverse(nfb - 1, last_slot)
        for t0 in chunks:
            rows = slice(t0, t0 + CONV_ROWS)
            z = gated(rows)
            zf_ref[rows, :] = z
            zb_ref[rows, :] = z.astype(BF16)

    @pl.when((j == nfb) & (o == 1))
    def _():
        inverse(nfb - 1, last_slot)
        for t0 in chunks:
            rows = slice(t0, t0 + CONV_ROWS)
            out_ref[0, rows, :] = _rms(gated(rows), gain_ref[...]).astype(BF16)


def _hyena(hy, conv_w, conv_b, skip, gain, fmat, hspec):
    b, seq, _ = hy.shape
    c = D_HYENA
    nfb = 2 * seq // FREQ_BLOCK
    gate_rows = seq // nfb
    halo_per_chunk = gate_rows // V7X_SUBLANES
    fwd_blk = lambda j: jnp.minimum(j, nfb - 1)
    wpart = lambda sel: pl.BlockSpec((1, 3, c), lambda bi, o, j: (sel(o), 0, 0))
    bpart = lambda sel: pl.BlockSpec((1, 1, c), lambda bi, o, j: (sel(o), 0, 0))
    value = lambda o: 0
    gate = lambda o: 1 + o
    halo = lambda first: pl.BlockSpec((1, V7X_SUBLANES, c), lambda bi, o, j: (bi, first(fwd_blk(j)), 1 + o))
    return pl.pallas_call(
        functools.partial(_hyena_kernel, seq),
        grid=(b, 2, nfb + 1),
        in_specs=[pl.BlockSpec((1, seq, c), lambda bi, o, j: (bi, 0, 0)),
                  pl.BlockSpec((1, gate_rows, c), lambda bi, o, j: (bi, fwd_blk(j), 1 + o)),
                  halo(lambda jc: jnp.maximum(jc * halo_per_chunk - 1, 0)),
                  halo(lambda jc: jnp.minimum((jc + 1) * halo_per_chunk, seq // V7X_SUBLANES - 1)),
                  wpart(value), bpart(value), wpart(gate), bpart(gate),
                  pl.BlockSpec((1, 1, c), lambda bi, o, j: (o, 0, 0)),
                  pl.BlockSpec((1, c), lambda bi, o, j: (0, 0)),
                  _resident(fmat.shape),
                  pl.BlockSpec((1, FREQ_BLOCK, c), lambda bi, o, j: (o, fwd_blk(j), 0))],
        out_specs=pl.BlockSpec((1, seq, c), lambda bi, o, j: (bi, 0, 0)),
        out_shape=jax.ShapeDtypeStruct((b, seq, c), BF16),
        scratch_shapes=[pltpu.VMEM((seq, c), F32), pltpu.VMEM((seq, c), BF16), pltpu.VMEM((c, seq), F32),
                        pltpu.VMEM((seq, c), F32), pltpu.VMEM((2, c, FREQ_BLOCK), BF16)],
        compiler_params=pltpu.CompilerParams(dimension_semantics=("parallel", "arbitrary", "arbitrary"),
                                             vmem_limit_bytes=V7X_VMEM_LIMIT_BYTES),
        name="hyena",
    )(hy, hy, hy, hy, conv_w, conv_b, conv_w, conv_b, skip, gain, fmat, hspec)


def _attn_kernel(seq, q_ref, k_ref, v_ref, bias_ref, bias16_ref, o_ref,
                 qs_ref, ks_ref, vs_ref, res_ref, nat_ref, s_ref, e_ref):
    lanes = V7X_LANES
    qscale = LOG2_E / math.sqrt(HEAD_DIM)

    for p, d in enumerate(DILATIONS):
        ls = seq // d
        first = lax.broadcasted_iota(jnp.int32, (ls, lanes), 1) < HEAD_DIM
        for r in range(d):
            src = pl.ds(r, ls, stride=d) if d > 1 else pl.ds(0, ls)
            rows = slice(r * ls, (r + 1) * ls)
            qq = q_ref[0, src, :] * qscale
            qs_ref[p, 0, rows, :] = jnp.where(first, qq, 0.0).astype(BF16)
            qs_ref[p, 1, rows, :] = jnp.where(first, 0.0, qq).astype(BF16)
            ks_ref[p, rows, :] = k_ref[0, src, :].astype(BF16)
            vv = v_ref[0, src, :]
            vs_ref[p, 0, rows, :] = jnp.where(first, vv, 1.0).astype(BF16)
            vs_ref[p, 1, rows, :] = jnp.where(first, 1.0, vv).astype(BF16)

    first = lax.broadcasted_iota(jnp.int32, (Q_BLOCK, lanes), 1) < HEAD_DIM

    def run_pattern(p, nkeys, placement):
        def body(g, carry):
            blocks = [placement(g * ATTN_GROUP + i) for i in range(ATTN_GROUP)]
            for i, (row0, krow0, bias_of_head) in enumerate(blocks):
                kw = ks_ref[p, pl.ds(krow0, nkeys), :]
                for h in range(2):
                    qh = qs_ref[p, h, pl.ds(row0, Q_BLOCK), :]
                    s = lax.dot_general(qh, kw, (((1,), (1,)), ((), ())), preferred_element_type=F32)
                    s_ref[2 * i + h, :, :nkeys] = s + bias_of_head(h)
            for i, (row0, krow0, bias_of_head) in enumerate(blocks):
                ms = []
                for h in range(2):
                    s = s_ref[2 * i + h, :, :nkeys]
                    m = jnp.max(s, axis=-1, keepdims=True)
                    e_ref[2 * i + h, :, :nkeys] = jnp.exp2(s - m).astype(BF16)
                    ms.append(m)
                res_ref[p, 1, pl.ds(row0, Q_BLOCK), :] = jnp.where(first, ms[0], ms[1])
            for i, (row0, krow0, bias_of_head) in enumerate(blocks):
                o0 = _dot(e_ref[2 * i, :, :nkeys], vs_ref[p, 0, pl.ds(krow0, nkeys), :])
                o1 = _dot(e_ref[2 * i + 1, :, :nkeys], vs_ref[p, 1, pl.ds(krow0, nkeys), :])
                res_ref[p, 0, pl.ds(row0, Q_BLOCK), :] = jnp.where(first, o0, o1)
                res_ref[p, 2, pl.ds(row0, Q_BLOCK), :] = jnp.where(first, o1, o0)
            return carry

        lax.fori_loop(0, seq // Q_BLOCK // ATTN_GROUP, body, 0)

    for p, d in enumerate(DILATIONS[:2]):
        ls = seq // d
        nblk = ls // Q_BLOCK

        def banded(n, p=p, ls=ls, nblk=nblk):
            r = n // nblk
            ib = n % nblk
            i0 = ib * Q_BLOCK
            k0 = jnp.clip(i0 - HALF_WINDOW, 0, ls - K_WINDOW)
            case = jnp.where(ib == 0, 0, jnp.where(ib == nblk - 1, 2, 1))
            row0 = pl.multiple_of(r * ls + i0, Q_BLOCK)
            krow0 = pl.multiple_of(r * ls + k0, HALF_WINDOW)
            return row0, krow0, lambda h: bias_ref[0, (p * 3 + case) * 2 + h]

        run_pattern(p, K_WINDOW, banded)

    def full(n):
        row0 = pl.multiple_of(n * Q_BLOCK, Q_BLOCK)
        return row0, row0, lambda h: bias16_ref[0, h]

    run_pattern(2, Q_BLOCK, full)

    for p, d in enumerate(DILATIONS[1:], start=1):
        ls = seq // d
        for r in range(d):
            for kind in range(3):
                nat_ref[p - 1, kind, pl.ds(r, ls, stride=d), :] = res_ref[p, kind, r * ls:(r + 1) * ls, :]

    for t0 in range(0, seq, CONV_ROWS):
        rows = slice(t0, t0 + CONV_ROWS)
        parts = [tuple(res_ref[0, kind, rows, :] for kind in range(3))]
        parts += [tuple(nat_ref[p, kind, rows, :] for kind in range(3)) for p in range(2)]
        m = functools.reduce(jnp.maximum, [pt[1] for pt in parts])
        num = 0.0
        den = 0.0
        for out, mp, lp in parts:
            w = jnp.exp2(mp - m)
            num = num + w * out
            den = den + w * pltpu.roll(lp, HEAD_DIM, axis=1)
        o_ref[0, rows, :] = num / den


def _attn_bias_tables():
    slopes = np.array([2.0 ** (-8.0 * (i + 1) / N_HEADS) for i in range(N_HEADS)], np.float32)
    slopes = jnp.asarray(slopes.reshape(N_HEADS // 2, 1, 2, 1, 1))
    qi = lax.broadcasted_iota(jnp.int32, (Q_BLOCK, K_WINDOW), 0)
    kj = lax.broadcasted_iota(jnp.int32, (Q_BLOCK, K_WINDOW), 1)
    offsets = (0, -HALF_WINDOW, -2 * HALF_WINDOW)
    dist = jnp.stack([jnp.abs(kj - qi + off) for off in offsets])
    valid = dist <= HALF_WINDOW
    dil = jnp.asarray(np.array(DILATIONS[:2], np.float32).reshape(2, 1, 1, 1))
    scaled = (dil * dist.astype(F32)[None])[None, :, :, None]
    banded = jnp.where(valid[None, None, :, None], -slopes[:, None] * scaled, NEG_INF)
    banded = banded.reshape(N_HEADS // 2, 12, Q_BLOCK, K_WINDOW)
    d16 = dist[0, :, :Q_BLOCK]
    full = jnp.where(d16 <= HALF_WINDOW, -slopes[:, 0] * (DILATIONS[2] * d16.astype(F32)), NEG_INF)
    to_base2 = lambda t: jnp.where(t > 0.5 * NEG_INF, t * LOG2_E, NEG_INF).astype(F32)
    return to_base2(banded), to_base2(full)


def _dil_attn(q, k, v):
    b, seq, _ = q.shape
    nhp = N_HEADS // 2
    bias, bias16 = _attn_bias_tables()
    head_pair = pl.BlockSpec((1, seq, V7X_LANES), lambda bi, hp: (bi, 0, hp))
    return pl.pallas_call(
        functools.partial(_attn_kernel, seq),
        grid=(b, nhp),
        in_specs=[head_pair, head_pair, head_pair,
                  pl.BlockSpec((1, 12, Q_BLOCK, K_WINDOW), lambda bi, hp: (hp, 0, 0, 0)),
                  pl.BlockSpec((1, 2, Q_BLOCK, Q_BLOCK), lambda bi, hp: (hp, 0, 0, 0))],
        out_specs=head_pair,
        out_shape=jax.ShapeDtypeStruct((b, seq, D_ATTN), F32),
        scratch_shapes=[pltpu.VMEM((3, 2, seq, V7X_LANES), BF16),
                        pltpu.VMEM((3, seq, V7X_LANES), BF16),
                        pltpu.VMEM((3, 2, seq, V7X_LANES), BF16),
                        pltpu.VMEM((3, 3, seq, V7X_LANES), F32),
                        pltpu.VMEM((2, 3, seq, V7X_LANES), F32),
                        pltpu.VMEM((2 * ATTN_GROUP, Q_BLOCK, K_WINDOW), F32),
                        pltpu.VMEM((2 * ATTN_GROUP, Q_BLOCK, K_WINDOW), BF16)],
        compiler_params=pltpu.CompilerParams(dimension_semantics=("parallel", "parallel"),
                                             vmem_limit_bytes=V7X_VMEM_LIMIT_BYTES),
        name="dil_attn",
    )(q, k, v, bias, bias16)


def _out_ffn2_kernel(x1_ref, yh_ref, ya_ref, ga_ref, wo_ref, g3_ref, wg_ref, wu_ref, wd_ref, gf_ref,
                     out_ref, act_ref):
    ya = _rms(ya_ref[...], ga_ref[...]).astype(BF16)
    mix = _dot(yh_ref[...], wo_ref[:D_HYENA, :]) + _dot(ya, wo_ref[D_HYENA:, :])
    x2 = x1_ref[...] + mix
    h = _rms(x2, g3_ref[...]).astype(BF16)
    x3 = x2 + 0.5 * _swiglu(h, wg_ref, wu_ref, wd_ref, act_ref)
    out_ref[...] = _rms(x3, gf_ref[...])


def _out_ffn2(x1, yh, ya, ga, wo, g3, wg, wu, wd, gf):
    n = x1.shape[0]
    row = lambda w: pl.BlockSpec((OUT_ROW_TILE, w), lambda i: (i, 0))
    return pl.pallas_call(
        _out_ffn2_kernel,
        grid=(n // OUT_ROW_TILE,),
        in_specs=[row(D_MODEL), row(D_HYENA), row(D_ATTN), _resident((1, D_ATTN)), _resident(wo.shape),
                  _resident((1, D_MODEL)), _resident(wg.shape), _resident(wu.shape), _resident(wd.shape),
                  _resident((1, D_MODEL))],
        out_specs=row(D_MODEL),
        out_shape=jax.ShapeDtypeStruct((n, D_MODEL), F32),
        scratch_shapes=[pltpu.VMEM((OUT_ROW_TILE, D_FF), BF16)],
        compiler_params=pltpu.CompilerParams(dimension_semantics=("parallel",),
                                             vmem_limit_bytes=V7X_VMEM_LIMIT_BYTES),
        name="out_ffn2",
    )(x1, yh, ya, ga, wo, g3, wg, wu, wd, gf)


def _dft_matrices(seq):
    n = 2 * seq
    hb = FREQ_BLOCK // 2
    s = np.arange(seq, dtype=np.int64)[None, :]
    turn = lambda steps: 2.0 * np.pi * (steps % (2 * n)) / (2 * n)
    theta = turn((2 * np.arange(DFT_BASE_ROWS, dtype=np.int64)[:, None] + 1) * s)
    phase = turn(2 * DFT_BASE_ROWS * np.arange(seq // DFT_BASE_ROWS, dtype=np.int64)[:, None] * s)
    tab = lambda x: jnp.asarray(x.astype(np.float32))
    c0, s0 = np.cos(theta), np.sin(theta)
    lead = tab(np.stack([c0, -s0]))[None, :, None]
    lag = tab(np.stack([s0, c0]))[None, :, None]
    group = lambda x: tab(x).reshape(seq // hb, 1, hb // DFT_BASE_ROWS, 1, seq)
    return (lead * group(np.cos(phase)) - lag * group(np.sin(phase))).reshape(n, seq).astype(BF16)


def _filter_features(seq):
    t = jnp.linspace(0.0, 1.0, seq, dtype=F32)[:, None]
    w = 2.0 * math.pi * jnp.arange(seq, dtype=F32)[:, None] / seq
    f = jnp.linspace(1e-4, FILTER_BANDS - 1, FILTER_BANDS, dtype=F32)[None, :]
    z = jnp.concatenate([t, jnp.cos(f * w), -jnp.sin(f * w)], axis=-1)
    return jnp.pad(z, ((0, 0), (0, FEAT_PAD - FILTER_EMB)))


def _decay_rates():
    max_decay = math.log(DECAY_TARGET) / FAST_DECAY_PCT
    min_decay = math.log(DECAY_TARGET) / SLOW_DECAY_PCT
    return jnp.linspace(min_decay, max_decay, D_HYENA, dtype=F32)[None, :]


def kernel(x, ffn1_norm_g, ffn1_w_gate, ffn1_w_up, ffn1_w_down, mix_norm_g, w_in, hy_conv_w, hy_conv_b, hy_filt_w1, hy_filt_b1, hy_filt_w2, hy_filt_b2, hy_filt_w3, hy_filt_b3, hy_filt_w_out, hy_filt_freq, hy_filt_skip, hy_out_norm_g, attn_out_norm_g, w_out, ffn2_norm_g, ffn2_w_gate, ffn2_w_up, ffn2_w_down, final_norm_g):
    b, seq, d = x.shape
    assert d == D_MODEL and (b * seq) % OUT_ROW_TILE == 0 and seq % (DILATIONS[-1] * Q_BLOCK) == 0
    row = lambda a: a.reshape(1, -1).astype(F32)
    f32 = lambda a: a.astype(F32)

    fmat = _dft_matrices(seq)
    w1 = jnp.pad(f32(hy_filt_w1), ((0, FEAT_PAD - FILTER_EMB), (0, 0)))
    hspec, wg1, wu1, wd1, win = _hy_filter(
        seq, _filter_features(seq), w1, row(hy_filt_b1), f32(hy_filt_w2), row(hy_filt_b2), f32(hy_filt_w3),
        row(hy_filt_b3), f32(hy_filt_w_out), row(hy_filt_freq), _decay_rates(), fmat,
        [f32(ffn1_w_gate), f32(ffn1_w_up), f32(ffn1_w_down), f32(w_in)])

    x1, hy, q, k, v, wo, wg2, wu2, wd2 = _ffn1_proj(
        x.reshape(b * seq, d), row(ffn1_norm_g), wg1, wu1, wd1, row(mix_norm_g), win,
        [f32(w_out), f32(ffn2_w_gate), f32(ffn2_w_up), f32(ffn2_w_down)])

    conv_w = hy_conv_w.astype(F32).reshape(3, 3, D_HYENA).transpose(1, 0, 2)
    conv_b = hy_conv_b.astype(F32).reshape(3, 1, D_HYENA)
    skip = hy_filt_skip.astype(F32).reshape(2, 1, D_HYENA)
    y_hy = _hyena(hy.reshape(b, seq, 3 * D_HYENA), conv_w, conv_b, skip, row(hy_out_norm_g),
                  fmat, hspec)

    shape3 = lambda a: a.reshape(b, seq, D_ATTN)
    y_at = _dil_attn(shape3(q), shape3(k), shape3(v))

    out = _out_ffn2(x1, y_hy.reshape(b * seq, D_HYENA), y_at.reshape(b * seq, D_ATTN),
                    row(attn_out_norm_g), wo, row(ffn2_norm_g), wg2, wu2, wd2, row(final_norm_g))
    return out.reshape(b, seq, d)
```

```python
import functools
import math

import numpy as np
import jax
import jax.numpy as jnp
from jax import lax
from jax.experimental import pallas as pl
from jax.experimental.pallas import tpu as pltpu

F32 = jnp.float32
BF16 = jnp.bfloat16

D_MODEL = 1024
D_HYENA = 512
D_ATTN = 512
HEAD_DIM = 64
N_HEADS = D_ATTN // HEAD_DIM
D_FF = 2816
FILTER_EMB = 33
FILTER_BANDS = 16
FILTER_WIDTH = 64
DECAY_TARGET = 1e-2
FAST_DECAY_PCT = 0.3
SLOW_DECAY_PCT = 1.5
DILATIONS = (1, 4, 16)
HALF_WINDOW = 64
RMS_EPS = 1e-6
NEG_INF = -1e30
LOG2_E = math.log2(math.e)

V7X_LANES = 128
V7X_SUBLANES = 8
V7X_BF16_SUBLANES = 16
V7X_VMEM_LIMIT_BYTES = 56 * 1024 * 1024

ROW_TILE = 512
OUT_ROW_TILE = 1024
FF_CHUNK = 256
FREQ_BLOCK = 1024
FILTER_ROWS = 512
DFT_BASE_ROWS = 256
Q_BLOCK = 128
K_WINDOW = 256
ATTN_GROUP = 16
CONV_ROWS = 256
GATE_SUB_ROWS = 64
FEAT_PAD = 128


def _dot(a, b):
    return jnp.dot(a, b, preferred_element_type=F32)


def _rms(x, g):
    return x * lax.rsqrt(jnp.mean(x * x, axis=-1, keepdims=True) + RMS_EPS) * g


def _swiglu(h, wg_ref, wu_ref, wd_ref, act_ref):
    for c in range(D_FF // FF_CHUNK):
        cols = slice(c * FF_CHUNK, (c + 1) * FF_CHUNK)
        g = _dot(h, wg_ref[:, cols])
        u = _dot(h, wu_ref[:, cols])
        act_ref[:, cols] = (g * jax.nn.sigmoid(g) * u).astype(BF16)
    return _dot(act_ref[...], wd_ref[...])


def _resident(shape):
    return pl.BlockSpec(shape, lambda *_: (0,) * len(shape), pipeline_mode=pl.Buffered(1))


def _cast_rider(weights, steps):
    in_specs, out_specs, out_shapes = [], [], []
    for w in weights:
        rows, cols = w.shape
        visits = 1
        while (rows * visits) % steps or (rows * visits // steps) % V7X_BF16_SUBLANES:
            visits *= 2
        slab = pl.BlockSpec((rows * visits // steps, cols), lambda i, visits=visits: (i // visits, 0))
        in_specs.append(slab)
        out_specs.append(slab)
        out_shapes.append(jax.ShapeDtypeStruct(w.shape, BF16))
    return in_specs, out_specs, out_shapes


def _cast_slabs(src_refs, dst_refs):
    for src, dst in zip(src_refs, dst_refs):
        dst[...] = src[...].astype(BF16)


def _ffn1_proj_kernel(x_ref, g1_ref, wg_ref, wu_ref, wd_ref, g2_ref, win_ref, *rest):
    n_cast = (len(rest) - 6) // 2
    x1_ref, hy_ref, q_ref, k_ref, v_ref = rest[n_cast:n_cast + 5]
    act_ref = rest[-1]
    _cast_slabs(rest[:n_cast], rest[n_cast + 5:-1])
    x = x_ref[...]
    h = _rms(x, g1_ref[...]).astype(BF16)
    x1 = x + 0.5 * _swiglu(h, wg_ref, wu_ref, wd_ref, act_ref)
    x1_ref[...] = x1
    h2 = _rms(x1, g2_ref[...]).astype(BF16)
    nh = 3 * D_HYENA
    hy_ref[...] = _dot(h2, win_ref[:, :nh])
    q_ref[...] = _dot(h2, win_ref[:, nh:nh + D_ATTN])
    k_ref[...] = _dot(h2, win_ref[:, nh + D_ATTN:nh + 2 * D_ATTN])
    v_ref[...] = _dot(h2, win_ref[:, nh + 2 * D_ATTN:])


def _ffn1_proj(x2d, g1, wg, wu, wd, g2, win, later_weights):
    n = x2d.shape[0]
    steps = n // ROW_TILE
    row = lambda w: pl.BlockSpec((ROW_TILE, w), lambda i: (i, 0))
    cast_in, cast_out, cast_shapes = _cast_rider(later_weights, steps)
    return pl.pallas_call(
        _ffn1_proj_kernel,
        grid=(steps,),
        in_specs=[row(D_MODEL), _resident((1, D_MODEL)), _resident(wg.shape), _resident(wu.shape),
                  _resident(wd.shape), _resident((1, D_MODEL)), _resident(win.shape)] + cast_in,
        out_specs=[row(D_MODEL), row(3 * D_HYENA), row(D_ATTN), row(D_ATTN), row(D_ATTN)] + cast_out,
        out_shape=[jax.ShapeDtypeStruct((n, D_MODEL), F32),
                   jax.ShapeDtypeStruct((n, 3 * D_HYENA), F32),
                   jax.ShapeDtypeStruct((n, D_ATTN), F32),
                   jax.ShapeDtypeStruct((n, D_ATTN), F32),
                   jax.ShapeDtypeStruct((n, D_ATTN), F32)] + cast_shapes,
        scratch_shapes=[pltpu.VMEM((ROW_TILE, D_FF), BF16)],
        compiler_params=pltpu.CompilerParams(dimension_semantics=("arbitrary",),
                                             vmem_limit_bytes=V7X_VMEM_LIMIT_BYTES),
        name="ffn1_proj",
    )(x2d, g1, wg, wu, wd, g2, win, *later_weights)


def _split_bf16(a):
    hi = a.astype(BF16)
    return hi, (a - hi.astype(F32)).astype(BF16)


def _dot_split(a, b):
    a_hi, a_lo = a
    b_hi, b_lo = b
    return _dot(a_hi, b_hi) + (_dot(a_hi, b_lo) + _dot(a_lo, b_hi))


def _filter_kernel(seq, feat_ref, w1_ref, b1_ref, w2_ref, b2_ref, w3_ref, b3_ref, wtop_ref, wbot_ref,
                   freq_ref, delta_ref, lead_ref, lag_ref, cosp_ref, sinp_ref, *rest):
    n_cast = (len(rest) - 3) // 2
    h_ref, f_ref, hcat_ref = rest[n_cast], rest[n_cast + 1], rest[-1]
    _cast_slabs(rest[:n_cast], rest[n_cast + 2:-1])
    c_ = D_HYENA
    half = seq // 2
    hb = FREQ_BLOCK // 2

    row0 = pl.program_id(0) * FILTER_ROWS
    is_sin = (row0 // hb) % 2
    group0 = (row0 // FREQ_BLOCK) * (hb // DFT_BASE_ROWS) + (row0 % hb) // DFT_BASE_ROWS
    for g in range(FILTER_ROWS // DFT_BASE_ROWS):
        rows = slice(g * DFT_BASE_ROWS, (g + 1) * DFT_BASE_ROWS)
        f_ref[rows, :] = (lead_ref[is_sin] * cosp_ref[group0 + g]
                          - lag_ref[is_sin] * sinp_ref[group0 + g]).astype(BF16)

    @pl.when(pl.program_id(0) == 0)
    def _():
        hi = lax.Precision.HIGHEST
        dot_hi = lambda a, b: jnp.dot(a, b, precision=hi, preferred_element_type=F32)
        freq = freq_ref[...]
        h = jnp.sin(freq * (dot_hi(feat_ref[...], w1_ref[...]) + b1_ref[...]))
        h = jnp.sin(freq * (dot_hi(h, w2_ref[...]) + b2_ref[...]))
        h = _split_bf16(jnp.sin(freq * (dot_hi(h, w3_ref[...]) + b3_ref[...])))
        row = lax.broadcasted_iota(jnp.int32, (half, c_), 0)
        for part, w_ref in enumerate((wtop_ref, wbot_ref)):
            t = (row[:, :1] + part * half).astype(F32) * (1.0 / (seq - 1))
            decay = jnp.exp(-t * jnp.abs(delta_ref[...]))
            rows = slice(part * half, (part + 1) * half)
            for o in range(2):
                fwd = _dot_split(h, _split_bf16(w_ref[:, (2 * o) * c_:(2 * o + 1) * c_])) * decay
                bwd = _dot_split(h, _split_bf16(w_ref[:, (2 * o + 1) * c_:(2 * o + 2) * c_])) * decay
                if part == 0:
                    bwd = jnp.where(row == 0, 0.0, bwd)
                hcat_ref[0, rows, o * c_:(o + 1) * c_] = (fwd + bwd).astype(BF16)
                hcat_ref[1, rows, o * c_:(o + 1) * c_] = (fwd - bwd).astype(BF16)

    p = _dot(f_ref[...], hcat_ref[is_sin]) * (1.0 / seq)
    for o in range(2):
        h_ref[o] = p[:, o * c_:(o + 1) * c_]


def _hy_filter(seq, feat, w1, b1, w2, b2, w3, b3, wout, freq, delta, dft_tables, later_weights):
    full = lambda a: _resident(a.shape)
    assert (FREQ_BLOCK // 2) % FILTER_ROWS == 0
    steps = 2 * seq // FILTER_ROWS
    cast_in, cast_out, cast_shapes = _cast_rider(later_weights, steps)
    twice = lambda r: jnp.concatenate([r, r], axis=1)
    diag2 = lambda w: jnp.concatenate([jnp.concatenate([w, jnp.zeros_like(w)], axis=1),
                                       jnp.concatenate([jnp.zeros_like(w), w], axis=1)], axis=0)
    feat = jnp.concatenate([feat[:seq // 2], feat[seq // 2:]], axis=1)
    w1, w2, w3 = diag2(w1), diag2(w2), diag2(w3)
    b1, b2, b3, freq = twice(b1), twice(b2), twice(b3), twice(freq)
    wtop = jnp.concatenate([wout, jnp.zeros_like(wout)], axis=0)
    wbot = jnp.concatenate([jnp.zeros_like(wout), wout], axis=0)
    return pl.pallas_call(
        functools.partial(_filter_kernel, seq),
        grid=(steps,),
        in_specs=[full(feat), full(w1), full(b1), full(w2), full(b2), full(w3), full(b3), full(wtop),
                  full(wbot), full(freq), full(delta)] + [full(t) for t in dft_tables] + cast_in,
        out_specs=[pl.BlockSpec((2, FILTER_ROWS, D_HYENA), lambda j: (0, j, 0)),
                   pl.BlockSpec((FILTER_ROWS, seq), lambda j: (j, 0))] + cast_out,
        out_shape=[jax.ShapeDtypeStruct((2, 2 * seq, D_HYENA), F32),
                   jax.ShapeDtypeStruct((2 * seq, seq), BF16)] + cast_shapes,
        scratch_shapes=[pltpu.VMEM((2, seq, 2 * D_HYENA), BF16)],
        compiler_params=pltpu.CompilerParams(dimension_semantics=("arbitrary",),
                                             vmem_limit_bytes=V7X_VMEM_LIMIT_BYTES),
        name="hy_filter",
    )(feat, w1, b1, w2, b2, w3, b3, wtop, wbot, freq, delta, *dft_tables, *later_weights)


def _short_conv(cur, before, after, w_ref, b_ref):
    rows = cur.shape[0]
    row = lax.broadcasted_iota(jnp.int32, cur.shape, 0)
    prev = jnp.where(row == 0, before, pltpu.roll(cur, 1, axis=0))
    nxt = jnp.where(row == rows - 1, after, pltpu.roll(cur, rows - 1, axis=0))
    return b_ref[0] + prev * w_ref[0, 0:1, :] + cur * w_ref[0, 1:2, :] + nxt * w_ref[0, 2:3, :]


def _short_conv_rows(u_ref, w_ref, b_ref, t0, seq):
    zero = jnp.zeros((1, u_ref.shape[2]), F32)
    before = u_ref[0, t0 - V7X_SUBLANES:t0, :][V7X_SUBLANES - 1:] if t0 > 0 else zero
    end = t0 + CONV_ROWS
    after = u_ref[0, end:end + V7X_SUBLANES, :][:1] if end < seq else zero
    return _short_conv(u_ref[0, t0:end, :], before, after, w_ref, b_ref)


def _hyena_kernel(seq, v_ref, gate_ref, gate_lo_ref, gate_hi_ref, wv_ref, bv_ref, wg_ref, bg_ref, skip_ref,
                  gain_ref, f_ref, h_ref, out_ref, zf_ref, zb_ref, acc_ref, g_ref, y_ref):
    o = pl.program_id(1)
    j = pl.program_id(2)
    nfb = 2 * seq // FREQ_BLOCK
    gate_rows = seq // nfb
    chunks = range(0, seq, CONV_ROWS)
    hb = FREQ_BLOCK // 2

    def f_block(jb):
        return f_ref[pl.ds(pl.multiple_of(jb * FREQ_BLOCK, FREQ_BLOCK), FREQ_BLOCK), :]

    def forward(jb, slot):
        zfreq = _dot(f_block(jb), zb_ref[...])
        zr, zi = zfreq[:hb], zfreq[hb:]
        hr, hi = h_ref[0, :hb, :], h_ref[0, hb:, :]
        y_ref[slot, :, :hb] = (zr * hr - zi * hi).T.astype(BF16)
        y_ref[slot, :, hb:] = (zr * hi + zi * hr).T.astype(BF16)

    def inverse(jb, slot):
        acc_ref[...] += _dot(y_ref[slot], f_block(jb))

    def gate_chunk():
        t0 = pl.multiple_of(j * gate_rows, gate_rows)
        for r0 in range(0, gate_rows, GATE_SUB_ROWS):
            r1 = r0 + GATE_SUB_ROWS
            if r0 == 0:
                before = jnp.where(j > 0, gate_lo_ref[0, V7X_SUBLANES - 1:, :], 0.0)
            else:
                before = gate_ref[0, r0 - V7X_SUBLANES:r0, :][V7X_SUBLANES - 1:]
            if r1 == gate_rows:
                after = jnp.where(j < nfb - 1, gate_hi_ref[0, :1, :], 0.0)
            else:
                after = gate_ref[0, r1:r1 + V7X_SUBLANES, :][:1]
            g_ref[pl.ds(t0 + r0, GATE_SUB_ROWS), :] = _short_conv(gate_ref[0, r0:r1, :], before, after,
                                                                   wg_ref, bg_ref)

    @pl.when((o == 0) & (j == 0))
    def _():
        for t0 in chunks:
            z = _short_conv_rows(v_ref, wv_ref, bv_ref, t0, seq)
            zf_ref[t0:t0 + CONV_ROWS, :] = z
            zb_ref[t0:t0 + CONV_ROWS, :] = z.astype(BF16)

    @pl.when(j == 0)
    def _():
        gate_chunk()
        acc_ref[...] = jnp.zeros_like(acc_ref)
        forward(0, 0)

    for parity in range(2):
        @pl.when((j > 0) & (j < nfb) & (j % 2 == parity))
        def _(parity=parity):
            gate_chunk()
            forward(j, parity)
            inverse(j - 1, 1 - parity)

    def gated(rows):
        return g_ref[rows, :] * (acc_ref[:, rows].T + zf_ref[rows, :] * skip_ref[0])

    last_slot = (nfb - 1) % 2

    @pl.when((j == nfb) & (o == 0))
    def _():
        inverse(nfb - 1, last_slot)
        for t0 in chunks:
            rows = slice(t0, t0 + CONV_ROWS)
            z = gated(rows)
            zf_ref[rows, :] = z
            zb_ref[rows, :] = z.astype(BF16)

    @pl.when((j == nfb) & (o == 1))
    def _():
        inverse(nfb - 1, last_slot)
        for t0 in chunks:
            rows = slice(t0, t0 + CONV_ROWS)
            out_ref[0, rows, :] = _rms(gated(rows), gain_ref[...]).astype(BF16)


def _hyena(hy, conv_w, conv_b, skip, gain, fmat, hspec):
    b, seq, _ = hy.shape
    c = D_HYENA
    nfb = 2 * seq // FREQ_BLOCK
    gate_rows = seq // nfb
    halo_per_chunk = gate_rows // V7X_SUBLANES
    fwd_blk = lambda j: jnp.minimum(j, nfb - 1)
    wpart = lambda sel: pl.BlockSpec((1, 3, c), lambda bi, o, j: (sel(o), 0, 0))
    bpart = lambda sel: pl.BlockSpec((1, 1, c), lambda bi, o, j: (sel(o), 0, 0))
    value = lambda o: 0
    gate = lambda o: 1 + o
    halo = lambda first: pl.BlockSpec((1, V7X_SUBLANES, c), lambda bi, o, j: (bi, first(fwd_blk(j)), 1 + o))
    return pl.pallas_call(
        functools.partial(_hyena_kernel, seq),
        grid=(b, 2, nfb + 1),
        in_specs=[pl.BlockSpec((1, seq, c), lambda bi, o, j: (bi, 0, 0)),
                  pl.BlockSpec((1, gate_rows, c), lambda bi, o, j: (bi, fwd_blk(j), 1 + o)),
                  halo(lambda jc: jnp.maximum(jc * halo_per_chunk - 1, 0)),
                  halo(lambda jc: jnp.minimum((jc + 1) * halo_per_chunk, seq // V7X_SUBLANES - 1)),
                  wpart(value), bpart(value), wpart(gate), bpart(gate),
                  pl.BlockSpec((1, 1, c), lambda bi, o, j: (o, 0, 0)),
                  pl.BlockSpec((1, c), lambda bi, o, j: (0, 0)),
                  _resident(fmat.shape),
                  pl.BlockSpec((1, FREQ_BLOCK, c), lambda bi, o, j: (o, fwd_blk(j), 0))],
        out_specs=pl.BlockSpec((1, seq, c), lambda bi, o, j: (bi, 0, 0)),
        out_shape=jax.ShapeDtypeStruct((b, seq, c), BF16),
        scratch_shapes=[pltpu.VMEM((seq, c), F32), pltpu.VMEM((seq, c), BF16), pltpu.VMEM((c, seq), F32),
                        pltpu.VMEM((seq, c), F32), pltpu.VMEM((2, c, FREQ_BLOCK), BF16)],
        compiler_params=pltpu.CompilerParams(dimension_semantics=("parallel", "arbitrary", "arbitrary"),
                                             vmem_limit_bytes=V7X_VMEM_LIMIT_BYTES),
        name="hyena",
    )(hy, hy, hy, hy, conv_w, conv_b, conv_w, conv_b, skip, gain, fmat, hspec)


def _attn_kernel(seq, q_ref, k_ref, v_ref, bias_ref, bias16_ref, o_ref,
                 qs_ref, ks_ref, vs_ref, res_ref, nat_ref, s_ref, e_ref):
    lanes = V7X_LANES
    qscale = LOG2_E / math.sqrt(HEAD_DIM)

    for p, d in enumerate(DILATIONS):
        ls = seq // d
        first = lax.broadcasted_iota(jnp.int32, (ls, lanes), 1) < HEAD_DIM
        for r in range(d):
            src = pl.ds(r, ls, stride=d) if d > 1 else pl.ds(0, ls)
            rows = slice(r * ls, (r + 1) * ls)
            qq = q_ref[0, src, :] * qscale
            qs_ref[p, 0, rows, :] = jnp.where(first, qq, 0.0).astype(BF16)
            qs_ref[p, 1, rows, :] = jnp.where(first, 0.0, qq).astype(BF16)
            ks_ref[p, rows, :] = k_ref[0, src, :].astype(BF16)
            vv = v_ref[0, src, :]
            vs_ref[p, 0, rows, :] = jnp.where(first, vv, 1.0).astype(BF16)
            vs_ref[p, 1, rows, :] = jnp.where(first, 1.0, vv).astype(BF16)

    first = lax.broadcasted_iota(jnp.int32, (Q_BLOCK, lanes), 1) < HEAD_DIM

    def run_pattern(p, nkeys, placement):
        def body(g, carry):
            blocks = [placement(g * ATTN_GROUP + i) for i in range(ATTN_GROUP)]
            for i, (row0, krow0, bias_of_head) in enumerate(blocks):
                kw = ks_ref[p, pl.ds(krow0, nkeys), :]
                for h in range(2):
                    qh = qs_ref[p, h, pl.ds(row0, Q_BLOCK), :]
                    s = lax.dot_general(qh, kw, (((1,), (1,)), ((), ())), preferred_element_type=F32)
                    s_ref[2 * i + h, :, :nkeys] = s + bias_of_head(h)
            for i, (row0, krow0, bias_of_head) in enumerate(blocks):
                ms = []
                for h in range(2):
                    s = s_ref[2 * i + h, :, :nkeys]
                    m = jnp.max(s, axis=-1, keepdims=True)
                    e_ref[2 * i + h, :, :nkeys] = jnp.exp2(s - m).astype(BF16)
                    ms.append(m)
                res_ref[p, 1, pl.ds(row0, Q_BLOCK), :] = jnp.where(first, ms[0], ms[1])
            for i, (row0, krow0, bias_of_head) in enumerate(blocks):
                o0 = _dot(e_ref[2 * i, :, :nkeys], vs_ref[p, 0, pl.ds(krow0, nkeys), :])
                o1 = _dot(e_ref[2 * i + 1, :, :nkeys], vs_ref[p, 1, pl.ds(krow0, nkeys), :])
                res_ref[p, 0, pl.ds(row0, Q_BLOCK), :] = jnp.where(first, o0, o1)
                res_ref[p, 2, pl.ds(row0, Q_BLOCK), :] = jnp.where(first, o1, o0)
            return carry

        lax.fori_loop(0, seq // Q_BLOCK // ATTN_GROUP, body, 0)

    for p, d in enumerate(DILATIONS[:2]):
        ls = seq // d
        nblk = ls // Q_BLOCK

        def banded(n, p=p, ls=ls, nblk=nblk):
            r = n // nblk
            ib = n % nblk
            i0 = ib * Q_BLOCK
            k0 = jnp.clip(i0 - HALF_WINDOW, 0, ls - K_WINDOW)
            case = jnp.where(ib == 0, 0, jnp.where(ib == nblk - 1, 2, 1))
            row0 = pl.multiple_of(r * ls + i0, Q_BLOCK)
            krow0 = pl.multiple_of(r * ls + k0, HALF_WINDOW)
            return row0, krow0, lambda h: bias_ref[0, (p * 3 + case) * 2 + h]

        run_pattern(p, K_WINDOW, banded)

    def full(n):
        row0 = pl.multiple_of(n * Q_BLOCK, Q_BLOCK)
        return row0, row0, lambda h: bias16_ref[0, h]

    run_pattern(2, Q_BLOCK, full)

    for p, d in enumerate(DILATIONS[1:], start=1):
        ls = seq // d
        for r in range(d):
            for kind in range(3):
                nat_ref[p - 1, kind, pl.ds(r, ls, stride=d), :] = res_ref[p, kind, r * ls:(r + 1) * ls, :]

    for t0 in range(0, seq, CONV_ROWS):
        rows = slice(t0, t0 + CONV_ROWS)
        parts = [tuple(res_ref[0, kind, rows, :] for kind in range(3))]
        parts += [tuple(nat_ref[p, kind, rows, :] for kind in range(3)) for p in range(2)]
        m = functools.reduce(jnp.maximum, [pt[1] for pt in parts])
        num = 0.0
        den = 0.0
        for out, mp, lp in parts:
            w = jnp.exp2(mp - m)
            num = num + w * out
            den = den + w * pltpu.roll(lp, HEAD_DIM, axis=1)
        o_ref[0, rows, :] = num / den


def _attn_bias_tables():
    slopes = np.array([2.0 ** (-8.0 * (i + 1) / N_HEADS) for i in range(N_HEADS)], np.float32)
    slopes = jnp.asarray(slopes.reshape(N_HEADS // 2, 1, 2, 1, 1))
    qi = lax.broadcasted_iota(jnp.int32, (Q_BLOCK, K_WINDOW), 0)
    kj = lax.broadcasted_iota(jnp.int32, (Q_BLOCK, K_WINDOW), 1)
    offsets = (0, -HALF_WINDOW, -2 * HALF_WINDOW)
    dist = jnp.stack([jnp.abs(kj - qi + off) for off in offsets])
    valid = dist <= HALF_WINDOW
    dil = jnp.asarray(np.array(DILATIONS[:2], np.float32).reshape(2, 1, 1, 1))
    scaled = (dil * dist.astype(F32)[None])[None, :, :, None]
    banded = jnp.where(valid[None, None, :, None], -slopes[:, None] * scaled, NEG_INF)
    banded = banded.reshape(N_HEADS // 2, 12, Q_BLOCK, K_WINDOW)
    d16 = dist[0, :, :Q_BLOCK]
    full = jnp.where(d16 <= HALF_WINDOW, -slopes[:, 0] * (DILATIONS[2] * d16.astype(F32)), NEG_INF)
    to_base2 = lambda t: jnp.where(t > 0.5 * NEG_INF, t * LOG2_E, NEG_INF).astype(F32)
    return to_base2(banded), to_base2(full)


def _dil_attn(q, k, v):
    b, seq, _ = q.shape
    nhp = N_HEADS // 2
    bias, bias16 = _attn_bias_tables()
    head_pair = pl.BlockSpec((1, seq, V7X_LANES), lambda bi, hp: (bi, 0, hp))
    return pl.pallas_call(
        functools.partial(_attn_kernel, seq),
        grid=(b, nhp),
        in_specs=[head_pair, head_pair, head_pair,
                  pl.BlockSpec((1, 12, Q_BLOCK, K_WINDOW), lambda bi, hp: (hp, 0, 0, 0)),
                  pl.BlockSpec((1, 2, Q_BLOCK, Q_BLOCK), lambda bi, hp: (hp, 0, 0, 0))],
        out_specs=head_pair,
        out_shape=jax.ShapeDtypeStruct((b, seq, D_ATTN), F32),
        scratch_shapes=[pltpu.VMEM((3, 2, seq, V7X_LANES), BF16),
                        pltpu.VMEM((3, seq, V7X_LANES), BF16),
                        pltpu.VMEM((3, 2, seq, V7X_LANES), BF16),
                        pltpu.VMEM((3, 3, seq, V7X_LANES), F32),
                        pltpu.VMEM((2, 3, seq, V7X_LANES), F32),
                        pltpu.VMEM((2 * ATTN_GROUP, Q_BLOCK, K_WINDOW), F32),
                        pltpu.VMEM((2 * ATTN_GROUP, Q_BLOCK, K_WINDOW), BF16)],
        compiler_params=pltpu.CompilerParams(dimension_semantics=("parallel", "parallel"),
                                             vmem_limit_bytes=V7X_VMEM_LIMIT_BYTES),
        name="dil_attn",
    )(q, k, v, bias, bias16)


def _out_ffn2_kernel(x1_ref, yh_ref, ya_ref, ga_ref, wo_ref, g3_ref, wg_ref, wu_ref, wd_ref, gf_ref,
                     out_ref, act_ref):
    ya = _rms(ya_ref[...], ga_ref[...]).astype(BF16)
    mix = _dot(yh_ref[...], wo_ref[:D_HYENA, :]) + _dot(ya, wo_ref[D_HYENA:, :])
    x2 = x1_ref[...] + mix
    h = _rms(x2, g3_ref[...]).astype(BF16)
    x3 = x2 + 0.5 * _swiglu(h, wg_ref, wu_ref, wd_ref, act_ref)
    out_ref[...] = _rms(x3, gf_ref[...])


def _out_ffn2(x1, yh, ya, ga, wo, g3, wg, wu, wd, gf):
    n = x1.shape[0]
    row = lambda w: pl.BlockSpec((OUT_ROW_TILE, w), lambda i: (i, 0))
    return pl.pallas_call(
        _out_ffn2_kernel,
        grid=(n // OUT_ROW_TILE,),
        in_specs=[row(D_MODEL), row(D_HYENA), row(D_ATTN), _resident((1, D_ATTN)), _resident(wo.shape),
                  _resident((1, D_MODEL)), _resident(wg.shape), _resident(wu.shape), _resident(wd.shape),
                  _resident((1, D_MODEL))],
        out_specs=row(D_MODEL),
        out_shape=jax.ShapeDtypeStruct((n, D_MODEL), F32),
        scratch_shapes=[pltpu.VMEM((OUT_ROW_TILE, D_FF), BF16)],
        compiler_params=pltpu.CompilerParams(dimension_semantics=("parallel",),
                                             vmem_limit_bytes=V7X_VMEM_LIMIT_BYTES),
        name="out_ffn2",
    )(x1, yh, ya, ga, wo, g3, wg, wu, wd, gf)


def _dft_tables(seq):
    n = 2 * seq
    s = np.arange(seq, dtype=np.int64)[None, :]
    turn = lambda steps: 2.0 * np.pi * (steps % (2 * n)) / (2 * n)
    theta = turn((2 * np.arange(DFT_BASE_ROWS, dtype=np.int64)[:, None] + 1) * s)
    phase = turn(2 * DFT_BASE_ROWS * np.arange(seq // DFT_BASE_ROWS, dtype=np.int64)[:, None] * s)
    tab = lambda x: jnp.asarray(x.astype(np.float32))
    c0, s0 = np.cos(theta), np.sin(theta)
    return (tab(np.stack([c0, -s0])), tab(np.stack([s0, c0])),
            tab(np.cos(phase))[:, None, :], tab(np.sin(phase))[:, None, :])


def _filter_features(seq):
    t = jnp.linspace(0.0, 1.0, seq, dtype=F32)[:, None]
    w = 2.0 * math.pi * jnp.arange(seq, dtype=F32)[:, None] / seq
    f = jnp.linspace(1e-4, FILTER_BANDS - 1, FILTER_BANDS, dtype=F32)[None, :]
    z = jnp.concatenate([t, jnp.cos(f * w), -jnp.sin(f * w)], axis=-1)
    return jnp.pad(z, ((0, 0), (0, FEAT_PAD - FILTER_EMB)))


def _decay_rates():
    max_decay = math.log(DECAY_TARGET) / FAST_DECAY_PCT
    min_decay = math.log(DECAY_TARGET) / SLOW_DECAY_PCT
    return jnp.linspace(min_decay, max_decay, D_HYENA, dtype=F32)[None, :]


def kernel(x, ffn1_norm_g, ffn1_w_gate, ffn1_w_up, ffn1_w_down, mix_norm_g, w_in, hy_conv_w, hy_conv_b, hy_filt_w1, hy_filt_b1, hy_filt_w2, hy_filt_b2, hy_filt_w3, hy_filt_b3, hy_filt_w_out, hy_filt_freq, hy_filt_skip, hy_out_norm_g, attn_out_norm_g, w_out, ffn2_norm_g, ffn2_w_gate, ffn2_w_up, ffn2_w_down, final_norm_g):
    b, seq, d = x.shape
    assert d == D_MODEL and (b * seq) % OUT_ROW_TILE == 0 and seq % (DILATIONS[-1] * Q_BLOCK) == 0
    row = lambda a: a.reshape(1, -1).astype(F32)
    f32 = lambda a: a.astype(F32)

    w1 = jnp.pad(f32(hy_filt_w1), ((0, FEAT_PAD - FILTER_EMB), (0, 0)))
    hspec, fmat, wg1, wu1, wd1, win = _hy_filter(
        seq, _filter_features(seq), w1, row(hy_filt_b1), f32(hy_filt_w2), row(hy_filt_b2), f32(hy_filt_w3),
        row(hy_filt_b3), f32(hy_filt_w_out), row(hy_filt_freq), _decay_rates(), _dft_tables(seq),
        [f32(ffn1_w_gate), f32(ffn1_w_up), f32(ffn1_w_down), f32(w_in)])

    x1, hy, q, k, v, wo, wg2, wu2, wd2 = _ffn1_proj(
        x.reshape(b * seq, d), row(ffn1_norm_g), wg1, wu1, wd1, row(mix_norm_g), win,
        [f32(w_out), f32(ffn2_w_gate), f32(ffn2_w_up), f32(ffn2_w_down)])

    conv_w = hy_conv_w.astype(F32).reshape(3, 3, D_HYENA).transpose(1, 0, 2)
    conv_b = hy_conv_b.astype(F32).reshape(3, 1, D_HYENA)
    skip = hy_filt_skip.astype(F32).reshape(2, 1, D_HYENA)
    y_hy = _hyena(hy.reshape(b, seq, 3 * D_HYENA), conv_w, conv_b, skip, row(hy_out_norm_g),
                  fmat, hspec)

    shape3 = lambda a: a.reshape(b, seq, D_ATTN)
    y_at = _dil_attn(shape3(q), shape3(k), shape3(v))

    out = _out_ffn2(x1, y_hy.reshape(b * seq, D_HYENA), y_at.reshape(b * seq, D_ATTN),
                    row(attn_out_norm_g), wo, row(ffn2_norm_g), wg2, wu2, wd2, row(final_norm_g))
    return out.reshape(b, seq, d)
```

```python
import functools
import math

import numpy as np
import jax
import jax.numpy as jnp
from jax import lax
from jax.experimental import pallas as pl
from jax.experimental.pallas import tpu as pltpu

F32 = jnp.float32
BF16 = jnp.bfloat16

D_MODEL = 1024
D_HYENA = 512
D_ATTN = 512
HEAD_DIM = 64
N_HEADS = D_ATTN // HEAD_DIM
D_FF = 2816
FILTER_EMB = 33
FILTER_BANDS = 16
FILTER_WIDTH = 64
DECAY_TARGET = 1e-2
FAST_DECAY_PCT = 0.3
SLOW_DECAY_PCT = 1.5
DILATIONS = (1, 4, 16)
HALF_WINDOW = 64
RMS_EPS = 1e-6
NEG_INF = -1e30
LOG2_E = math.log2(math.e)

V7X_LANES = 128
V7X_SUBLANES = 8
V7X_BF16_SUBLANES = 16
V7X_VMEM_LIMIT_BYTES = 56 * 1024 * 1024

ROW_TILE = 512
OUT_ROW_TILE = 1024
FF_CHUNK = 256
FREQ_BLOCK = 1024
FILTER_ROWS = 512
DFT_BASE_ROWS = 256
FILTER_PREP_STEPS = 8
Q_BLOCK = 128
K_WINDOW = 256
ATTN_GROUP = 16
CONV_ROWS = 256
GATE_SUB_ROWS = 64
FEAT_PAD = 128


def _dot(a, b):
    return jnp.dot(a, b, preferred_element_type=F32)


def _rms(x, g):
    return x * lax.rsqrt(jnp.mean(x * x, axis=-1, keepdims=True) + RMS_EPS) * g


def _swiglu(h, wg_ref, wu_ref, wd_ref, act_ref):
    for c in range(D_FF // FF_CHUNK):
        cols = slice(c * FF_CHUNK, (c + 1) * FF_CHUNK)
        g = _dot(h, wg_ref[:, cols])
        u = _dot(h, wu_ref[:, cols])
        act_ref[:, cols] = (g * jax.nn.sigmoid(g) * u).astype(BF16)
    return _dot(act_ref[...], wd_ref[...])


def _resident(shape):
    return pl.BlockSpec(shape, lambda *_: (0,) * len(shape), pipeline_mode=pl.Buffered(1))


def _cast_rider(weights, steps):
    in_specs, out_specs, out_shapes = [], [], []
    for w in weights:
        rows, cols = w.shape
        visits = 1
        while (rows * visits) % steps or (rows * visits // steps) % V7X_BF16_SUBLANES:
            visits *= 2
        slab = pl.BlockSpec((rows * visits // steps, cols), lambda i, visits=visits: (i // visits, 0))
        in_specs.append(slab)
        out_specs.append(slab)
        out_shapes.append(jax.ShapeDtypeStruct(w.shape, BF16))
    return in_specs, out_specs, out_shapes


def _cast_slabs(src_refs, dst_refs):
    for src, dst in zip(src_refs, dst_refs):
        dst[...] = src[...].astype(BF16)


def _ffn1_proj_kernel(x_ref, g1_ref, wg_ref, wu_ref, wd_ref, g2_ref, win_ref, *rest):
    n_cast = (len(rest) - 6) // 2
    x1_ref, hy_ref, q_ref, k_ref, v_ref = rest[n_cast:n_cast + 5]
    act_ref = rest[-1]
    _cast_slabs(rest[:n_cast], rest[n_cast + 5:-1])
    x = x_ref[...]
    h = _rms(x, g1_ref[...]).astype(BF16)
    x1 = x + 0.5 * _swiglu(h, wg_ref, wu_ref, wd_ref, act_ref)
    x1_ref[...] = x1
    h2 = _rms(x1, g2_ref[...]).astype(BF16)
    nh = 3 * D_HYENA
    hy_ref[...] = _dot(h2, win_ref[:, :nh])
    q_ref[...] = _dot(h2, win_ref[:, nh:nh + D_ATTN])
    k_ref[...] = _dot(h2, win_ref[:, nh + D_ATTN:nh + 2 * D_ATTN])
    v_ref[...] = _dot(h2, win_ref[:, nh + 2 * D_ATTN:])


def _ffn1_proj(x2d, g1, wg, wu, wd, g2, win, later_weights):
    n = x2d.shape[0]
    steps = n // ROW_TILE
    row = lambda w: pl.BlockSpec((ROW_TILE, w), lambda i: (i, 0))
    cast_in, cast_out, cast_shapes = _cast_rider(later_weights, steps)
    return pl.pallas_call(
        _ffn1_proj_kernel,
        grid=(steps,),
        in_specs=[row(D_MODEL), _resident((1, D_MODEL)), _resident(wg.shape), _resident(wu.shape),
                  _resident(wd.shape), _resident((1, D_MODEL)), _resident(win.shape)] + cast_in,
        out_specs=[row(D_MODEL), row(3 * D_HYENA), row(D_ATTN), row(D_ATTN), row(D_ATTN)] + cast_out,
        out_shape=[jax.ShapeDtypeStruct((n, D_MODEL), F32),
                   jax.ShapeDtypeStruct((n, 3 * D_HYENA), F32),
                   jax.ShapeDtypeStruct((n, D_ATTN), F32),
                   jax.ShapeDtypeStruct((n, D_ATTN), F32),
                   jax.ShapeDtypeStruct((n, D_ATTN), F32)] + cast_shapes,
        scratch_shapes=[pltpu.VMEM((ROW_TILE, D_FF), BF16)],
        compiler_params=pltpu.CompilerParams(dimension_semantics=("arbitrary",),
                                             vmem_limit_bytes=V7X_VMEM_LIMIT_BYTES),
        name="ffn1_proj",
    )(x2d, g1, wg, wu, wd, g2, win, *later_weights)


def _split_bf16(a):
    hi = a.astype(BF16)
    return hi, (a - hi.astype(F32)).astype(BF16)


def _dot_split(a, b):
    a_hi, a_lo = a
    b_hi, b_lo = b
    return _dot(a_hi, b_hi) + (_dot(a_hi, b_lo) + _dot(a_lo, b_hi))


def _filter_kernel(seq, feat_ref, w1_ref, b1_ref, w2_ref, b2_ref, w3_ref, b3_ref, wtop_ref, wbot_ref,
                   freq_ref, delta_ref, lead_ref, lag_ref, cosp_ref, sinp_ref, *rest):
    n_cast = (len(rest) - 5) // 2
    h_ref, f_ref = rest[n_cast], rest[n_cast + 1]
    hcat_ref, hid_ref, act_ref = rest[-3:]
    _cast_slabs(rest[:n_cast], rest[n_cast + 2:-3])
    c_ = D_HYENA
    half = seq // 2
    hb = FREQ_BLOCK // 2
    i = pl.program_id(0)

    hi = lax.Precision.HIGHEST
    layer = lambda h, w_ref, b_ref: jnp.sin(freq_ref[...] * (
        jnp.dot(h, w_ref[...], precision=hi, preferred_element_type=F32) + b_ref[...]))

    @pl.when(i == 0)
    def _():
        act_ref[...] = layer(feat_ref[...], w1_ref, b1_ref)

    @pl.when(i == 1)
    def _():
        act_ref[...] = layer(act_ref[...], w2_ref, b2_ref)

    @pl.when(i == 2)
    def _():
        hid_ref[0], hid_ref[1] = _split_bf16(layer(act_ref[...], w3_ref, b3_ref))

    n_layers = 3
    pieces = [(part, o) for part in range(2) for o in range(2)]
    assert n_layers + len(pieces) <= FILTER_PREP_STEPS
    for step, (part, o) in enumerate(pieces, start=n_layers):
        @pl.when(i == step)
        def _(part=part, o=o):
            w_ref = (wtop_ref, wbot_ref)[part]
            h = (hid_ref[0], hid_ref[1])
            row = lax.broadcasted_iota(jnp.int32, (half, c_), 0)
            t = (row[:, :1] + part * half).astype(F32) * (1.0 / (seq - 1))
            decay = jnp.exp(-t * jnp.abs(delta_ref[...]))
            rows = slice(part * half, (part + 1) * half)
            fwd = _dot_split(h, _split_bf16(w_ref[:, (2 * o) * c_:(2 * o + 1) * c_])) * decay
            bwd = _dot_split(h, _split_bf16(w_ref[:, (2 * o + 1) * c_:(2 * o + 2) * c_])) * decay
            if part == 0:
                bwd = jnp.where(row == 0, 0.0, bwd)
            hcat_ref[0, rows, o * c_:(o + 1) * c_] = (fwd + bwd).astype(BF16)
            hcat_ref[1, rows, o * c_:(o + 1) * c_] = (fwd - bwd).astype(BF16)

    @pl.when(i >= FILTER_PREP_STEPS)
    def _():
        row0 = (i - FILTER_PREP_STEPS) * FILTER_ROWS
        is_sin = (row0 // hb) % 2
        group0 = (row0 // FREQ_BLOCK) * (hb // DFT_BASE_ROWS) + (row0 % hb) // DFT_BASE_ROWS
        for g in range(FILTER_ROWS // DFT_BASE_ROWS):
            rows = slice(g * DFT_BASE_ROWS, (g + 1) * DFT_BASE_ROWS)
            f_ref[rows, :] = (lead_ref[is_sin] * cosp_ref[group0 + g]
                              - lag_ref[is_sin] * sinp_ref[group0 + g]).astype(BF16)
        p = _dot(f_ref[...], hcat_ref[is_sin]) * (1.0 / seq)
        for o in range(2):
            h_ref[o] = p[:, o * c_:(o + 1) * c_]


def _hy_filter(seq, feat, w1, b1, w2, b2, w3, b3, wout, freq, delta, dft_tables, later_weights):
    full = lambda a: _resident(a.shape)
    assert (FREQ_BLOCK // 2) % FILTER_ROWS == 0
    steps = FILTER_PREP_STEPS + 2 * seq // FILTER_ROWS
    block = lambda i: jnp.maximum(i - FILTER_PREP_STEPS, 0)
    cast_in, cast_out, cast_shapes = _cast_rider(later_weights, steps)
    twice = lambda r: jnp.concatenate([r, r], axis=1)
    diag2 = lambda w: jnp.concatenate([jnp.concatenate([w, jnp.zeros_like(w)], axis=1),
                                       jnp.concatenate([jnp.zeros_like(w), w], axis=1)], axis=0)
    feat = jnp.concatenate([feat[:seq // 2], feat[seq // 2:]], axis=1)
    w1, w2, w3 = diag2(w1), diag2(w2), diag2(w3)
    b1, b2, b3, freq = twice(b1), twice(b2), twice(b3), twice(freq)
    wtop = jnp.concatenate([wout, jnp.zeros_like(wout)], axis=0)
    wbot = jnp.concatenate([jnp.zeros_like(wout), wout], axis=0)
    return pl.pallas_call(
        functools.partial(_filter_kernel, seq),
        grid=(steps,),
        in_specs=[full(feat), full(w1), full(b1), full(w2), full(b2), full(w3), full(b3), full(wtop),
                  full(wbot), full(freq), full(delta)] + [full(t) for t in dft_tables] + cast_in,
        out_specs=[pl.BlockSpec((2, FILTER_ROWS, D_HYENA), lambda i: (0, block(i), 0)),
                   pl.BlockSpec((FILTER_ROWS, seq), lambda i: (block(i), 0))] + cast_out,
        out_shape=[jax.ShapeDtypeStruct((2, 2 * seq, D_HYENA), F32),
                   jax.ShapeDtypeStruct((2 * seq, seq), BF16)] + cast_shapes,
        scratch_shapes=[pltpu.VMEM((2, seq, 2 * D_HYENA), BF16),
                        pltpu.VMEM((2, seq // 2, 2 * FILTER_WIDTH), BF16),
                        pltpu.VMEM((seq // 2, 2 * FILTER_WIDTH), F32)],
        compiler_params=pltpu.CompilerParams(dimension_semantics=("arbitrary",),
                                             vmem_limit_bytes=V7X_VMEM_LIMIT_BYTES),
        name="hy_filter",
    )(feat, w1, b1, w2, b2, w3, b3, wtop, wbot, freq, delta, *dft_tables, *later_weights)


def _short_conv(cur, before, after, w_ref, b_ref):
    rows = cur.shape[0]
    row = lax.broadcasted_iota(jnp.int32, cur.shape, 0)
    prev = jnp.where(row == 0, before, pltpu.roll(cur, 1, axis=0))
    nxt = jnp.where(row == rows - 1, after, pltpu.roll(cur, rows - 1, axis=0))
    return b_ref[0] + prev * w_ref[0, 0:1, :] + cur * w_ref[0, 1:2, :] + nxt * w_ref[0, 2:3, :]


def _short_conv_rows(u_ref, w_ref, b_ref, t0, seq):
    zero = jnp.zeros((1, u_ref.shape[2]), F32)
    before = u_ref[0, t0 - V7X_SUBLANES:t0, :][V7X_SUBLANES - 1:] if t0 > 0 else zero
    end = t0 + CONV_ROWS
    after = u_ref[0, end:end + V7X_SUBLANES, :][:1] if end < seq else zero
    return _short_conv(u_ref[0, t0:end, :], before, after, w_ref, b_ref)


def _hyena_kernel(seq, v_ref, gate_ref, gate_lo_ref, gate_hi_ref, wv_ref, bv_ref, wg_ref, bg_ref, skip_ref,
                  gain_ref, f_ref, h_ref, out_ref, zf_ref, zb_ref, acc_ref, g_ref, y_ref):
    o = pl.program_id(1)
    j = pl.program_id(2)
    nfb = 2 * seq // FREQ_BLOCK
    gate_rows = seq // nfb
    chunks = range(0, seq, CONV_ROWS)
    hb = FREQ_BLOCK // 2

    def f_block(jb):
        return f_ref[pl.ds(pl.multiple_of(jb * FREQ_BLOCK, FREQ_BLOCK), FREQ_BLOCK), :]

    def forward(jb, slot):
        zfreq = _dot(f_block(jb), zb_ref[...])
        zr, zi = zfreq[:hb], zfreq[hb:]
        hr, hi = h_ref[0, :hb, :], h_ref[0, hb:, :]
        y_ref[slot, :, :hb] = (zr * hr - zi * hi).T.astype(BF16)
        y_ref[slot, :, hb:] = (zr * hi + zi * hr).T.astype(BF16)

    def inverse(jb, slot):
        acc_ref[...] += _dot(y_ref[slot], f_block(jb))

    def gate_chunk():
        t0 = pl.multiple_of(j * gate_rows, gate_rows)
        for r0 in range(0, gate_rows, GATE_SUB_ROWS):
            r1 = r0 + GATE_SUB_ROWS
            if r0 == 0:
                before = jnp.where(j > 0, gate_lo_ref[0, V7X_SUBLANES - 1:, :], 0.0)
            else:
                before = gate_ref[0, r0 - V7X_SUBLANES:r0, :][V7X_SUBLANES - 1:]
            if r1 == gate_rows:
                after = jnp.where(j < nfb - 1, gate_hi_ref[0, :1, :], 0.0)
            else:
                after = gate_ref[0, r1:r1 + V7X_SUBLANES, :][:1]
            g_ref[pl.ds(t0 + r0, GATE_SUB_ROWS), :] = _short_conv(gate_ref[0, r0:r1, :], before, after,
                                                                   wg_ref, bg_ref)

    @pl.when((o == 0) & (j == 0))
    def _():
        for t0 in chunks:
            z = _short_conv_rows(v_ref, wv_ref, bv_ref, t0, seq)
            zf_ref[t0:t0 + CONV_ROWS, :] = z
            zb_ref[t0:t0 + CONV_ROWS, :] = z.astype(BF16)

    @pl.when(j == 0)
    def _():
        gate_chunk()
        acc_ref[...] = jnp.zeros_like(acc_ref)
        forward(0, 0)

    for parity in range(2):
        @pl.when((j > 0) & (j < nfb) & (j % 2 == parity))
        def _(parity=parity):
            gate_chunk()
            forward(j, parity)
            inverse(j - 1, 1 - parity)

    def gated(rows):
        return g_ref[rows, :] * (acc_ref[:, rows].T + zf_ref[rows, :] * skip_ref[0])

    last_slot = (nfb - 1) % 2

    @pl.when((j == nfb) & (o == 0))
    def _():
        inverse(nfb - 1, last_slot)
        for t0 in chunks:
            rows = slice(t0, t0 + CONV_ROWS)
            z = gated(rows)
            zf_ref[rows, :] = z
            zb_ref[rows, :] = z.astype(BF16)

    @pl.when((j == nfb) & (o == 1))
    def _():
        inverse(nfb - 1, last_slot)
        for t0 in chunks:
            rows = slice(t0, t0 + CONV_ROWS)
            out_ref[0, rows, :] = _rms(gated(rows), gain_ref[...]).astype(BF16)


def _hyena(hy, conv_w, conv_b, skip, gain, fmat, hspec):
    b, seq, _ = hy.shape
    c = D_HYENA
    nfb = 2 * seq // FREQ_BLOCK
    gate_rows = seq // nfb
    halo_per_chunk = gate_rows // V7X_SUBLANES
    fwd_blk = lambda j: jnp.minimum(j, nfb - 1)
    wpart = lambda sel: pl.BlockSpec((1, 3, c), lambda bi, o, j: (sel(o), 0, 0))
    bpart = lambda sel: pl.BlockSpec((1, 1, c), lambda bi, o, j: (sel(o), 0, 0))
    value = lambda o: 0
    gate = lambda o: 1 + o
    halo = lambda first: pl.BlockSpec((1, V7X_SUBLANES, c), lambda bi, o, j: (bi, first(fwd_blk(j)), 1 + o))
    return pl.pallas_call(
        functools.partial(_hyena_kernel, seq),
        grid=(b, 2, nfb + 1),
        in_specs=[pl.BlockSpec((1, seq, c), lambda bi, o, j: (bi, 0, 0)),
                  pl.BlockSpec((1, gate_rows, c), lambda bi, o, j: (bi, fwd_blk(j), 1 + o)),
                  halo(lambda jc: jnp.maximum(jc * halo_per_chunk - 1, 0)),
                  halo(lambda jc: jnp.minimum((jc + 1) * halo_per_chunk, seq // V7X_SUBLANES - 1)),
                  wpart(value), bpart(value), wpart(gate), bpart(gate),
                  pl.BlockSpec((1, 1, c), lambda bi, o, j: (o, 0, 0)),
                  pl.BlockSpec((1, c), lambda bi, o, j: (0, 0)),
                  _resident(fmat.shape),
                  pl.BlockSpec((1, FREQ_BLOCK, c), lambda bi, o, j: (o, fwd_blk(j), 0))],
        out_specs=pl.BlockSpec((1, seq, c), lambda bi, o, j: (bi, 0, 0)),
        out_shape=jax.ShapeDtypeStruct((b, seq, c), BF16),
        scratch_shapes=[pltpu.VMEM((seq, c), F32), pltpu.VMEM((seq, c), BF16), pltpu.VMEM((c, seq), F32),
                        pltpu.VMEM((seq, c), F32), pltpu.VMEM((2, c, FREQ_BLOCK), BF16)],
        compiler_params=pltpu.CompilerParams(dimension_semantics=("parallel", "arbitrary", "arbitrary"),
                                             vmem_limit_bytes=V7X_VMEM_LIMIT_BYTES),
        name="hyena",
    )(hy, hy, hy, hy, conv_w, conv_b, conv_w, conv_b, skip, gain, fmat, hspec)


def _attn_kernel(seq, q_ref, k_ref, v_ref, bias_ref, bias16_ref, o_ref,
                 qs_ref, ks_ref, vs_ref, res_ref, nat_ref, s_ref, e_ref):
    lanes = V7X_LANES
    qscale = LOG2_E / math.sqrt(HEAD_DIM)

    for p, d in enumerate(DILATIONS):
        ls = seq // d
        first = lax.broadcasted_iota(jnp.int32, (ls, lanes), 1) < HEAD_DIM
        for r in range(d):
            src = pl.ds(r, ls, stride=d) if d > 1 else pl.ds(0, ls)
            rows = slice(r * ls, (r + 1) * ls)
            qq = q_ref[0, src, :] * qscale
            qs_ref[p, 0, rows, :] = jnp.where(first, qq, 0.0).astype(BF16)
            qs_ref[p, 1, rows, :] = jnp.where(first, 0.0, qq).astype(BF16)
            ks_ref[p, rows, :] = k_ref[0, src, :].astype(BF16)
            vv = v_ref[0, src, :]
            vs_ref[p, 0, rows, :] = jnp.where(first, vv, 1.0).astype(BF16)
            vs_ref[p, 1, rows, :] = jnp.where(first, 1.0, vv).astype(BF16)

    first = lax.broadcasted_iota(jnp.int32, (Q_BLOCK, lanes), 1) < HEAD_DIM

    def run_pattern(p, nkeys, placement):
        def body(g, carry):
            blocks = [placement(g * ATTN_GROUP + i) for i in range(ATTN_GROUP)]
            for i, (row0, krow0, bias_of_head) in enumerate(blocks):
                kw = ks_ref[p, pl.ds(krow0, nkeys), :]
                for h in range(2):
                    qh = qs_ref[p, h, pl.ds(row0, Q_BLOCK), :]
                    s = lax.dot_general(qh, kw, (((1,), (1,)), ((), ())), preferred_element_type=F32)
                    s_ref[2 * i + h, :, :nkeys] = s + bias_of_head(h)
            for i, (row0, krow0, bias_of_head) in enumerate(blocks):
                ms = []
                for h in range(2):
                    s = s_ref[2 * i + h, :, :nkeys]
                    m = jnp.max(s, axis=-1, keepdims=True)
                    e_ref[2 * i + h, :, :nkeys] = jnp.exp2(s - m).astype(BF16)
                    ms.append(m)
                res_ref[p, 1, pl.ds(row0, Q_BLOCK), :] = jnp.where(first, ms[0], ms[1])
            for i, (row0, krow0, bias_of_head) in enumerate(blocks):
                o0 = _dot(e_ref[2 * i, :, :nkeys], vs_ref[p, 0, pl.ds(krow0, nkeys), :])
                o1 = _dot(e_ref[2 * i + 1, :, :nkeys], vs_ref[p, 1, pl.ds(krow0, nkeys), :])
                res_ref[p, 0, pl.ds(row0, Q_BLOCK), :] = jnp.where(first, o0, o1)
                res_ref[p, 2, pl.ds(row0, Q_BLOCK), :] = jnp.where(first, o1, o0)
            return carry

        lax.fori_loop(0, seq // Q_BLOCK // ATTN_GROUP, body, 0)

    for p, d in enumerate(DILATIONS[:2]):
        ls = seq // d
        nblk = ls // Q_BLOCK

        def banded(n, p=p, ls=ls, nblk=nblk):
            r = n // nblk
            ib = n % nblk
            i0 = ib * Q_BLOCK
            k0 = jnp.clip(i0 - HALF_WINDOW, 0, ls - K_WINDOW)
            case = jnp.where(ib == 0, 0, jnp.where(ib == nblk - 1, 2, 1))
            row0 = pl.multiple_of(r * ls + i0, Q_BLOCK)
            krow0 = pl.multiple_of(r * ls + k0, HALF_WINDOW)
            return row0, krow0, lambda h: bias_ref[0, (p * 3 + case) * 2 + h]

        run_pattern(p, K_WINDOW, banded)

    def full(n):
        row0 = pl.multiple_of(n * Q_BLOCK, Q_BLOCK)
        return row0, row0, lambda h: bias16_ref[0, h]

    run_pattern(2, Q_BLOCK, full)

    for p, d in enumerate(DILATIONS[1:], start=1):
        ls = seq // d
        for r in range(d):
            for kind in range(3):
                nat_ref[p - 1, kind, pl.ds(r, ls, stride=d), :] = res_ref[p, kind, r * ls:(r + 1) * ls, :]

    for t0 in range(0, seq, CONV_ROWS):
        rows = slice(t0, t0 + CONV_ROWS)
        parts = [tuple(res_ref[0, kind, rows, :] for kind in range(3))]
        parts += [tuple(nat_ref[p, kind, rows, :] for kind in range(3)) for p in range(2)]
        m = functools.reduce(jnp.maximum, [pt[1] for pt in parts])
        num = 0.0
        den = 0.0
        for out, mp, lp in parts:
            w = jnp.exp2(mp - m)
            num = num + w * out
            den = den + w * pltpu.roll(lp, HEAD_DIM, axis=1)
        o_ref[0, rows, :] = num / den


def _attn_bias_tables():
    slopes = np.array([2.0 ** (-8.0 * (i + 1) / N_HEADS) for i in range(N_HEADS)], np.float32)
    slopes = jnp.asarray(slopes.reshape(N_HEADS // 2, 1, 2, 1, 1))
    qi = lax.broadcasted_iota(jnp.int32, (Q_BLOCK, K_WINDOW), 0)
    kj = lax.broadcasted_iota(jnp.int32, (Q_BLOCK, K_WINDOW), 1)
    offsets = (0, -HALF_WINDOW, -2 * HALF_WINDOW)
    dist = jnp.stack([jnp.abs(kj - qi + off) for off in offsets])
    valid = dist <= HALF_WINDOW
    dil = jnp.asarray(np.array(DILATIONS[:2], np.float32).reshape(2, 1, 1, 1))
    scaled = (dil * dist.astype(F32)[None])[None, :, :, None]
    banded = jnp.where(valid[None, None, :, None], -slopes[:, None] * scaled, NEG_INF)
    banded = banded.reshape(N_HEADS // 2, 12, Q_BLOCK, K_WINDOW)
    d16 = dist[0, :, :Q_BLOCK]
    full = jnp.where(d16 <= HALF_WINDOW, -slopes[:, 0] * (DILATIONS[2] * d16.astype(F32)), NEG_INF)
    to_base2 = lambda t: jnp.where(t > 0.5 * NEG_INF, t * LOG2_E, NEG_INF).astype(F32)
    return to_base2(banded), to_base2(full)


def _dil_attn(q, k, v):
    b, seq, _ = q.shape
    nhp = N_HEADS // 2
    bias, bias16 = _attn_bias_tables()
    head_pair = pl.BlockSpec((1, seq, V7X_LANES), lambda bi, hp: (bi, 0, hp))
    return pl.pallas_call(
        functools.partial(_attn_kernel, seq),
        grid=(b, nhp),
        in_specs=[head_pair, head_pair, head_pair,
                  pl.BlockSpec((1, 12, Q_BLOCK, K_WINDOW), lambda bi, hp: (hp, 0, 0, 0)),
                  pl.BlockSpec((1, 2, Q_BLOCK, Q_BLOCK), lambda bi, hp: (hp, 0, 0, 0))],
        out_specs=head_pair,
        out_shape=jax.ShapeDtypeStruct((b, seq, D_ATTN), F32),
        scratch_shapes=[pltpu.VMEM((3, 2, seq, V7X_LANES), BF16),
                        pltpu.VMEM((3, seq, V7X_LANES), BF16),
                        pltpu.VMEM((3, 2, seq, V7X_LANES), BF16),
                        pltpu.VMEM((3, 3, seq, V7X_LANES), F32),
                        pltpu.VMEM((2, 3, seq, V7X_LANES), F32),
                        pltpu.VMEM((2 * ATTN_GROUP, Q_BLOCK, K_WINDOW), F32),
                        pltpu.VMEM((2 * ATTN_GROUP, Q_BLOCK, K_WINDOW), BF16)],
        compiler_params=pltpu.CompilerParams(dimension_semantics=("parallel", "parallel"),
                                             vmem_limit_bytes=V7X_VMEM_LIMIT_BYTES),
        name="dil_attn",
    )(q, k, v, bias, bias16)


def _out_ffn2_kernel(x1_ref, yh_ref, ya_ref, ga_ref, wo_ref, g3_ref, wg_ref, wu_ref, wd_ref, gf_ref,
                     out_ref, act_ref):
    ya = _rms(ya_ref[...], ga_ref[...]).astype(BF16)
    mix = _dot(yh_ref[...], wo_ref[:D_HYENA, :]) + _dot(ya, wo_ref[D_HYENA:, :])
    x2 = x1_ref[...] + mix
    h = _rms(x2, g3_ref[...]).astype(BF16)
    x3 = x2 + 0.5 * _swiglu(h, wg_ref, wu_ref, wd_ref, act_ref)
    out_ref[...] = _rms(x3, gf_ref[...])


def _out_ffn2(x1, yh, ya, ga, wo, g3, wg, wu, wd, gf):
    n = x1.shape[0]
    row = lambda w: pl.BlockSpec((OUT_ROW_TILE, w), lambda i: (i, 0))
    return pl.pallas_call(
        _out_ffn2_kernel,
        grid=(n // OUT_ROW_TILE,),
        in_specs=[row(D_MODEL), row(D_HYENA), row(D_ATTN), _resident((1, D_ATTN)), _resident(wo.shape),
                  _resident((1, D_MODEL)), _resident(wg.shape), _resident(wu.shape), _resident(wd.shape),
                  _resident((1, D_MODEL))],
        out_specs=row(D_MODEL),
        out_shape=jax.ShapeDtypeStruct((n, D_MODEL), F32),
        scratch_shapes=[pltpu.VMEM((OUT_ROW_TILE, D_FF), BF16)],
        compiler_params=pltpu.CompilerParams(dimension_semantics=("parallel",),
                                             vmem_limit_bytes=V7X_VMEM_LIMIT_BYTES),
        name="out_ffn2",
    )(x1, yh, ya, ga, wo, g3, wg, wu, wd, gf)


def _dft_tables(seq):
    n = 2 * seq
    s = np.arange(seq, dtype=np.int64)[None, :]
    turn = lambda steps: 2.0 * np.pi * (steps % (2 * n)) / (2 * n)
    theta = turn((2 * np.arange(DFT_BASE_ROWS, dtype=np.int64)[:, None] + 1) * s)
    phase = turn(2 * DFT_BASE_ROWS * np.arange(seq // DFT_BASE_ROWS, dtype=np.int64)[:, None] * s)
    tab = lambda x: jnp.asarray(x.astype(np.float32))
    c0, s0 = np.cos(theta), np.sin(theta)
    return (tab(np.stack([c0, -s0])), tab(np.stack([s0, c0])),
            tab(np.cos(phase))[:, None, :], tab(np.sin(phase))[:, None, :])


def _filter_features(seq):
    t = jnp.linspace(0.0, 1.0, seq, dtype=F32)[:, None]
    w = 2.0 * math.pi * jnp.arange(seq, dtype=F32)[:, None] / seq
    f = jnp.linspace(1e-4, FILTER_BANDS - 1, FILTER_BANDS, dtype=F32)[None, :]
    z = jnp.concatenate([t, jnp.cos(f * w), -jnp.sin(f * w)], axis=-1)
    return jnp.pad(z, ((0, 0), (0, FEAT_PAD - FILTER_EMB)))


def _decay_rates():
    max_decay = math.log(DECAY_TARGET) / FAST_DECAY_PCT
    min_decay = math.log(DECAY_TARGET) / SLOW_DECAY_PCT
    return jnp.linspace(min_decay, max_decay, D_HYENA, dtype=F32)[None, :]


def kernel(x, ffn1_norm_g, ffn1_w_gate, ffn1_w_up, ffn1_w_down, mix_norm_g, w_in, hy_conv_w, hy_conv_b, hy_filt_w1, hy_filt_b1, hy_filt_w2, hy_filt_b2, hy_filt_w3, hy_filt_b3, hy_filt_w_out, hy_filt_freq, hy_filt_skip, hy_out_norm_g, attn_out_norm_g, w_out, ffn2_norm_g, ffn2_w_gate, ffn2_w_up, ffn2_w_down, final_norm_g):
    b, seq, d = x.shape
    assert d == D_MODEL and (b * seq) % OUT_ROW_TILE == 0 and seq % (DILATIONS[-1] * Q_BLOCK) == 0
    row = lambda a: a.reshape(1, -1).astype(F32)
    f32 = lambda a: a.astype(F32)

    w1 = jnp.pad(f32(hy_filt_w1), ((0, FEAT_PAD - FILTER_EMB), (0, 0)))
    hspec, fmat, wg1, wu1, wd1, win = _hy_filter(
        seq, _filter_features(seq), w1, row(hy_filt_b1), f32(hy_filt_w2), row(hy_filt_b2), f32(hy_filt_w3),
        row(hy_filt_b3), f32(hy_filt_w_out), row(hy_filt_freq), _decay_rates(), _dft_tables(seq),
        [f32(ffn1_w_gate), f32(ffn1_w_up), f32(ffn1_w_down), f32(w_in)])

    x1, hy, q, k, v, wo, wg2, wu2, wd2 = _ffn1_proj(
        x.reshape(b * seq, d), row(ffn1_norm_g), wg1, wu1, wd1, row(mix_norm_g), win,
        [f32(w_out), f32(ffn2_w_gate), f32(ffn2_w_up), f32(ffn2_w_down)])

    conv_w = hy_conv_w.astype(F32).reshape(3, 3, D_HYENA).transpose(1, 0, 2)
    conv_b = hy_conv_b.astype(F32).reshape(3, 1, D_HYENA)
    skip = hy_filt_skip.astype(F32).reshape(2, 1, D_HYENA)
    y_hy = _hyena(hy.reshape(b, seq, 3 * D_HYENA), conv_w, conv_b, skip, row(hy_out_norm_g),
                  fmat, hspec)

    shape3 = lambda a: a.reshape(b, seq, D_ATTN)
    y_at = _dil_attn(shape3(q), shape3(k), shape3(v))

    out = _out_ffn2(x1, y_hy.reshape(b * seq, D_HYENA), y_at.reshape(b * seq, D_ATTN),
                    row(attn_out_norm_g), wo, row(ffn2_norm_g), wg2, wu2, wd2, row(final_norm_g))
    return out.reshape(b, seq, d)
```

```python
import functools
import math

import numpy as np
import jax
import jax.numpy as jnp
from jax import lax
from jax.experimental import pallas as pl
from jax.experimental.pallas import tpu as pltpu

F32 = jnp.float32
BF16 = jnp.bfloat16

D_MODEL = 1024
D_HYENA = 512
D_ATTN = 512
HEAD_DIM = 64
N_HEADS = D_ATTN // HEAD_DIM
D_FF = 2816
FILTER_EMB = 33
FILTER_BANDS = 16
FILTER_WIDTH = 64
DECAY_TARGET = 1e-2
FAST_DECAY_PCT = 0.3
SLOW_DECAY_PCT = 1.5
DILATIONS = (1, 4, 16)
HALF_WINDOW = 64
RMS_EPS = 1e-6
NEG_INF = -1e30
LOG2_E = math.log2(math.e)

V7X_LANES = 128
V7X_SUBLANES = 8
V7X_BF16_SUBLANES = 16
V7X_VMEM_LIMIT_BYTES = 56 * 1024 * 1024

ROW_TILE = 512
OUT_ROW_TILE = 1024
FF_CHUNK = 256
FREQ_BLOCK = 1024
FILTER_ROWS = 512
DFT_BASE_ROWS = 256
FILTER_PREP_STEPS = 8
Q_BLOCK = 128
K_WINDOW = 256
ATTN_GROUP = 16
CONV_ROWS = 256
GATE_SUB_ROWS = 64
FEAT_PAD = 128


def _dot(a, b):
    return jnp.dot(a, b, preferred_element_type=F32)


def _rms(x, g):
    return x * lax.rsqrt(jnp.mean(x * x, axis=-1, keepdims=True) + RMS_EPS) * g


def _swiglu(h, wg_ref, wu_ref, wd_ref, act_ref):
    for c in range(D_FF // FF_CHUNK):
        cols = slice(c * FF_CHUNK, (c + 1) * FF_CHUNK)
        g = _dot(h, wg_ref[:, cols])
        u = _dot(h, wu_ref[:, cols])
        act_ref[:, cols] = (g * jax.nn.sigmoid(g) * u).astype(BF16)
    return _dot(act_ref[...], wd_ref[...])


def _resident(shape):
    return pl.BlockSpec(shape, lambda *_: (0,) * len(shape), pipeline_mode=pl.Buffered(1))


def _cast_rider(weights, steps):
    in_specs, out_specs, out_shapes = [], [], []
    for w in weights:
        rows, cols = w.shape
        visits = 1
        while (rows * visits) % steps or (rows * visits // steps) % V7X_BF16_SUBLANES:
            visits *= 2
        slab = pl.BlockSpec((rows * visits // steps, cols), lambda i, visits=visits: (i // visits, 0))
        in_specs.append(slab)
        out_specs.append(slab)
        out_shapes.append(jax.ShapeDtypeStruct(w.shape, BF16))
    return in_specs, out_specs, out_shapes


def _cast_slabs(src_refs, dst_refs):
    for src, dst in zip(src_refs, dst_refs):
        dst[...] = src[...].astype(BF16)


def _ffn1_proj_kernel(x_ref, g1_ref, wg_ref, wu_ref, wd_ref, g2_ref, win_ref, *rest):
    n_cast = (len(rest) - 6) // 2
    x1_ref, hy_ref, q_ref, k_ref, v_ref = rest[n_cast:n_cast + 5]
    act_ref = rest[-1]
    _cast_slabs(rest[:n_cast], rest[n_cast + 5:-1])
    x = x_ref[...]
    h = _rms(x, g1_ref[...]).astype(BF16)
    x1 = x + 0.5 * _swiglu(h, wg_ref, wu_ref, wd_ref, act_ref)
    x1_ref[...] = x1
    h2 = _rms(x1, g2_ref[...]).astype(BF16)
    nh = 3 * D_HYENA
    hy_ref[...] = _dot(h2, win_ref[:, :nh])
    q_ref[...] = _dot(h2, win_ref[:, nh:nh + D_ATTN])
    k_ref[...] = _dot(h2, win_ref[:, nh + D_ATTN:nh + 2 * D_ATTN])
    v_ref[...] = _dot(h2, win_ref[:, nh + 2 * D_ATTN:])


def _ffn1_proj(x2d, g1, wg, wu, wd, g2, win, later_weights):
    n = x2d.shape[0]
    steps = n // ROW_TILE
    row = lambda w: pl.BlockSpec((ROW_TILE, w), lambda i: (i, 0))
    cast_in, cast_out, cast_shapes = _cast_rider(later_weights, steps)
    return pl.pallas_call(
        _ffn1_proj_kernel,
        grid=(steps,),
        in_specs=[row(D_MODEL), _resident((1, D_MODEL)), _resident(wg.shape), _resident(wu.shape),
                  _resident(wd.shape), _resident((1, D_MODEL)), _resident(win.shape)] + cast_in,
        out_specs=[row(D_MODEL), row(3 * D_HYENA), row(D_ATTN), row(D_ATTN), row(D_ATTN)] + cast_out,
        out_shape=[jax.ShapeDtypeStruct((n, D_MODEL), F32),
                   jax.ShapeDtypeStruct((n, 3 * D_HYENA), F32),
                   jax.ShapeDtypeStruct((n, D_ATTN), F32),
                   jax.ShapeDtypeStruct((n, D_ATTN), F32),
                   jax.ShapeDtypeStruct((n, D_ATTN), F32)] + cast_shapes,
        scratch_shapes=[pltpu.VMEM((ROW_TILE, D_FF), BF16)],
        compiler_params=pltpu.CompilerParams(dimension_semantics=("arbitrary",),
                                             vmem_limit_bytes=V7X_VMEM_LIMIT_BYTES),
        name="ffn1_proj",
    )(x2d, g1, wg, wu, wd, g2, win, *later_weights)


def _split_bf16(a):
    hi = a.astype(BF16)
    return hi, (a - hi.astype(F32)).astype(BF16)


def _dot_split(a, b):
    a_hi, a_lo = a
    b_hi, b_lo = b
    return _dot(a_hi, b_hi) + (_dot(a_hi, b_lo) + _dot(a_lo, b_hi))


def _filter_kernel(seq, feat_ref, w1_ref, b1_ref, w2_ref, b2_ref, w3_ref, b3_ref, wtop_ref, wbot_ref,
                   freq_ref, delta_ref, lead_ref, lag_ref, cosp_ref, sinp_ref, *rest):
    n_cast = (len(rest) - 5) // 2
    h_ref, f_ref = rest[n_cast], rest[n_cast + 1]
    hcat_ref, hid_ref, act_ref = rest[-3:]
    _cast_slabs(rest[:n_cast], rest[n_cast + 2:-3])
    c_ = D_HYENA
    half = seq // 2
    hb = FREQ_BLOCK // 2
    i = pl.program_id(0)

    hi = lax.Precision.HIGHEST
    layer = lambda h, w_ref, b_ref: jnp.sin(freq_ref[...] * (
        jnp.dot(h, w_ref[...], precision=hi, preferred_element_type=F32) + b_ref[...]))

    @pl.when(i == 0)
    def _():
        act_ref[...] = layer(feat_ref[...], w1_ref, b1_ref)

    @pl.when(i == 1)
    def _():
        act_ref[...] = layer(act_ref[...], w2_ref, b2_ref)

    @pl.when(i == 2)
    def _():
        hid_ref[0], hid_ref[1] = _split_bf16(layer(act_ref[...], w3_ref, b3_ref))

    n_layers = 3
    pieces = [(part, o) for part in range(2) for o in range(2)]
    assert n_layers + len(pieces) <= FILTER_PREP_STEPS
    for step, (part, o) in enumerate(pieces, start=n_layers):
        @pl.when(i == step)
        def _(part=part, o=o):
            w_ref = (wtop_ref, wbot_ref)[part]
            h = (hid_ref[0], hid_ref[1])
            row = lax.broadcasted_iota(jnp.int32, (half, c_), 0)
            t = (row[:, :1] + part * half).astype(F32) * (1.0 / (seq - 1))
            decay = jnp.exp(-t * jnp.abs(delta_ref[...]))
            rows = slice(part * half, (part + 1) * half)
            fwd = _dot_split(h, _split_bf16(w_ref[:, (2 * o) * c_:(2 * o + 1) * c_])) * decay
            bwd = _dot_split(h, _split_bf16(w_ref[:, (2 * o + 1) * c_:(2 * o + 2) * c_])) * decay
            if part == 0:
                bwd = jnp.where(row == 0, 0.0, bwd)
            hcat_ref[0, rows, o * c_:(o + 1) * c_] = (fwd + bwd).astype(BF16)
            hcat_ref[1, rows, o * c_:(o + 1) * c_] = (fwd - bwd).astype(BF16)

    @pl.when(i >= FILTER_PREP_STEPS)
    def _():
        row0 = (i - FILTER_PREP_STEPS) * FILTER_ROWS
        is_sin = (row0 // hb) % 2
        group0 = (row0 // FREQ_BLOCK) * (hb // DFT_BASE_ROWS) + (row0 % hb) // DFT_BASE_ROWS
        for g in range(FILTER_ROWS // DFT_BASE_ROWS):
            rows = slice(g * DFT_BASE_ROWS, (g + 1) * DFT_BASE_ROWS)
            f_ref[rows, :] = (lead_ref[is_sin] * cosp_ref[group0 + g]
                              - lag_ref[is_sin] * sinp_ref[group0 + g]).astype(BF16)
        p = _dot(f_ref[...], hcat_ref[is_sin]) * (1.0 / seq)
        for o in range(2):
            h_ref[o] = p[:, o * c_:(o + 1) * c_]


def _hy_filter(seq, feat, w1, b1, w2, b2, w3, b3, wout, freq, delta, dft_tables, later_weights):
    full = lambda a: _resident(a.shape)
    assert (FREQ_BLOCK // 2) % FILTER_ROWS == 0
    steps = FILTER_PREP_STEPS + 2 * seq // FILTER_ROWS
    block = lambda i: jnp.maximum(i - FILTER_PREP_STEPS, 0)
    cast_in, cast_out, cast_shapes = _cast_rider(later_weights, steps)
    twice = lambda r: jnp.concatenate([r, r], axis=1)
    diag2 = lambda w: jnp.concatenate([jnp.concatenate([w, jnp.zeros_like(w)], axis=1),
                                       jnp.concatenate([jnp.zeros_like(w), w], axis=1)], axis=0)
    feat = jnp.concatenate([feat[:seq // 2], feat[seq // 2:]], axis=1)
    w1, w2, w3 = diag2(w1), diag2(w2), diag2(w3)
    b1, b2, b3, freq = twice(b1), twice(b2), twice(b3), twice(freq)
    wtop = jnp.concatenate([wout, jnp.zeros_like(wout)], axis=0)
    wbot = jnp.concatenate([jnp.zeros_like(wout), wout], axis=0)
    return pl.pallas_call(
        functools.partial(_filter_kernel, seq),
        grid=(steps,),
        in_specs=[full(feat), full(w1), full(b1), full(w2), full(b2), full(w3), full(b3), full(wtop),
                  full(wbot), full(freq), full(delta)] + [full(t) for t in dft_tables] + cast_in,
        out_specs=[pl.BlockSpec((2, FILTER_ROWS, D_HYENA), lambda i: (0, block(i), 0)),
                   pl.BlockSpec((FILTER_ROWS, seq), lambda i: (block(i), 0))] + cast_out,
        out_shape=[jax.ShapeDtypeStruct((2, 2 * seq, D_HYENA), F32),
                   jax.ShapeDtypeStruct((2 * seq, seq), BF16)] + cast_shapes,
        scratch_shapes=[pltpu.VMEM((2, seq, 2 * D_HYENA), BF16),
                        pltpu.VMEM((2, seq // 2, 2 * FILTER_WIDTH), BF16),
                        pltpu.VMEM((seq // 2, 2 * FILTER_WIDTH), F32)],
        compiler_params=pltpu.CompilerParams(dimension_semantics=("arbitrary",),
                                             vmem_limit_bytes=V7X_VMEM_LIMIT_BYTES),
        name="hy_filter",
    )(feat, w1, b1, w2, b2, w3, b3, wtop, wbot, freq, delta, *dft_tables, *later_weights)


def _short_conv(cur, before, after, w_ref, b_ref):
    rows = cur.shape[0]
    row = lax.broadcasted_iota(jnp.int32, cur.shape, 0)
    prev = jnp.where(row == 0, before, pltpu.roll(cur, 1, axis=0))
    nxt = jnp.where(row == rows - 1, after, pltpu.roll(cur, rows - 1, axis=0))
    return b_ref[0] + prev * w_ref[0, 0:1, :] + cur * w_ref[0, 1:2, :] + nxt * w_ref[0, 2:3, :]


def _short_conv_rows(u_ref, w_ref, b_ref, t0, seq):
    zero = jnp.zeros((1, u_ref.shape[2]), F32)
    before = u_ref[0, t0 - V7X_SUBLANES:t0, :][V7X_SUBLANES - 1:] if t0 > 0 else zero
    end = t0 + CONV_ROWS
    after = u_ref[0, end:end + V7X_SUBLANES, :][:1] if end < seq else zero
    return _short_conv(u_ref[0, t0:end, :], before, after, w_ref, b_ref)


def _hyena_kernel(seq, v_ref, gate_ref, gate_lo_ref, gate_hi_ref, wv_ref, bv_ref, wg_ref, bg_ref, skip_ref,
                  gain_ref, f_ref, h_ref, out_ref, zf_ref, zb_ref, acc_ref, g_ref, y_ref):
    o = pl.program_id(1)
    j = pl.program_id(2)
    nfb = 2 * seq // FREQ_BLOCK
    gate_rows = seq // nfb
    chunks = range(0, seq, CONV_ROWS)
    hb = FREQ_BLOCK // 2

    def f_block(jb):
        return f_ref[pl.ds(pl.multiple_of(jb * FREQ_BLOCK, FREQ_BLOCK), FREQ_BLOCK), :]

    def forward(jb, slot):
        zfreq = _dot(f_block(jb), zb_ref[...])
        zr, zi = zfreq[:hb], zfreq[hb:]
        hr, hi = h_ref[0, :hb, :], h_ref[0, hb:, :]
        y_ref[slot, :, :hb] = (zr * hr - zi * hi).T.astype(BF16)
        y_ref[slot, :, hb:] = (zr * hi + zi * hr).T.astype(BF16)

    def inverse(jb, slot):
        acc_ref[...] += _dot(y_ref[slot], f_block(jb))

    def gate_chunk():
        t0 = pl.multiple_of(j * gate_rows, gate_rows)
        for r0 in range(0, gate_rows, GATE_SUB_ROWS):
            r1 = r0 + GATE_SUB_ROWS
            if r0 == 0:
                before = jnp.where(j > 0, gate_lo_ref[0, V7X_SUBLANES - 1:, :], 0.0)
            else:
                before = gate_ref[0, r0 - V7X_SUBLANES:r0, :][V7X_SUBLANES - 1:]
            if r1 == gate_rows:
                after = jnp.where(j < nfb - 1, gate_hi_ref[0, :1, :], 0.0)
            else:
                after = gate_ref[0, r1:r1 + V7X_SUBLANES, :][:1]
            g_ref[pl.ds(t0 + r0, GATE_SUB_ROWS), :] = _short_conv(gate_ref[0, r0:r1, :], before, after,
                                                                   wg_ref, bg_ref)

    @pl.when((o == 0) & (j == 0))
    def _():
        for t0 in chunks:
            z = _short_conv_rows(v_ref, wv_ref, bv_ref, t0, seq)
            zf_ref[t0:t0 + CONV_ROWS, :] = z
            zb_ref[t0:t0 + CONV_ROWS, :] = z.astype(BF16)

    @pl.when(j == 0)
    def _():
        gate_chunk()
        acc_ref[...] = jnp.zeros_like(acc_ref)
        forward(0, 0)

    for parity in range(2):
        @pl.when((j > 0) & (j < nfb) & (j % 2 == parity))
        def _(parity=parity):
            gate_chunk()
            forward(j, parity)
            inverse(j - 1, 1 - parity)

    def gated(rows):
        return g_ref[rows, :] * (acc_ref[:, rows].T + zf_ref[rows, :] * skip_ref[0])

    last_slot = (nfb - 1) % 2

    @pl.when((j == nfb) & (o == 0))
    def _():
        inverse(nfb - 1, last_slot)
        for t0 in chunks:
            rows = slice(t0, t0 + CONV_ROWS)
            z = gated(rows)
            zf_ref[rows, :] = z
            zb_ref[rows, :] = z.astype(BF16)

    @pl.when((j == nfb) & (o == 1))
    def _():
        inverse(nfb - 1, last_slot)
        for t0 in chunks:
            rows = slice(t0, t0 + CONV_ROWS)
            out_ref[0, rows, :] = _rms(gated(rows), gain_ref[...]).astype(BF16)


def _hyena(hy, conv_w, conv_b, skip, gain, fmat, hspec):
    b, seq, _ = hy.shape
    c = D_HYENA
    nfb = 2 * seq // FREQ_BLOCK
    gate_rows = seq // nfb
    halo_per_chunk = gate_rows // V7X_SUBLANES
    fwd_blk = lambda j: jnp.minimum(j, nfb - 1)
    wpart = lambda sel: pl.BlockSpec((1, 3, c), lambda bi, o, j: (sel(o), 0, 0))
    bpart = lambda sel: pl.BlockSpec((1, 1, c), lambda bi, o, j: (sel(o), 0, 0))
    value = lambda o: 0
    gate = lambda o: 1 + o
    halo = lambda first: pl.BlockSpec((1, V7X_SUBLANES, c), lambda bi, o, j: (bi, first(fwd_blk(j)), 1 + o))
    return pl.pallas_call(
        functools.partial(_hyena_kernel, seq),
        grid=(b, 2, nfb + 1),
        in_specs=[pl.BlockSpec((1, seq, c), lambda bi, o, j: (bi, 0, 0)),
                  pl.BlockSpec((1, gate_rows, c), lambda bi, o, j: (bi, fwd_blk(j), 1 + o)),
                  halo(lambda jc: jnp.maximum(jc * halo_per_chunk - 1, 0)),
                  halo(lambda jc: jnp.minimum((jc + 1) * halo_per_chunk, seq // V7X_SUBLANES - 1)),
                  wpart(value), bpart(value), wpart(gate), bpart(gate),
                  pl.BlockSpec((1, 1, c), lambda bi, o, j: (o, 0, 0)),
                  pl.BlockSpec((1, c), lambda bi, o, j: (0, 0)),
                  _resident(fmat.shape),
                  pl.BlockSpec((1, FREQ_BLOCK, c), lambda bi, o, j: (o, fwd_blk(j), 0))],
        out_specs=pl.BlockSpec((1, seq, c), lambda bi, o, j: (bi, 0, 0)),
        out_shape=jax.ShapeDtypeStruct((b, seq, c), BF16),
        scratch_shapes=[pltpu.VMEM((seq, c), F32), pltpu.VMEM((seq, c), BF16), pltpu.VMEM((c, seq), F32),
                        pltpu.VMEM((seq, c), F32), pltpu.VMEM((2, c, FREQ_BLOCK), BF16)],
        compiler_params=pltpu.CompilerParams(dimension_semantics=("parallel", "arbitrary", "arbitrary"),
                                             vmem_limit_bytes=V7X_VMEM_LIMIT_BYTES),
        name="hyena",
    )(hy, hy, hy, hy, conv_w, conv_b, conv_w, conv_b, skip, gain, fmat, hspec)


def _attn_kernel(seq, q_ref, k_ref, v_ref, bias_ref, bias16_ref, o_ref,
                 qs_ref, ks_ref, vs_ref, res_ref, nat_ref, s_ref, e_ref):
    lanes = V7X_LANES
    qscale = LOG2_E / math.sqrt(HEAD_DIM)

    for p, d in enumerate(DILATIONS):
        ls = seq // d
        first = lax.broadcasted_iota(jnp.int32, (ls, lanes), 1) < HEAD_DIM
        for r in range(d):
            src = pl.ds(r, ls, stride=d) if d > 1 else pl.ds(0, ls)
            rows = slice(r * ls, (r + 1) * ls)
            qq = q_ref[0, src, :] * qscale
            qs_ref[p, 0, rows, :] = jnp.where(first, qq, 0.0).astype(BF16)
            qs_ref[p, 1, rows, :] = jnp.where(first, 0.0, qq).astype(BF16)
            ks_ref[p, rows, :] = k_ref[0, src, :].astype(BF16)
            vv = v_ref[0, src, :]
            vs_ref[p, 0, rows, :] = jnp.where(first, vv, 1.0).astype(BF16)
            vs_ref[p, 1, rows, :] = jnp.where(first, 1.0, vv).astype(BF16)

    first = lax.broadcasted_iota(jnp.int32, (Q_BLOCK, lanes), 1) < HEAD_DIM

    def run_pattern(p, nkeys, placement):
        def body(g, carry):
            blocks = [placement(g * ATTN_GROUP + i) for i in range(ATTN_GROUP)]
            for i, (row0, krow0, bias_of_head) in enumerate(blocks):
                kw = ks_ref[p, pl.ds(krow0, nkeys), :]
                for h in range(2):
                    qh = qs_ref[p, h, pl.ds(row0, Q_BLOCK), :]
                    s = lax.dot_general(qh, kw, (((1,), (1,)), ((), ())), preferred_element_type=F32)
                    s_ref[2 * i + h, :, :nkeys] = s + bias_of_head(h)
            for i, (row0, krow0, bias_of_head) in enumerate(blocks):
                ms = []
                for h in range(2):
                    s = s_ref[2 * i + h, :, :nkeys]
                    m = jnp.max(s, axis=-1, keepdims=True)
                    e_ref[2 * i + h, :, :nkeys] = jnp.exp2(s - m).astype(BF16)
                    ms.append(m)
                res_ref[p, 1, pl.ds(row0, Q_BLOCK), :] = jnp.where(first, ms[0], ms[1])
            for i, (row0, krow0, bias_of_head) in enumerate(blocks):
                o0 = _dot(e_ref[2 * i, :, :nkeys], vs_ref[p, 0, pl.ds(krow0, nkeys), :])
                o1 = _dot(e_ref[2 * i + 1, :, :nkeys], vs_ref[p, 1, pl.ds(krow0, nkeys), :])
                res_ref[p, 0, pl.ds(row0, Q_BLOCK), :] = jnp.where(first, o0, o1)
                res_ref[p, 2, pl.ds(row0, Q_BLOCK), :] = jnp.where(first, o1, o0)
            return carry

        lax.fori_loop(0, seq // Q_BLOCK // ATTN_GROUP, body, 0)

    for p, d in enumerate(DILATIONS[:2]):
        ls = seq // d
        nblk = ls // Q_BLOCK

        def banded(n, p=p, ls=ls, nblk=nblk):
            r = n // nblk
            ib = n % nblk
            i0 = ib * Q_BLOCK
            k0 = jnp.clip(i0 - HALF_WINDOW, 0, ls - K_WINDOW)
            case = jnp.where(ib == 0, 0, jnp.where(ib == nblk - 1, 2, 1))
            row0 = pl.multiple_of(r * ls + i0, Q_BLOCK)
            krow0 = pl.multiple_of(r * ls + k0, HALF_WINDOW)
            return row0, krow0, lambda h: bias_ref[0, (p * 3 + case) * 2 + h]

        run_pattern(p, K_WINDOW, banded)

    def full(n):
        row0 = pl.multiple_of(n * Q_BLOCK, Q_BLOCK)
        return row0, row0, lambda h: bias16_ref[0, h]

    run_pattern(2, Q_BLOCK, full)

    for p, d in enumerate(DILATIONS[1:], start=1):
        ls = seq // d
        for r in range(d):
            for kind in range(3):
                nat_ref[p - 1, kind, pl.ds(r, ls, stride=d), :] = res_ref[p, kind, r * ls:(r + 1) * ls, :]

    for t0 in range(0, seq, CONV_ROWS):
        rows = slice(t0, t0 + CONV_ROWS)
        parts = [tuple(res_ref[0, kind, rows, :] for kind in range(3))]
        parts += [tuple(nat_ref[p, kind, rows, :] for kind in range(3)) for p in range(2)]
        m = functools.reduce(jnp.maximum, [pt[1] for pt in parts])
        num = 0.0
        den = 0.0
        for out, mp, lp in parts:
            w = jnp.exp2(mp - m)
            num = num + w * out
            den = den + w * pltpu.roll(lp, HEAD_DIM, axis=1)
        o_ref[0, rows, :] = num / den


def _attn_bias_tables():
    slopes = np.array([2.0 ** (-8.0 * (i + 1) / N_HEADS) for i in range(N_HEADS)], np.float32)
    f32 = np.float32
    slopes = slopes.reshape(N_HEADS // 2, 1, 2, 1, 1)
    qi = np.arange(Q_BLOCK)[:, None]
    kj = np.arange(K_WINDOW)[None, :]
    offsets = (0, -HALF_WINDOW, -2 * HALF_WINDOW)
    dist = np.stack([np.abs(kj - qi + off) for off in offsets])
    valid = dist <= HALF_WINDOW
    dil = np.array(DILATIONS[:2], f32).reshape(2, 1, 1, 1)
    scaled = (dil * dist.astype(f32)[None])[None, :, :, None]
    banded = np.where(valid[None, None, :, None], -slopes[:, None] * scaled, f32(NEG_INF))
    banded = banded.reshape(N_HEADS // 2, 12, Q_BLOCK, K_WINDOW)
    d16 = dist[0, :, :Q_BLOCK]
    full = np.where(d16 <= HALF_WINDOW, -slopes[:, 0] * (f32(DILATIONS[2]) * d16.astype(f32)), f32(NEG_INF))
    to_base2 = lambda t: jnp.asarray(np.where(t > 0.5 * NEG_INF, t * f32(LOG2_E), f32(NEG_INF)).astype(f32))
    return to_base2(banded), to_base2(full)


def _dil_attn(q, k, v):
    b, seq, _ = q.shape
    nhp = N_HEADS // 2
    bias, bias16 = _attn_bias_tables()
    head_pair = pl.BlockSpec((1, seq, V7X_LANES), lambda bi, hp: (bi, 0, hp))
    return pl.pallas_call(
        functools.partial(_attn_kernel, seq),
        grid=(b, nhp),
        in_specs=[head_pair, head_pair, head_pair,
                  pl.BlockSpec((1, 12, Q_BLOCK, K_WINDOW), lambda bi, hp: (hp, 0, 0, 0)),
                  pl.BlockSpec((1, 2, Q_BLOCK, Q_BLOCK), lambda bi, hp: (hp, 0, 0, 0))],
        out_specs=head_pair,
        out_shape=jax.ShapeDtypeStruct((b, seq, D_ATTN), F32),
        scratch_shapes=[pltpu.VMEM((3, 2, seq, V7X_LANES), BF16),
                        pltpu.VMEM((3, seq, V7X_LANES), BF16),
                        pltpu.VMEM((3, 2, seq, V7X_LANES), BF16),
                        pltpu.VMEM((3, 3, seq, V7X_LANES), F32),
                        pltpu.VMEM((2, 3, seq, V7X_LANES), F32),
                        pltpu.VMEM((2 * ATTN_GROUP, Q_BLOCK, K_WINDOW), F32),
                        pltpu.VMEM((2 * ATTN_GROUP, Q_BLOCK, K_WINDOW), BF16)],
        compiler_params=pltpu.CompilerParams(dimension_semantics=("parallel", "parallel"),
                                             vmem_limit_bytes=V7X_VMEM_LIMIT_BYTES),
        name="dil_attn",
    )(q, k, v, bias, bias16)


def _out_ffn2_kernel(x1_ref, yh_ref, ya_ref, ga_ref, wo_ref, g3_ref, wg_ref, wu_ref, wd_ref, gf_ref,
                     out_ref, act_ref):
    ya = _rms(ya_ref[...], ga_ref[...]).astype(BF16)
    mix = _dot(yh_ref[...], wo_ref[:D_HYENA, :]) + _dot(ya, wo_ref[D_HYENA:, :])
    x2 = x1_ref[...] + mix
    h = _rms(x2, g3_ref[...]).astype(BF16)
    x3 = x2 + 0.5 * _swiglu(h, wg_ref, wu_ref, wd_ref, act_ref)
    out_ref[...] = _rms(x3, gf_ref[...])


def _out_ffn2(x1, yh, ya, ga, wo, g3, wg, wu, wd, gf):
    n = x1.shape[0]
    row = lambda w: pl.BlockSpec((OUT_ROW_TILE, w), lambda i: (i, 0))
    return pl.pallas_call(
        _out_ffn2_kernel,
        grid=(n // OUT_ROW_TILE,),
        in_specs=[row(D_MODEL), row(D_HYENA), row(D_ATTN), _resident((1, D_ATTN)), _resident(wo.shape),
                  _resident((1, D_MODEL)), _resident(wg.shape), _resident(wu.shape), _resident(wd.shape),
                  _resident((1, D_MODEL))],
        out_specs=row(D_MODEL),
        out_shape=jax.ShapeDtypeStruct((n, D_MODEL), F32),
        scratch_shapes=[pltpu.VMEM((OUT_ROW_TILE, D_FF), BF16)],
        compiler_params=pltpu.CompilerParams(dimension_semantics=("parallel",),
                                             vmem_limit_bytes=V7X_VMEM_LIMIT_BYTES),
        name="out_ffn2",
    )(x1, yh, ya, ga, wo, g3, wg, wu, wd, gf)


def _dft_tables(seq):
    n = 2 * seq
    s = np.arange(seq, dtype=np.int64)[None, :]
    turn = lambda steps: 2.0 * np.pi * (steps % (2 * n)) / (2 * n)
    theta = turn((2 * np.arange(DFT_BASE_ROWS, dtype=np.int64)[:, None] + 1) * s)
    phase = turn(2 * DFT_BASE_ROWS * np.arange(seq // DFT_BASE_ROWS, dtype=np.int64)[:, None] * s)
    tab = lambda x: jnp.asarray(x.astype(np.float32))
    c0, s0 = np.cos(theta), np.sin(theta)
    return (tab(np.stack([c0, -s0])), tab(np.stack([s0, c0])),
            tab(np.cos(phase))[:, None, :], tab(np.sin(phase))[:, None, :])


def _filter_features(seq):
    t = jnp.linspace(0.0, 1.0, seq, dtype=F32)[:, None]
    w = 2.0 * math.pi * jnp.arange(seq, dtype=F32)[:, None] / seq
    f = jnp.linspace(1e-4, FILTER_BANDS - 1, FILTER_BANDS, dtype=F32)[None, :]
    z = jnp.concatenate([t, jnp.cos(f * w), -jnp.sin(f * w)], axis=-1)
    return jnp.pad(z, ((0, 0), (0, FEAT_PAD - FILTER_EMB)))


def _decay_rates():
    max_decay = math.log(DECAY_TARGET) / FAST_DECAY_PCT
    min_decay = math.log(DECAY_TARGET) / SLOW_DECAY_PCT
    return jnp.linspace(min_decay, max_decay, D_HYENA, dtype=F32)[None, :]


def kernel(x, ffn1_norm_g, ffn1_w_gate, ffn1_w_up, ffn1_w_down, mix_norm_g, w_in, hy_conv_w, hy_conv_b, hy_filt_w1, hy_filt_b1, hy_filt_w2, hy_filt_b2, hy_filt_w3, hy_filt_b3, hy_filt_w_out, hy_filt_freq, hy_filt_skip, hy_out_norm_g, attn_out_norm_g, w_out, ffn2_norm_g, ffn2_w_gate, ffn2_w_up, ffn2_w_down, final_norm_g):
    b, seq, d = x.shape
    assert d == D_MODEL and (b * seq) % OUT_ROW_TILE == 0 and seq % (DILATIONS[-1] * Q_BLOCK) == 0
    row = lambda a: a.reshape(1, -1).astype(F32)
    f32 = lambda a: a.astype(F32)

    w1 = jnp.pad(f32(hy_filt_w1), ((0, FEAT_PAD - FILTER_EMB), (0, 0)))
    hspec, fmat, wg1, wu1, wd1, win = _hy_filter(
        seq, _filter_features(seq), w1, row(hy_filt_b1), f32(hy_filt_w2), row(hy_filt_b2), f32(hy_filt_w3),
        row(hy_filt_b3), f32(hy_filt_w_out), row(hy_filt_freq), _decay_rates(), _dft_tables(seq),
        [f32(ffn1_w_gate), f32(ffn1_w_up), f32(ffn1_w_down), f32(w_in)])

    x1, hy, q, k, v, wo, wg2, wu2, wd2 = _ffn1_proj(
        x.reshape(b * seq, d), row(ffn1_norm_g), wg1, wu1, wd1, row(mix_norm_g), win,
        [f32(w_out), f32(ffn2_w_gate), f32(ffn2_w_up), f32(ffn2_w_down)])

    conv_w = hy_conv_w.astype(F32).reshape(3, 3, D_HYENA).transpose(1, 0, 2)
    conv_b = hy_conv_b.astype(F32).reshape(3, 1, D_HYENA)
    skip = hy_filt_skip.astype(F32).reshape(2, 1, D_HYENA)
    y_hy = _hyena(hy.reshape(b, seq, 3 * D_HYENA), conv_w, conv_b, skip, row(hy_out_norm_g),
                  fmat, hspec)

    shape3 = lambda a: a.reshape(b, seq, D_ATTN)
    y_at = _dil_attn(shape3(q), shape3(k), shape3(v))

    out = _out_ffn2(x1, y_hy.reshape(b * seq, D_HYENA), y_at.reshape(b * seq, D_ATTN),
                    row(attn_out_norm_g), wo, row(ffn2_norm_g), wg2, wu2, wd2, row(final_norm_g))
    return out.reshape(b, seq, d)
```

```python
import functools
import math

import numpy as np
import jax
import jax.numpy as jnp
from jax import lax
from jax.experimental import pallas as pl
from jax.experimental.pallas import tpu as pltpu

F32 = jnp.float32
BF16 = jnp.bfloat16

D_MODEL = 1024
D_HYENA = 512
D_ATTN = 512
HEAD_DIM = 64
N_HEADS = D_ATTN // HEAD_DIM
D_FF = 2816
FILTER_EMB = 33
FILTER_BANDS = 16
FILTER_WIDTH = 64
DECAY_TARGET = 1e-2
FAST_DECAY_PCT = 0.3
SLOW_DECAY_PCT = 1.5
DILATIONS = (1, 4, 16)
HALF_WINDOW = 64
RMS_EPS = 1e-6
NEG_INF = -1e30
LOG2_E = math.log2(math.e)

V7X_LANES = 128
V7X_SUBLANES = 8
V7X_BF16_SUBLANES = 16
V7X_VMEM_LIMIT_BYTES = 56 * 1024 * 1024

ROW_TILE = 512
OUT_ROW_TILE = 1024
FF_CHUNK = 256
FREQ_BLOCK = 1024
FILTER_ROWS = 512
DFT_BASE_ROWS = 256
FILTER_PREP_STEPS = 8
Q_BLOCK = 128
K_WINDOW = 256
ATTN_GROUP = 16
CONV_ROWS = 256
GATE_SUB_ROWS = 64
FEAT_PAD = 128


def _dot(a, b):
    return jnp.dot(a, b, preferred_element_type=F32)


def _rms(x, g):
    return x * lax.rsqrt(jnp.mean(x * x, axis=-1, keepdims=True) + RMS_EPS) * g


def _swiglu(h, wg_ref, wu_ref, wd_ref, act_ref):
    for c in range(D_FF // FF_CHUNK):
        cols = slice(c * FF_CHUNK, (c + 1) * FF_CHUNK)
        g = _dot(h, wg_ref[:, cols])
        u = _dot(h, wu_ref[:, cols])
        act_ref[:, cols] = (g * jax.nn.sigmoid(g) * u).astype(BF16)
    return _dot(act_ref[...], wd_ref[...])


def _resident(shape):
    return pl.BlockSpec(shape, lambda *_: (0,) * len(shape), pipeline_mode=pl.Buffered(1))


def _cast_rider(weights, steps):
    in_specs, out_specs, out_shapes = [], [], []
    for w in weights:
        rows, cols = w.shape
        visits = 1
        while (rows * visits) % steps or (rows * visits // steps) % V7X_BF16_SUBLANES:
            visits *= 2
        slab = pl.BlockSpec((rows * visits // steps, cols), lambda i, visits=visits: (i // visits, 0))
        in_specs.append(slab)
        out_specs.append(slab)
        out_shapes.append(jax.ShapeDtypeStruct(w.shape, BF16))
    return in_specs, out_specs, out_shapes


def _cast_slabs(src_refs, dst_refs):
    for src, dst in zip(src_refs, dst_refs):
        dst[...] = src[...].astype(BF16)


def _ffn1_proj_kernel(x_ref, g1_ref, wg_ref, wu_ref, wd_ref, g2_ref, win_ref, *rest):
    n_cast = (len(rest) - 6) // 2
    x1_ref, hy_ref, q_ref, k_ref, v_ref = rest[n_cast:n_cast + 5]
    act_ref = rest[-1]
    _cast_slabs(rest[:n_cast], rest[n_cast + 5:-1])
    x = x_ref[...]
    h = _rms(x, g1_ref[...]).astype(BF16)
    x1 = x + 0.5 * _swiglu(h, wg_ref, wu_ref, wd_ref, act_ref)
    x1_ref[...] = x1
    h2 = _rms(x1, g2_ref[...]).astype(BF16)
    nh = 3 * D_HYENA
    hy_ref[...] = _dot(h2, win_ref[:, :nh])
    q_ref[...] = _dot(h2, win_ref[:, nh:nh + D_ATTN])
    k_ref[...] = _dot(h2, win_ref[:, nh + D_ATTN:nh + 2 * D_ATTN])
    v_ref[...] = _dot(h2, win_ref[:, nh + 2 * D_ATTN:])


def _ffn1_proj(x2d, g1, wg, wu, wd, g2, win, later_weights):
    n = x2d.shape[0]
    steps = n // ROW_TILE
    row = lambda w: pl.BlockSpec((ROW_TILE, w), lambda i: (i, 0))
    cast_in, cast_out, cast_shapes = _cast_rider(later_weights, steps)
    return pl.pallas_call(
        _ffn1_proj_kernel,
        grid=(steps,),
        in_specs=[row(D_MODEL), _resident((1, D_MODEL)), _resident(wg.shape), _resident(wu.shape),
                  _resident(wd.shape), _resident((1, D_MODEL)), _resident(win.shape)] + cast_in,
        out_specs=[row(D_MODEL), row(3 * D_HYENA), row(D_ATTN), row(D_ATTN), row(D_ATTN)] + cast_out,
        out_shape=[jax.ShapeDtypeStruct((n, D_MODEL), F32),
                   jax.ShapeDtypeStruct((n, 3 * D_HYENA), F32),
                   jax.ShapeDtypeStruct((n, D_ATTN), F32),
                   jax.ShapeDtypeStruct((n, D_ATTN), F32),
                   jax.ShapeDtypeStruct((n, D_ATTN), F32)] + cast_shapes,
        scratch_shapes=[pltpu.VMEM((ROW_TILE, D_FF), BF16)],
        compiler_params=pltpu.CompilerParams(dimension_semantics=("arbitrary",),
                                             vmem_limit_bytes=V7X_VMEM_LIMIT_BYTES),
        name="ffn1_proj",
    )(x2d, g1, wg, wu, wd, g2, win, *later_weights)


def _split_bf16(a):
    hi = a.astype(BF16)
    return hi, (a - hi.astype(F32)).astype(BF16)


def _dot_split(a, b):
    a_hi, a_lo = a
    b_hi, b_lo = b
    return _dot(a_hi, b_hi) + (_dot(a_hi, b_lo) + _dot(a_lo, b_hi))


def _filter_kernel(seq, feat_ref, w1_ref, b1_ref, w2_ref, b2_ref, w3_ref, b3_ref, wtop_ref, wbot_ref,
                   freq_ref, delta_ref, lead_ref, lag_ref, cosp_ref, sinp_ref, *rest):
    n_cast = (len(rest) - 5) // 2
    h_ref, f_ref = rest[n_cast], rest[n_cast + 1]
    hcat_ref, hid_ref, act_ref = rest[-3:]
    _cast_slabs(rest[:n_cast], rest[n_cast + 2:-3])
    c_ = D_HYENA
    half = seq // 2
    hb = FREQ_BLOCK // 2
    i = pl.program_id(0)

    hi = lax.Precision.HIGHEST
    layer = lambda h, w_ref, b_ref: jnp.sin(freq_ref[...] * (
        jnp.dot(h, w_ref[...], precision=hi, preferred_element_type=F32) + b_ref[...]))

    @pl.when(i == 0)
    def _():
        act_ref[...] = layer(feat_ref[...], w1_ref, b1_ref)

    @pl.when(i == 1)
    def _():
        act_ref[...] = layer(act_ref[...], w2_ref, b2_ref)

    @pl.when(i == 2)
    def _():
        hid_ref[0], hid_ref[1] = _split_bf16(layer(act_ref[...], w3_ref, b3_ref))

    n_layers = 3
    pieces = [(part, o) for part in range(2) for o in range(2)]
    assert n_layers + len(pieces) <= FILTER_PREP_STEPS
    for step, (part, o) in enumerate(pieces, start=n_layers):
        @pl.when(i == step)
        def _(part=part, o=o):
            w_ref = (wtop_ref, wbot_ref)[part]
            h = (hid_ref[0], hid_ref[1])
            row = lax.broadcasted_iota(jnp.int32, (half, c_), 0)
            t = (row[:, :1] + part * half).astype(F32) * (1.0 / (seq - 1))
            decay = jnp.exp(-t * jnp.abs(delta_ref[...]))
            rows = slice(part * half, (part + 1) * half)
            fwd = _dot_split(h, _split_bf16(w_ref[:, (2 * o) * c_:(2 * o + 1) * c_])) * decay
            bwd = _dot_split(h, _split_bf16(w_ref[:, (2 * o + 1) * c_:(2 * o + 2) * c_])) * decay
            if part == 0:
                bwd = jnp.where(row == 0, 0.0, bwd)
            hcat_ref[0, rows, o * c_:(o + 1) * c_] = (fwd + bwd).astype(BF16)
            hcat_ref[1, rows, o * c_:(o + 1) * c_] = (fwd - bwd).astype(BF16)

    @pl.when(i >= FILTER_PREP_STEPS)
    def _():
        row0 = (i - FILTER_PREP_STEPS) * FILTER_ROWS
        is_sin = (row0 // hb) % 2
        group0 = (row0 // FREQ_BLOCK) * (hb // DFT_BASE_ROWS) + (row0 % hb) // DFT_BASE_ROWS
        for g in range(FILTER_ROWS // DFT_BASE_ROWS):
            rows = slice(g * DFT_BASE_ROWS, (g + 1) * DFT_BASE_ROWS)
            f_ref[rows, :] = (lead_ref[is_sin] * cosp_ref[group0 + g]
                              - lag_ref[is_sin] * sinp_ref[group0 + g]).astype(BF16)
        p = _dot(f_ref[...], hcat_ref[is_sin]) * (1.0 / seq)
        for o in range(2):
            h_ref[o] = p[:, o * c_:(o + 1) * c_]


def _hy_filter(seq, feat, w1, b1, w2, b2, w3, b3, wout, freq, delta, dft_tables, later_weights):
    full = lambda a: _resident(a.shape)
    assert (FREQ_BLOCK // 2) % FILTER_ROWS == 0
    steps = FILTER_PREP_STEPS + 2 * seq // FILTER_ROWS
    block = lambda i: jnp.maximum(i - FILTER_PREP_STEPS, 0)
    cast_in, cast_out, cast_shapes = _cast_rider(later_weights, steps)
    twice = lambda r: jnp.concatenate([r, r], axis=1)
    diag2 = lambda w: jnp.concatenate([jnp.concatenate([w, jnp.zeros_like(w)], axis=1),
                                       jnp.concatenate([jnp.zeros_like(w), w], axis=1)], axis=0)
    feat = jnp.concatenate([feat[:seq // 2], feat[seq // 2:]], axis=1)
    w1, w2, w3 = diag2(w1), diag2(w2), diag2(w3)
    b1, b2, b3, freq = twice(b1), twice(b2), twice(b3), twice(freq)
    wtop = jnp.concatenate([wout, jnp.zeros_like(wout)], axis=0)
    wbot = jnp.concatenate([jnp.zeros_like(wout), wout], axis=0)
    return pl.pallas_call(
        functools.partial(_filter_kernel, seq),
        grid=(steps,),
        in_specs=[full(feat), full(w1), full(b1), full(w2), full(b2), full(w3), full(b3), full(wtop),
                  full(wbot), full(freq), full(delta)] + [full(t) for t in dft_tables] + cast_in,
        out_specs=[pl.BlockSpec((2, FILTER_ROWS, D_HYENA), lambda i: (0, block(i), 0)),
                   pl.BlockSpec((FILTER_ROWS, seq), lambda i: (block(i), 0))] + cast_out,
        out_shape=[jax.ShapeDtypeStruct((2, 2 * seq, D_HYENA), F32),
                   jax.ShapeDtypeStruct((2 * seq, seq), BF16)] + cast_shapes,
        scratch_shapes=[pltpu.VMEM((2, seq, 2 * D_HYENA), BF16),
                        pltpu.VMEM((2, seq // 2, 2 * FILTER_WIDTH), BF16),
                        pltpu.VMEM((seq // 2, 2 * FILTER_WIDTH), F32)],
        compiler_params=pltpu.CompilerParams(dimension_semantics=("arbitrary",),
                                             vmem_limit_bytes=V7X_VMEM_LIMIT_BYTES),
        name="hy_filter",
    )(feat, w1, b1, w2, b2, w3, b3, wtop, wbot, freq, delta, *dft_tables, *later_weights)


def _short_conv(cur, before, after, w_ref, b_ref):
    rows = cur.shape[0]
    row = lax.broadcasted_iota(jnp.int32, cur.shape, 0)
    prev = jnp.where(row == 0, before, pltpu.roll(cur, 1, axis=0))
    nxt = jnp.where(row == rows - 1, after, pltpu.roll(cur, rows - 1, axis=0))
    return b_ref[0] + prev * w_ref[0, 0:1, :] + cur * w_ref[0, 1:2, :] + nxt * w_ref[0, 2:3, :]


def _short_conv_rows(u_ref, w_ref, b_ref, t0, seq):
    zero = jnp.zeros((1, u_ref.shape[2]), F32)
    before = u_ref[0, t0 - V7X_SUBLANES:t0, :][V7X_SUBLANES - 1:] if t0 > 0 else zero
    end = t0 + CONV_ROWS
    after = u_ref[0, end:end + V7X_SUBLANES, :][:1] if end < seq else zero
    return _short_conv(u_ref[0, t0:end, :], before, after, w_ref, b_ref)


def _hyena_kernel(seq, v_ref, gate_ref, gate_lo_ref, gate_hi_ref, wv_ref, bv_ref, wg_ref, bg_ref, skip_ref,
                  gain_ref, f_ref, h_ref, out_ref, zf_ref, zb_ref, acc_ref, g_ref, y_ref):
    o = pl.program_id(1)
    j = pl.program_id(2)
    nfb = 2 * seq // FREQ_BLOCK
    gate_rows = seq // nfb
    chunks = range(0, seq, CONV_ROWS)
    hb = FREQ_BLOCK // 2

    def f_block(jb):
        return f_ref[pl.ds(pl.multiple_of(jb * FREQ_BLOCK, FREQ_BLOCK), FREQ_BLOCK), :]

    def forward(jb, slot):
        zfreq = _dot(f_block(jb), zb_ref[...])
        zr, zi = zfreq[:hb], zfreq[hb:]
        hr, hi = h_ref[0, :hb, :], h_ref[0, hb:, :]
        y_ref[slot, :, :hb] = (zr * hr - zi * hi).T.astype(BF16)
        y_ref[slot, :, hb:] = (zr * hi + zi * hr).T.astype(BF16)

    def inverse(jb, slot):
        acc_ref[...] += _dot(y_ref[slot], f_block(jb))

    def gate_chunk():
        t0 = pl.multiple_of(j * gate_rows, gate_rows)
        for r0 in range(0, gate_rows, GATE_SUB_ROWS):
            r1 = r0 + GATE_SUB_ROWS
            if r0 == 0:
                before = jnp.where(j > 0, gate_lo_ref[0, V7X_SUBLANES - 1:, :], 0.0)
            else:
                before = gate_ref[0, r0 - V7X_SUBLANES:r0, :][V7X_SUBLANES - 1:]
            if r1 == gate_rows:
                after = jnp.where(j < nfb - 1, gate_hi_ref[0, :1, :], 0.0)
            else:
                after = gate_ref[0, r1:r1 + V7X_SUBLANES, :][:1]
            g_ref[pl.ds(t0 + r0, GATE_SUB_ROWS), :] = _short_conv(gate_ref[0, r0:r1, :], before, after,
                                                                   wg_ref, bg_ref)

    @pl.when((o == 0) & (j == 0))
    def _():
        for t0 in chunks:
            z = _short_conv_rows(v_ref, wv_ref, bv_ref, t0, seq)
            zf_ref[t0:t0 + CONV_ROWS, :] = z
            zb_ref[t0:t0 + CONV_ROWS, :] = z.astype(BF16)

    @pl.when(j == 0)
    def _():
        gate_chunk()
        acc_ref[...] = jnp.zeros_like(acc_ref)
        forward(0, 0)

    for parity in range(2):
        @pl.when((j > 0) & (j < nfb) & (j % 2 == parity))
        def _(parity=parity):
            gate_chunk()
            forward(j, parity)
            inverse(j - 1, 1 - parity)

    def gated(rows):
        return g_ref[rows, :] * (acc_ref[:, rows].T + zf_ref[rows, :] * skip_ref[0])

    last_slot = (nfb - 1) % 2

    @pl.when((j == nfb) & (o == 0))
    def _():
        inverse(nfb - 1, last_slot)
        for t0 in chunks:
            rows = slice(t0, t0 + CONV_ROWS)
            z = gated(rows)
            zf_ref[rows, :] = z
            zb_ref[rows, :] = z.astype(BF16)

    @pl.when((j == nfb) & (o == 1))
    def _():
        inverse(nfb - 1, last_slot)
        for t0 in chunks:
            rows = slice(t0, t0 + CONV_ROWS)
            out_ref[0, rows, :] = _rms(gated(rows), gain_ref[...]).astype(BF16)


def _hyena(hy, conv_w, conv_b, skip, gain, fmat, hspec):
    b, seq, _ = hy.shape
    c = D_HYENA
    nfb = 2 * seq // FREQ_BLOCK
    gate_rows = seq // nfb
    halo_per_chunk = gate_rows // V7X_SUBLANES
    fwd_blk = lambda j: jnp.minimum(j, nfb - 1)
    wpart = lambda sel: pl.BlockSpec((1, 3, c), lambda bi, o, j: (sel(o), 0, 0))
    bpart = lambda sel: pl.BlockSpec((1, 1, c), lambda bi, o, j: (sel(o), 0, 0))
    value = lambda o: 0
    gate = lambda o: 1 + o
    halo = lambda first: pl.BlockSpec((1, V7X_SUBLANES, c), lambda bi, o, j: (bi, first(fwd_blk(j)), 1 + o))
    return pl.pallas_call(
        functools.partial(_hyena_kernel, seq),
        grid=(b, 2, nfb + 1),
        in_specs=[pl.BlockSpec((1, seq, c), lambda bi, o, j: (bi, 0, 0)),
                  pl.BlockSpec((1, gate_rows, c), lambda bi, o, j: (bi, fwd_blk(j), 1 + o)),
                  halo(lambda jc: jnp.maximum(jc * halo_per_chunk - 1, 0)),
                  halo(lambda jc: jnp.minimum((jc + 1) * halo_per_chunk, seq // V7X_SUBLANES - 1)),
                  wpart(value), bpart(value), wpart(gate), bpart(gate),
                  pl.BlockSpec((1, 1, c), lambda bi, o, j: (o, 0, 0)),
                  pl.BlockSpec((1, c), lambda bi, o, j: (0, 0)),
                  _resident(fmat.shape),
                  pl.BlockSpec((1, FREQ_BLOCK, c), lambda bi, o, j: (o, fwd_blk(j), 0))],
        out_specs=pl.BlockSpec((1, seq, c), lambda bi, o, j: (bi, 0, 0)),
        out_shape=jax.ShapeDtypeStruct((b, seq, c), BF16),
        scratch_shapes=[pltpu.VMEM((seq, c), F32), pltpu.VMEM((seq, c), BF16), pltpu.VMEM((c, seq), F32),
                        pltpu.VMEM((seq, c), F32), pltpu.VMEM((2, c, FREQ_BLOCK), BF16)],
        compiler_params=pltpu.CompilerParams(dimension_semantics=("parallel", "arbitrary", "arbitrary"),
                                             vmem_limit_bytes=V7X_VMEM_LIMIT_BYTES),
        name="hyena",
    )(hy, hy, hy, hy, conv_w, conv_b, conv_w, conv_b, skip, gain, fmat, hspec)


def _attn_kernel(seq, q_ref, k_ref, v_ref, bias_ref, bias16_ref, o_ref,
                 qs_ref, ks_ref, vs_ref, res_ref, nat_ref, s_ref, e_ref):
    lanes = V7X_LANES
    qscale = LOG2_E / math.sqrt(HEAD_DIM)

    for p, d in enumerate(DILATIONS):
        ls = seq // d
        first = lax.broadcasted_iota(jnp.int32, (ls, lanes), 1) < HEAD_DIM
        for r in range(d):
            src = pl.ds(r, ls, stride=d) if d > 1 else pl.ds(0, ls)
            rows = slice(r * ls, (r + 1) * ls)
            qq = q_ref[0, src, :] * qscale
            qs_ref[p, 0, rows, :] = jnp.where(first, qq, 0.0).astype(BF16)
            qs_ref[p, 1, rows, :] = jnp.where(first, 0.0, qq).astype(BF16)
            ks_ref[p, rows, :] = k_ref[0, src, :].astype(BF16)
            vv = v_ref[0, src, :]
            vs_ref[p, 0, rows, :] = jnp.where(first, vv, 1.0).astype(BF16)
            vs_ref[p, 1, rows, :] = jnp.where(first, 1.0, vv).astype(BF16)

    first = lax.broadcasted_iota(jnp.int32, (Q_BLOCK, lanes), 1) < HEAD_DIM

    def run_pattern(p, nkeys, placement):
        def body(g, carry):
            blocks = [placement(g * ATTN_GROUP + i) for i in range(ATTN_GROUP)]
            for i, (row0, krow0, bias_of_head) in enumerate(blocks):
                kw = ks_ref[p, pl.ds(krow0, nkeys), :]
                for h in range(2):
                    qh = qs_ref[p, h, pl.ds(row0, Q_BLOCK), :]
                    s = lax.dot_general(qh, kw, (((1,), (1,)), ((), ())), preferred_element_type=F32)
                    s_ref[2 * i + h, :, :nkeys] = s + bias_of_head(h)
            for i, (row0, krow0, bias_of_head) in enumerate(blocks):
                ms = []
                for h in range(2):
                    s = s_ref[2 * i + h, :, :nkeys]
                    m = jnp.max(s, axis=-1, keepdims=True)
                    e_ref[2 * i + h, :, :nkeys] = jnp.exp2(s - m).astype(BF16)
                    ms.append(m)
                res_ref[p, 1, pl.ds(row0, Q_BLOCK), :] = jnp.where(first, ms[0], ms[1])
            for i, (row0, krow0, bias_of_head) in enumerate(blocks):
                o0 = _dot(e_ref[2 * i, :, :nkeys], vs_ref[p, 0, pl.ds(krow0, nkeys), :])
                o1 = _dot(e_ref[2 * i + 1, :, :nkeys], vs_ref[p, 1, pl.ds(krow0, nkeys), :])
                res_ref[p, 0, pl.ds(row0, Q_BLOCK), :] = jnp.where(first, o0, o1)
                res_ref[p, 2, pl.ds(row0, Q_BLOCK), :] = jnp.where(first, o1, o0)
            return carry

        lax.fori_loop(0, seq // Q_BLOCK // ATTN_GROUP, body, 0)

    for p, d in enumerate(DILATIONS[:2]):
        ls = seq // d
        nblk = ls // Q_BLOCK

        def banded(n, p=p, ls=ls, nblk=nblk):
            r = n // nblk
            ib = n % nblk
            i0 = ib * Q_BLOCK
            k0 = jnp.clip(i0 - HALF_WINDOW, 0, ls - K_WINDOW)
            case = jnp.where(ib == 0, 0, jnp.where(ib == nblk - 1, 2, 1))
            row0 = pl.multiple_of(r * ls + i0, Q_BLOCK)
            krow0 = pl.multiple_of(r * ls + k0, HALF_WINDOW)
            return row0, krow0, lambda h: bias_ref[0, (p * 3 + case) * 2 + h]

        run_pattern(p, K_WINDOW, banded)

    def full(n):
        row0 = pl.multiple_of(n * Q_BLOCK, Q_BLOCK)
        return row0, row0, lambda h: bias16_ref[0, h]

    run_pattern(2, Q_BLOCK, full)

    for p, d in enumerate(DILATIONS[1:], start=1):
        ls = seq // d
        for r in range(d):
            for kind in range(3):
                nat_ref[p - 1, kind, pl.ds(r, ls, stride=d), :] = res_ref[p, kind, r * ls:(r + 1) * ls, :]

    for t0 in range(0, seq, CONV_ROWS):
        rows = slice(t0, t0 + CONV_ROWS)
        parts = [tuple(res_ref[0, kind, rows, :] for kind in range(3))]
        parts += [tuple(nat_ref[p, kind, rows, :] for kind in range(3)) for p in range(2)]
        m = functools.reduce(jnp.maximum, [pt[1] for pt in parts])
        num = 0.0
        den = 0.0
        for out, mp, lp in parts:
            w = jnp.exp2(mp - m)
            num = num + w * out
            den = den + w * pltpu.roll(lp, HEAD_DIM, axis=1)
        o_ref[0, rows, :] = num / den


def _attn_bias_tables():
    slopes = np.array([2.0 ** (-8.0 * (i + 1) / N_HEADS) for i in range(N_HEADS)], np.float32)
    slopes = jnp.asarray(slopes.reshape(N_HEADS // 2, 1, 2, 1, 1))
    qi = lax.broadcasted_iota(jnp.int32, (Q_BLOCK, K_WINDOW), 0)
    kj = lax.broadcasted_iota(jnp.int32, (Q_BLOCK, K_WINDOW), 1)
    offsets = (0, -HALF_WINDOW, -2 * HALF_WINDOW)
    dist = jnp.stack([jnp.abs(kj - qi + off) for off in offsets])
    valid = dist <= HALF_WINDOW
    dil = jnp.asarray(np.array(DILATIONS[:2], np.float32).reshape(2, 1, 1, 1))
    scaled = (dil * dist.astype(F32)[None])[None, :, :, None]
    banded = jnp.where(valid[None, None, :, None], -slopes[:, None] * scaled, NEG_INF)
    banded = banded.reshape(N_HEADS // 2, 12, Q_BLOCK, K_WINDOW)
    d16 = dist[0, :, :Q_BLOCK]
    full = jnp.where(d16 <= HALF_WINDOW, -slopes[:, 0] * (DILATIONS[2] * d16.astype(F32)), NEG_INF)
    to_base2 = lambda t: jnp.where(t > 0.5 * NEG_INF, t * LOG2_E, NEG_INF).astype(F32)
    return to_base2(banded), to_base2(full)


def _dil_attn(q, k, v):
    b, seq, _ = q.shape
    nhp = N_HEADS // 2
    bias, bias16 = _attn_bias_tables()
    head_pair = pl.BlockSpec((1, seq, V7X_LANES), lambda bi, hp: (bi, 0, hp))
    return pl.pallas_call(
        functools.partial(_attn_kernel, seq),
        grid=(b, nhp),
        in_specs=[head_pair, head_pair, head_pair,
                  pl.BlockSpec((1, 12, Q_BLOCK, K_WINDOW), lambda bi, hp: (hp, 0, 0, 0)),
                  pl.BlockSpec((1, 2, Q_BLOCK, Q_BLOCK), lambda bi, hp: (hp, 0, 0, 0))],
        out_specs=head_pair,
        out_shape=jax.ShapeDtypeStruct((b, seq, D_ATTN), F32),
        scratch_shapes=[pltpu.VMEM((3, 2, seq, V7X_LANES), BF16),
                        pltpu.VMEM((3, seq, V7X_LANES), BF16),
                        pltpu.VMEM((3, 2, seq, V7X_LANES), BF16),
                        pltpu.VMEM((3, 3, seq, V7X_LANES), F32),
                        pltpu.VMEM((2, 3, seq, V7X_LANES), F32),
                        pltpu.VMEM((2 * ATTN_GROUP, Q_BLOCK, K_WINDOW), F32),
                        pltpu.VMEM((2 * ATTN_GROUP, Q_BLOCK, K_WINDOW), BF16)],
        compiler_params=pltpu.CompilerParams(dimension_semantics=("parallel", "parallel"),
                                             vmem_limit_bytes=V7X_VMEM_LIMIT_BYTES),
        name="dil_attn",
    )(q, k, v, bias, bias16)


def _out_ffn2_kernel(x1_ref, yh_ref, ya_ref, ga_ref, wo_ref, g3_ref, wg_ref, wu_ref, wd_ref, gf_ref,
                     out_ref, act_ref):
    ya = _rms(ya_ref[...], ga_ref[...]).astype(BF16)
    mix = _dot(yh_ref[...], wo_ref[:D_HYENA, :]) + _dot(ya, wo_ref[D_HYENA:, :])
    x2 = x1_ref[...] + mix
    h = _rms(x2, g3_ref[...]).astype(BF16)
    x3 = x2 + 0.5 * _swiglu(h, wg_ref, wu_ref, wd_ref, act_ref)
    out_ref[...] = _rms(x3, gf_ref[...])


def _out_ffn2(x1, yh, ya, ga, wo, g3, wg, wu, wd, gf):
    n = x1.shape[0]
    row = lambda w: pl.BlockSpec((OUT_ROW_TILE, w), lambda i: (i, 0))
    return pl.pallas_call(
        _out_ffn2_kernel,
        grid=(n // OUT_ROW_TILE,),
        in_specs=[row(D_MODEL), row(D_HYENA), row(D_ATTN), _resident((1, D_ATTN)), _resident(wo.shape),
                  _resident((1, D_MODEL)), _resident(wg.shape), _resident(wu.shape), _resident(wd.shape),
                  _resident((1, D_MODEL))],
        out_specs=row(D_MODEL),
        out_shape=jax.ShapeDtypeStruct((n, D_MODEL), F32),
        scratch_shapes=[pltpu.VMEM((OUT_ROW_TILE, D_FF), BF16)],
        compiler_params=pltpu.CompilerParams(dimension_semantics=("parallel",),
                                             vmem_limit_bytes=V7X_VMEM_LIMIT_BYTES),
        name="out_ffn2",
    )(x1, yh, ya, ga, wo, g3, wg, wu, wd, gf)


def _dft_tables(seq):
    n = 2 * seq
    s = np.arange(seq, dtype=np.int64)[None, :]
    turn = lambda steps: 2.0 * np.pi * (steps % (2 * n)) / (2 * n)
    theta = turn((2 * np.arange(DFT_BASE_ROWS, dtype=np.int64)[:, None] + 1) * s)
    phase = turn(2 * DFT_BASE_ROWS * np.arange(seq // DFT_BASE_ROWS, dtype=np.int64)[:, None] * s)
    tab = lambda x: jnp.asarray(x.astype(np.float32))
    c0, s0 = np.cos(theta), np.sin(theta)
    return (tab(np.stack([c0, -s0])), tab(np.stack([s0, c0])),
            tab(np.cos(phase))[:, None, :], tab(np.sin(phase))[:, None, :])


def _filter_features(seq):
    t = jnp.linspace(0.0, 1.0, seq, dtype=F32)[:, None]
    w = 2.0 * math.pi * jnp.arange(seq, dtype=F32)[:, None] / seq
    f = jnp.linspace(1e-4, FILTER_BANDS - 1, FILTER_BANDS, dtype=F32)[None, :]
    z = jnp.concatenate([t, jnp.cos(f * w), -jnp.sin(f * w)], axis=-1)
    return jnp.pad(z, ((0, 0), (0, FEAT_PAD - FILTER_EMB)))


def _decay_rates():
    max_decay = math.log(DECAY_TARGET) / FAST_DECAY_PCT
    min_decay = math.log(DECAY_TARGET) / SLOW_DECAY_PCT
    return jnp.linspace(min_decay, max_decay, D_HYENA, dtype=F32)[None, :]


def kernel(x, ffn1_norm_g, ffn1_w_gate, ffn1_w_up, ffn1_w_down, mix_norm_g, w_in, hy_conv_w, hy_conv_b, hy_filt_w1, hy_filt_b1, hy_filt_w2, hy_filt_b2, hy_filt_w3, hy_filt_b3, hy_filt_w_out, hy_filt_freq, hy_filt_skip, hy_out_norm_g, attn_out_norm_g, w_out, ffn2_norm_g, ffn2_w_gate, ffn2_w_up, ffn2_w_down, final_norm_g):
    b, seq, d = x.shape
    assert d == D_MODEL and (b * seq) % OUT_ROW_TILE == 0 and seq % (DILATIONS[-1] * Q_BLOCK) == 0
    row = lambda a: a.reshape(1, -1).astype(F32)
    f32 = lambda a: a.astype(F32)

    w1 = jnp.pad(f32(hy_filt_w1), ((0, FEAT_PAD - FILTER_EMB), (0, 0)))
    hspec, fmat, wg1, wu1, wd1, win = _hy_filter(
        seq, _filter_features(seq), w1, row(hy_filt_b1), f32(hy_filt_w2), row(hy_filt_b2), f32(hy_filt_w3),
        row(hy_filt_b3), f32(hy_filt_w_out), row(hy_filt_freq), _decay_rates(), _dft_tables(seq),
        [f32(ffn1_w_gate), f32(ffn1_w_up), f32(ffn1_w_down), f32(w_in)])

    x1, hy, q, k, v, wo, wg2, wu2, wd2 = _ffn1_proj(
        x.reshape(b * seq, d), row(ffn1_norm_g), wg1, wu1, wd1, row(mix_norm_g), win,
        [f32(w_out), f32(ffn2_w_gate), f32(ffn2_w_up), f32(ffn2_w_down)])

    conv_w = hy_conv_w.astype(F32).reshape(3, 3, D_HYENA).transpose(1, 0, 2)
    conv_b = hy_conv_b.astype(F32).reshape(3, 1, D_HYENA)
    skip = hy_filt_skip.astype(F32).reshape(2, 1, D_HYENA)
    y_hy = _hyena(hy.reshape(b, seq, 3 * D_HYENA), conv_w, conv_b, skip, row(hy_out_norm_g),
                  fmat, hspec)

    shape3 = lambda a: a.reshape(b, seq, D_ATTN)
    y_at = _dil_attn(shape3(q), shape3(k), shape3(v))

    out = _out_ffn2(x1, y_hy.reshape(b * seq, D_HYENA), y_at.reshape(b * seq, D_ATTN),
                    row(attn_out_norm_g), wo, row(ffn2_norm_g), wg2, wu2, wd2, row(final_norm_g))
    return out.reshape(b, seq, d)
```

```python
import functools
import math

import numpy as np
import jax
import jax.numpy as jnp
from jax import lax
from jax.experimental import pallas as pl
from jax.experimental.pallas import tpu as pltpu

F32 = jnp.float32
BF16 = jnp.bfloat16

D_MODEL = 1024
D_HYENA = 512
D_ATTN = 512
HEAD_DIM = 64
N_HEADS = D_ATTN // HEAD_DIM
D_FF = 2816
FILTER_EMB = 33
FILTER_BANDS = 16
FILTER_WIDTH = 64
DECAY_TARGET = 1e-2
FAST_DECAY_PCT = 0.3
SLOW_DECAY_PCT = 1.5
DILATIONS = (1, 4, 16)
HALF_WINDOW = 64
RMS_EPS = 1e-6
NEG_INF = -1e30
LOG2_E = math.log2(math.e)

V7X_LANES = 128
V7X_SUBLANES = 8
V7X_BF16_SUBLANES = 16
V7X_VMEM_LIMIT_BYTES = 56 * 1024 * 1024

ROW_TILE = 512
OUT_ROW_TILE = 1024
FF_CHUNK = 256
FREQ_BLOCK = 1024
FILTER_ROWS = 512
DFT_BASE_ROWS = 256
FILTER_PREP_STEPS = 8
Q_BLOCK = 128
K_WINDOW = 256
ATTN_GROUP = 16
CONV_ROWS = 256
GATE_SUB_ROWS = 64
FEAT_PAD = 128


def _dot(a, b):
    return jnp.dot(a, b, preferred_element_type=F32)


def _rms(x, g):
    return x * lax.rsqrt(jnp.mean(x * x, axis=-1, keepdims=True) + RMS_EPS) * g


def _swiglu(h, wg_ref, wu_ref, wd_ref, act_ref):
    for c in range(D_FF // FF_CHUNK):
        cols = slice(c * FF_CHUNK, (c + 1) * FF_CHUNK)
        g = _dot(h, wg_ref[:, cols])
        u = _dot(h, wu_ref[:, cols])
        act_ref[:, cols] = (g * jax.nn.sigmoid(g) * u).astype(BF16)
    return _dot(act_ref[...], wd_ref[...])


def _resident(shape):
    return pl.BlockSpec(shape, lambda *_: (0,) * len(shape), pipeline_mode=pl.Buffered(1))


def _cast_rider(weights, steps):
    in_specs, out_specs, out_shapes = [], [], []
    for w in weights:
        rows, cols = w.shape
        visits = 1
        while (rows * visits) % steps or (rows * visits // steps) % V7X_BF16_SUBLANES:
            visits *= 2
        slab = pl.BlockSpec((rows * visits // steps, cols), lambda i, visits=visits: (i // visits, 0))
        in_specs.append(slab)
        out_specs.append(slab)
        out_shapes.append(jax.ShapeDtypeStruct(w.shape, BF16))
    return in_specs, out_specs, out_shapes


def _cast_slabs(src_refs, dst_refs):
    for src, dst in zip(src_refs, dst_refs):
        dst[...] = src[...].astype(BF16)


def _ffn1_proj_kernel(x_ref, g1_ref, wg_ref, wu_ref, wd_ref, g2_ref, win_ref, *rest):
    n_cast = (len(rest) - 6) // 2
    x1_ref, hy_ref, q_ref, k_ref, v_ref = rest[n_cast:n_cast + 5]
    act_ref = rest[-1]
    _cast_slabs(rest[:n_cast], rest[n_cast + 5:-1])
    x = x_ref[...]
    h = _rms(x, g1_ref[...]).astype(BF16)
    x1 = x + 0.5 * _swiglu(h, wg_ref, wu_ref, wd_ref, act_ref)
    x1_ref[...] = x1
    h2 = _rms(x1, g2_ref[...]).astype(BF16)
    nh = 3 * D_HYENA
    hy_ref[...] = _dot(h2, win_ref[:, :nh])
    q_ref[...] = _dot(h2, win_ref[:, nh:nh + D_ATTN])
    k_ref[...] = _dot(h2, win_ref[:, nh + D_ATTN:nh + 2 * D_ATTN])
    v_ref[...] = _dot(h2, win_ref[:, nh + 2 * D_ATTN:])


def _ffn1_proj(x2d, g1, wg, wu, wd, g2, win, later_weights):
    n = x2d.shape[0]
    steps = n // ROW_TILE
    row = lambda w: pl.BlockSpec((ROW_TILE, w), lambda i: (i, 0))
    cast_in, cast_out, cast_shapes = _cast_rider(later_weights, steps)
    return pl.pallas_call(
        _ffn1_proj_kernel,
        grid=(steps,),
        in_specs=[row(D_MODEL), _resident((1, D_MODEL)), _resident(wg.shape), _resident(wu.shape),
                  _resident(wd.shape), _resident((1, D_MODEL)), _resident(win.shape)] + cast_in,
        out_specs=[row(D_MODEL), row(3 * D_HYENA), row(D_ATTN), row(D_ATTN), row(D_ATTN)] + cast_out,
        out_shape=[jax.ShapeDtypeStruct((n, D_MODEL), F32),
                   jax.ShapeDtypeStruct((n, 3 * D_HYENA), F32),
                   jax.ShapeDtypeStruct((n, D_ATTN), F32),
                   jax.ShapeDtypeStruct((n, D_ATTN), F32),
                   jax.ShapeDtypeStruct((n, D_ATTN), F32)] + cast_shapes,
        scratch_shapes=[pltpu.VMEM((ROW_TILE, D_FF), BF16)],
        compiler_params=pltpu.CompilerParams(dimension_semantics=("arbitrary",),
                                             vmem_limit_bytes=V7X_VMEM_LIMIT_BYTES),
        name="ffn1_proj",
    )(x2d, g1, wg, wu, wd, g2, win, *later_weights)


def _split_bf16(a):
    hi = a.astype(BF16)
    return hi, (a - hi.astype(F32)).astype(BF16)


def _dot_split(a, b):
    a_hi, a_lo = a
    b_hi, b_lo = b
    return _dot(a_hi, b_hi) + (_dot(a_hi, b_lo) + _dot(a_lo, b_hi))


def _filter_kernel(seq, feat_ref, w1_ref, b1_ref, w2_ref, b2_ref, w3_ref, b3_ref, wtop_ref, wbot_ref,
                   freq_ref, delta_ref, lead_ref, lag_ref, cosp_ref, sinp_ref, *rest):
    n_cast = (len(rest) - 5) // 2
    h_ref, f_ref = rest[n_cast], rest[n_cast + 1]
    hcat_ref, hid_ref, act_ref = rest[-3:]
    _cast_slabs(rest[:n_cast], rest[n_cast + 2:-3])
    c_ = D_HYENA
    half = seq // 2
    hb = FREQ_BLOCK // 2
    i = pl.program_id(0)

    hi = lax.Precision.HIGHEST
    layer = lambda h, w_ref, b_ref: jnp.sin(freq_ref[...] * (
        jnp.dot(h, w_ref[...], precision=hi, preferred_element_type=F32) + b_ref[...]))

    @pl.when(i == 0)
    def _():
        act_ref[...] = layer(feat_ref[...], w1_ref, b1_ref)

    @pl.when(i == 1)
    def _():
        act_ref[...] = layer(act_ref[...], w2_ref, b2_ref)

    @pl.when(i == 2)
    def _():
        hid_ref[0], hid_ref[1] = _split_bf16(layer(act_ref[...], w3_ref, b3_ref))

    n_layers = 3
    pieces = [(part, o) for part in range(2) for o in range(2)]
    assert n_layers + len(pieces) <= FILTER_PREP_STEPS
    for step, (part, o) in enumerate(pieces, start=n_layers):
        @pl.when(i == step)
        def _(part=part, o=o):
            w_ref = (wtop_ref, wbot_ref)[part]
            h = (hid_ref[0], hid_ref[1])
            row = lax.broadcasted_iota(jnp.int32, (half, c_), 0)
            t = (row[:, :1] + part * half).astype(F32) * (1.0 / (seq - 1))
            decay = jnp.exp(-t * jnp.abs(delta_ref[...]))
            rows = slice(part * half, (part + 1) * half)
            fwd = _dot_split(h, _split_bf16(w_ref[:, (2 * o) * c_:(2 * o + 1) * c_])) * decay
            bwd = _dot_split(h, _split_bf16(w_ref[:, (2 * o + 1) * c_:(2 * o + 2) * c_])) * decay
            if part == 0:
                bwd = jnp.where(row == 0, 0.0, bwd)
            hcat_ref[0, rows, o * c_:(o + 1) * c_] = (fwd + bwd).astype(BF16)
            hcat_ref[1, rows, o * c_:(o + 1) * c_] = (fwd - bwd).astype(BF16)

    @pl.when(i >= FILTER_PREP_STEPS)
    def _():
        row0 = (i - FILTER_PREP_STEPS) * FILTER_ROWS
        is_sin = (row0 // hb) % 2
        group0 = (row0 // FREQ_BLOCK) * (hb // DFT_BASE_ROWS) + (row0 % hb) // DFT_BASE_ROWS
        for g in range(FILTER_ROWS // DFT_BASE_ROWS):
            rows = slice(g * DFT_BASE_ROWS, (g + 1) * DFT_BASE_ROWS)
            f_ref[rows, :] = (lead_ref[is_sin] * cosp_ref[group0 + g]
                              - lag_ref[is_sin] * sinp_ref[group0 + g]).astype(BF16)
        p = _dot(f_ref[...], hcat_ref[is_sin]) * (1.0 / seq)
        for o in range(2):
            h_ref[o] = p[:, o * c_:(o + 1) * c_]


def _hy_filter(seq, feat, w1, b1, w2, b2, w3, b3, wout, freq, delta, dft_tables, later_weights):
    full = lambda a: _resident(a.shape)
    assert (FREQ_BLOCK // 2) % FILTER_ROWS == 0
    steps = FILTER_PREP_STEPS + 2 * seq // FILTER_ROWS
    block = lambda i: jnp.maximum(i - FILTER_PREP_STEPS, 0)
    cast_in, cast_out, cast_shapes = _cast_rider(later_weights, steps)
    twice = lambda r: jnp.concatenate([r, r], axis=1)
    diag2 = lambda w: jnp.concatenate([jnp.concatenate([w, jnp.zeros_like(w)], axis=1),
                                       jnp.concatenate([jnp.zeros_like(w), w], axis=1)], axis=0)
    feat = jnp.concatenate([feat[:seq // 2], feat[seq // 2:]], axis=1)
    w1, w2, w3 = diag2(w1), diag2(w2), diag2(w3)
    b1, b2, b3, freq = twice(b1), twice(b2), twice(b3), twice(freq)
    wtop = jnp.concatenate([wout, jnp.zeros_like(wout)], axis=0)
    wbot = jnp.concatenate([jnp.zeros_like(wout), wout], axis=0)
    return pl.pallas_call(
        functools.partial(_filter_kernel, seq),
        grid=(steps,),
        in_specs=[full(feat), full(w1), full(b1), full(w2), full(b2), full(w3), full(b3), full(wtop),
                  full(wbot), full(freq), full(delta)] + [full(t) for t in dft_tables] + cast_in,
        out_specs=[pl.BlockSpec((2, FILTER_ROWS, D_HYENA), lambda i: (0, block(i), 0)),
                   pl.BlockSpec((FILTER_ROWS, seq), lambda i: (block(i), 0))] + cast_out,
        out_shape=[jax.ShapeDtypeStruct((2, 2 * seq, D_HYENA), F32),
                   jax.ShapeDtypeStruct((2 * seq, seq), BF16)] + cast_shapes,
        scratch_shapes=[pltpu.VMEM((2, seq, 2 * D_HYENA), BF16),
                        pltpu.VMEM((2, seq // 2, 2 * FILTER_WIDTH), BF16),
                        pltpu.VMEM((seq // 2, 2 * FILTER_WIDTH), F32)],
        compiler_params=pltpu.CompilerParams(dimension_semantics=("arbitrary",),
                                             vmem_limit_bytes=V7X_VMEM_LIMIT_BYTES),
        name="hy_filter",
    )(feat, w1, b1, w2, b2, w3, b3, wtop, wbot, freq, delta, *dft_tables, *later_weights)


def _short_conv(cur, before, after, w_ref, b_ref):
    rows = cur.shape[0]
    row = lax.broadcasted_iota(jnp.int32, cur.shape, 0)
    prev = jnp.where(row == 0, before, pltpu.roll(cur, 1, axis=0))
    nxt = jnp.where(row == rows - 1, after, pltpu.roll(cur, rows - 1, axis=0))
    return b_ref[0] + prev * w_ref[0, 0:1, :] + cur * w_ref[0, 1:2, :] + nxt * w_ref[0, 2:3, :]


def _short_conv_rows(u_ref, w_ref, b_ref, t0, seq):
    zero = jnp.zeros((1, u_ref.shape[2]), F32)
    before = u_ref[0, t0 - V7X_SUBLANES:t0, :][V7X_SUBLANES - 1:] if t0 > 0 else zero
    end = t0 + CONV_ROWS
    after = u_ref[0, end:end + V7X_SUBLANES, :][:1] if end < seq else zero
    return _short_conv(u_ref[0, t0:end, :], before, after, w_ref, b_ref)


def _hyena_kernel(seq, v_ref, gate_ref, gate_lo_ref, gate_hi_ref, wv_ref, bv_ref, wg_ref, bg_ref, skip_ref,
                  gain_ref, f_ref, h_ref, out_ref, zf_ref, zb_ref, acc_ref, g_ref, y_ref):
    o = pl.program_id(1)
    j = pl.program_id(2)
    nfb = 2 * seq // FREQ_BLOCK
    gate_rows = seq // nfb
    chunks = range(0, seq, CONV_ROWS)
    hb = FREQ_BLOCK // 2

    def f_block(jb):
        return f_ref[jb * FREQ_BLOCK:(jb + 1) * FREQ_BLOCK, :]

    def forward(jb):
        zfreq = _dot(f_block(jb), zb_ref[...])
        zr, zi = zfreq[:hb], zfreq[hb:]
        hr, hi = h_ref[0, :hb, :], h_ref[0, hb:, :]
        y_ref[jb % 2, :, :hb] = (zr * hr - zi * hi).T.astype(BF16)
        y_ref[jb % 2, :, hb:] = (zr * hi + zi * hr).T.astype(BF16)

    def inverse(jb):
        contrib = _dot(y_ref[jb % 2], f_block(jb))
        if jb == 0:
            acc_ref[...] = contrib
        else:
            acc_ref[...] += contrib

    def gate_chunk():
        t0 = pl.multiple_of(j * gate_rows, gate_rows)
        for r0 in range(0, gate_rows, GATE_SUB_ROWS):
            r1 = r0 + GATE_SUB_ROWS
            if r0 == 0:
                before = jnp.where(j > 0, gate_lo_ref[0, V7X_SUBLANES - 1:, :], 0.0)
            else:
                before = gate_ref[0, r0 - V7X_SUBLANES:r0, :][V7X_SUBLANES - 1:]
            if r1 == gate_rows:
                after = jnp.where(j < nfb - 1, gate_hi_ref[0, :1, :], 0.0)
            else:
                after = gate_ref[0, r1:r1 + V7X_SUBLANES, :][:1]
            g_ref[pl.ds(t0 + r0, GATE_SUB_ROWS), :] = _short_conv(gate_ref[0, r0:r1, :], before, after,
                                                                   wg_ref, bg_ref)

    @pl.when((o == 0) & (j == 0))
    def _():
        for t0 in chunks:
            z = _short_conv_rows(v_ref, wv_ref, bv_ref, t0, seq)
            zf_ref[t0:t0 + CONV_ROWS, :] = z
            zb_ref[t0:t0 + CONV_ROWS, :] = z.astype(BF16)

    @pl.when(j == 0)
    def _():
        gate_chunk()
        forward(0)

    for step in range(1, nfb):
        @pl.when(j == step)
        def _(step=step):
            gate_chunk()
            forward(step)
            inverse(step - 1)

    def gated(rows):
        return g_ref[rows, :] * (acc_ref[:, rows].T + zf_ref[rows, :] * skip_ref[0])

    @pl.when((j == nfb) & (o == 0))
    def _():
        inverse(nfb - 1)
        for t0 in chunks:
            rows = slice(t0, t0 + CONV_ROWS)
            z = gated(rows)
            zf_ref[rows, :] = z
            zb_ref[rows, :] = z.astype(BF16)

    @pl.when((j == nfb) & (o == 1))
    def _():
        inverse(nfb - 1)
        for t0 in chunks:
            rows = slice(t0, t0 + CONV_ROWS)
            out_ref[0, rows, :] = _rms(gated(rows), gain_ref[...]).astype(BF16)


def _hyena(hy, conv_w, conv_b, skip, gain, fmat, hspec):
    b, seq, _ = hy.shape
    c = D_HYENA
    nfb = 2 * seq // FREQ_BLOCK
    gate_rows = seq // nfb
    halo_per_chunk = gate_rows // V7X_SUBLANES
    fwd_blk = lambda j: jnp.minimum(j, nfb - 1)
    wpart = lambda sel: pl.BlockSpec((1, 3, c), lambda bi, o, j: (sel(o), 0, 0))
    bpart = lambda sel: pl.BlockSpec((1, 1, c), lambda bi, o, j: (sel(o), 0, 0))
    value = lambda o: 0
    gate = lambda o: 1 + o
    halo = lambda first: pl.BlockSpec((1, V7X_SUBLANES, c), lambda bi, o, j: (bi, first(fwd_blk(j)), 1 + o))
    return pl.pallas_call(
        functools.partial(_hyena_kernel, seq),
        grid=(b, 2, nfb + 1),
        in_specs=[pl.BlockSpec((1, seq, c), lambda bi, o, j: (bi, 0, 0)),
                  pl.BlockSpec((1, gate_rows, c), lambda bi, o, j: (bi, fwd_blk(j), 1 + o)),
                  halo(lambda jc: jnp.maximum(jc * halo_per_chunk - 1, 0)),
                  halo(lambda jc: jnp.minimum((jc + 1) * halo_per_chunk, seq // V7X_SUBLANES - 1)),
                  wpart(value), bpart(value), wpart(gate), bpart(gate),
                  pl.BlockSpec((1, 1, c), lambda bi, o, j: (o, 0, 0)),
                  pl.BlockSpec((1, c), lambda bi, o, j: (0, 0)),
                  _resident(fmat.shape),
                  pl.BlockSpec((1, FREQ_BLOCK, c), lambda bi, o, j: (o, fwd_blk(j), 0))],
        out_specs=pl.BlockSpec((1, seq, c), lambda bi, o, j: (bi, 0, 0)),
        out_shape=jax.ShapeDtypeStruct((b, seq, c), BF16),
        scratch_shapes=[pltpu.VMEM((seq, c), F32), pltpu.VMEM((seq, c), BF16), pltpu.VMEM((c, seq), F32),
                        pltpu.VMEM((seq, c), F32), pltpu.VMEM((2, c, FREQ_BLOCK), BF16)],
        compiler_params=pltpu.CompilerParams(dimension_semantics=("parallel", "arbitrary", "arbitrary"),
                                             vmem_limit_bytes=V7X_VMEM_LIMIT_BYTES),
        name="hyena",
    )(hy, hy, hy, hy, conv_w, conv_b, conv_w, conv_b, skip, gain, fmat, hspec)


def _attn_kernel(seq, q_ref, k_ref, v_ref, bias_ref, bias16_ref, o_ref,
                 qs_ref, ks_ref, vs_ref, res_ref, nat_ref, s_ref, e_ref):
    lanes = V7X_LANES
    qscale = LOG2_E / math.sqrt(HEAD_DIM)

    for p, d in enumerate(DILATIONS):
        ls = seq // d
        first = lax.broadcasted_iota(jnp.int32, (ls, lanes), 1) < HEAD_DIM
        for r in range(d):
            src = pl.ds(r, ls, stride=d) if d > 1 else pl.ds(0, ls)
            rows = slice(r * ls, (r + 1) * ls)
            qq = q_ref[0, src, :] * qscale
            qs_ref[p, 0, rows, :] = jnp.where(first, qq, 0.0).astype(BF16)
            qs_ref[p, 1, rows, :] = jnp.where(first, 0.0, qq).astype(BF16)
            ks_ref[p, rows, :] = k_ref[0, src, :].astype(BF16)
            vv = v_ref[0, src, :]
            vs_ref[p, 0, rows, :] = jnp.where(first, vv, 1.0).astype(BF16)
            vs_ref[p, 1, rows, :] = jnp.where(first, 1.0, vv).astype(BF16)

    first = lax.broadcasted_iota(jnp.int32, (Q_BLOCK, lanes), 1) < HEAD_DIM

    def run_pattern(p, nkeys, placement):
        def body(g, carry):
            blocks = [placement(g * ATTN_GROUP + i) for i in range(ATTN_GROUP)]
            for i, (row0, krow0, bias_of_head) in enumerate(blocks):
                kw = ks_ref[p, pl.ds(krow0, nkeys), :]
                for h in range(2):
                    qh = qs_ref[p, h, pl.ds(row0, Q_BLOCK), :]
                    s = lax.dot_general(qh, kw, (((1,), (1,)), ((), ())), preferred_element_type=F32)
                    s_ref[2 * i + h, :, :nkeys] = s + bias_of_head(h)
            for i, (row0, krow0, bias_of_head) in enumerate(blocks):
                ms = []
                for h in range(2):
                    s = s_ref[2 * i + h, :, :nkeys]
                    m = jnp.max(s, axis=-1, keepdims=True)
                    e_ref[2 * i + h, :, :nkeys] = jnp.exp2(s - m).astype(BF16)
                    ms.append(m)
                res_ref[p, 1, pl.ds(row0, Q_BLOCK), :] = jnp.where(first, ms[0], ms[1])
            for i, (row0, krow0, bias_of_head) in enumerate(blocks):
                o0 = _dot(e_ref[2 * i, :, :nkeys], vs_ref[p, 0, pl.ds(krow0, nkeys), :])
                o1 = _dot(e_ref[2 * i + 1, :, :nkeys], vs_ref[p, 1, pl.ds(krow0, nkeys), :])
                res_ref[p, 0, pl.ds(row0, Q_BLOCK), :] = jnp.where(first, o0, o1)
                res_ref[p, 2, pl.ds(row0, Q_BLOCK), :] = jnp.where(first, o1, o0)
            return carry

        lax.fori_loop(0, seq // Q_BLOCK // ATTN_GROUP, body, 0)

    for p, d in enumerate(DILATIONS[:2]):
        ls = seq // d
        nblk = ls // Q_BLOCK

        def banded(n, p=p, ls=ls, nblk=nblk):
            r = n // nblk
            ib = n % nblk
            i0 = ib * Q_BLOCK
            k0 = jnp.clip(i0 - HALF_WINDOW, 0, ls - K_WINDOW)
            case = jnp.where(ib == 0, 0, jnp.where(ib == nblk - 1, 2, 1))
            row0 = pl.multiple_of(r * ls + i0, Q_BLOCK)
            krow0 = pl.multiple_of(r * ls + k0, HALF_WINDOW)
            return row0, krow0, lambda h: bias_ref[0, (p * 3 + case) * 2 + h]

        run_pattern(p, K_WINDOW, banded)

    def full(n):
        row0 = pl.multiple_of(n * Q_BLOCK, Q_BLOCK)
        return row0, row0, lambda h: bias16_ref[0, h]

    run_pattern(2, Q_BLOCK, full)

    for p, d in enumerate(DILATIONS[1:], start=1):
        ls = seq // d
        for r in range(d):
            for kind in range(3):
                nat_ref[p - 1, kind, pl.ds(r, ls, stride=d), :] = res_ref[p, kind, r * ls:(r + 1) * ls, :]

    for t0 in range(0, seq, CONV_ROWS):
        rows = slice(t0, t0 + CONV_ROWS)
        parts = [tuple(res_ref[0, kind, rows, :] for kind in range(3))]
        parts += [tuple(nat_ref[p, kind, rows, :] for kind in range(3)) for p in range(2)]
        m = functools.reduce(jnp.maximum, [pt[1] for pt in parts])
        num = 0.0
        den = 0.0
        for out, mp, lp in parts:
            w = jnp.exp2(mp - m)
            num = num + w * out
            den = den + w * pltpu.roll(lp, HEAD_DIM, axis=1)
        o_ref[0, rows, :] = num / den


def _attn_bias_tables():
    slopes = np.array([2.0 ** (-8.0 * (i + 1) / N_HEADS) for i in range(N_HEADS)], np.float32)
    slopes = jnp.asarray(slopes.reshape(N_HEADS // 2, 1, 2, 1, 1))
    qi = lax.broadcasted_iota(jnp.int32, (Q_BLOCK, K_WINDOW), 0)
    kj = lax.broadcasted_iota(jnp.int32, (Q_BLOCK, K_WINDOW), 1)
    offsets = (0, -HALF_WINDOW, -2 * HALF_WINDOW)
    dist = jnp.stack([jnp.abs(kj - qi + off) for off in offsets])
    valid = dist <= HALF_WINDOW
    dil = jnp.asarray(np.array(DILATIONS[:2], np.float32).reshape(2, 1, 1, 1))
    scaled = (dil * dist.astype(F32)[None])[None, :, :, None]
    banded = jnp.where(valid[None, None, :, None], -slopes[:, None] * scaled, NEG_INF)
    banded = banded.reshape(N_HEADS // 2, 12, Q_BLOCK, K_WINDOW)
    d16 = dist[0, :, :Q_BLOCK]
    full = jnp.where(d16 <= HALF_WINDOW, -slopes[:, 0] * (DILATIONS[2] * d16.astype(F32)), NEG_INF)
    to_base2 = lambda t: jnp.where(t > 0.5 * NEG_INF, t * LOG2_E, NEG_INF).astype(F32)
    return to_base2(banded), to_base2(full)


def _dil_attn(q, k, v):
    b, seq, _ = q.shape
    nhp = N_HEADS // 2
    bias, bias16 = _attn_bias_tables()
    head_pair = pl.BlockSpec((1, seq, V7X_LANES), lambda bi, hp: (bi, 0, hp))
    return pl.pallas_call(
        functools.partial(_attn_kernel, seq),
        grid=(b, nhp),
        in_specs=[head_pair, head_pair, head_pair,
                  pl.BlockSpec((1, 12, Q_BLOCK, K_WINDOW), lambda bi, hp: (hp, 0, 0, 0)),
                  pl.BlockSpec((1, 2, Q_BLOCK, Q_BLOCK), lambda bi, hp: (hp, 0, 0, 0))],
        out_specs=head_pair,
        out_shape=jax.ShapeDtypeStruct((b, seq, D_ATTN), F32),
        scratch_shapes=[pltpu.VMEM((3, 2, seq, V7X_LANES), BF16),
                        pltpu.VMEM((3, seq, V7X_LANES), BF16),
                        pltpu.VMEM((3, 2, seq, V7X_LANES), BF16),
                        pltpu.VMEM((3, 3, seq, V7X_LANES), F32),
                        pltpu.VMEM((2, 3, seq, V7X_LANES), F32),
                        pltpu.VMEM((2 * ATTN_GROUP, Q_BLOCK, K_WINDOW), F32),
                        pltpu.VMEM((2 * ATTN_GROUP, Q_BLOCK, K_WINDOW), BF16)],
        compiler_params=pltpu.CompilerParams(dimension_semantics=("parallel", "parallel"),
                                             vmem_limit_bytes=V7X_VMEM_LIMIT_BYTES),
        name="dil_attn",
    )(q, k, v, bias, bias16)


def _out_ffn2_kernel(x1_ref, yh_ref, ya_ref, ga_ref, wo_ref, g3_ref, wg_ref, wu_ref, wd_ref, gf_ref,
                     out_ref, act_ref):
    ya = _rms(ya_ref[...], ga_ref[...]).astype(BF16)
    mix = _dot(yh_ref[...], wo_ref[:D_HYENA, :]) + _dot(ya, wo_ref[D_HYENA:, :])
    x2 = x1_ref[...] + mix
    h = _rms(x2, g3_ref[...]).astype(BF16)
    x3 = x2 + 0.5 * _swiglu(h, wg_ref, wu_ref, wd_ref, act_ref)
    out_ref[...] = _rms(x3, gf_ref[...])


def _out_ffn2(x1, yh, ya, ga, wo, g3, wg, wu, wd, gf):
    n = x1.shape[0]
    row = lambda w: pl.BlockSpec((OUT_ROW_TILE, w), lambda i: (i, 0))
    return pl.pallas_call(
        _out_ffn2_kernel,
        grid=(n // OUT_ROW_TILE,),
        in_specs=[row(D_MODEL), row(D_HYENA), row(D_ATTN), _resident((1, D_ATTN)), _resident(wo.shape),
                  _resident((1, D_MODEL)), _resident(wg.shape), _resident(wu.shape), _resident(wd.shape),
                  _resident((1, D_MODEL))],
        out_specs=row(D_MODEL),
        out_shape=jax.ShapeDtypeStruct((n, D_MODEL), F32),
        scratch_shapes=[pltpu.VMEM((OUT_ROW_TILE, D_FF), BF16)],
        compiler_params=pltpu.CompilerParams(dimension_semantics=("parallel",),
                                             vmem_limit_bytes=V7X_VMEM_LIMIT_BYTES),
        name="out_ffn2",
    )(x1, yh, ya, ga, wo, g3, wg, wu, wd, gf)


def _dft_tables(seq):
    n = 2 * seq
    s = np.arange(seq, dtype=np.int64)[None, :]
    turn = lambda steps: 2.0 * np.pi * (steps % (2 * n)) / (2 * n)
    theta = turn((2 * np.arange(DFT_BASE_ROWS, dtype=np.int64)[:, None] + 1) * s)
    phase = turn(2 * DFT_BASE_ROWS * np.arange(seq // DFT_BASE_ROWS, dtype=np.int64)[:, None] * s)
    tab = lambda x: jnp.asarray(x.astype(np.float32))
    c0, s0 = np.cos(theta), np.sin(theta)
    return (tab(np.stack([c0, -s0])), tab(np.stack([s0, c0])),
            tab(np.cos(phase))[:, None, :], tab(np.sin(phase))[:, None, :])


def _filter_features(seq):
    t = jnp.linspace(0.0, 1.0, seq, dtype=F32)[:, None]
    w = 2.0 * math.pi * jnp.arange(seq, dtype=F32)[:, None] / seq
    f = jnp.linspace(1e-4, FILTER_BANDS - 1, FILTER_BANDS, dtype=F32)[None, :]
    z = jnp.concatenate([t, jnp.cos(f * w), -jnp.sin(f * w)], axis=-1)
    return jnp.pad(z, ((0, 0), (0, FEAT_PAD - FILTER_EMB)))


def _decay_rates():
    max_decay = math.log(DECAY_TARGET) / FAST_DECAY_PCT
    min_decay = math.log(DECAY_TARGET) / SLOW_DECAY_PCT
    return jnp.linspace(min_decay, max_decay, D_HYENA, dtype=F32)[None, :]


def kernel(x, ffn1_norm_g, ffn1_w_gate, ffn1_w_up, ffn1_w_down, mix_norm_g, w_in, hy_conv_w, hy_conv_b, hy_filt_w1, hy_filt_b1, hy_filt_w2, hy_filt_b2, hy_filt_w3, hy_filt_b3, hy_filt_w_out, hy_filt_freq, hy_filt_skip, hy_out_norm_g, attn_out_norm_g, w_out, ffn2_norm_g, ffn2_w_gate, ffn2_w_up, ffn2_w_down, final_norm_g):
    b, seq, d = x.shape
    assert d == D_MODEL and (b * seq) % OUT_ROW_TILE == 0 and seq % (DILATIONS[-1] * Q_BLOCK) == 0
    row = lambda a: a.reshape(1, -1).astype(F32)
    f32 = lambda a: a.astype(F32)

    w1 = jnp.pad(f32(hy_filt_w1), ((0, FEAT_PAD - FILTER_EMB), (0, 0)))
    hspec, fmat, wg1, wu1, wd1, win = _hy_filter(
        seq, _filter_features(seq), w1, row(hy_filt_b1), f32(hy_filt_w2), row(hy_filt_b2), f32(hy_filt_w3),
        row(hy_filt_b3), f32(hy_filt_w_out), row(hy_filt_freq), _decay_rates(), _dft_tables(seq),
        [f32(ffn1_w_gate), f32(ffn1_w_up), f32(ffn1_w_down), f32(w_in)])

    x1, hy, q, k, v, wo, wg2, wu2, wd2 = _ffn1_proj(
        x.reshape(b * seq, d), row(ffn1_norm_g), wg1, wu1, wd1, row(mix_norm_g), win,
        [f32(w_out), f32(ffn2_w_gate), f32(ffn2_w_up), f32(ffn2_w_down)])

    conv_w = hy_conv_w.astype(F32).reshape(3, 3, D_HYENA).transpose(1, 0, 2)
    conv_b = hy_conv_b.astype(F32).reshape(3, 1, D_HYENA)
    skip = hy_filt_skip.astype(F32).reshape(2, 1, D_HYENA)
    y_hy = _hyena(hy.reshape(b, seq, 3 * D_HYENA), conv_w, conv_b, skip, row(hy_out_norm_g),
                  fmat, hspec)

    shape3 = lambda a: a.reshape(b, seq, D_ATTN)
    y_at = _dil_attn(shape3(q), shape3(k), shape3(v))

    out = _out_ffn2(x1, y_hy.reshape(b * seq, D_HYENA), y_at.reshape(b * seq, D_ATTN),
                    row(attn_out_norm_g), wo, row(ffn2_norm_g), wg2, wu2, wd2, row(final_norm_g))
    return out.reshape(b, seq, d)
```

```python
import functools
import math

import numpy as np
import jax
import jax.numpy as jnp
from jax import lax
from jax.experimental import pallas as pl
from jax.experimental.pallas import tpu as pltpu

F32 = jnp.float32
BF16 = jnp.bfloat16

D_MODEL = 1024
D_HYENA = 512
D_ATTN = 512
HEAD_DIM = 64
N_HEADS = D_ATTN // HEAD_DIM
D_FF = 2816
FILTER_EMB = 33
FILTER_BANDS = 16
FILTER_WIDTH = 64
DECAY_TARGET = 1e-2
FAST_DECAY_PCT = 0.3
SLOW_DECAY_PCT = 1.5
DILATIONS = (1, 4, 16)
HALF_WINDOW = 64
RMS_EPS = 1e-6
NEG_INF = -1e30
LOG2_E = math.log2(math.e)

V7X_LANES = 128
V7X_SUBLANES = 8
V7X_BF16_SUBLANES = 16
V7X_VMEM_LIMIT_BYTES = 56 * 1024 * 1024

ROW_TILE = 512
OUT_ROW_TILE = 1024
FF_CHUNK = 256
FREQ_BLOCK = 1024
FILTER_ROWS = 512
DFT_BASE_ROWS = 256
FILTER_PREP_STEPS = 8
Q_BLOCK = 128
K_WINDOW = 256
ATTN_GROUP = 16
CONV_ROWS = 256
GATE_SUB_ROWS = 64
FEAT_PAD = 128


def _dot(a, b):
    return jnp.dot(a, b, preferred_element_type=F32)


def _rms(x, g):
    return x * lax.rsqrt(jnp.mean(x * x, axis=-1, keepdims=True) + RMS_EPS) * g


def _swiglu(h, wg_ref, wu_ref, wd_ref, act_ref):
    for c in range(D_FF // FF_CHUNK):
        cols = slice(c * FF_CHUNK, (c + 1) * FF_CHUNK)
        g = _dot(h, wg_ref[:, cols])
        u = _dot(h, wu_ref[:, cols])
        act_ref[:, cols] = (g * jax.nn.sigmoid(g) * u).astype(BF16)
    return _dot(act_ref[...], wd_ref[...])


def _resident(shape):
    return pl.BlockSpec(shape, lambda *_: (0,) * len(shape), pipeline_mode=pl.Buffered(1))


def _cast_rider(weights, steps):
    in_specs, out_specs, out_shapes = [], [], []
    for w in weights:
        rows, cols = w.shape
        visits = 1
        while (rows * visits) % steps or (rows * visits // steps) % V7X_BF16_SUBLANES:
            visits *= 2
        slab = pl.BlockSpec((rows * visits // steps, cols), lambda i, visits=visits: (i // visits, 0))
        in_specs.append(slab)
        out_specs.append(slab)
        out_shapes.append(jax.ShapeDtypeStruct(w.shape, BF16))
    return in_specs, out_specs, out_shapes


def _cast_slabs(src_refs, dst_refs):
    for src, dst in zip(src_refs, dst_refs):
        dst[...] = src[...].astype(BF16)


def _ffn1_proj_kernel(x_ref, g1_ref, wg_ref, wu_ref, wd_ref, g2_ref, win_ref, *rest):
    n_cast = (len(rest) - 6) // 2
    x1_ref, hy_ref, q_ref, k_ref, v_ref = rest[n_cast:n_cast + 5]
    act_ref = rest[-1]
    _cast_slabs(rest[:n_cast], rest[n_cast + 5:-1])
    x = x_ref[...]
    h = _rms(x, g1_ref[...]).astype(BF16)
    x1 = x + 0.5 * _swiglu(h, wg_ref, wu_ref, wd_ref, act_ref)
    x1_ref[...] = x1
    h2 = _rms(x1, g2_ref[...]).astype(BF16)
    nh = 3 * D_HYENA
    hy_ref[...] = _dot(h2, win_ref[:, :nh])
    q_ref[...] = _dot(h2, win_ref[:, nh:nh + D_ATTN])
    k_ref[...] = _dot(h2, win_ref[:, nh + D_ATTN:nh + 2 * D_ATTN])
    v_ref[...] = _dot(h2, win_ref[:, nh + 2 * D_ATTN:])


def _ffn1_proj(x2d, g1, wg, wu, wd, g2, win, later_weights):
    n = x2d.shape[0]
    steps = n // ROW_TILE
    row = lambda w: pl.BlockSpec((ROW_TILE, w), lambda i: (i, 0))
    cast_in, cast_out, cast_shapes = _cast_rider(later_weights, steps)
    return pl.pallas_call(
        _ffn1_proj_kernel,
        grid=(steps,),
        in_specs=[row(D_MODEL), _resident((1, D_MODEL)), _resident(wg.shape), _resident(wu.shape),
                  _resident(wd.shape), _resident((1, D_MODEL)), _resident(win.shape)] + cast_in,
        out_specs=[row(D_MODEL), row(3 * D_HYENA), row(D_ATTN), row(D_ATTN), row(D_ATTN)] + cast_out,
        out_shape=[jax.ShapeDtypeStruct((n, D_MODEL), F32),
                   jax.ShapeDtypeStruct((n, 3 * D_HYENA), F32),
                   jax.ShapeDtypeStruct((n, D_ATTN), F32),
                   jax.ShapeDtypeStruct((n, D_ATTN), F32),
                   jax.ShapeDtypeStruct((n, D_ATTN), F32)] + cast_shapes,
        scratch_shapes=[pltpu.VMEM((ROW_TILE, D_FF), BF16)],
        compiler_params=pltpu.CompilerParams(dimension_semantics=("arbitrary",),
                                             vmem_limit_bytes=V7X_VMEM_LIMIT_BYTES),
        name="ffn1_proj",
    )(x2d, g1, wg, wu, wd, g2, win, *later_weights)


def _split_bf16(a):
    hi = a.astype(BF16)
    return hi, (a - hi.astype(F32)).astype(BF16)


def _dot_split(a, b):
    a_hi, a_lo = a
    b_hi, b_lo = b
    return _dot(a_hi, b_hi) + (_dot(a_hi, b_lo) + _dot(a_lo, b_hi))


def _filter_kernel(seq, feat_ref, w1_ref, b1_ref, w2_ref, b2_ref, w3_ref, b3_ref, wtop_ref, wbot_ref,
                   freq_ref, delta_ref, skip_ref, lead_ref, lag_ref, cosp_ref, sinp_ref, *rest):
    n_cast = (len(rest) - 5) // 2
    h_ref, f_ref = rest[n_cast], rest[n_cast + 1]
    hcat_ref, hid_ref, act_ref = rest[-3:]
    _cast_slabs(rest[:n_cast], rest[n_cast + 2:-3])
    c_ = D_HYENA
    half = seq // 2
    hb = FREQ_BLOCK // 2
    i = pl.program_id(0)

    hi = lax.Precision.HIGHEST
    layer = lambda h, w_ref, b_ref: jnp.sin(freq_ref[...] * (
        jnp.dot(h, w_ref[...], precision=hi, preferred_element_type=F32) + b_ref[...]))

    @pl.when(i == 0)
    def _():
        act_ref[...] = layer(feat_ref[...], w1_ref, b1_ref)

    @pl.when(i == 1)
    def _():
        act_ref[...] = layer(act_ref[...], w2_ref, b2_ref)

    @pl.when(i == 2)
    def _():
        hid_ref[0], hid_ref[1] = _split_bf16(layer(act_ref[...], w3_ref, b3_ref))

    n_layers = 3
    pieces = [(part, o) for part in range(2) for o in range(2)]
    assert n_layers + len(pieces) <= FILTER_PREP_STEPS
    for step, (part, o) in enumerate(pieces, start=n_layers):
        @pl.when(i == step)
        def _(part=part, o=o):
            w_ref = (wtop_ref, wbot_ref)[part]
            h = (hid_ref[0], hid_ref[1])
            row = lax.broadcasted_iota(jnp.int32, (half, c_), 0)
            t = (row[:, :1] + part * half).astype(F32) * (1.0 / (seq - 1))
            decay = jnp.exp(-t * jnp.abs(delta_ref[...]))
            rows = slice(part * half, (part + 1) * half)
            fwd = _dot_split(h, _split_bf16(w_ref[:, (2 * o) * c_:(2 * o + 1) * c_])) * decay
            bwd = _dot_split(h, _split_bf16(w_ref[:, (2 * o + 1) * c_:(2 * o + 2) * c_])) * decay
            if part == 0:
                bwd = jnp.where(row == 0, 0.0, bwd)
            hcat_ref[0, rows, o * c_:(o + 1) * c_] = (fwd + bwd).astype(BF16)
            hcat_ref[1, rows, o * c_:(o + 1) * c_] = (fwd - bwd).astype(BF16)

    @pl.when(i >= FILTER_PREP_STEPS)
    def _():
        row0 = (i - FILTER_PREP_STEPS) * FILTER_ROWS
        is_sin = (row0 // hb) % 2
        group0 = (row0 // FREQ_BLOCK) * (hb // DFT_BASE_ROWS) + (row0 % hb) // DFT_BASE_ROWS
        for g in range(FILTER_ROWS // DFT_BASE_ROWS):
            rows = slice(g * DFT_BASE_ROWS, (g + 1) * DFT_BASE_ROWS)
            f_ref[rows, :] = (lead_ref[is_sin] * cosp_ref[group0 + g]
                              - lag_ref[is_sin] * sinp_ref[group0 + g]).astype(BF16)
        p = _dot(f_ref[...], hcat_ref[is_sin])
        re_part = (1 - is_sin).astype(F32)
        for o in range(2):
            h_ref[o] = (p[:, o * c_:(o + 1) * c_] + re_part * skip_ref[o:o + 1, :]) * (1.0 / seq)


def _hy_filter(seq, feat, w1, b1, w2, b2, w3, b3, wout, freq, delta, skip, dft_tables, later_weights):
    full = lambda a: _resident(a.shape)
    assert (FREQ_BLOCK // 2) % FILTER_ROWS == 0
    steps = FILTER_PREP_STEPS + 2 * seq // FILTER_ROWS
    block = lambda i: jnp.maximum(i - FILTER_PREP_STEPS, 0)
    cast_in, cast_out, cast_shapes = _cast_rider(later_weights, steps)
    twice = lambda r: jnp.concatenate([r, r], axis=1)
    diag2 = lambda w: jnp.concatenate([jnp.concatenate([w, jnp.zeros_like(w)], axis=1),
                                       jnp.concatenate([jnp.zeros_like(w), w], axis=1)], axis=0)
    feat = jnp.concatenate([feat[:seq // 2], feat[seq // 2:]], axis=1)
    w1, w2, w3 = diag2(w1), diag2(w2), diag2(w3)
    b1, b2, b3, freq = twice(b1), twice(b2), twice(b3), twice(freq)
    wtop = jnp.concatenate([wout, jnp.zeros_like(wout)], axis=0)
    wbot = jnp.concatenate([jnp.zeros_like(wout), wout], axis=0)
    return pl.pallas_call(
        functools.partial(_filter_kernel, seq),
        grid=(steps,),
        in_specs=[full(feat), full(w1), full(b1), full(w2), full(b2), full(w3), full(b3), full(wtop),
                  full(wbot), full(freq), full(delta), full(skip)] + [full(t) for t in dft_tables] + cast_in,
        out_specs=[pl.BlockSpec((2, FILTER_ROWS, D_HYENA), lambda i: (0, block(i), 0)),
                   pl.BlockSpec((FILTER_ROWS, seq), lambda i: (block(i), 0))] + cast_out,
        out_shape=[jax.ShapeDtypeStruct((2, 2 * seq, D_HYENA), F32),
                   jax.ShapeDtypeStruct((2 * seq, seq), BF16)] + cast_shapes,
        scratch_shapes=[pltpu.VMEM((2, seq, 2 * D_HYENA), BF16),
                        pltpu.VMEM((2, seq // 2, 2 * FILTER_WIDTH), BF16),
                        pltpu.VMEM((seq // 2, 2 * FILTER_WIDTH), F32)],
        compiler_params=pltpu.CompilerParams(dimension_semantics=("arbitrary",),
                                             vmem_limit_bytes=V7X_VMEM_LIMIT_BYTES),
        name="hy_filter",
    )(feat, w1, b1, w2, b2, w3, b3, wtop, wbot, freq, delta, skip, *dft_tables, *later_weights)


def _short_conv(cur, before, after, w_ref, b_ref):
    rows = cur.shape[0]
    row = lax.broadcasted_iota(jnp.int32, cur.shape, 0)
    prev = jnp.where(row == 0, before, pltpu.roll(cur, 1, axis=0))
    nxt = jnp.where(row == rows - 1, after, pltpu.roll(cur, rows - 1, axis=0))
    return b_ref[0] + prev * w_ref[0, 0:1, :] + cur * w_ref[0, 1:2, :] + nxt * w_ref[0, 2:3, :]


def _short_conv_rows(u_ref, w_ref, b_ref, t0, seq):
    zero = jnp.zeros((1, u_ref.shape[2]), F32)
    before = u_ref[0, t0 - V7X_SUBLANES:t0, :][V7X_SUBLANES - 1:] if t0 > 0 else zero
    end = t0 + CONV_ROWS
    after = u_ref[0, end:end + V7X_SUBLANES, :][:1] if end < seq else zero
    return _short_conv(u_ref[0, t0:end, :], before, after, w_ref, b_ref)


def _hyena_kernel(seq, v_ref, gate_ref, gate_lo_ref, gate_hi_ref, wv_ref, bv_ref, wg_ref, bg_ref,
                  gain_ref, f_ref, h_ref, out_ref, zb_ref, acc_ref, g_ref, y_ref):
    o = pl.program_id(1)
    j = pl.program_id(2)
    nfb = 2 * seq // FREQ_BLOCK
    gate_rows = seq // nfb
    chunks = range(0, seq, CONV_ROWS)
    hb = FREQ_BLOCK // 2

    def f_block(jb):
        return f_ref[jb * FREQ_BLOCK:(jb + 1) * FREQ_BLOCK, :]

    def forward(jb):
        zfreq = _dot(f_block(jb), zb_ref[...])
        zr, zi = zfreq[:hb], zfreq[hb:]
        hr, hi = h_ref[0, :hb, :], h_ref[0, hb:, :]
        y_ref[jb % 2, :, :hb] = (zr * hr - zi * hi).T.astype(BF16)
        y_ref[jb % 2, :, hb:] = (zr * hi + zi * hr).T.astype(BF16)

    def inverse(jb):
        contrib = _dot(y_ref[jb % 2], f_block(jb))
        if jb == 0:
            acc_ref[...] = contrib
        else:
            acc_ref[...] += contrib

    def gate_chunk():
        t0 = pl.multiple_of(j * gate_rows, gate_rows)
        for r0 in range(0, gate_rows, GATE_SUB_ROWS):
            r1 = r0 + GATE_SUB_ROWS
            if r0 == 0:
                before = jnp.where(j > 0, gate_lo_ref[0, V7X_SUBLANES - 1:, :], 0.0)
            else:
                before = gate_ref[0, r0 - V7X_SUBLANES:r0, :][V7X_SUBLANES - 1:]
            if r1 == gate_rows:
                after = jnp.where(j < nfb - 1, gate_hi_ref[0, :1, :], 0.0)
            else:
                after = gate_ref[0, r1:r1 + V7X_SUBLANES, :][:1]
            g_ref[pl.ds(t0 + r0, GATE_SUB_ROWS), :] = _short_conv(gate_ref[0, r0:r1, :], before, after,
                                                                   wg_ref, bg_ref)

    @pl.when((o == 0) & (j == 0))
    def _():
        for t0 in chunks:
            zb_ref[t0:t0 + CONV_ROWS, :] = _short_conv_rows(v_ref, wv_ref, bv_ref, t0, seq).astype(BF16)

    @pl.when(j == 0)
    def _():
        gate_chunk()
        forward(0)

    for step in range(1, nfb):
        @pl.when(j == step)
        def _(step=step):
            gate_chunk()
            forward(step)
            inverse(step - 1)

    def gated(rows):
        return g_ref[rows, :] * acc_ref[:, rows].T

    @pl.when((j == nfb) & (o == 0))
    def _():
        inverse(nfb - 1)
        for t0 in chunks:
            rows = slice(t0, t0 + CONV_ROWS)
            zb_ref[rows, :] = gated(rows).astype(BF16)

    @pl.when((j == nfb) & (o == 1))
    def _():
        inverse(nfb - 1)
        for t0 in chunks:
            rows = slice(t0, t0 + CONV_ROWS)
            out_ref[0, rows, :] = _rms(gated(rows), gain_ref[...]).astype(BF16)


def _hyena(hy, conv_w, conv_b, gain, fmat, hspec):
    b, seq, _ = hy.shape
    c = D_HYENA
    nfb = 2 * seq // FREQ_BLOCK
    gate_rows = seq // nfb
    halo_per_chunk = gate_rows // V7X_SUBLANES
    fwd_blk = lambda j: jnp.minimum(j, nfb - 1)
    wpart = lambda sel: pl.BlockSpec((1, 3, c), lambda bi, o, j: (sel(o), 0, 0))
    bpart = lambda sel: pl.BlockSpec((1, 1, c), lambda bi, o, j: (sel(o), 0, 0))
    value = lambda o: 0
    gate = lambda o: 1 + o
    halo = lambda first: pl.BlockSpec((1, V7X_SUBLANES, c), lambda bi, o, j: (bi, first(fwd_blk(j)), 1 + o))
    return pl.pallas_call(
        functools.partial(_hyena_kernel, seq),
        grid=(b, 2, nfb + 1),
        in_specs=[pl.BlockSpec((1, seq, c), lambda bi, o, j: (bi, 0, 0)),
                  pl.BlockSpec((1, gate_rows, c), lambda bi, o, j: (bi, fwd_blk(j), 1 + o)),
                  halo(lambda jc: jnp.maximum(jc * halo_per_chunk - 1, 0)),
                  halo(lambda jc: jnp.minimum((jc + 1) * halo_per_chunk, seq // V7X_SUBLANES - 1)),
                  wpart(value), bpart(value), wpart(gate), bpart(gate),
                  pl.BlockSpec((1, c), lambda bi, o, j: (0, 0)),
                  _resident(fmat.shape),
                  pl.BlockSpec((1, FREQ_BLOCK, c), lambda bi, o, j: (o, fwd_blk(j), 0))],
        out_specs=pl.BlockSpec((1, seq, c), lambda bi, o, j: (bi, 0, 0)),
        out_shape=jax.ShapeDtypeStruct((b, seq, c), BF16),
        scratch_shapes=[pltpu.VMEM((seq, c), BF16), pltpu.VMEM((c, seq), F32),
                        pltpu.VMEM((seq, c), F32), pltpu.VMEM((2, c, FREQ_BLOCK), BF16)],
        compiler_params=pltpu.CompilerParams(dimension_semantics=("parallel", "arbitrary", "arbitrary"),
                                             vmem_limit_bytes=V7X_VMEM_LIMIT_BYTES),
        name="hyena",
    )(hy, hy, hy, hy, conv_w, conv_b, conv_w, conv_b, gain, fmat, hspec)


def _attn_kernel(seq, q_ref, k_ref, v_ref, bias_ref, bias16_ref, o_ref,
                 qs_ref, ks_ref, vs_ref, res_ref, nat_ref, s_ref, e_ref):
    lanes = V7X_LANES
    qscale = LOG2_E / math.sqrt(HEAD_DIM)

    for p, d in enumerate(DILATIONS):
        ls = seq // d
        first = lax.broadcasted_iota(jnp.int32, (ls, lanes), 1) < HEAD_DIM
        for r in range(d):
            src = pl.ds(r, ls, stride=d) if d > 1 else pl.ds(0, ls)
            rows = slice(r * ls, (r + 1) * ls)
            qq = q_ref[0, src, :] * qscale
            qs_ref[p, 0, rows, :] = jnp.where(first, qq, 0.0).astype(BF16)
            qs_ref[p, 1, rows, :] = jnp.where(first, 0.0, qq).astype(BF16)
            ks_ref[p, rows, :] = k_ref[0, src, :].astype(BF16)
            vv = v_ref[0, src, :]
            vs_ref[p, 0, rows, :] = jnp.where(first, vv, 1.0).astype(BF16)
            vs_ref[p, 1, rows, :] = jnp.where(first, 1.0, vv).astype(BF16)

    first = lax.broadcasted_iota(jnp.int32, (Q_BLOCK, lanes), 1) < HEAD_DIM

    def run_pattern(p, nkeys, placement):
        def body(g, carry):
            blocks = [placement(g * ATTN_GROUP + i) for i in range(ATTN_GROUP)]
            for i, (row0, krow0, bias_of_head) in enumerate(blocks):
                kw = ks_ref[p, pl.ds(krow0, nkeys), :]
                for h in range(2):
                    qh = qs_ref[p, h, pl.ds(row0, Q_BLOCK), :]
                    s = lax.dot_general(qh, kw, (((1,), (1,)), ((), ())), preferred_element_type=F32)
                    s_ref[2 * i + h, :, :nkeys] = s + bias_of_head(h)
            for i, (row0, krow0, bias_of_head) in enumerate(blocks):
                ms = []
                for h in range(2):
                    s = s_ref[2 * i + h, :, :nkeys]
                    m = jnp.max(s, axis=-1, keepdims=True)
                    e_ref[2 * i + h, :, :nkeys] = jnp.exp2(s - m).astype(BF16)
                    ms.append(m)
                res_ref[p, 1, pl.ds(row0, Q_BLOCK), :] = jnp.where(first, ms[0], ms[1])
            for i, (row0, krow0, bias_of_head) in enumerate(blocks):
                o0 = _dot(e_ref[2 * i, :, :nkeys], vs_ref[p, 0, pl.ds(krow0, nkeys), :])
                o1 = _dot(e_ref[2 * i + 1, :, :nkeys], vs_ref[p, 1, pl.ds(krow0, nkeys), :])
                res_ref[p, 0, pl.ds(row0, Q_BLOCK), :] = jnp.where(first, o0, o1)
                res_ref[p, 2, pl.ds(row0, Q_BLOCK), :] = jnp.where(first, o1, o0)
            return carry

        lax.fori_loop(0, seq // Q_BLOCK // ATTN_GROUP, body, 0)

    for p, d in enumerate(DILATIONS[:2]):
        ls = seq // d
        nblk = ls // Q_BLOCK

        def banded(n, p=p, ls=ls, nblk=nblk):
            r = n // nblk
            ib = n % nblk
            i0 = ib * Q_BLOCK
            k0 = jnp.clip(i0 - HALF_WINDOW, 0, ls - K_WINDOW)
            case = jnp.where(ib == 0, 0, jnp.where(ib == nblk - 1, 2, 1))
            row0 = pl.multiple_of(r * ls + i0, Q_BLOCK)
            krow0 = pl.multiple_of(r * ls + k0, HALF_WINDOW)
            return row0, krow0, lambda h: bias_ref[0, (p * 3 + case) * 2 + h]

        run_pattern(p, K_WINDOW, banded)

    def full(n):
        row0 = pl.multiple_of(n * Q_BLOCK, Q_BLOCK)
        return row0, row0, lambda h: bias16_ref[0, h]

    run_pattern(2, Q_BLOCK, full)

    for p, d in enumerate(DILATIONS[1:], start=1):
        ls = seq // d
        for r in range(d):
            for kind in range(3):
                nat_ref[p - 1, kind, pl.ds(r, ls, stride=d), :] = res_ref[p, kind, r * ls:(r + 1) * ls, :]

    for t0 in range(0, seq, CONV_ROWS):
        rows = slice(t0, t0 + CONV_ROWS)
        parts = [tuple(res_ref[0, kind, rows, :] for kind in range(3))]
        parts += [tuple(nat_ref[p, kind, rows, :] for kind in range(3)) for p in range(2)]
        m = functools.reduce(jnp.maximum, [pt[1] for pt in parts])
        num = 0.0
        den = 0.0
        for out, mp, lp in parts:
            w = jnp.exp2(mp - m)
            num = num + w * out
            den = den + w * pltpu.roll(lp, HEAD_DIM, axis=1)
        o_ref[0, rows, :] = num / den


def _attn_bias_tables():
    slopes = np.array([2.0 ** (-8.0 * (i + 1) / N_HEADS) for i in range(N_HEADS)], np.float32)
    slopes = jnp.asarray(slopes.reshape(N_HEADS // 2, 1, 2, 1, 1))
    qi = lax.broadcasted_iota(jnp.int32, (Q_BLOCK, K_WINDOW), 0)
    kj = lax.broadcasted_iota(jnp.int32, (Q_BLOCK, K_WINDOW), 1)
    offsets = (0, -HALF_WINDOW, -2 * HALF_WINDOW)
    dist = jnp.stack([jnp.abs(kj - qi + off) for off in offsets])
    valid = dist <= HALF_WINDOW
    dil = jnp.asarray(np.array(DILATIONS[:2], np.float32).reshape(2, 1, 1, 1))
    scaled = (dil * dist.astype(F32)[None])[None, :, :, None]
    banded = jnp.where(valid[None, None, :, None], -slopes[:, None] * scaled, NEG_INF)
    banded = banded.reshape(N_HEADS // 2, 12, Q_BLOCK, K_WINDOW)
    d16 = dist[0, :, :Q_BLOCK]
    full = jnp.where(d16 <= HALF_WINDOW, -slopes[:, 0] * (DILATIONS[2] * d16.astype(F32)), NEG_INF)
    to_base2 = lambda t: jnp.where(t > 0.5 * NEG_INF, t * LOG2_E, NEG_INF).astype(F32)
    return to_base2(banded), to_base2(full)


def _dil_attn(q, k, v):
    b, seq, _ = q.shape
    nhp = N_HEADS // 2
    bias, bias16 = _attn_bias_tables()
    head_pair = pl.BlockSpec((1, seq, V7X_LANES), lambda bi, hp: (bi, 0, hp))
    return pl.pallas_call(
        functools.partial(_attn_kernel, seq),
        grid=(b, nhp),
        in_specs=[head_pair, head_pair, head_pair,
                  pl.BlockSpec((1, 12, Q_BLOCK, K_WINDOW), lambda bi, hp: (hp, 0, 0, 0)),
                  pl.BlockSpec((1, 2, Q_BLOCK, Q_BLOCK), lambda bi, hp: (hp, 0, 0, 0))],
        out_specs=head_pair,
        out_shape=jax.ShapeDtypeStruct((b, seq, D_ATTN), F32),
        scratch_shapes=[pltpu.VMEM((3, 2, seq, V7X_LANES), BF16),
                        pltpu.VMEM((3, seq, V7X_LANES), BF16),
                        pltpu.VMEM((3, 2, seq, V7X_LANES), BF16),
                        pltpu.VMEM((3, 3, seq, V7X_LANES), F32),
                        pltpu.VMEM((2, 3, seq, V7X_LANES), F32),
                        pltpu.VMEM((2 * ATTN_GROUP, Q_BLOCK, K_WINDOW), F32),
                        pltpu.VMEM((2 * ATTN_GROUP, Q_BLOCK, K_WINDOW), BF16)],
        compiler_params=pltpu.CompilerParams(dimension_semantics=("parallel", "parallel"),
                                             vmem_limit_bytes=V7X_VMEM_LIMIT_BYTES),
        name="dil_attn",
    )(q, k, v, bias, bias16)


def _out_ffn2_kernel(x1_ref, yh_ref, ya_ref, ga_ref, wo_ref, g3_ref, wg_ref, wu_ref, wd_ref, gf_ref,
                     out_ref, act_ref):
    ya = _rms(ya_ref[...], ga_ref[...]).astype(BF16)
    mix = _dot(yh_ref[...], wo_ref[:D_HYENA, :]) + _dot(ya, wo_ref[D_HYENA:, :])
    x2 = x1_ref[...] + mix
    h = _rms(x2, g3_ref[...]).astype(BF16)
    x3 = x2 + 0.5 * _swiglu(h, wg_ref, wu_ref, wd_ref, act_ref)
    out_ref[...] = _rms(x3, gf_ref[...])


def _out_ffn2(x1, yh, ya, ga, wo, g3, wg, wu, wd, gf):
    n = x1.shape[0]
    row = lambda w: pl.BlockSpec((OUT_ROW_TILE, w), lambda i: (i, 0))
    return pl.pallas_call(
        _out_ffn2_kernel,
        grid=(n // OUT_ROW_TILE,),
        in_specs=[row(D_MODEL), row(D_HYENA), row(D_ATTN), _resident((1, D_ATTN)), _resident(wo.shape),
                  _resident((1, D_MODEL)), _resident(wg.shape), _resident(wu.shape), _resident(wd.shape),
                  _resident((1, D_MODEL))],
        out_specs=row(D_MODEL),
        out_shape=jax.ShapeDtypeStruct((n, D_MODEL), F32),
        scratch_shapes=[pltpu.VMEM((OUT_ROW_TILE, D_FF), BF16)],
        compiler_params=pltpu.CompilerParams(dimension_semantics=("parallel",),
                                             vmem_limit_bytes=V7X_VMEM_LIMIT_BYTES),
        name="out_ffn2",
    )(x1, yh, ya, ga, wo, g3, wg, wu, wd, gf)


def _dft_tables(seq):
    n = 2 * seq
    s = np.arange(seq, dtype=np.int64)[None, :]
    turn = lambda steps: 2.0 * np.pi * (steps % (2 * n)) / (2 * n)
    theta = turn((2 * np.arange(DFT_BASE_ROWS, dtype=np.int64)[:, None] + 1) * s)
    phase = turn(2 * DFT_BASE_ROWS * np.arange(seq // DFT_BASE_ROWS, dtype=np.int64)[:, None] * s)
    tab = lambda x: jnp.asarray(x.astype(np.float32))
    c0, s0 = np.cos(theta), np.sin(theta)
    return (tab(np.stack([c0, -s0])), tab(np.stack([s0, c0])),
            tab(np.cos(phase))[:, None, :], tab(np.sin(phase))[:, None, :])


def _filter_features(seq):
    t = jnp.linspace(0.0, 1.0, seq, dtype=F32)[:, None]
    w = 2.0 * math.pi * jnp.arange(seq, dtype=F32)[:, None] / seq
    f = jnp.linspace(1e-4, FILTER_BANDS - 1, FILTER_BANDS, dtype=F32)[None, :]
    z = jnp.concatenate([t, jnp.cos(f * w), -jnp.sin(f * w)], axis=-1)
    return jnp.pad(z, ((0, 0), (0, FEAT_PAD - FILTER_EMB)))


def _decay_rates():
    max_decay = math.log(DECAY_TARGET) / FAST_DECAY_PCT
    min_decay = math.log(DECAY_TARGET) / SLOW_DECAY_PCT
    return jnp.linspace(min_decay, max_decay, D_HYENA, dtype=F32)[None, :]


def kernel(x, ffn1_norm_g, ffn1_w_gate, ffn1_w_up, ffn1_w_down, mix_norm_g, w_in, hy_conv_w, hy_conv_b, hy_filt_w1, hy_filt_b1, hy_filt_w2, hy_filt_b2, hy_filt_w3, hy_filt_b3, hy_filt_w_out, hy_filt_freq, hy_filt_skip, hy_out_norm_g, attn_out_norm_g, w_out, ffn2_norm_g, ffn2_w_gate, ffn2_w_up, ffn2_w_down, final_norm_g):
    b, seq, d = x.shape
    assert d == D_MODEL and (b * seq) % OUT_ROW_TILE == 0 and seq % (DILATIONS[-1] * Q_BLOCK) == 0
    row = lambda a: a.reshape(1, -1).astype(F32)
    f32 = lambda a: a.astype(F32)

    w1 = jnp.pad(f32(hy_filt_w1), ((0, FEAT_PAD - FILTER_EMB), (0, 0)))
    hspec, fmat, wg1, wu1, wd1, win = _hy_filter(
        seq, _filter_features(seq), w1, row(hy_filt_b1), f32(hy_filt_w2), row(hy_filt_b2), f32(hy_filt_w3),
        row(hy_filt_b3), f32(hy_filt_w_out), row(hy_filt_freq), _decay_rates(), f32(hy_filt_skip),
        _dft_tables(seq), [f32(ffn1_w_gate), f32(ffn1_w_up), f32(ffn1_w_down), f32(w_in)])

    x1, hy, q, k, v, wo, wg2, wu2, wd2 = _ffn1_proj(
        x.reshape(b * seq, d), row(ffn1_norm_g), wg1, wu1, wd1, row(mix_norm_g), win,
        [f32(w_out), f32(ffn2_w_gate), f32(ffn2_w_up), f32(ffn2_w_down)])

    conv_w = hy_conv_w.astype(F32).reshape(3, 3, D_HYENA).transpose(1, 0, 2)
    conv_b = hy_conv_b.astype(F32).reshape(3, 1, D_HYENA)
    y_hy = _hyena(hy.reshape(b, seq, 3 * D_HYENA), conv_w, conv_b, row(hy_out_norm_g), fmat, hspec)

    shape3 = lambda a: a.reshape(b, seq, D_ATTN)
    y_at = _dil_attn(shape3(q), shape3(k), shape3(v))

    out = _out_ffn2(x1, y_hy.reshape(b * seq, D_HYENA), y_at.reshape(b * seq, D_ATTN),
                    row(attn_out_norm_g), wo, row(ffn2_norm_g), wg2, wu2, wd2, row(final_norm_g))
    return out.reshape(b, seq, d)
```
